```python
import math
import jax, jax.numpy as jnp
from jax import lax
import numpy as np

D_MODEL = 2048
BATCH = 2
SEQ = 4096
DEPTH = 4

CHUNK = 64
LEFT_CHUNKS = 8
BAND = (LEFT_CHUNKS + 1) * CHUNK
HEAD_DIM = 128
A_HEADS = 8
B_HEADS = 8
B_QK_DIM = HEAD_DIM // 2
A_WIDTH = A_HEADS * HEAD_DIM
B_WIDTH = B_HEADS * HEAD_DIM
MIX_WIDTH = A_WIDTH + B_WIDTH
IN_WIDTH = 3 * A_WIDTH + 3 * B_WIDTH
REL_CLIP = 128
T5_BUCKETS = 32
T5_MAX_DIST = 512
Q_BLOCK = 128
MEM_LEN = 256
X_HEADS = 4
X_HEAD_DIM = D_MODEL // X_HEADS
N_GROUPS = 4
EXPERTS_PER_GROUP = 8
N_EXPERTS = N_GROUPS * EXPERTS_PER_GROUP
TOP_K_INNER = 2
D_EXPERT = D_MODEL // 8

EPS = 1e-6
NEG_INF = -1e30

kernel_name = "hybrid_chunk_stream_diff_moe"


def rmsnorm(x, g):
    x32 = x.astype(jnp.float32)
    y = x32 * lax.rsqrt(jnp.mean(x32 * x32, axis=-1, keepdims=True) + EPS)
    return (y * g.astype(jnp.float32)).astype(x.dtype)


def t5_bucket(rel):
    half = T5_BUCKETS // 2
    max_exact = half // 2
    ret = jnp.where(rel > 0, half, 0)
    n = jnp.abs(rel)
    n_f = jnp.maximum(n, max_exact).astype(jnp.float32)
    large = max_exact + (jnp.log(n_f / max_exact) / math.log(T5_MAX_DIST / max_exact)
                         * (half - max_exact)).astype(jnp.int32)
    large = jnp.minimum(large, half - 1)
    return ret + jnp.where(n < max_exact, n, large)


def chunked_band_attention(q, k, v, rel_table):
    b, s, h, dh = q.shape
    nc = s // CHUNK
    qc = q.reshape(b, nc, CHUNK, h, dh)
    pad = ((0, 0), (LEFT_CHUNKS * CHUNK, 0), (0, 0), (0, 0))
    kp = jnp.pad(k, pad).reshape(b, nc + LEFT_CHUNKS, CHUNK, h, dh)
    vp = jnp.pad(v, pad).reshape(b, nc + LEFT_CHUNKS, CHUNK, h, dh)
    band_idx = jnp.arange(nc)[:, None] + jnp.arange(LEFT_CHUNKS + 1)[None, :]
    kb = kp[:, band_idx].reshape(b, nc, BAND, h, dh)
    vb = vp[:, band_idx].reshape(b, nc, BAND, h, dh)
    logits = jnp.einsum('bcqhd,bckhd->bhcqk', qc, kb).astype(jnp.float32) * (dh ** -0.5)
    rel = (jnp.arange(BAND)[None, :] - LEFT_CHUNKS * CHUNK) - jnp.arange(CHUNK)[:, None]
    bias = rel_table[jnp.clip(rel, -REL_CLIP, REL_CLIP) + REL_CLIP].astype(jnp.float32)
    logits = logits + bias.transpose(2, 0, 1)[None, :, None]
    valid = jnp.repeat(band_idx >= LEFT_CHUNKS, CHUNK, axis=1)
    logits = jnp.where(valid[None, None, :, None, :], logits, NEG_INF)
    p = jax.nn.softmax(logits, axis=-1).astype(v.dtype)
    o = jnp.einsum('bhcqk,bckhd->bcqhd', p, vb)
    return o.reshape(b, s, h * dh)


def diff_attention(q, k, v, t5_table, lam_vecs, subln_g, lam_init):
    b, s, h, _, dq = q.shape
    dh = v.shape[-1]
    nblk = s // Q_BLOCK
    lv = lam_vecs.astype(jnp.float32)
    lam = jnp.exp(jnp.sum(lv[0] * lv[1])) - jnp.exp(jnp.sum(lv[2] * lv[3])) + lam_init
    kpos = jnp.arange(s)
    kchunk = kpos // CHUNK
    scale = dq ** -0.5
    qblocks = q.reshape(b, nblk, Q_BLOCK, h, 2, dq).transpose(1, 0, 2, 3, 4, 5)

    def block(args):
        qblk, i = args
        qpos = i * Q_BLOCK + jnp.arange(Q_BLOCK)
        logits = jnp.einsum('bqhmd,bkhmd->bmhqk', qblk, k).astype(jnp.float32) * scale
        bias = t5_table[t5_bucket(kpos[None, :] - qpos[:, None])].astype(jnp.float32)
        logits = logits + bias.transpose(2, 0, 1)[None, None]
        mask = kchunk[None, :] <= (qpos // CHUNK)[:, None]
        logits = jnp.where(mask, logits, NEG_INF)
        p = jax.nn.softmax(logits, axis=-1)
        pd = p[:, 0] - lam * p[:, 1]
        return jnp.einsum('bhqk,bkhd->bqhd', pd.astype(v.dtype), v)

    o = lax.map(block, (qblocks, jnp.arange(nblk)))
    o = o.transpose(1, 0, 2, 3, 4).reshape(b, s, h, dh)
    o = rmsnorm(o, subln_g) * (1.0 - lam_init)
    return o.reshape(b, s, h * dh)


def cross_attention(xn, memn, wq, wkv, wo):
    b, s, d = xn.shape
    m = memn.shape[1]
    q = (xn @ wq).reshape(b, s, X_HEADS, X_HEAD_DIM)
    kv = (memn @ wkv).reshape(b, m, 2, X_HEADS, X_HEAD_DIM)
    logits = jnp.einsum('bshd,bmhd->bhsm', q, kv[:, :, 0]).astype(jnp.float32) * (X_HEAD_DIM ** -0.5)
    p = jax.nn.softmax(logits, axis=-1).astype(xn.dtype)
    o = jnp.einsum('bhsm,bmhd->bshd', p, kv[:, :, 1]).reshape(b, s, d)
    return o @ wo


def hier_moe(xn, w_group, b_group, w_expert, b_expert, w1, w3, w2):
    b, s, d = xn.shape
    t = xn.reshape(-1, d)
    n_tok = t.shape[0]
    g_logits = (t @ w_group).astype(jnp.float32) + b_group.astype(jnp.float32)
    g_prob = jax.nn.softmax(g_logits, axis=-1)
    g_gate, g_sel = lax.top_k(g_prob, 1)
    e_logits = ((t @ w_expert).astype(jnp.float32) + b_expert.astype(jnp.float32))
    e_logits = e_logits.reshape(n_tok, N_GROUPS, EXPERTS_PER_GROUP)
    e_in = jnp.take_along_axis(e_logits, g_sel[:, :, None], axis=1)[:, 0]
    top_val, top_idx = lax.top_k(e_in, TOP_K_INNER)
    top_w = jax.nn.softmax(top_val, axis=-1) * g_gate
    expert_id = g_sel * EXPERTS_PER_GROUP + top_idx
    combine = jnp.sum(jax.nn.one_hot(expert_id, N_EXPERTS, dtype=jnp.float32) * top_w[..., None], axis=1)
    hid = jax.nn.silu(jnp.einsum('td,edf->tef', t, w1)) * jnp.einsum('td,edf->tef', t, w3)
    hid = hid * combine[:, :, None].astype(hid.dtype)
    out = jnp.einsum('tef,efd->td', hid, w2)
    return out.reshape(b, s, d)


def setup_inputs(seed: int = 0) -> dict:
    key = jax.random.key(seed)
    ks = jax.random.split(key, 24)
    f32 = jnp.float32
    nrm = lambda k, shape, sc: jax.random.normal(k, shape, f32) * sc
    gain = lambda k, shape: 1.0 + 0.05 * jax.random.normal(k, shape, f32)
    return {
        "x": nrm(ks[0], (BATCH, SEQ, D_MODEL), 1.0),
        "mem": nrm(ks[1], (BATCH, MEM_LEN, D_MODEL), 1.0),
        "norm_mix": gain(ks[2], (DEPTH, D_MODEL)),
        "w_in": nrm(ks[3], (DEPTH, D_MODEL, IN_WIDTH), D_MODEL ** -0.5),
        "rel_bias_a": nrm(ks[4], (DEPTH, 2 * REL_CLIP + 1, A_HEADS), 0.5),
        "t5_table": nrm(ks[5], (T5_BUCKETS, B_HEADS), 0.5),
        "diff_lambda": nrm(ks[6], (DEPTH, 4, B_QK_DIM), 0.1),
        "diff_subln": gain(ks[7], (DEPTH, HEAD_DIM)),
        "w_out": nrm(ks[8], (DEPTH, MIX_WIDTH, D_MODEL), MIX_WIDTH ** -0.5),
        "norm_cross": gain(ks[9], (DEPTH, D_MODEL)),
        "norm_mem": gain(ks[10], (DEPTH, D_MODEL)),
        "w_xq": nrm(ks[11], (DEPTH, D_MODEL, D_MODEL), D_MODEL ** -0.5),
        "w_xkv": nrm(ks[12], (DEPTH, D_MODEL, 2 * D_MODEL), D_MODEL ** -0.5),
        "w_xo": nrm(ks[13], (DEPTH, D_MODEL, D_MODEL), D_MODEL ** -0.5),
        "norm_ffn": gain(ks[14], (DEPTH, D_MODEL)),
        "w_group": nrm(ks[15], (DEPTH, D_MODEL, N_GROUPS), D_MODEL ** -0.5),
        "b_group": nrm(ks[16], (DEPTH, N_GROUPS), 0.01),
        "w_expert": nrm(ks[17], (DEPTH, D_MODEL, N_EXPERTS), D_MODEL ** -0.5),
        "b_expert": nrm(ks[18], (DEPTH, N_EXPERTS), 0.01),
        "w1": nrm(ks[19], (DEPTH, N_EXPERTS, D_MODEL, D_EXPERT), D_MODEL ** -0.5),
        "w3": nrm(ks[20], (DEPTH, N_EXPERTS, D_MODEL, D_EXPERT), D_MODEL ** -0.5),
        "w2": nrm(ks[21], (DEPTH, N_EXPERTS, D_EXPERT, D_MODEL), D_EXPERT ** -0.5),
        "norm_final": gain(ks[22], (D_MODEL,)),
    }


def reference(x, mem, norm_mix, w_in, rel_bias_a, t5_table, diff_lambda, diff_subln, w_out,
              norm_cross, norm_mem, w_xq, w_xkv, w_xo, norm_ffn, w_group, b_group,
              w_expert, b_expert, w1, w3, w2, norm_final):
    b, s, _ = x.shape
    for l in range(DEPTH):
        lam_init = 0.8 - 0.6 * math.exp(-0.3 * l)
        h = rmsnorm(x, norm_mix[l])
        proj = h @ w_in[l]
        qa, ka, va, qb, kb, vb = jnp.split(proj, 6, axis=-1)
        oa = chunked_band_attention(qa.reshape(b, s, A_HEADS, HEAD_DIM),
                                    ka.reshape(b, s, A_HEADS, HEAD_DIM),
                                    va.reshape(b, s, A_HEADS, HEAD_DIM),
                                    rel_bias_a[l])
        ob = diff_attention(qb.reshape(b, s, B_HEADS, 2, B_QK_DIM),
                            kb.reshape(b, s, B_HEADS, 2, B_QK_DIM),
                            vb.reshape(b, s, B_HEADS, HEAD_DIM),
                            t5_table, diff_lambda[l], diff_subln[l], lam_init)
        x = x + jnp.concatenate([oa, ob], axis=-1) @ w_out[l]
        x = x + cross_attention(rmsnorm(x, norm_cross[l]), rmsnorm(mem, norm_mem[l]),
                                w_xq[l], w_xkv[l], w_xo[l])
        x = x + hier_moe(rmsnorm(x, norm_ffn[l]), w_group[l], b_group[l], w_expert[l],
                         b_expert[l], w1[l], w3[l], w2[l])
    return rmsnorm(x, norm_final)
```

```python
import functools
import math

import jax
import jax.numpy as jnp
import numpy as np
from jax import lax
from jax.experimental import pallas as pl
from jax.experimental.pallas import tpu as pltpu

D_MODEL = 2048
SEQ = 4096
DEPTH = 4
CHUNK = 64
LEFT_CHUNKS = 8
HEAD_DIM = 128
A_HEADS = 8
B_HEADS = 8
B_QK_DIM = 64
A_WIDTH = 1024
B_WIDTH = 1024
IN_WIDTH = 6144
REL_CLIP = 128
T5_BUCKETS = 32
T5_MAX_DIST = 512
MEM_LEN = 256
X_HEADS = 4
X_HEAD_DIM = 512
N_GROUPS = 4
EXPERTS_PER_GROUP = 8
N_EXPERTS = 32
D_EXPERT = 256
EPS = 1e-6
NEG_INF = -1e30

LANES = 128
VMEM_LIMIT = 56 * 1024 * 1024

BF16 = jnp.bfloat16
F32 = jnp.float32

A_QB = 256
A_WIN = A_QB + LEFT_CHUNKS * CHUNK
B_QB = 256
B_NEAR = 3
X_QB = 512


def _params(sem):
    return pltpu.CompilerParams(dimension_semantics=sem, vmem_limit_bytes=VMEM_LIMIT)


def _rmsnorm_kernel(x_ref, g_ref, o_ref):
    x = x_ref[...]
    ms = jnp.mean(x * x, axis=-1, keepdims=True)
    o_ref[...] = (x * lax.rsqrt(ms + EPS) * g_ref[...]).astype(o_ref.dtype)


def rmsnorm(x, g3, l, out_dtype, tm=512):
    m, d = x.shape
    tm = min(tm, m)
    return pl.pallas_call(
        _rmsnorm_kernel,
        out_shape=jax.ShapeDtypeStruct((m, d), out_dtype),
        grid=(m // tm,),
        in_specs=[pl.BlockSpec((tm, d), lambda i: (i, 0)),
                  pl.BlockSpec((None, 1, d), lambda i: (l, 0, 0))],
        out_specs=pl.BlockSpec((tm, d), lambda i: (i, 0)),
        compiler_params=_params(("parallel",)),
        name="rmsnorm",
    )(x, g3)


def _matmul_kernel(*refs, has_scale, has_resid):
    a_ref, w_ref = refs[0], refs[1]
    k = 2
    s_ref = r_ref = None
    if has_scale:
        s_ref = refs[k]; k += 1
    if has_resid:
        r_ref = refs[k]; k += 1
    o_ref, wbf_ref = refs[k], refs[k + 1]

    @pl.when(pl.program_id(1) == 0)
    def _():
        wbf_ref[...] = w_ref[...].astype(BF16)

    acc = jnp.dot(a_ref[...], wbf_ref[...], preferred_element_type=F32)
    if has_scale:
        acc = acc * s_ref[...]
    if has_resid:
        acc = acc + r_ref[...]
    o_ref[...] = acc.astype(o_ref.dtype)


def matmul(a, w, l, out_dtype, colscale=None, resid=None, tm=512, tn=1024):
    m, k = a.shape
    n = w.shape[-1]
    tm = min(tm, m)
    tn = min(tn, n)
    in_specs = [pl.BlockSpec((tm, k), lambda j, i: (i, 0)),
                pl.BlockSpec((None, k, tn), lambda j, i: (l, 0, j))]
    args = [a, w]
    if colscale is not None:
        in_specs.append(pl.BlockSpec((1, tn), lambda j, i: (0, j)))
        args.append(colscale)
    if resid is not None:
        in_specs.append(pl.BlockSpec((tm, tn), lambda j, i: (i, j)))
        args.append(resid)
    return pl.pallas_call(
        functools.partial(_matmul_kernel, has_scale=colscale is not None,
                          has_resid=resid is not None),
        out_shape=jax.ShapeDtypeStruct((m, n), out_dtype),
        grid=(n // tn, m // tm),
        in_specs=in_specs,
        out_specs=pl.BlockSpec((tm, tn), lambda j, i: (i, j)),
        scratch_shapes=[pltpu.VMEM((k, tn), BF16)],
        compiler_params=_params(("parallel", "arbitrary")),
        name="matmul",
    )(*args)


def _attn_a_kernel(q_ref, k_ref, v_ref, bias_ref, o_ref):
    i = pl.program_id(2)
    start = pl.multiple_of(jnp.maximum(i * A_QB - LEFT_CHUNKS * CHUNK, 0), A_QB)
    k = k_ref[pl.ds(start, A_WIN), :]
    v = v_ref[pl.ds(start, A_WIN), :]
    s = lax.dot_general(q_ref[...], k, (((1,), (1,)), ((), ())),
                        preferred_element_type=F32)
    s = s + bias_ref[...]
    m = jnp.max(s, axis=-1, keepdims=True)
    p = jnp.exp(s - m)
    denom = jnp.sum(p, axis=-1, keepdims=True)
    o = jnp.dot(p.astype(BF16), v, preferred_element_type=F32)
    o_ref[...] = (o / denom).astype(o_ref.dtype)


def attn_a(proj, bias_a, batch):
    t = proj.shape[0]
    nq = SEQ // A_QB
    n_var = bias_a.shape[1]
    return pl.pallas_call(
        _attn_a_kernel,
        out_shape=jax.ShapeDtypeStruct((t, A_WIDTH), BF16),
        grid=(batch, A_HEADS, nq),
        in_specs=[
            pl.BlockSpec((A_QB, HEAD_DIM), lambda b, h, i: (b * nq + i, h)),
            pl.BlockSpec((SEQ, HEAD_DIM), lambda b, h, i: (b, A_HEADS + h)),
            pl.BlockSpec((SEQ, HEAD_DIM), lambda b, h, i: (b, 2 * A_HEADS + h)),
            pl.BlockSpec((None, None, A_QB, A_WIN),
                         lambda b, h, i: (h, jnp.minimum(i, n_var - 1), 0, 0)),
        ],
        out_specs=pl.BlockSpec((A_QB, HEAD_DIM), lambda b, h, i: (b * nq + i, h)),
        compiler_params=_params(("parallel", "parallel", "arbitrary")),
        name="attn_a",
    )(proj, proj, proj, bias_a)


def band_bias_tables(rel_table):
    lead = LEFT_CHUNKS * CHUNK
    r = np.arange(A_QB)[:, None]
    jj = np.arange(A_WIN + lead)[None, :]
    rel = jj - lead - r
    dchunk = (jj - lead) // CHUNK - r // CHUNK
    valid = (dchunk >= -LEFT_CHUNKS) & (dchunk <= 0)
    idx = np.clip(rel, -REL_CLIP, REL_CLIP) + REL_CLIP
    wide = jnp.where(valid[:, :, None], rel_table.astype(F32)[idx], NEG_INF)
    wide = wide.transpose(2, 0, 1)
    offs = (lead, lead - A_QB, 0)
    return jnp.stack([wide[:, :, o:o + A_WIN] for o in offs], axis=1)


def _attn_b_kernel(far_ref, q_ref, k_ref, v_ref, bias_ref, lam_ref, g_ref, o_ref,
                   qq_ref, m_ref, l_ref, acc_ref, *, lam_init):
    h = pl.program_id(1)
    i = pl.program_id(2)
    qb = B_QB

    q = q_ref[...]
    lane = lax.broadcasted_iota(jnp.int32, q.shape, 1)
    zero = jnp.zeros_like(q)
    qq_ref[0:qb, :] = jnp.where(lane < B_QK_DIM, q, zero)
    qq_ref[qb:2 * qb, :] = jnp.where(lane >= B_QK_DIM, q, zero)
    m_ref[...] = jnp.full(m_ref.shape, NEG_INF, F32)
    l_ref[...] = jnp.zeros(l_ref.shape, F32)
    acc_ref[...] = jnp.zeros(acc_ref.shape, F32)

    def tile(j, bias, shift):
        off = pl.multiple_of(j * qb, qb)
        k = k_ref[pl.ds(off, qb), :]
        v = v_ref[pl.ds(off, qb), :]
        s = lax.dot_general(qq_ref[...], k, (((1,), (1,)), ((), ())),
                            preferred_element_type=F32)
        if bias is not None:
            s = (s.reshape(2, qb, qb) + bias[None]).reshape(2 * qb, qb)
        m_old = m_ref[...]
        m_new = jnp.maximum(m_old, jnp.max(s, axis=-1, keepdims=True) + shift)
        alpha = jnp.exp(m_old - m_new)
        p = jnp.exp(s - (m_new - shift))
        l_ref[...] = alpha * l_ref[...] + jnp.sum(p, axis=-1, keepdims=True)
        acc_ref[...] = alpha * acc_ref[...] + jnp.dot(
            p.astype(BF16), v, preferred_element_type=F32)
        m_ref[...] = m_new

    far = far_ref[h]

    def far_body(j, c):
        tile(j, None, far)
        return c

    lax.fori_loop(0, jnp.maximum(i - (B_NEAR - 1), 0), far_body, 0)

    for d in range(B_NEAR - 1, -1, -1):
        @pl.when(i >= d)
        def _(d=d):
            tile(i - d, bias_ref[d], 0.0)

    lv = lam_ref[...]
    lam = (jnp.exp(jnp.sum(lv[0:1] * lv[1:2], axis=-1, keepdims=True))
           - jnp.exp(jnp.sum(lv[2:3] * lv[3:4], axis=-1, keepdims=True)) + lam_init)
    o = acc_ref[...] / l_ref[...]
    od = o[0:qb] - lam * o[qb:2 * qb]
    ms = jnp.mean(od * od, axis=-1, keepdims=True)
    y = od * lax.rsqrt(ms + EPS) * g_ref[...]
    o_ref[...] = (y * (1.0 - lam_init)).astype(o_ref.dtype)


def attn_b(proj, far_bias, bias_b, lam_vecs, subln3, l, lam_init, batch):
    t = proj.shape[0]
    nq = SEQ // B_QB
    qcol = 3 * A_HEADS
    return pl.pallas_call(
        functools.partial(_attn_b_kernel, lam_init=lam_init),
        out_shape=jax.ShapeDtypeStruct((t, B_WIDTH), BF16),
        grid=(batch, B_HEADS, nq),
        in_specs=[
            pl.BlockSpec(memory_space=pltpu.SMEM),
            pl.BlockSpec((B_QB, HEAD_DIM), lambda b, h, i: (b * nq + i, qcol + h)),
            pl.BlockSpec((SEQ, HEAD_DIM), lambda b, h, i: (b, qcol + B_HEADS + h)),
            pl.BlockSpec((SEQ, HEAD_DIM), lambda b, h, i: (b, qcol + 2 * B_HEADS + h)),
            pl.BlockSpec((None, B_NEAR, B_QB, B_QB), lambda b, h, i: (h, 0, 0, 0)),
            pl.BlockSpec((None, 4, B_QK_DIM), lambda b, h, i: (l, 0, 0)),
            pl.BlockSpec((None, 1, HEAD_DIM), lambda b, h, i: (l, 0, 0)),
        ],
        out_specs=pl.BlockSpec((B_QB, HEAD_DIM), lambda b, h, i: (b * nq + i, h)),
        scratch_shapes=[pltpu.VMEM((2 * B_QB, HEAD_DIM), BF16),
                        pltpu.VMEM((2 * B_QB, 1), F32),
                        pltpu.VMEM((2 * B_QB, 1), F32),
                        pltpu.VMEM((2 * B_QB, HEAD_DIM), F32)],
        compiler_params=_params(("parallel", "parallel", "arbitrary")),
        name="attn_b",
    )(far_bias, proj, proj, proj, bias_b, lam_vecs, subln3)


def _t5_bucket(rel):
    half = T5_BUCKETS // 2
    max_exact = half // 2
    ret = jnp.where(rel > 0, half, 0)
    n = jnp.abs(rel)
    n_f = jnp.maximum(n, max_exact).astype(jnp.float32)
    large = max_exact + (jnp.log(n_f / max_exact) / math.log(T5_MAX_DIST / max_exact)
                         * (half - max_exact)).astype(jnp.int32)
    large = jnp.minimum(large, half - 1)
    return ret + jnp.where(n < max_exact, n, large)


def t5_bias_tables(t5_table):
    assert B_NEAR * B_QB - (B_QB - 1) >= T5_MAX_DIST
    lo = -(B_NEAR * B_QB)
    rel1d = jnp.arange(lo, B_QB)
    bias1d = t5_table.astype(F32)[_t5_bucket(rel1d)]
    far = t5_table.astype(F32)[_t5_bucket(jnp.array([-(SEQ - 1)]))][0]
    r = np.arange(B_QB)[:, None]
    c = np.arange(B_QB)[None, :]
    tiles = []
    for d in range(B_NEAR):
        rel = c - r - d * B_QB
        t = bias1d[rel - lo]
        if d == 0:
            mask = (c // CHUNK) <= (r // CHUNK)
            t = jnp.where(mask[:, :, None], t, NEG_INF)
        tiles.append(t.transpose(2, 0, 1))
    return far, jnp.stack(tiles, axis=1)


def _xattn_kernel(q_ref, k_ref, v_ref, o_ref):
    s = lax.dot_general(q_ref[...], k_ref[...], (((1,), (1,)), ((), ())),
                        preferred_element_type=F32)
    m = jnp.max(s, axis=-1, keepdims=True)
    p = jnp.exp(s - m)
    denom = jnp.sum(p, axis=-1, keepdims=True)
    o = jnp.dot(p.astype(BF16), v_ref[...], preferred_element_type=F32)
    o_ref[...] = (o / denom).astype(o_ref.dtype)


def xattn(q, kv, batch):
    t = q.shape[0]
    nq = SEQ // X_QB
    return pl.pallas_call(
        _xattn_kernel,
        out_shape=jax.ShapeDtypeStruct((t, D_MODEL), BF16),
        grid=(batch, X_HEADS, nq),
        in_specs=[
            pl.BlockSpec((X_QB, X_HEAD_DIM), lambda b, h, i: (b * nq + i, h)),
            pl.BlockSpec((MEM_LEN, X_HEAD_DIM), lambda b, h, i: (b, h)),
            pl.BlockSpec((MEM_LEN, X_HEAD_DIM), lambda b, h, i: (b, X_HEADS + h)),
        ],
        out_specs=pl.BlockSpec((X_QB, X_HEAD_DIM), lambda b, h, i: (b * nq + i, h)),
        compiler_params=_params(("parallel", "parallel", "parallel")),
        name="xattn",
    )(q, kv, kv)


def _router_kernel(x_ref, g_ref, wr_ref, br_ref, h_ref, comb_ref):
    x = x_ref[...]
    ms = jnp.mean(x * x, axis=-1, keepdims=True)
    hb = (x * lax.rsqrt(ms + EPS) * g_ref[...]).astype(BF16)
    h_ref[...] = hb
    logits = jnp.dot(hb, wr_ref[...].astype(BF16), preferred_element_type=F32) + br_ref[...]

    lane = lax.broadcasted_iota(jnp.int32, logits.shape, 1)
    lane_f = lane.astype(F32)
    big = jnp.float32(1e9)

    def top1(mask):
        v = jnp.where(mask, logits, -jnp.inf)
        vmax = jnp.max(v, axis=-1, keepdims=True)
        idx = jnp.min(jnp.where(mask & (v == vmax), lane_f, big), axis=-1, keepdims=True)
        return vmax, idx

    gmask = (lane >= N_EXPERTS) & (lane < N_EXPERTS + N_GROUPS)
    gmax, gidx = top1(gmask)
    gsum = jnp.sum(jnp.where(gmask, jnp.exp(logits - gmax), 0.0), axis=-1, keepdims=True)
    g_gate = 1.0 / gsum
    g_sel = gidx.astype(jnp.int32) - N_EXPERTS

    emask = (lane < N_EXPERTS) & ((lane // EXPERTS_PER_GROUP) == g_sel)
    v1, i1 = top1(emask)
    v2, i2 = top1(emask & (lane_f != i1))
    e2 = jnp.exp(v2 - v1)
    w1 = g_gate / (1.0 + e2)
    w2 = g_gate * e2 / (1.0 + e2)
    comb_ref[...] = (jnp.where(lane_f == i1, w1, 0.0) + jnp.where(lane_f == i2, w2, 0.0))


def router(x, g3, wr, br, l, tm=512):
    t, d = x.shape
    return pl.pallas_call(
        _router_kernel,
        out_shape=(jax.ShapeDtypeStruct((t, d), BF16),
                   jax.ShapeDtypeStruct((t, LANES), F32)),
        grid=(t // tm,),
        in_specs=[pl.BlockSpec((tm, d), lambda i: (i, 0)),
                  pl.BlockSpec((None, 1, d), lambda i: (l, 0, 0)),
                  pl.BlockSpec((None, d, LANES), lambda i: (l, 0, 0)),
                  pl.BlockSpec((None, 1, LANES), lambda i: (l, 0, 0))],
        out_specs=(pl.BlockSpec((tm, d), lambda i: (i, 0)),
                   pl.BlockSpec((tm, LANES), lambda i: (i, 0))),
        compiler_params=_params(("parallel",)),
        name="router",
    )(x, g3, wr, br)


def _moe_dense_kernel(h_ref, comb_ref, w1_ref, w3_ref, w2_ref, x_ref, o_ref, acc_ref):
    e = pl.program_id(1)

    @pl.when(e == 0)
    def _():
        acc_ref[...] = x_ref[...]

    hb = h_ref[...]
    a1 = jnp.dot(hb, w1_ref[...].astype(BF16), preferred_element_type=F32)
    a3 = jnp.dot(hb, w3_ref[...].astype(BF16), preferred_element_type=F32)
    comb = comb_ref[...]
    lane = lax.broadcasted_iota(jnp.int32, comb.shape, 1)
    ce = jnp.sum(jnp.where(lane == e, comb, 0.0), axis=-1, keepdims=True)
    hid = (a1 / (1.0 + jnp.exp(-a1))) * a3 * ce
    acc_ref[...] += jnp.dot(hid.astype(BF16), w2_ref[...].astype(BF16),
                            preferred_element_type=F32)

    @pl.when(e == N_EXPERTS - 1)
    def _():
        o_ref[...] = acc_ref[...]


def moe_dense(h, comb, w1, w3, w2, x, l, tm=512):
    t, d = x.shape
    return pl.pallas_call(
        _moe_dense_kernel,
        out_shape=jax.ShapeDtypeStruct((t, d), F32),
        grid=(t // tm, N_EXPERTS),
        in_specs=[pl.BlockSpec((tm, d), lambda i, e: (i, 0)),
                  pl.BlockSpec((tm, LANES), lambda i, e: (i, 0)),
                  pl.BlockSpec((None, None, d, D_EXPERT), lambda i, e: (l, e, 0, 0)),
                  pl.BlockSpec((None, None, d, D_EXPERT), lambda i, e: (l, e, 0, 0)),
                  pl.BlockSpec((None, None, D_EXPERT, d), lambda i, e: (l, e, 0, 0)),
                  pl.BlockSpec((tm, d), lambda i, e: (i, 0))],
        out_specs=pl.BlockSpec((tm, d), lambda i, e: (i, 0)),
        scratch_shapes=[pltpu.VMEM((tm, d), F32)],
        compiler_params=_params(("parallel", "arbitrary")),
        name="moe_dense",
    )(h, comb, w1, w3, w2, x)


def kernel(x, mem, norm_mix, w_in, rel_bias_a, t5_table, diff_lambda, diff_subln, w_out,
           norm_cross, norm_mem, w_xq, w_xkv, w_xo, norm_ffn, w_group, b_group,
           w_expert, b_expert, w1, w3, w2, norm_final):
    batch, seq, d = x.shape
    t = batch * seq
    xf = x.reshape(t, d)
    memf = mem.reshape(batch * MEM_LEN, d)

    g_mix = norm_mix.reshape(DEPTH, 1, d)
    g_cross = norm_cross.reshape(DEPTH, 1, d)
    g_mem = norm_mem.reshape(DEPTH, 1, d)
    g_ffn = norm_ffn.reshape(DEPTH, 1, d)
    g_final = norm_final.reshape(1, 1, d)
    subln3 = diff_subln.reshape(DEPTH, 1, HEAD_DIM)

    in_scale = jnp.concatenate([
        jnp.full((A_WIDTH,), HEAD_DIM ** -0.5, F32), jnp.ones((2 * A_WIDTH,), F32),
        jnp.full((B_WIDTH,), B_QK_DIM ** -0.5, F32), jnp.ones((2 * B_WIDTH,), F32)]
    ).reshape(1, IN_WIDTH)
    xq_scale = jnp.full((1, d), X_HEAD_DIM ** -0.5, F32)

    far_b, bias_b = t5_bias_tables(t5_table)

    pad = LANES - N_EXPERTS - N_GROUPS
    wr = jnp.concatenate([w_expert, w_group, jnp.zeros((DEPTH, d, pad), F32)], axis=-1)
    br = jnp.concatenate([b_expert, b_group, jnp.zeros((DEPTH, pad), F32)],
                         axis=-1).reshape(DEPTH, 1, LANES)

    for l in range(DEPTH):
        lam_init = 0.8 - 0.6 * math.exp(-0.3 * l)
        hn = rmsnorm(xf, g_mix, l, BF16)
        proj = matmul(hn, w_in, l, BF16, colscale=in_scale)
        oa = attn_a(proj, band_bias_tables(rel_bias_a[l]), batch)
        ob = attn_b(proj, far_b, bias_b, diff_lambda, subln3, l, lam_init, batch)
        xf = matmul(jnp.concatenate([oa, ob], axis=-1), w_out, l, F32, resid=xf)
        xn = rmsnorm(xf, g_cross, l, BF16)
        mn = rmsnorm(memf, g_mem, l, BF16)
        q = matmul(xn, w_xq, l, BF16, colscale=xq_scale)
        kv = matmul(mn, w_xkv, l, BF16)
        xo = xattn(q, kv, batch)
        xf = matmul(xo, w_xo, l, F32, resid=xf)
        hf, comb = router(xf, g_ffn, wr, br, l)
        xf = moe_dense(hf, comb, w1, w3, w2, xf, l)
    out = rmsnorm(xf, g_final, 0, F32)
    return out.reshape(batch, seq, d)
```

```python
import functools
import math

import jax
import jax.numpy as jnp
import numpy as np
from jax import lax
from jax.experimental import pallas as pl
from jax.experimental.pallas import tpu as pltpu

D_MODEL = 2048
SEQ = 4096
DEPTH = 4
CHUNK = 64
LEFT_CHUNKS = 8
HEAD_DIM = 128
A_HEADS = 8
B_HEADS = 8
B_QK_DIM = 64
A_WIDTH = 1024
B_WIDTH = 1024
IN_WIDTH = 6144
REL_CLIP = 128
T5_BUCKETS = 32
T5_MAX_DIST = 512
MEM_LEN = 256
X_HEADS = 4
X_HEAD_DIM = 512
N_GROUPS = 4
EXPERTS_PER_GROUP = 8
N_EXPERTS = 32
D_EXPERT = 256
EPS = 1e-6
NEG_INF = -1e30

LANES = 128
VMEM_LIMIT = 56 * 1024 * 1024

BF16 = jnp.bfloat16
F32 = jnp.float32

A_QB = 256
A_WIN = A_QB + LEFT_CHUNKS * CHUNK
B_QB = 256
B_NEAR = 3
B_HP = 4
LOG2E = math.log2(math.e)
X_QB = 512


def _params(sem):
    return pltpu.CompilerParams(dimension_semantics=sem, vmem_limit_bytes=VMEM_LIMIT)


def _rmsnorm_kernel(x_ref, g_ref, o_ref):
    x = x_ref[...]
    ms = jnp.mean(x * x, axis=-1, keepdims=True)
    o_ref[...] = (x * lax.rsqrt(ms + EPS) * g_ref[...]).astype(o_ref.dtype)


def rmsnorm(x, g3, l, out_dtype, tm=512):
    m, d = x.shape
    tm = min(tm, m)
    return pl.pallas_call(
        _rmsnorm_kernel,
        out_shape=jax.ShapeDtypeStruct((m, d), out_dtype),
        grid=(m // tm,),
        in_specs=[pl.BlockSpec((tm, d), lambda i: (i, 0)),
                  pl.BlockSpec((None, 1, d), lambda i: (l, 0, 0))],
        out_specs=pl.BlockSpec((tm, d), lambda i: (i, 0)),
        compiler_params=_params(("parallel",)),
        name="rmsnorm",
    )(x, g3)


def _matmul_kernel(*refs, n_parts, has_scale, has_resid):
    a_refs, w_ref = refs[:n_parts], refs[n_parts]
    k = n_parts + 1
    s_ref = r_ref = None
    if has_scale:
        s_ref = refs[k]; k += 1
    if has_resid:
        r_ref = refs[k]; k += 1
    o_ref, wbf_ref = refs[k], refs[k + 1]

    @pl.when(pl.program_id(1) == 0)
    def _():
        wbf_ref[...] = w_ref[...].astype(BF16)

    acc = None
    k0 = 0
    for a_ref in a_refs:
        kp = a_ref.shape[1]
        part = jnp.dot(a_ref[...], wbf_ref[k0:k0 + kp, :], preferred_element_type=F32)
        acc = part if acc is None else acc + part
        k0 += kp
    if has_scale:
        acc = acc * s_ref[...]
    if has_resid:
        acc = acc + r_ref[...]
    o_ref[...] = acc.astype(o_ref.dtype)


def matmul(a_parts, w, l, out_dtype, colscale=None, resid=None, tm=512, tn=1024):
    if not isinstance(a_parts, (list, tuple)):
        a_parts = [a_parts]
    m = a_parts[0].shape[0]
    k = sum(a.shape[1] for a in a_parts)
    n = w.shape[-1]
    assert w.shape[-2] == k
    tm = min(tm, m)
    tn = min(tn, n)
    in_specs = [pl.BlockSpec((tm, a.shape[1]), lambda j, i: (i, 0)) for a in a_parts]
    in_specs.append(pl.BlockSpec((None, k, tn), lambda j, i: (l, 0, j)))
    args = list(a_parts) + [w]
    if colscale is not None:
        in_specs.append(pl.BlockSpec((1, tn), lambda j, i: (0, j)))
        args.append(colscale)
    if resid is not None:
        in_specs.append(pl.BlockSpec((tm, tn), lambda j, i: (i, j)))
        args.append(resid)
    return pl.pallas_call(
        functools.partial(_matmul_kernel, n_parts=len(a_parts),
                          has_scale=colscale is not None, has_resid=resid is not None),
        out_shape=jax.ShapeDtypeStruct((m, n), out_dtype),
        grid=(n // tn, m // tm),
        in_specs=in_specs,
        out_specs=pl.BlockSpec((tm, tn), lambda j, i: (i, j)),
        scratch_shapes=[pltpu.VMEM((k, tn), BF16)],
        compiler_params=_params(("parallel", "arbitrary")),
        name="matmul",
    )(*args)


def _toeplitz(v, rows, cols):
    hh, n = v.shape
    assert n == rows + cols - 1
    vp = jnp.pad(v, ((0, 0), (0, 1)))
    skew = jnp.tile(vp, (1, rows))[:, :rows * n].reshape(hh, rows, n)
    return skew[:, :, rows - 1:]


def _attn_a_kernel(q_ref, k_ref, v_ref, bias_ref, o_ref):
    i = pl.program_id(2)
    start = pl.multiple_of(jnp.maximum(i * A_QB - LEFT_CHUNKS * CHUNK, 0), A_QB)
    k = k_ref[pl.ds(start, A_WIN), :]
    v = v_ref[pl.ds(start, A_WIN), :]
    s = lax.dot_general(q_ref[...], k, (((1,), (1,)), ((), ())),
                        preferred_element_type=F32)
    s = s + bias_ref[...]
    m = jnp.max(s, axis=-1, keepdims=True)
    p = jnp.exp(s - m)
    denom = jnp.sum(p, axis=-1, keepdims=True)
    o = jnp.dot(p.astype(BF16), v, preferred_element_type=F32)
    o_ref[...] = (o / denom).astype(o_ref.dtype)


def attn_a(proj, bias_a, l, batch):
    t = proj.shape[0]
    nq = SEQ // A_QB
    n_var = bias_a.shape[2]
    return pl.pallas_call(
        _attn_a_kernel,
        out_shape=jax.ShapeDtypeStruct((t, A_WIDTH), BF16),
        grid=(batch, A_HEADS, nq),
        in_specs=[
            pl.BlockSpec((A_QB, HEAD_DIM), lambda b, h, i: (b * nq + i, h)),
            pl.BlockSpec((SEQ, HEAD_DIM), lambda b, h, i: (b, A_HEADS + h)),
            pl.BlockSpec((SEQ, HEAD_DIM), lambda b, h, i: (b, 2 * A_HEADS + h)),
            pl.BlockSpec((None, None, None, A_QB, A_WIN),
                         lambda b, h, i: (l, h, jnp.minimum(i, n_var - 1), 0, 0)),
        ],
        out_specs=pl.BlockSpec((A_QB, HEAD_DIM), lambda b, h, i: (b * nq + i, h)),
        compiler_params=_params(("parallel", "parallel", "arbitrary")),
        name="attn_a",
    )(proj, proj, proj, bias_a)


def band_bias_tables(rel_tables):
    depth, _, heads = rel_tables.shape
    lead = LEFT_CHUNKS * CHUNK
    wide_w = A_WIN + lead
    r = np.arange(A_QB)[:, None]
    jj = np.arange(wide_w)[None, :]
    dchunk = (jj - lead) // CHUNK - r // CHUNK
    valid = (dchunk >= -LEFT_CHUNKS) & (dchunk <= 0)
    n = np.arange(A_QB + wide_w - 1)
    idx = np.clip(n - (A_QB - 1) - lead, -REL_CLIP, REL_CLIP) + REL_CLIP
    v = rel_tables.astype(F32)[:, idx, :].transpose(0, 2, 1).reshape(depth * heads, -1)
    wide = jnp.where(valid[None], _toeplitz(v, A_QB, wide_w), NEG_INF)
    offs = (lead, lead - A_QB, 0)
    out = jnp.stack([wide[:, :, o:o + A_WIN] for o in offs], axis=1)
    return out.reshape(depth, heads, len(offs), A_QB, A_WIN)


def _attn_b_kernel(far_ref, q_ref, k_ref, v_ref, bias_ref, lam_ref, g_ref, o_ref,
                   qqt_ref, vt_ref, m_ref, l_ref, acc_ref, *, lam_init):
    hp = pl.program_id(1)
    i = pl.program_id(2)
    qb = B_QB

    @pl.when(i == 0)
    def _():
        for hh in range(B_HP):
            for c in range(SEQ // qb):
                blk = v_ref[c * qb:(c + 1) * qb, hh * HEAD_DIM:(hh + 1) * HEAD_DIM]
                vt_ref[hh, c] = blk.astype(F32).T.astype(BF16)

    for hh in range(B_HP):
        qt = q_ref[:, hh * HEAD_DIM:(hh + 1) * HEAD_DIM].astype(F32).T
        row = lax.broadcasted_iota(jnp.int32, qt.shape, 0)
        qqt_ref[hh, :, 0:qb] = jnp.where(row < B_QK_DIM, qt, 0.0).astype(BF16)
        qqt_ref[hh, :, qb:2 * qb] = jnp.where(row >= B_QK_DIM, qt, 0.0).astype(BF16)
        m_ref[hh] = jnp.full((1, 2 * qb), NEG_INF, F32)
        l_ref[hh] = jnp.zeros((1, 2 * qb), F32)
        acc_ref[hh] = jnp.zeros((HEAD_DIM, 2 * qb), F32)

    def tile(j, d):
        off = pl.multiple_of(j * qb, qb)
        sts = [jnp.dot(k_ref[pl.ds(off, qb), hh * HEAD_DIM:(hh + 1) * HEAD_DIM], qqt_ref[hh],
                       preferred_element_type=F32) for hh in range(B_HP)]
        for hh in range(B_HP):
            st = sts[hh]
            if d is None:
                shift = far_ref[hp * B_HP + hh]
            else:
                b = bias_ref[hh, d]
                st = jnp.concatenate([st[:, 0:qb] + b, st[:, qb:2 * qb] + b], axis=1)
                shift = 0.0
            m_old = m_ref[hh]
            m_new = jnp.maximum(m_old, jnp.max(st, axis=0, keepdims=True) + shift)
            alpha = jnp.exp2(m_old - m_new)
            p = jnp.exp2(st - (m_new - shift))
            l_ref[hh] = alpha * l_ref[hh] + jnp.sum(p, axis=0, keepdims=True)
            acc_ref[hh] = alpha * acc_ref[hh] + jnp.dot(
                vt_ref[hh, j], p.astype(BF16), preferred_element_type=F32)
            m_ref[hh] = m_new

    def far_body(j, c):
        tile(j, None)
        return c

    lax.fori_loop(0, jnp.maximum(i - (B_NEAR - 1), 0), far_body, 0)

    for d in range(B_NEAR - 1, -1, -1):
        @pl.when(i >= d)
        def _(d=d):
            tile(i - d, d)

    lv = lam_ref[...]
    lam = (jnp.exp(jnp.sum(lv[0:1] * lv[1:2], axis=-1, keepdims=True))
           - jnp.exp(jnp.sum(lv[2:3] * lv[3:4], axis=-1, keepdims=True)) + lam_init)
    for hh in range(B_HP):
        o = acc_ref[hh] / l_ref[hh]
        od = o[:, 0:qb] - lam * o[:, qb:2 * qb]
        ms = jnp.mean(od * od, axis=0, keepdims=True)
        y = (od * lax.rsqrt(ms + EPS)).T * g_ref[...]
        o_ref[:, hh * HEAD_DIM:(hh + 1) * HEAD_DIM] = (y * (1.0 - lam_init)).astype(o_ref.dtype)


def attn_b(proj, far_bias, bias_b, lam_vecs, subln3, l, lam_init, batch):
    t = proj.shape[0]
    nq = SEQ // B_QB
    wb = B_HP * HEAD_DIM
    qcol = 3 * A_WIDTH // wb
    ncol = B_WIDTH // wb
    return pl.pallas_call(
        functools.partial(_attn_b_kernel, lam_init=lam_init),
        out_shape=jax.ShapeDtypeStruct((t, B_WIDTH), BF16),
        grid=(batch, B_HEADS // B_HP, nq),
        in_specs=[
            pl.BlockSpec(memory_space=pltpu.SMEM),
            pl.BlockSpec((B_QB, wb), lambda b, h, i: (b * nq + i, qcol + h)),
            pl.BlockSpec((SEQ, wb), lambda b, h, i: (b, qcol + ncol + h)),
            pl.BlockSpec((SEQ, wb), lambda b, h, i: (b, qcol + 2 * ncol + h)),
            pl.BlockSpec((B_HP, B_NEAR, B_QB, B_QB), lambda b, h, i: (h, 0, 0, 0)),
            pl.BlockSpec((None, 4, B_QK_DIM), lambda b, h, i: (l, 0, 0)),
            pl.BlockSpec((None, 1, HEAD_DIM), lambda b, h, i: (l, 0, 0)),
        ],
        out_specs=pl.BlockSpec((B_QB, wb), lambda b, h, i: (b * nq + i, h)),
        scratch_shapes=[pltpu.VMEM((B_HP, HEAD_DIM, 2 * B_QB), BF16),
                        pltpu.VMEM((B_HP, SEQ // B_QB, HEAD_DIM, B_QB), BF16),
                        pltpu.VMEM((B_HP, 1, 2 * B_QB), F32),
                        pltpu.VMEM((B_HP, 1, 2 * B_QB), F32),
                        pltpu.VMEM((B_HP, HEAD_DIM, 2 * B_QB), F32)],
        compiler_params=_params(("arbitrary", "arbitrary", "arbitrary")),
        name="attn_b",
    )(far_bias, proj, proj, proj, bias_b, lam_vecs, subln3)


def _t5_bucket(rel):
    half = T5_BUCKETS // 2
    max_exact = half // 2
    ret = jnp.where(rel > 0, half, 0)
    n = jnp.abs(rel)
    n_f = jnp.maximum(n, max_exact).astype(jnp.float32)
    large = max_exact + (jnp.log(n_f / max_exact) / math.log(T5_MAX_DIST / max_exact)
                         * (half - max_exact)).astype(jnp.int32)
    large = jnp.minimum(large, half - 1)
    return ret + jnp.where(n < max_exact, n, large)


def t5_bias_tables(t5_table):
    assert B_NEAR * B_QB - (B_QB - 1) >= T5_MAX_DIST
    lo = -(B_NEAR * B_QB)
    bias1d = t5_table.astype(F32)[_t5_bucket(jnp.arange(lo, B_QB))].T
    far = t5_table.astype(F32)[_t5_bucket(jnp.array([-(SEQ - 1)]))][0]
    u = np.arange(2 * B_QB - 1)
    kk = np.arange(B_QB)[:, None]
    qq = np.arange(B_QB)[None, :]
    tiles = []
    for d in range(B_NEAR):
        t = _toeplitz(bias1d[:, (B_QB - 1 - u - d * B_QB) - lo], B_QB, B_QB)
        if d == 0:
            t = jnp.where(((kk // CHUNK) <= (qq // CHUNK))[None], t * LOG2E, NEG_INF)
        else:
            t = t * LOG2E
        tiles.append(t)
    return far * LOG2E, jnp.stack(tiles, axis=1)


def _xattn_kernel(q_ref, k_ref, v_ref, o_ref):
    s = lax.dot_general(q_ref[...], k_ref[...], (((1,), (1,)), ((), ())),
                        preferred_element_type=F32)
    m = jnp.max(s, axis=-1, keepdims=True)
    p = jnp.exp(s - m)
    denom = jnp.sum(p, axis=-1, keepdims=True)
    o = jnp.dot(p.astype(BF16), v_ref[...], preferred_element_type=F32)
    o_ref[...] = (o / denom).astype(o_ref.dtype)


def xattn(q, kv, batch):
    t = q.shape[0]
    nq = SEQ // X_QB
    return pl.pallas_call(
        _xattn_kernel,
        out_shape=jax.ShapeDtypeStruct((t, D_MODEL), BF16),
        grid=(batch, X_HEADS, nq),
        in_specs=[
            pl.BlockSpec((X_QB, X_HEAD_DIM), lambda b, h, i: (b * nq + i, h)),
            pl.BlockSpec((MEM_LEN, X_HEAD_DIM), lambda b, h, i: (b, h)),
            pl.BlockSpec((MEM_LEN, X_HEAD_DIM), lambda b, h, i: (b, X_HEADS + h)),
        ],
        out_specs=pl.BlockSpec((X_QB, X_HEAD_DIM), lambda b, h, i: (b * nq + i, h)),
        compiler_params=_params(("parallel", "parallel", "parallel")),
        name="xattn",
    )(q, kv, kv)


def _router_kernel(x_ref, g_ref, wr_ref, br_ref, h_ref, comb_ref):
    x = x_ref[...]
    ms = jnp.mean(x * x, axis=-1, keepdims=True)
    hb = (x * lax.rsqrt(ms + EPS) * g_ref[...]).astype(BF16)
    h_ref[...] = hb
    logits = jnp.dot(hb, wr_ref[...].astype(BF16), preferred_element_type=F32) + br_ref[...]

    lane = lax.broadcasted_iota(jnp.int32, logits.shape, 1)
    lane_f = lane.astype(F32)
    big = jnp.float32(1e9)

    def top1(mask):
        v = jnp.where(mask, logits, -jnp.inf)
        vmax = jnp.max(v, axis=-1, keepdims=True)
        idx = jnp.min(jnp.where(mask & (v == vmax), lane_f, big), axis=-1, keepdims=True)
        return vmax, idx

    gmask = (lane >= N_EXPERTS) & (lane < N_EXPERTS + N_GROUPS)
    gmax, gidx = top1(gmask)
    gsum = jnp.sum(jnp.where(gmask, jnp.exp(logits - gmax), 0.0), axis=-1, keepdims=True)
    g_gate = 1.0 / gsum
    g_sel = gidx.astype(jnp.int32) - N_EXPERTS

    emask = (lane < N_EXPERTS) & ((lane // EXPERTS_PER_GROUP) == g_sel)
    v1, i1 = top1(emask)
    v2, i2 = top1(emask & (lane_f != i1))
    e2 = jnp.exp(v2 - v1)
    w1 = g_gate / (1.0 + e2)
    w2 = g_gate * e2 / (1.0 + e2)
    comb_ref[...] = (jnp.where(lane_f == i1, w1, 0.0) + jnp.where(lane_f == i2, w2, 0.0))


def router(x, g3, wr, br, l, tm=512):
    t, d = x.shape
    return pl.pallas_call(
        _router_kernel,
        out_shape=(jax.ShapeDtypeStruct((t, d), BF16),
                   jax.ShapeDtypeStruct((t, LANES), F32)),
        grid=(t // tm,),
        in_specs=[pl.BlockSpec((tm, d), lambda i: (i, 0)),
                  pl.BlockSpec((None, 1, d), lambda i: (l, 0, 0)),
                  pl.BlockSpec((None, d, LANES), lambda i: (l, 0, 0)),
                  pl.BlockSpec((None, 1, LANES), lambda i: (l, 0, 0))],
        out_specs=(pl.BlockSpec((tm, d), lambda i: (i, 0)),
                   pl.BlockSpec((tm, LANES), lambda i: (i, 0))),
        compiler_params=_params(("parallel",)),
        name="router",
    )(x, g3, wr, br)


def _moe_dense_kernel(h_ref, comb_ref, w1_ref, w3_ref, w2_ref, x_ref, o_ref, acc_ref):
    e = pl.program_id(1)

    @pl.when(e == 0)
    def _():
        acc_ref[...] = x_ref[...]

    hb = h_ref[...]
    a1 = jnp.dot(hb, w1_ref[...].astype(BF16), preferred_element_type=F32)
    a3 = jnp.dot(hb, w3_ref[...].astype(BF16), preferred_element_type=F32)
    comb = comb_ref[...]
    lane = lax.broadcasted_iota(jnp.int32, comb.shape, 1)
    ce = jnp.sum(jnp.where(lane == e, comb, 0.0), axis=-1, keepdims=True)
    hid = (a1 / (1.0 + jnp.exp(-a1))) * a3 * ce
    acc_ref[...] += jnp.dot(hid.astype(BF16), w2_ref[...].astype(BF16),
                            preferred_element_type=F32)

    @pl.when(e == N_EXPERTS - 1)
    def _():
        o_ref[...] = acc_ref[...]


def moe_dense(h, comb, w1, w3, w2, x, l, tm=512):
    t, d = x.shape
    return pl.pallas_call(
        _moe_dense_kernel,
        out_shape=jax.ShapeDtypeStruct((t, d), F32),
        grid=(t // tm, N_EXPERTS),
        in_specs=[pl.BlockSpec((tm, d), lambda i, e: (i, 0)),
                  pl.BlockSpec((tm, LANES), lambda i, e: (i, 0)),
                  pl.BlockSpec((None, None, d, D_EXPERT), lambda i, e: (l, e, 0, 0)),
                  pl.BlockSpec((None, None, d, D_EXPERT), lambda i, e: (l, e, 0, 0)),
                  pl.BlockSpec((None, None, D_EXPERT, d), lambda i, e: (l, e, 0, 0)),
                  pl.BlockSpec((tm, d), lambda i, e: (i, 0))],
        out_specs=pl.BlockSpec((tm, d), lambda i, e: (i, 0)),
        scratch_shapes=[pltpu.VMEM((tm, d), F32)],
        compiler_params=_params(("parallel", "arbitrary")),
        name="moe_dense",
    )(h, comb, w1, w3, w2, x)


def kernel(x, mem, norm_mix, w_in, rel_bias_a, t5_table, diff_lambda, diff_subln, w_out,
           norm_cross, norm_mem, w_xq, w_xkv, w_xo, norm_ffn, w_group, b_group,
           w_expert, b_expert, w1, w3, w2, norm_final):
    batch, seq, d = x.shape
    t = batch * seq
    xf = x.reshape(t, d)
    memf = mem.reshape(batch * MEM_LEN, d)

    g_mix = norm_mix.reshape(DEPTH, 1, d)
    g_cross = norm_cross.reshape(DEPTH, 1, d)
    g_mem = norm_mem.reshape(DEPTH, 1, d)
    g_ffn = norm_ffn.reshape(DEPTH, 1, d)
    g_final = norm_final.reshape(1, 1, d)
    subln3 = diff_subln.reshape(DEPTH, 1, HEAD_DIM)

    in_scale = jnp.concatenate([
        jnp.full((A_WIDTH,), HEAD_DIM ** -0.5, F32), jnp.ones((2 * A_WIDTH,), F32),
        jnp.full((B_WIDTH,), B_QK_DIM ** -0.5 * LOG2E, F32), jnp.ones((2 * B_WIDTH,), F32)]
    ).reshape(1, IN_WIDTH)
    xq_scale = jnp.full((1, d), X_HEAD_DIM ** -0.5, F32)

    far_b, bias_b = t5_bias_tables(t5_table)
    bias_a = band_bias_tables(rel_bias_a)

    pad = LANES - N_EXPERTS - N_GROUPS
    wr = jnp.concatenate([w_expert, w_group, jnp.zeros((DEPTH, d, pad), F32)], axis=-1)
    br = jnp.concatenate([b_expert, b_group, jnp.zeros((DEPTH, pad), F32)],
                         axis=-1).reshape(DEPTH, 1, LANES)

    for l in range(DEPTH):
        lam_init = 0.8 - 0.6 * math.exp(-0.3 * l)
        hn = rmsnorm(xf, g_mix, l, BF16)
        proj = matmul(hn, w_in, l, BF16, colscale=in_scale)
        oa = attn_a(proj, bias_a, l, batch)
        ob = attn_b(proj, far_b, bias_b, diff_lambda, subln3, l, lam_init, batch)
        xf = matmul([oa, ob], w_out, l, F32, resid=xf)
        xn = rmsnorm(xf, g_cross, l, BF16)
        mn = rmsnorm(memf, g_mem, l, BF16)
        q = matmul(xn, w_xq, l, BF16, colscale=xq_scale)
        kv = matmul(mn, w_xkv, l, BF16)
        xo = xattn(q, kv, batch)
        xf = matmul(xo, w_xo, l, F32, resid=xf)
        hf, comb = router(xf, g_ffn, wr, br, l)
        xf = moe_dense(hf, comb, w1, w3, w2, xf, l)
    out = rmsnorm(xf, g_final, 0, F32)
    return out.reshape(batch, seq, d)
```

```python
import functools
import math

import jax
import jax.numpy as jnp
import numpy as np
from jax import lax
from jax.experimental import pallas as pl
from jax.experimental.pallas import tpu as pltpu

D_MODEL = 2048
SEQ = 4096
DEPTH = 4
CHUNK = 64
LEFT_CHUNKS = 8
HEAD_DIM = 128
A_HEADS = 8
B_HEADS = 8
B_QK_DIM = 64
A_WIDTH = 1024
B_WIDTH = 1024
IN_WIDTH = 6144
REL_CLIP = 128
T5_BUCKETS = 32
T5_MAX_DIST = 512
MEM_LEN = 256
X_HEADS = 4
X_HEAD_DIM = 512
N_GROUPS = 4
EXPERTS_PER_GROUP = 8
N_EXPERTS = 32
D_EXPERT = 256
EPS = 1e-6
NEG_INF = -1e30

LANES = 128
VMEM_LIMIT = 56 * 1024 * 1024

BF16 = jnp.bfloat16
F32 = jnp.float32

A_QB = 256
A_WIN = A_QB + LEFT_CHUNKS * CHUNK
B_QB = 256
B_NEAR = 3
B_HP = 4
LOG2E = math.log2(math.e)
X_QB = 512

E_TM = 256
C_TM = 256


def _params(sem):
    return pltpu.CompilerParams(dimension_semantics=sem, vmem_limit_bytes=VMEM_LIMIT)


def _rmsnorm_kernel(x_ref, g_ref, o_ref):
    x = x_ref[...]
    ms = jnp.mean(x * x, axis=-1, keepdims=True)
    o_ref[...] = (x * lax.rsqrt(ms + EPS) * g_ref[...]).astype(o_ref.dtype)


def rmsnorm(x, g3, l, out_dtype, tm=512):
    m, d = x.shape
    tm = min(tm, m)
    return pl.pallas_call(
        _rmsnorm_kernel,
        out_shape=jax.ShapeDtypeStruct((m, d), out_dtype),
        grid=(m // tm,),
        in_specs=[pl.BlockSpec((tm, d), lambda i: (i, 0)),
                  pl.BlockSpec((None, 1, d), lambda i: (l, 0, 0))],
        out_specs=pl.BlockSpec((tm, d), lambda i: (i, 0)),
        compiler_params=_params(("parallel",)),
        name="rmsnorm",
    )(x, g3)


def _matmul_kernel(*refs, n_parts, has_scale, has_resid):
    a_refs, w_ref = refs[:n_parts], refs[n_parts]
    k = n_parts + 1
    s_ref = r_ref = None
    if has_scale:
        s_ref = refs[k]; k += 1
    if has_resid:
        r_ref = refs[k]; k += 1
    o_ref, wbf_ref = refs[k], refs[k + 1]

    @pl.when(pl.program_id(1) == 0)
    def _():
        wbf_ref[...] = w_ref[...].astype(BF16)

    acc = None
    k0 = 0
    for a_ref in a_refs:
        kp = a_ref.shape[1]
        part = jnp.dot(a_ref[...], wbf_ref[k0:k0 + kp, :], preferred_element_type=F32)
        acc = part if acc is None else acc + part
        k0 += kp
    if has_scale:
        acc = acc * s_ref[...]
    if has_resid:
        acc = acc + r_ref[...]
    o_ref[...] = acc.astype(o_ref.dtype)


def matmul(a_parts, w, l, out_dtype, colscale=None, resid=None, tm=512, tn=1024):
    if not isinstance(a_parts, (list, tuple)):
        a_parts = [a_parts]
    m = a_parts[0].shape[0]
    k = sum(a.shape[1] for a in a_parts)
    n = w.shape[-1]
    assert w.shape[-2] == k
    tm = min(tm, m)
    tn = min(tn, n)
    in_specs = [pl.BlockSpec((tm, a.shape[1]), lambda j, i: (i, 0)) for a in a_parts]
    in_specs.append(pl.BlockSpec((None, k, tn), lambda j, i: (l, 0, j)))
    args = list(a_parts) + [w]
    if colscale is not None:
        in_specs.append(pl.BlockSpec((1, tn), lambda j, i: (0, j)))
        args.append(colscale)
    if resid is not None:
        in_specs.append(pl.BlockSpec((tm, tn), lambda j, i: (i, j)))
        args.append(resid)
    return pl.pallas_call(
        functools.partial(_matmul_kernel, n_parts=len(a_parts),
                          has_scale=colscale is not None, has_resid=resid is not None),
        out_shape=jax.ShapeDtypeStruct((m, n), out_dtype),
        grid=(n // tn, m // tm),
        in_specs=in_specs,
        out_specs=pl.BlockSpec((tm, tn), lambda j, i: (i, j)),
        scratch_shapes=[pltpu.VMEM((k, tn), BF16)],
        compiler_params=_params(("parallel", "arbitrary")),
        name="matmul",
    )(*args)


def _toeplitz(v, rows, cols):
    hh, n = v.shape
    assert n == rows + cols - 1
    vp = jnp.pad(v, ((0, 0), (0, 1)))
    skew = jnp.tile(vp, (1, rows))[:, :rows * n].reshape(hh, rows, n)
    return skew[:, :, rows - 1:]


def _attn_a_kernel(q_ref, k_ref, v_ref, bias_ref, o_ref):
    i = pl.program_id(2)
    start = pl.multiple_of(jnp.maximum(i * A_QB - LEFT_CHUNKS * CHUNK, 0), A_QB)
    k = k_ref[pl.ds(start, A_WIN), :]
    v = v_ref[pl.ds(start, A_WIN), :]
    s = lax.dot_general(q_ref[...], k, (((1,), (1,)), ((), ())),
                        preferred_element_type=F32)
    s = s + bias_ref[...]
    m = jnp.max(s, axis=-1, keepdims=True)
    p = jnp.exp(s - m)
    denom = jnp.sum(p, axis=-1, keepdims=True)
    o = jnp.dot(p.astype(BF16), v, preferred_element_type=F32)
    o_ref[...] = (o / denom).astype(o_ref.dtype)


def attn_a(proj, bias_a, l, batch):
    t = proj.shape[0]
    nq = SEQ // A_QB
    n_var = bias_a.shape[2]
    return pl.pallas_call(
        _attn_a_kernel,
        out_shape=jax.ShapeDtypeStruct((t, A_WIDTH), BF16),
        grid=(batch, A_HEADS, nq),
        in_specs=[
            pl.BlockSpec((A_QB, HEAD_DIM), lambda b, h, i: (b * nq + i, h)),
            pl.BlockSpec((SEQ, HEAD_DIM), lambda b, h, i: (b, A_HEADS + h)),
            pl.BlockSpec((SEQ, HEAD_DIM), lambda b, h, i: (b, 2 * A_HEADS + h)),
            pl.BlockSpec((None, None, None, A_QB, A_WIN),
                         lambda b, h, i: (l, h, jnp.minimum(i, n_var - 1), 0, 0)),
        ],
        out_specs=pl.BlockSpec((A_QB, HEAD_DIM), lambda b, h, i: (b * nq + i, h)),
        compiler_params=_params(("parallel", "parallel", "arbitrary")),
        name="attn_a",
    )(proj, proj, proj, bias_a)


def band_bias_tables(rel_tables):
    depth, _, heads = rel_tables.shape
    lead = LEFT_CHUNKS * CHUNK
    wide_w = A_WIN + lead
    r = np.arange(A_QB)[:, None]
    jj = np.arange(wide_w)[None, :]
    dchunk = (jj - lead) // CHUNK - r // CHUNK
    valid = (dchunk >= -LEFT_CHUNKS) & (dchunk <= 0)
    n = np.arange(A_QB + wide_w - 1)
    idx = np.clip(n - (A_QB - 1) - lead, -REL_CLIP, REL_CLIP) + REL_CLIP
    v = rel_tables.astype(F32)[:, idx, :].transpose(0, 2, 1).reshape(depth * heads, -1)
    wide = jnp.where(valid[None], _toeplitz(v, A_QB, wide_w), NEG_INF)
    offs = (lead, lead - A_QB, 0)
    out = jnp.stack([wide[:, :, o:o + A_WIN] for o in offs], axis=1)
    return out.reshape(depth, heads, len(offs), A_QB, A_WIN)


def _attn_b_kernel(far_ref, q_ref, k_ref, v_ref, bias_ref, lam_ref, g_ref, o_ref,
                   qqt_ref, vt_ref, m_ref, l_ref, acc_ref, *, lam_init):
    hp = pl.program_id(1)
    i = pl.program_id(2)
    qb = B_QB

    @pl.when(i == 0)
    def _():
        for hh in range(B_HP):
            for c in range(SEQ // qb):
                blk = v_ref[c * qb:(c + 1) * qb, hh * HEAD_DIM:(hh + 1) * HEAD_DIM]
                vt_ref[hh, c] = blk.astype(F32).T.astype(BF16)

    for hh in range(B_HP):
        qt = q_ref[:, hh * HEAD_DIM:(hh + 1) * HEAD_DIM].astype(F32).T
        row = lax.broadcasted_iota(jnp.int32, qt.shape, 0)
        qqt_ref[hh, :, 0:qb] = jnp.where(row < B_QK_DIM, qt, 0.0).astype(BF16)
        qqt_ref[hh, :, qb:2 * qb] = jnp.where(row >= B_QK_DIM, qt, 0.0).astype(BF16)
        m_ref[hh] = jnp.full((1, 2 * qb), NEG_INF, F32)
        l_ref[hh] = jnp.zeros((1, 2 * qb), F32)
        acc_ref[hh] = jnp.zeros((HEAD_DIM, 2 * qb), F32)

    def tile(j, d):
        off = pl.multiple_of(j * qb, qb)
        sts = [jnp.dot(k_ref[pl.ds(off, qb), hh * HEAD_DIM:(hh + 1) * HEAD_DIM], qqt_ref[hh],
                       preferred_element_type=F32) for hh in range(B_HP)]
        for hh in range(B_HP):
            st = sts[hh]
            if d is None:
                shift = far_ref[hp * B_HP + hh]
            else:
                b = bias_ref[hh, d]
                st = jnp.concatenate([st[:, 0:qb] + b, st[:, qb:2 * qb] + b], axis=1)
                shift = 0.0
            m_old = m_ref[hh]
            m_new = jnp.maximum(m_old, jnp.max(st, axis=0, keepdims=True) + shift)
            alpha = jnp.exp2(m_old - m_new)
            p = jnp.exp2(st - (m_new - shift))
            l_ref[hh] = alpha * l_ref[hh] + jnp.sum(p, axis=0, keepdims=True)
            acc_ref[hh] = alpha * acc_ref[hh] + jnp.dot(
                vt_ref[hh, j], p.astype(BF16), preferred_element_type=F32)
            m_ref[hh] = m_new

    def far_body(j, c):
        tile(j, None)
        return c

    lax.fori_loop(0, jnp.maximum(i - (B_NEAR - 1), 0), far_body, 0)

    for d in range(B_NEAR - 1, -1, -1):
        @pl.when(i >= d)
        def _(d=d):
            tile(i - d, d)

    lv = lam_ref[...]
    lam = (jnp.exp(jnp.sum(lv[0:1] * lv[1:2], axis=-1, keepdims=True))
           - jnp.exp(jnp.sum(lv[2:3] * lv[3:4], axis=-1, keepdims=True)) + lam_init)
    for hh in range(B_HP):
        o = acc_ref[hh] / l_ref[hh]
        od = o[:, 0:qb] - lam * o[:, qb:2 * qb]
        ms = jnp.mean(od * od, axis=0, keepdims=True)
        y = (od * lax.rsqrt(ms + EPS)).T * g_ref[...]
        o_ref[:, hh * HEAD_DIM:(hh + 1) * HEAD_DIM] = (y * (1.0 - lam_init)).astype(o_ref.dtype)


def attn_b(proj, far_bias, bias_b, lam_vecs, subln3, l, lam_init, batch):
    t = proj.shape[0]
    nq = SEQ // B_QB
    wb = B_HP * HEAD_DIM
    qcol = 3 * A_WIDTH // wb
    ncol = B_WIDTH // wb
    return pl.pallas_call(
        functools.partial(_attn_b_kernel, lam_init=lam_init),
        out_shape=jax.ShapeDtypeStruct((t, B_WIDTH), BF16),
        grid=(batch, B_HEADS // B_HP, nq),
        in_specs=[
            pl.BlockSpec(memory_space=pltpu.SMEM),
            pl.BlockSpec((B_QB, wb), lambda b, h, i: (b * nq + i, qcol + h)),
            pl.BlockSpec((SEQ, wb), lambda b, h, i: (b, qcol + ncol + h)),
            pl.BlockSpec((SEQ, wb), lambda b, h, i: (b, qcol + 2 * ncol + h)),
            pl.BlockSpec((B_HP, B_NEAR, B_QB, B_QB), lambda b, h, i: (h, 0, 0, 0)),
            pl.BlockSpec((None, 4, B_QK_DIM), lambda b, h, i: (l, 0, 0)),
            pl.BlockSpec((None, 1, HEAD_DIM), lambda b, h, i: (l, 0, 0)),
        ],
        out_specs=pl.BlockSpec((B_QB, wb), lambda b, h, i: (b * nq + i, h)),
        scratch_shapes=[pltpu.VMEM((B_HP, HEAD_DIM, 2 * B_QB), BF16),
                        pltpu.VMEM((B_HP, SEQ // B_QB, HEAD_DIM, B_QB), BF16),
                        pltpu.VMEM((B_HP, 1, 2 * B_QB), F32),
                        pltpu.VMEM((B_HP, 1, 2 * B_QB), F32),
                        pltpu.VMEM((B_HP, HEAD_DIM, 2 * B_QB), F32)],
        compiler_params=_params(("arbitrary", "arbitrary", "arbitrary")),
        name="attn_b",
    )(far_bias, proj, proj, proj, bias_b, lam_vecs, subln3)


def _t5_bucket(rel):
    half = T5_BUCKETS // 2
    max_exact = half // 2
    ret = jnp.where(rel > 0, half, 0)
    n = jnp.abs(rel)
    n_f = jnp.maximum(n, max_exact).astype(jnp.float32)
    large = max_exact + (jnp.log(n_f / max_exact) / math.log(T5_MAX_DIST / max_exact)
                         * (half - max_exact)).astype(jnp.int32)
    large = jnp.minimum(large, half - 1)
    return ret + jnp.where(n < max_exact, n, large)


def t5_bias_tables(t5_table):
    assert B_NEAR * B_QB - (B_QB - 1) >= T5_MAX_DIST
    lo = -(B_NEAR * B_QB)
    bias1d = t5_table.astype(F32)[_t5_bucket(jnp.arange(lo, B_QB))].T
    far = t5_table.astype(F32)[_t5_bucket(jnp.array([-(SEQ - 1)]))][0]
    u = np.arange(2 * B_QB - 1)
    kk = np.arange(B_QB)[:, None]
    qq = np.arange(B_QB)[None, :]
    tiles = []
    for d in range(B_NEAR):
        t = _toeplitz(bias1d[:, (B_QB - 1 - u - d * B_QB) - lo], B_QB, B_QB)
        if d == 0:
            t = jnp.where(((kk // CHUNK) <= (qq // CHUNK))[None], t * LOG2E, NEG_INF)
        else:
            t = t * LOG2E
        tiles.append(t)
    return far * LOG2E, jnp.stack(tiles, axis=1)


def _xattn_kernel(q_ref, k_ref, v_ref, o_ref):
    s = lax.dot_general(q_ref[...], k_ref[...], (((1,), (1,)), ((), ())),
                        preferred_element_type=F32)
    m = jnp.max(s, axis=-1, keepdims=True)
    p = jnp.exp(s - m)
    denom = jnp.sum(p, axis=-1, keepdims=True)
    o = jnp.dot(p.astype(BF16), v_ref[...], preferred_element_type=F32)
    o_ref[...] = (o / denom).astype(o_ref.dtype)


def xattn(q, kv, batch):
    t = q.shape[0]
    nq = SEQ // X_QB
    return pl.pallas_call(
        _xattn_kernel,
        out_shape=jax.ShapeDtypeStruct((t, D_MODEL), BF16),
        grid=(batch, X_HEADS, nq),
        in_specs=[
            pl.BlockSpec((X_QB, X_HEAD_DIM), lambda b, h, i: (b * nq + i, h)),
            pl.BlockSpec((MEM_LEN, X_HEAD_DIM), lambda b, h, i: (b, h)),
            pl.BlockSpec((MEM_LEN, X_HEAD_DIM), lambda b, h, i: (b, X_HEADS + h)),
        ],
        out_specs=pl.BlockSpec((X_QB, X_HEAD_DIM), lambda b, h, i: (b * nq + i, h)),
        compiler_params=_params(("parallel", "parallel", "parallel")),
        name="xattn",
    )(q, kv, kv)


def _router_kernel(x_ref, g_ref, wr_ref, br_ref, route_ref, cnt_ref, carry_ref, tri_ref):
    tm = x_ref.shape[0]

    @pl.when(pl.program_id(0) == 0)
    def _():
        carry_ref[...] = jnp.zeros(carry_ref.shape, F32)
        r = lax.broadcasted_iota(jnp.int32, (tm, tm), 0)
        c = lax.broadcasted_iota(jnp.int32, (tm, tm), 1)
        tri_ref[...] = jnp.where(c < r, 1.0, 0.0).astype(BF16)

    x = x_ref[...]
    ms = jnp.mean(x * x, axis=-1, keepdims=True)
    hb = (x * lax.rsqrt(ms + EPS) * g_ref[...]).astype(BF16)
    logits = jnp.dot(hb, wr_ref[...].astype(BF16), preferred_element_type=F32) + br_ref[...]

    lane = lax.broadcasted_iota(jnp.int32, logits.shape, 1)
    lane_f = lane.astype(F32)
    big = jnp.float32(1e9)

    def top1(mask):
        v = jnp.where(mask, logits, -jnp.inf)
        vmax = jnp.max(v, axis=-1, keepdims=True)
        idx = jnp.min(jnp.where(mask & (v == vmax), lane_f, big), axis=-1, keepdims=True)
        return vmax, idx

    gmask = (lane >= N_EXPERTS) & (lane < N_EXPERTS + N_GROUPS)
    gmax, gidx = top1(gmask)
    gsum = jnp.sum(jnp.where(gmask, jnp.exp(logits - gmax), 0.0), axis=-1, keepdims=True)
    g_gate = 1.0 / gsum
    g_sel = gidx.astype(jnp.int32) - N_EXPERTS

    emask = (lane < N_EXPERTS) & ((lane // EXPERTS_PER_GROUP) == g_sel)
    v1, i1 = top1(emask)
    v2, i2 = top1(emask & (lane_f != i1))
    e2 = jnp.exp(v2 - v1)
    w1 = g_gate / (1.0 + e2)
    w2 = g_gate * e2 / (1.0 + e2)

    oh1 = jnp.where(lane_f == i1, 1.0, 0.0)
    oh2 = jnp.where(lane_f == i2, 1.0, 0.0)
    oh = oh1 + oh2
    before = jnp.dot(tri_ref[...], oh.astype(BF16), preferred_element_type=F32) + carry_ref[...]
    r1 = jnp.sum(oh1 * before, axis=-1, keepdims=True)
    r2 = jnp.sum(oh2 * before, axis=-1, keepdims=True)
    carry_ref[...] = carry_ref[...] + jnp.sum(oh, axis=0, keepdims=True)
    cnt_ref[...] = carry_ref[...]

    route = jnp.zeros(logits.shape, F32)
    for col, val in enumerate((i1, i2, w1, w2, r1, r2)):
        route = jnp.where(lane == col, val, route)
    route_ref[...] = route


def router(x, g3, wr, br, l, tm=512):
    t, d = x.shape
    return pl.pallas_call(
        _router_kernel,
        out_shape=(jax.ShapeDtypeStruct((t, LANES), F32),
                   jax.ShapeDtypeStruct((1, LANES), F32)),
        grid=(t // tm,),
        in_specs=[pl.BlockSpec((tm, d), lambda i: (i, 0)),
                  pl.BlockSpec((None, 1, d), lambda i: (l, 0, 0)),
                  pl.BlockSpec((None, d, LANES), lambda i: (l, 0, 0)),
                  pl.BlockSpec((None, 1, LANES), lambda i: (l, 0, 0))],
        out_specs=(pl.BlockSpec((tm, LANES), lambda i: (i, 0)),
                   pl.BlockSpec((1, LANES), lambda i: (0, 0))),
        scratch_shapes=[pltpu.VMEM((1, LANES), F32), pltpu.VMEM((tm, tm), BF16)],
        compiler_params=_params(("arbitrary",)),
        name="router",
    )(x, g3, wr, br)


def _max_tiles(t):
    return (2 * t) // E_TM + N_EXPERTS


def dispatch_plan(route, counts_row):
    t = route.shape[0]
    max_tiles = _max_tiles(t)
    ids = route[:, 0:2].astype(jnp.int32)
    ranks = route[:, 4:6].astype(jnp.int32)
    counts = counts_row[0, :N_EXPERTS].astype(jnp.int32)
    padded = ((counts + E_TM - 1) // E_TM) * E_TM
    ends = jnp.cumsum(padded)
    offs = ends - padded
    onehot = ids[:, :, None] == jnp.arange(N_EXPERTS, dtype=jnp.int32)
    dest = (jnp.sum(jnp.where(onehot, offs, 0), axis=-1) + ranks).reshape(2 * t)
    n_tiles = ends[-1] // E_TM
    tile_i = jnp.minimum(jnp.arange(max_tiles, dtype=jnp.int32), n_tiles - 1)
    tile_expert = jnp.sum(ends[None, :] <= tile_i[:, None] * E_TM, axis=1).astype(jnp.int32)
    tokens = jnp.arange(2 * t, dtype=jnp.int32) // 2
    row_token = jnp.zeros((max_tiles * E_TM,), jnp.int32).at[dest].set(
        tokens, unique_indices=True, mode="promise_in_bounds")
    return dest, row_token, tile_expert, n_tiles.reshape(1).astype(jnp.int32)


def _experts_kernel(te_ref, nt_ref, rt_ref, x_hbm, g_ref, w1_ref, w3_ref, w2_ref, ys_ref,
                    xbuf, sem, w1b, w3b, w2b):
    i = pl.program_id(0)
    nt = nt_ref[0]
    slot = i % 2

    def row_copy(tok, slot_, r):
        return pltpu.make_async_copy(x_hbm.at[pl.ds(tok, 1)], xbuf.at[slot_, pl.ds(r, 1)],
                                     sem.at[slot_])

    def start_gather(tile, slot_):
        base = tile * E_TM

        def body(r, c):
            row_copy(rt_ref[base + r], slot_, r).start()
            return c

        lax.fori_loop(0, E_TM, body, 0, unroll=8)

    @pl.when(i == 0)
    def _():
        start_gather(0, 0)

    @pl.when(i + 1 < nt)
    def _():
        start_gather(i + 1, 1 - slot)

    @pl.when(i < nt)
    def _():
        pltpu.make_async_copy(x_hbm.at[pl.ds(0, E_TM)], xbuf.at[slot], sem.at[slot]).wait()

        @pl.when((i == 0) | (te_ref[i] != te_ref[jnp.maximum(i - 1, 0)]))
        def _():
            w1b[...] = w1_ref[...].astype(BF16)
            w3b[...] = w3_ref[...].astype(BF16)
            w2b[...] = w2_ref[...].astype(BF16)

        x = xbuf[slot]
        ms = jnp.mean(x * x, axis=-1, keepdims=True)
        hb = (x * lax.rsqrt(ms + EPS) * g_ref[...]).astype(BF16)
        a1 = jnp.dot(hb, w1b[...], preferred_element_type=F32)
        a3 = jnp.dot(hb, w3b[...], preferred_element_type=F32)
        hid = (a1 / (1.0 + jnp.exp(-a1))) * a3
        ys_ref[...] = jnp.dot(hid.astype(BF16), w2b[...], preferred_element_type=F32)

    @pl.when(i >= nt)
    def _():
        ys_ref[...] = jnp.zeros(ys_ref.shape, F32)


def experts(x, g3, w1, w3, w2, l, tile_expert, n_tiles, row_token):
    t, d = x.shape
    max_tiles = _max_tiles(t)
    grid_spec = pltpu.PrefetchScalarGridSpec(
        num_scalar_prefetch=3,
        grid=(max_tiles,),
        in_specs=[pl.BlockSpec(memory_space=pl.ANY),
                  pl.BlockSpec((None, 1, d), lambda i, te, nt, rt: (l, 0, 0)),
                  pl.BlockSpec((None, None, d, D_EXPERT), lambda i, te, nt, rt: (l, te[i], 0, 0)),
                  pl.BlockSpec((None, None, d, D_EXPERT), lambda i, te, nt, rt: (l, te[i], 0, 0)),
                  pl.BlockSpec((None, None, D_EXPERT, d), lambda i, te, nt, rt: (l, te[i], 0, 0))],
        out_specs=pl.BlockSpec((E_TM, d), lambda i, te, nt, rt: (i, 0)),
        scratch_shapes=[pltpu.VMEM((2, E_TM, d), F32),
                        pltpu.SemaphoreType.DMA((2,)),
                        pltpu.VMEM((d, D_EXPERT), BF16),
                        pltpu.VMEM((d, D_EXPERT), BF16),
                        pltpu.VMEM((D_EXPERT, d), BF16)],
    )
    return pl.pallas_call(
        _experts_kernel,
        out_shape=jax.ShapeDtypeStruct((max_tiles * E_TM, d), F32),
        grid_spec=grid_spec,
        compiler_params=_params(("arbitrary",)),
        name="experts",
    )(tile_expert, n_tiles, row_token, x, g3, w1, w3, w2)


def _combine_kernel(dest_ref, ys_hbm, x_ref, route_ref, o_ref, ybuf, sem):
    i = pl.program_id(0)
    n = pl.num_programs(0)
    slot = i % 2

    def start_gather(tile, slot_):
        base = tile * C_TM

        def body(r, c):
            for k in range(2):
                row = dest_ref[2 * (base + r) + k]
                pltpu.make_async_copy(ys_hbm.at[pl.ds(row, 1)], ybuf.at[slot_, k, pl.ds(r, 1)],
                                      sem.at[slot_]).start()
            return c

        lax.fori_loop(0, C_TM, body, 0, unroll=4)

    @pl.when(i == 0)
    def _():
        start_gather(0, 0)

    @pl.when(i + 1 < n)
    def _():
        start_gather(i + 1, 1 - slot)

    for k in range(2):
        pltpu.make_async_copy(ys_hbm.at[pl.ds(0, C_TM)], ybuf.at[slot, k], sem.at[slot]).wait()

    route = route_ref[...]
    lane = lax.broadcasted_iota(jnp.int32, route.shape, 1)
    w1 = jnp.sum(jnp.where(lane == 2, route, 0.0), axis=-1, keepdims=True)
    w2 = jnp.sum(jnp.where(lane == 3, route, 0.0), axis=-1, keepdims=True)
    o_ref[...] = x_ref[...] + w1 * ybuf[slot, 0] + w2 * ybuf[slot, 1]


def combine(ys, x, route, dest):
    t, d = x.shape
    grid_spec = pltpu.PrefetchScalarGridSpec(
        num_scalar_prefetch=1,
        grid=(t // C_TM,),
        in_specs=[pl.BlockSpec(memory_space=pl.ANY),
                  pl.BlockSpec((C_TM, d), lambda i, dest: (i, 0)),
                  pl.BlockSpec((C_TM, LANES), lambda i, dest: (i, 0))],
        out_specs=pl.BlockSpec((C_TM, d), lambda i, dest: (i, 0)),
        scratch_shapes=[pltpu.VMEM((2, 2, C_TM, d), F32),
                        pltpu.SemaphoreType.DMA((2,))],
    )
    return pl.pallas_call(
        _combine_kernel,
        out_shape=jax.ShapeDtypeStruct((t, d), F32),
        grid_spec=grid_spec,
        compiler_params=_params(("arbitrary",)),
        name="combine",
    )(dest, ys, x, route)


def kernel(x, mem, norm_mix, w_in, rel_bias_a, t5_table, diff_lambda, diff_subln, w_out,
           norm_cross, norm_mem, w_xq, w_xkv, w_xo, norm_ffn, w_group, b_group,
           w_expert, b_expert, w1, w3, w2, norm_final):
    batch, seq, d = x.shape
    t = batch * seq
    xf = x.reshape(t, d)
    memf = mem.reshape(batch * MEM_LEN, d)

    g_mix = norm_mix.reshape(DEPTH, 1, d)
    g_cross = norm_cross.reshape(DEPTH, 1, d)
    g_mem = norm_mem.reshape(DEPTH, 1, d)
    g_ffn = norm_ffn.reshape(DEPTH, 1, d)
    g_final = norm_final.reshape(1, 1, d)
    subln3 = diff_subln.reshape(DEPTH, 1, HEAD_DIM)

    in_scale = jnp.concatenate([
        jnp.full((A_WIDTH,), HEAD_DIM ** -0.5, F32), jnp.ones((2 * A_WIDTH,), F32),
        jnp.full((B_WIDTH,), B_QK_DIM ** -0.5 * LOG2E, F32), jnp.ones((2 * B_WIDTH,), F32)]
    ).reshape(1, IN_WIDTH)
    xq_scale = jnp.full((1, d), X_HEAD_DIM ** -0.5, F32)

    far_b, bias_b = t5_bias_tables(t5_table)
    bias_a = band_bias_tables(rel_bias_a)

    pad = LANES - N_EXPERTS - N_GROUPS
    wr = jnp.concatenate([w_expert, w_group, jnp.zeros((DEPTH, d, pad), F32)], axis=-1)
    br = jnp.concatenate([b_expert, b_group, jnp.zeros((DEPTH, pad), F32)],
                         axis=-1).reshape(DEPTH, 1, LANES)

    for l in range(DEPTH):
        lam_init = 0.8 - 0.6 * math.exp(-0.3 * l)
        hn = rmsnorm(xf, g_mix, l, BF16)
        proj = matmul(hn, w_in, l, BF16, colscale=in_scale)
        oa = attn_a(proj, bias_a, l, batch)
        ob = attn_b(proj, far_b, bias_b, diff_lambda, subln3, l, lam_init, batch)
        xf = matmul([oa, ob], w_out, l, F32, resid=xf)
        xn = rmsnorm(xf, g_cross, l, BF16)
        mn = rmsnorm(memf, g_mem, l, BF16)
        q = matmul(xn, w_xq, l, BF16, colscale=xq_scale)
        kv = matmul(mn, w_xkv, l, BF16)
        xo = xattn(q, kv, batch)
        xf = matmul(xo, w_xo, l, F32, resid=xf)
        route, counts = router(xf, g_ffn, wr, br, l)
        dest, row_token, tile_expert, n_tiles = dispatch_plan(route, counts)
        ys = experts(xf, g_ffn, w1, w3, w2, l, tile_expert, n_tiles, row_token)
        xf = combine(ys, xf, route, dest)
    out = rmsnorm(xf, g_final, 0, F32)
    return out.reshape(batch, seq, d)
```

```python
import functools
import math

import jax
import jax.numpy as jnp
import numpy as np
from jax import lax
from jax.experimental import pallas as pl
from jax.experimental.pallas import tpu as pltpu

D_MODEL = 2048
SEQ = 4096
DEPTH = 4
CHUNK = 64
LEFT_CHUNKS = 8
HEAD_DIM = 128
A_HEADS = 8
B_HEADS = 8
B_QK_DIM = 64
A_WIDTH = 1024
B_WIDTH = 1024
IN_WIDTH = 6144
REL_CLIP = 128
T5_BUCKETS = 32
T5_MAX_DIST = 512
MEM_LEN = 256
X_HEADS = 4
X_HEAD_DIM = 512
N_GROUPS = 4
EXPERTS_PER_GROUP = 8
N_EXPERTS = 32
D_EXPERT = 256
EPS = 1e-6
NEG_INF = -1e30

LANES = 128
VMEM_LIMIT = 56 * 1024 * 1024

BF16 = jnp.bfloat16
F32 = jnp.float32

A_QB = 256
A_WIN = A_QB + LEFT_CHUNKS * CHUNK
B_QB = 256
B_NEAR = 3
B_HP = 4
LOG2E = math.log2(math.e)
X_QB = 512

E_TM = 256
C_TM = 256


def _params(sem):
    return pltpu.CompilerParams(dimension_semantics=sem, vmem_limit_bytes=VMEM_LIMIT)


def _rmsnorm_kernel(x_ref, g_ref, o_ref):
    x = x_ref[...]
    ms = jnp.mean(x * x, axis=-1, keepdims=True)
    o_ref[...] = (x * lax.rsqrt(ms + EPS) * g_ref[...]).astype(o_ref.dtype)


def rmsnorm(x, g3, l, out_dtype, tm=512):
    m, d = x.shape
    tm = min(tm, m)
    return pl.pallas_call(
        _rmsnorm_kernel,
        out_shape=jax.ShapeDtypeStruct((m, d), out_dtype),
        grid=(m // tm,),
        in_specs=[pl.BlockSpec((tm, d), lambda i: (i, 0)),
                  pl.BlockSpec((None, 1, d), lambda i: (l, 0, 0))],
        out_specs=pl.BlockSpec((tm, d), lambda i: (i, 0)),
        compiler_params=_params(("parallel",)),
        name="rmsnorm",
    )(x, g3)


def _matmul_kernel(*refs, n_parts, has_norm, has_scale, has_resid):
    a_refs, w_ref = refs[:n_parts], refs[n_parts]
    k = n_parts + 1
    g_ref = s_ref = r_ref = None
    if has_norm:
        g_ref = refs[k]; k += 1
    if has_scale:
        s_ref = refs[k]; k += 1
    if has_resid:
        r_ref = refs[k]; k += 1
    o_ref, wbf_ref = refs[k], refs[k + 1]

    @pl.when(pl.program_id(1) == 0)
    def _():
        wbf_ref[...] = w_ref[...].astype(BF16)

    acc = None
    k0 = 0
    for a_ref in a_refs:
        kp = a_ref.shape[1]
        a = a_ref[...]
        if has_norm:
            ms = jnp.mean(a * a, axis=-1, keepdims=True)
            a = (a * lax.rsqrt(ms + EPS) * g_ref[...]).astype(BF16)
        part = jnp.dot(a, wbf_ref[k0:k0 + kp, :], preferred_element_type=F32)
        acc = part if acc is None else acc + part
        k0 += kp
    if has_scale:
        acc = acc * s_ref[...]
    if has_resid:
        acc = acc + r_ref[...]
    o_ref[...] = acc.astype(o_ref.dtype)


def matmul(a_parts, w, l, out_dtype, norm_gain=None, colscale=None, resid=None,
           tm=512, tn=1024):
    if not isinstance(a_parts, (list, tuple)):
        a_parts = [a_parts]
    assert norm_gain is None or len(a_parts) == 1
    m = a_parts[0].shape[0]
    k = sum(a.shape[1] for a in a_parts)
    n = w.shape[-1]
    assert w.shape[-2] == k
    tm = min(tm, m)
    tn = min(tn, n)
    in_specs = [pl.BlockSpec((tm, a.shape[1]), lambda j, i: (i, 0)) for a in a_parts]
    in_specs.append(pl.BlockSpec((None, k, tn), lambda j, i: (l, 0, j)))
    args = list(a_parts) + [w]
    if norm_gain is not None:
        in_specs.append(pl.BlockSpec((None, 1, k), lambda j, i: (l, 0, 0)))
        args.append(norm_gain)
    if colscale is not None:
        in_specs.append(pl.BlockSpec((1, tn), lambda j, i: (0, j)))
        args.append(colscale)
    if resid is not None:
        in_specs.append(pl.BlockSpec((tm, tn), lambda j, i: (i, j)))
        args.append(resid)
    return pl.pallas_call(
        functools.partial(_matmul_kernel, n_parts=len(a_parts), has_norm=norm_gain is not None,
                          has_scale=colscale is not None, has_resid=resid is not None),
        out_shape=jax.ShapeDtypeStruct((m, n), out_dtype),
        grid=(n // tn, m // tm),
        in_specs=in_specs,
        out_specs=pl.BlockSpec((tm, tn), lambda j, i: (i, j)),
        scratch_shapes=[pltpu.VMEM((k, tn), BF16)],
        compiler_params=_params(("parallel", "arbitrary")),
        name="matmul",
    )(*args)


def _toeplitz(v, rows, cols):
    hh, n = v.shape
    assert n == rows + cols - 1
    vp = jnp.pad(v, ((0, 0), (0, 1)))
    skew = jnp.tile(vp, (1, rows))[:, :rows * n].reshape(hh, rows, n)
    return skew[:, :, rows - 1:]


def _attn_a_kernel(q_ref, k_ref, v_ref, bias_ref, o_ref):
    i = pl.program_id(2)
    start = pl.multiple_of(jnp.maximum(i * A_QB - LEFT_CHUNKS * CHUNK, 0), A_QB)
    k = k_ref[pl.ds(start, A_WIN), :]
    v = v_ref[pl.ds(start, A_WIN), :]
    s = lax.dot_general(q_ref[...], k, (((1,), (1,)), ((), ())),
                        preferred_element_type=F32)
    s = s + bias_ref[...]
    m = jnp.max(s, axis=-1, keepdims=True)
    p = jnp.exp(s - m)
    denom = jnp.sum(p, axis=-1, keepdims=True)
    o = jnp.dot(p.astype(BF16), v, preferred_element_type=F32)
    o_ref[...] = (o / denom).astype(o_ref.dtype)


def attn_a(proj, bias_a, l, batch):
    t = proj.shape[0]
    nq = SEQ // A_QB
    n_var = bias_a.shape[2]
    return pl.pallas_call(
        _attn_a_kernel,
        out_shape=jax.ShapeDtypeStruct((t, A_WIDTH), BF16),
        grid=(batch, A_HEADS, nq),
        in_specs=[
            pl.BlockSpec((A_QB, HEAD_DIM), lambda b, h, i: (b * nq + i, h)),
            pl.BlockSpec((SEQ, HEAD_DIM), lambda b, h, i: (b, A_HEADS + h)),
            pl.BlockSpec((SEQ, HEAD_DIM), lambda b, h, i: (b, 2 * A_HEADS + h)),
            pl.BlockSpec((None, None, None, A_QB, A_WIN),
                         lambda b, h, i: (l, h, jnp.minimum(i, n_var - 1), 0, 0)),
        ],
        out_specs=pl.BlockSpec((A_QB, HEAD_DIM), lambda b, h, i: (b * nq + i, h)),
        compiler_params=_params(("parallel", "parallel", "arbitrary")),
        name="attn_a",
    )(proj, proj, proj, bias_a)


def band_bias_tables(rel_tables):
    depth, _, heads = rel_tables.shape
    lead = LEFT_CHUNKS * CHUNK
    wide_w = A_WIN + lead
    r = np.arange(A_QB)[:, None]
    jj = np.arange(wide_w)[None, :]
    dchunk = (jj - lead) // CHUNK - r // CHUNK
    valid = (dchunk >= -LEFT_CHUNKS) & (dchunk <= 0)
    n = np.arange(A_QB + wide_w - 1)
    idx = np.clip(n - (A_QB - 1) - lead, -REL_CLIP, REL_CLIP) + REL_CLIP
    v = rel_tables.astype(F32)[:, idx, :].transpose(0, 2, 1).reshape(depth * heads, -1)
    wide = jnp.where(valid[None], _toeplitz(v, A_QB, wide_w), NEG_INF)
    offs = (lead, lead - A_QB, 0)
    out = jnp.stack([wide[:, :, o:o + A_WIN] for o in offs], axis=1)
    return out.reshape(depth, heads, len(offs), A_QB, A_WIN)


def _attn_b_kernel(far_ref, q_ref, k_ref, v_ref, bias_ref, lam_ref, g_ref, o_ref,
                   qqt_ref, vt_ref, m_ref, l_ref, acc_ref, *, lam_init):
    hp = pl.program_id(1)
    i = pl.program_id(2)
    qb = B_QB

    @pl.when(i == 0)
    def _():
        for hh in range(B_HP):
            for c in range(SEQ // qb):
                blk = v_ref[c * qb:(c + 1) * qb, hh * HEAD_DIM:(hh + 1) * HEAD_DIM]
                vt_ref[hh, c] = blk.astype(F32).T.astype(BF16)

    for hh in range(B_HP):
        qt = q_ref[:, hh * HEAD_DIM:(hh + 1) * HEAD_DIM].astype(F32).T
        row = lax.broadcasted_iota(jnp.int32, qt.shape, 0)
        qqt_ref[hh, :, 0:qb] = jnp.where(row < B_QK_DIM, qt, 0.0).astype(BF16)
        qqt_ref[hh, :, qb:2 * qb] = jnp.where(row >= B_QK_DIM, qt, 0.0).astype(BF16)
        m_ref[hh] = jnp.full((1, 2 * qb), NEG_INF, F32)
        l_ref[hh] = jnp.zeros((1, 2 * qb), F32)
        acc_ref[hh] = jnp.zeros((HEAD_DIM, 2 * qb), F32)

    def tile(j, d):
        off = pl.multiple_of(j * qb, qb)
        sts = [jnp.dot(k_ref[pl.ds(off, qb), hh * HEAD_DIM:(hh + 1) * HEAD_DIM], qqt_ref[hh],
                       preferred_element_type=F32) for hh in range(B_HP)]
        for hh in range(B_HP):
            st = sts[hh]
            if d is None:
                shift = far_ref[hp * B_HP + hh]
            else:
                b = bias_ref[hh, d]
                st = jnp.concatenate([st[:, 0:qb] + b, st[:, qb:2 * qb] + b], axis=1)
                shift = 0.0
            m_old = m_ref[hh]
            m_new = jnp.maximum(m_old, jnp.max(st, axis=0, keepdims=True) + shift)
            alpha = jnp.exp2(m_old - m_new)
            p = jnp.exp2(st - (m_new - shift))
            l_ref[hh] = alpha * l_ref[hh] + jnp.sum(p, axis=0, keepdims=True)
            acc_ref[hh] = alpha * acc_ref[hh] + jnp.dot(
                vt_ref[hh, j], p.astype(BF16), preferred_element_type=F32)
            m_ref[hh] = m_new

    def far_body(j, c):
        tile(j, None)
        return c

    lax.fori_loop(0, jnp.maximum(i - (B_NEAR - 1), 0), far_body, 0)

    for d in range(B_NEAR - 1, -1, -1):
        @pl.when(i >= d)
        def _(d=d):
            tile(i - d, d)

    lv = lam_ref[...]
    lam = (jnp.exp(jnp.sum(lv[0:1] * lv[1:2], axis=-1, keepdims=True))
           - jnp.exp(jnp.sum(lv[2:3] * lv[3:4], axis=-1, keepdims=True)) + lam_init)
    for hh in range(B_HP):
        o = acc_ref[hh] / l_ref[hh]
        od = o[:, 0:qb] - lam * o[:, qb:2 * qb]
        ms = jnp.mean(od * od, axis=0, keepdims=True)
        y = (od * lax.rsqrt(ms + EPS)).T * g_ref[...]
        o_ref[:, hh * HEAD_DIM:(hh + 1) * HEAD_DIM] = (y * (1.0 - lam_init)).astype(o_ref.dtype)


def attn_b(proj, far_bias, bias_b, lam_vecs, subln3, l, lam_init, batch):
    t = proj.shape[0]
    nq = SEQ // B_QB
    wb = B_HP * HEAD_DIM
    qcol = 3 * A_WIDTH // wb
    ncol = B_WIDTH // wb
    return pl.pallas_call(
        functools.partial(_attn_b_kernel, lam_init=lam_init),
        out_shape=jax.ShapeDtypeStruct((t, B_WIDTH), BF16),
        grid=(batch, B_HEADS // B_HP, nq),
        in_specs=[
            pl.BlockSpec(memory_space=pltpu.SMEM),
            pl.BlockSpec((B_QB, wb), lambda b, h, i: (b * nq + i, qcol + h)),
            pl.BlockSpec((SEQ, wb), lambda b, h, i: (b, qcol + ncol + h)),
            pl.BlockSpec((SEQ, wb), lambda b, h, i: (b, qcol + 2 * ncol + h)),
            pl.BlockSpec((B_HP, B_NEAR, B_QB, B_QB), lambda b, h, i: (h, 0, 0, 0)),
            pl.BlockSpec((None, 4, B_QK_DIM), lambda b, h, i: (l, 0, 0)),
            pl.BlockSpec((None, 1, HEAD_DIM), lambda b, h, i: (l, 0, 0)),
        ],
        out_specs=pl.BlockSpec((B_QB, wb), lambda b, h, i: (b * nq + i, h)),
        scratch_shapes=[pltpu.VMEM((B_HP, HEAD_DIM, 2 * B_QB), BF16),
                        pltpu.VMEM((B_HP, SEQ // B_QB, HEAD_DIM, B_QB), BF16),
                        pltpu.VMEM((B_HP, 1, 2 * B_QB), F32),
                        pltpu.VMEM((B_HP, 1, 2 * B_QB), F32),
                        pltpu.VMEM((B_HP, HEAD_DIM, 2 * B_QB), F32)],
        compiler_params=_params(("arbitrary", "arbitrary", "arbitrary")),
        name="attn_b",
    )(far_bias, proj, proj, proj, bias_b, lam_vecs, subln3)


def _t5_bucket(rel):
    half = T5_BUCKETS // 2
    max_exact = half // 2
    ret = jnp.where(rel > 0, half, 0)
    n = jnp.abs(rel)
    n_f = jnp.maximum(n, max_exact).astype(jnp.float32)
    large = max_exact + (jnp.log(n_f / max_exact) / math.log(T5_MAX_DIST / max_exact)
                         * (half - max_exact)).astype(jnp.int32)
    large = jnp.minimum(large, half - 1)
    return ret + jnp.where(n < max_exact, n, large)


def t5_bias_tables(t5_table):
    assert B_NEAR * B_QB - (B_QB - 1) >= T5_MAX_DIST
    lo = -(B_NEAR * B_QB)
    bias1d = t5_table.astype(F32)[_t5_bucket(jnp.arange(lo, B_QB))].T
    far = t5_table.astype(F32)[_t5_bucket(jnp.array([-(SEQ - 1)]))][0]
    u = np.arange(2 * B_QB - 1)
    kk = np.arange(B_QB)[:, None]
    qq = np.arange(B_QB)[None, :]
    tiles = []
    for d in range(B_NEAR):
        t = _toeplitz(bias1d[:, (B_QB - 1 - u - d * B_QB) - lo], B_QB, B_QB)
        if d == 0:
            t = jnp.where(((kk // CHUNK) <= (qq // CHUNK))[None], t * LOG2E, NEG_INF)
        else:
            t = t * LOG2E
        tiles.append(t)
    return far * LOG2E, jnp.stack(tiles, axis=1)


def _xattn_kernel(q_ref, k_ref, v_ref, o_ref):
    s = lax.dot_general(q_ref[...], k_ref[...], (((1,), (1,)), ((), ())),
                        preferred_element_type=F32)
    m = jnp.max(s, axis=-1, keepdims=True)
    p = jnp.exp(s - m)
    denom = jnp.sum(p, axis=-1, keepdims=True)
    o = jnp.dot(p.astype(BF16), v_ref[...], preferred_element_type=F32)
    o_ref[...] = (o / denom).astype(o_ref.dtype)


def xattn(q, kv, batch):
    t = q.shape[0]
    nq = SEQ // X_QB
    return pl.pallas_call(
        _xattn_kernel,
        out_shape=jax.ShapeDtypeStruct((t, D_MODEL), BF16),
        grid=(batch, X_HEADS, nq),
        in_specs=[
            pl.BlockSpec((X_QB, X_HEAD_DIM), lambda b, h, i: (b * nq + i, h)),
            pl.BlockSpec((MEM_LEN, X_HEAD_DIM), lambda b, h, i: (b, h)),
            pl.BlockSpec((MEM_LEN, X_HEAD_DIM), lambda b, h, i: (b, X_HEADS + h)),
        ],
        out_specs=pl.BlockSpec((X_QB, X_HEAD_DIM), lambda b, h, i: (b * nq + i, h)),
        compiler_params=_params(("parallel", "parallel", "parallel")),
        name="xattn",
    )(q, kv, kv)


def _router_kernel(x_ref, g_ref, wr_ref, br_ref, route_ref, cnt_ref, carry_ref, tri_ref):
    tm = x_ref.shape[0]

    @pl.when(pl.program_id(0) == 0)
    def _():
        carry_ref[...] = jnp.zeros(carry_ref.shape, F32)
        r = lax.broadcasted_iota(jnp.int32, (tm, tm), 0)
        c = lax.broadcasted_iota(jnp.int32, (tm, tm), 1)
        tri_ref[...] = jnp.where(c < r, 1.0, 0.0).astype(BF16)

    x = x_ref[...]
    ms = jnp.mean(x * x, axis=-1, keepdims=True)
    hb = (x * lax.rsqrt(ms + EPS) * g_ref[...]).astype(BF16)
    logits = jnp.dot(hb, wr_ref[...].astype(BF16), preferred_element_type=F32) + br_ref[...]

    lane = lax.broadcasted_iota(jnp.int32, logits.shape, 1)
    lane_f = lane.astype(F32)
    big = jnp.float32(1e9)

    def top1(mask):
        v = jnp.where(mask, logits, -jnp.inf)
        vmax = jnp.max(v, axis=-1, keepdims=True)
        idx = jnp.min(jnp.where(mask & (v == vmax), lane_f, big), axis=-1, keepdims=True)
        return vmax, idx

    gmask = (lane >= N_EXPERTS) & (lane < N_EXPERTS + N_GROUPS)
    gmax, gidx = top1(gmask)
    gsum = jnp.sum(jnp.where(gmask, jnp.exp(logits - gmax), 0.0), axis=-1, keepdims=True)
    g_gate = 1.0 / gsum
    g_sel = gidx.astype(jnp.int32) - N_EXPERTS

    emask = (lane < N_EXPERTS) & ((lane // EXPERTS_PER_GROUP) == g_sel)
    v1, i1 = top1(emask)
    v2, i2 = top1(emask & (lane_f != i1))
    e2 = jnp.exp(v2 - v1)
    w1 = g_gate / (1.0 + e2)
    w2 = g_gate * e2 / (1.0 + e2)

    oh1 = jnp.where(lane_f == i1, 1.0, 0.0)
    oh2 = jnp.where(lane_f == i2, 1.0, 0.0)
    oh = oh1 + oh2
    before = jnp.dot(tri_ref[...], oh.astype(BF16), preferred_element_type=F32) + carry_ref[...]
    r1 = jnp.sum(oh1 * before, axis=-1, keepdims=True)
    r2 = jnp.sum(oh2 * before, axis=-1, keepdims=True)
    carry_ref[...] = carry_ref[...] + jnp.sum(oh, axis=0, keepdims=True)
    cnt_ref[...] = carry_ref[...]

    route = jnp.zeros(logits.shape, F32)
    for col, val in enumerate((i1, i2, w1, w2, r1, r2)):
        route = jnp.where(lane == col, val, route)
    route_ref[...] = route


def router(x, g3, wr, br, l, tm=512):
    t, d = x.shape
    return pl.pallas_call(
        _router_kernel,
        out_shape=(jax.ShapeDtypeStruct((t, LANES), F32),
                   jax.ShapeDtypeStruct((1, LANES), F32)),
        grid=(t // tm,),
        in_specs=[pl.BlockSpec((tm, d), lambda i: (i, 0)),
                  pl.BlockSpec((None, 1, d), lambda i: (l, 0, 0)),
                  pl.BlockSpec((None, d, LANES), lambda i: (l, 0, 0)),
                  pl.BlockSpec((None, 1, LANES), lambda i: (l, 0, 0))],
        out_specs=(pl.BlockSpec((tm, LANES), lambda i: (i, 0)),
                   pl.BlockSpec((1, LANES), lambda i: (0, 0))),
        scratch_shapes=[pltpu.VMEM((1, LANES), F32), pltpu.VMEM((tm, tm), BF16)],
        compiler_params=_params(("arbitrary",)),
        name="router",
    )(x, g3, wr, br)


def _max_tiles(t):
    return (2 * t) // E_TM + N_EXPERTS


def dispatch_plan(route, counts_row):
    t = route.shape[0]
    max_tiles = _max_tiles(t)
    ids = route[:, 0:2].astype(jnp.int32)
    ranks = route[:, 4:6].astype(jnp.int32)
    counts = counts_row[0, :N_EXPERTS].astype(jnp.int32)
    padded = ((counts + E_TM - 1) // E_TM) * E_TM
    ends = jnp.cumsum(padded)
    offs = ends - padded
    onehot = ids[:, :, None] == jnp.arange(N_EXPERTS, dtype=jnp.int32)
    dest = (jnp.sum(jnp.where(onehot, offs, 0), axis=-1) + ranks).reshape(2 * t)
    n_tiles = ends[-1] // E_TM
    tile_i = jnp.minimum(jnp.arange(max_tiles, dtype=jnp.int32), n_tiles - 1)
    tile_expert = jnp.sum(ends[None, :] <= tile_i[:, None] * E_TM, axis=1).astype(jnp.int32)
    tokens = jnp.arange(2 * t, dtype=jnp.int32) // 2
    row_token = jnp.zeros((max_tiles * E_TM,), jnp.int32).at[dest].set(
        tokens, unique_indices=True, mode="promise_in_bounds")
    return dest, row_token, tile_expert, n_tiles.reshape(1).astype(jnp.int32)


def _experts_kernel(te_ref, nt_ref, rt_ref, x_hbm, g_ref, w1_ref, w3_ref, w2_ref, ys_ref,
                    xbuf0, xbuf1, sem, w1b, w3b, w2b, *, max_tiles):
    i = pl.program_id(0)
    nt = nt_ref[0]
    bufs = (xbuf0, xbuf1)

    def row_copy(tok, s, r):
        return pltpu.make_async_copy(x_hbm.at[pl.ds(tok, 1)], bufs[s].at[pl.ds(r, 1)], sem.at[s])

    def tile_wait(s):
        pltpu.make_async_copy(x_hbm.at[pl.ds(0, E_TM)], bufs[s], sem.at[s]).wait()

    @pl.when(i == 0)
    def _():
        def body(r, c):
            row_copy(rt_ref[r], 0, r).start()
            return c

        lax.fori_loop(0, E_TM, body, 0, unroll=8)

    def step(s):
        @pl.when((i == 0) | (te_ref[i] != te_ref[jnp.maximum(i - 1, 0)]))
        def _():
            w1b[...] = w1_ref[...].astype(BF16)
            w3b[...] = w3_ref[...].astype(BF16)
            w2b[...] = w2_ref[...].astype(BF16)

        tile_wait(s)

        base = jnp.minimum(i + 1, max_tiles - 1) * E_TM
        for r in range(E_TM):
            row_copy(rt_ref[base + r], 1 - s, r).start()

        x = bufs[s][...]
        ms = jnp.mean(x * x, axis=-1, keepdims=True)
        hb = (x * lax.rsqrt(ms + EPS) * g_ref[...]).astype(BF16)
        a1 = jnp.dot(hb, w1b[...], preferred_element_type=F32)
        a3 = jnp.dot(hb, w3b[...], preferred_element_type=F32)
        hid = (a1 / (1.0 + jnp.exp(-a1))) * a3
        ys_ref[...] = jnp.dot(hid.astype(BF16), w2b[...], preferred_element_type=F32)

        @pl.when(i == nt - 1)
        def _():
            tile_wait(1 - s)

    for s in range(2):
        @pl.when((i < nt) & (i % 2 == s))
        def _(s=s):
            step(s)

    @pl.when(i >= nt)
    def _():
        ys_ref[...] = jnp.zeros(ys_ref.shape, F32)


def experts(x, g3, w1, w3, w2, l, tile_expert, n_tiles, row_token):
    t, d = x.shape
    max_tiles = _max_tiles(t)
    grid_spec = pltpu.PrefetchScalarGridSpec(
        num_scalar_prefetch=3,
        grid=(max_tiles,),
        in_specs=[pl.BlockSpec(memory_space=pl.ANY),
                  pl.BlockSpec((None, 1, d), lambda i, te, nt, rt: (l, 0, 0)),
                  pl.BlockSpec((None, None, d, D_EXPERT), lambda i, te, nt, rt: (l, te[i], 0, 0)),
                  pl.BlockSpec((None, None, d, D_EXPERT), lambda i, te, nt, rt: (l, te[i], 0, 0)),
                  pl.BlockSpec((None, None, D_EXPERT, d), lambda i, te, nt, rt: (l, te[i], 0, 0))],
        out_specs=pl.BlockSpec((E_TM, d), lambda i, te, nt, rt: (i, 0)),
        scratch_shapes=[pltpu.VMEM((E_TM, d), F32),
                        pltpu.VMEM((E_TM, d), F32),
                        pltpu.SemaphoreType.DMA((2,)),
                        pltpu.VMEM((d, D_EXPERT), BF16),
                        pltpu.VMEM((d, D_EXPERT), BF16),
                        pltpu.VMEM((D_EXPERT, d), BF16)],
    )
    return pl.pallas_call(
        functools.partial(_experts_kernel, max_tiles=max_tiles),
        out_shape=jax.ShapeDtypeStruct((max_tiles * E_TM, d), F32),
        grid_spec=grid_spec,
        compiler_params=_params(("arbitrary",)),
        name="experts",
    )(tile_expert, n_tiles, row_token, x, g3, w1, w3, w2)


def _combine_kernel(dest_ref, ys_hbm, x_ref, route_ref, o_ref, ybuf0, ybuf1, sem):
    i = pl.program_id(0)
    n = pl.num_programs(0)
    bufs = (ybuf0, ybuf1)

    def row_copy(row, s, k, r):
        return pltpu.make_async_copy(ys_hbm.at[pl.ds(row, 1)], bufs[s].at[k, pl.ds(r, 1)],
                                     sem.at[s])

    def tile_wait(s):
        for k in range(2):
            pltpu.make_async_copy(ys_hbm.at[pl.ds(0, C_TM)], bufs[s].at[k], sem.at[s]).wait()

    @pl.when(i == 0)
    def _():
        def body(r, c):
            for k in range(2):
                row_copy(dest_ref[2 * r + k], 0, k, r).start()
            return c

        lax.fori_loop(0, C_TM, body, 0, unroll=4)

    def step(s):
        tile_wait(s)
        base = 2 * jnp.minimum(i + 1, n - 1) * C_TM
        for r in range(C_TM):
            for k in range(2):
                row_copy(dest_ref[base + 2 * r + k], 1 - s, k, r).start()

        route = route_ref[...]
        lane = lax.broadcasted_iota(jnp.int32, route.shape, 1)
        w1 = jnp.sum(jnp.where(lane == 2, route, 0.0), axis=-1, keepdims=True)
        w2 = jnp.sum(jnp.where(lane == 3, route, 0.0), axis=-1, keepdims=True)
        o_ref[...] = x_ref[...] + w1 * bufs[s][0] + w2 * bufs[s][1]

        @pl.when(i == n - 1)
        def _():
            tile_wait(1 - s)

    for s in range(2):
        @pl.when(i % 2 == s)
        def _(s=s):
            step(s)


def combine(ys, x, route, dest):
    t, d = x.shape
    grid_spec = pltpu.PrefetchScalarGridSpec(
        num_scalar_prefetch=1,
        grid=(t // C_TM,),
        in_specs=[pl.BlockSpec(memory_space=pl.ANY),
                  pl.BlockSpec((C_TM, d), lambda i, dest: (i, 0)),
                  pl.BlockSpec((C_TM, LANES), lambda i, dest: (i, 0))],
        out_specs=pl.BlockSpec((C_TM, d), lambda i, dest: (i, 0)),
        scratch_shapes=[pltpu.VMEM((2, C_TM, d), F32),
                        pltpu.VMEM((2, C_TM, d), F32),
                        pltpu.SemaphoreType.DMA((2,))],
    )
    return pl.pallas_call(
        _combine_kernel,
        out_shape=jax.ShapeDtypeStruct((t, d), F32),
        grid_spec=grid_spec,
        compiler_params=_params(("arbitrary",)),
        name="combine",
    )(dest, ys, x, route)


def kernel(x, mem, norm_mix, w_in, rel_bias_a, t5_table, diff_lambda, diff_subln, w_out,
           norm_cross, norm_mem, w_xq, w_xkv, w_xo, norm_ffn, w_group, b_group,
           w_expert, b_expert, w1, w3, w2, norm_final):
    batch, seq, d = x.shape
    t = batch * seq
    xf = x.reshape(t, d)
    memf = mem.reshape(batch * MEM_LEN, d)

    g_mix = norm_mix.reshape(DEPTH, 1, d)
    g_cross = norm_cross.reshape(DEPTH, 1, d)
    g_mem = norm_mem.reshape(DEPTH, 1, d)
    g_ffn = norm_ffn.reshape(DEPTH, 1, d)
    g_final = norm_final.reshape(1, 1, d)
    subln3 = diff_subln.reshape(DEPTH, 1, HEAD_DIM)

    in_scale = jnp.concatenate([
        jnp.full((A_WIDTH,), HEAD_DIM ** -0.5, F32), jnp.ones((2 * A_WIDTH,), F32),
        jnp.full((B_WIDTH,), B_QK_DIM ** -0.5 * LOG2E, F32), jnp.ones((2 * B_WIDTH,), F32)]
    ).reshape(1, IN_WIDTH)
    xq_scale = jnp.full((1, d), X_HEAD_DIM ** -0.5, F32)

    far_b, bias_b = t5_bias_tables(t5_table)
    bias_a = band_bias_tables(rel_bias_a)

    pad = LANES - N_EXPERTS - N_GROUPS
    wr = jnp.concatenate([w_expert, w_group, jnp.zeros((DEPTH, d, pad), F32)], axis=-1)
    br = jnp.concatenate([b_expert, b_group, jnp.zeros((DEPTH, pad), F32)],
                         axis=-1).reshape(DEPTH, 1, LANES)

    for l in range(DEPTH):
        lam_init = 0.8 - 0.6 * math.exp(-0.3 * l)
        proj = matmul(xf, w_in, l, BF16, norm_gain=g_mix, colscale=in_scale)
        oa = attn_a(proj, bias_a, l, batch)
        ob = attn_b(proj, far_b, bias_b, diff_lambda, subln3, l, lam_init, batch)
        xf = matmul([oa, ob], w_out, l, F32, resid=xf)
        q = matmul(xf, w_xq, l, BF16, norm_gain=g_cross, colscale=xq_scale, tn=2048)
        kv = matmul(memf, w_xkv, l, BF16, norm_gain=g_mem)
        xo = xattn(q, kv, batch)
        xf = matmul(xo, w_xo, l, F32, resid=xf)
        route, counts = router(xf, g_ffn, wr, br, l)
        dest, row_token, tile_expert, n_tiles = dispatch_plan(route, counts)
        ys = experts(xf, g_ffn, w1, w3, w2, l, tile_expert, n_tiles, row_token)
        xf = combine(ys, xf, route, dest)
    out = rmsnorm(xf, g_final, 0, F32)
    return out.reshape(batch, seq, d)
```

```python
import functools
import math

import jax
import jax.numpy as jnp
import numpy as np
from jax import lax
from jax.experimental import pallas as pl
from jax.experimental.pallas import tpu as pltpu

D_MODEL = 2048
SEQ = 4096
DEPTH = 4
CHUNK = 64
LEFT_CHUNKS = 8
HEAD_DIM = 128
A_HEADS = 8
B_HEADS = 8
B_QK_DIM = 64
A_WIDTH = 1024
B_WIDTH = 1024
IN_WIDTH = 6144
REL_CLIP = 128
T5_BUCKETS = 32
T5_MAX_DIST = 512
MEM_LEN = 256
X_HEADS = 4
X_HEAD_DIM = 512
N_GROUPS = 4
EXPERTS_PER_GROUP = 8
N_EXPERTS = 32
D_EXPERT = 256
EPS = 1e-6
NEG_INF = -1e30

LANES = 128
VMEM_LIMIT = 56 * 1024 * 1024

BF16 = jnp.bfloat16
F32 = jnp.float32

A_QB = 256
A_WIN = A_QB + LEFT_CHUNKS * CHUNK
B_QB = 256
B_NEAR = 3
B_HP = 4
LOG2E = math.log2(math.e)
X_QB = 512

E_TM = 256
C_TM = 256
PACK_SUB = 8


def _params(sem):
    return pltpu.CompilerParams(dimension_semantics=sem, vmem_limit_bytes=VMEM_LIMIT)


def _rmsnorm_kernel(x_ref, g_ref, o_ref):
    x = x_ref[...]
    ms = jnp.mean(x * x, axis=-1, keepdims=True)
    o_ref[...] = (x * lax.rsqrt(ms + EPS) * g_ref[...]).astype(o_ref.dtype)


def rmsnorm(x, g3, l, out_dtype, tm=512):
    m, d = x.shape
    tm = min(tm, m)
    return pl.pallas_call(
        _rmsnorm_kernel,
        out_shape=jax.ShapeDtypeStruct((m, d), out_dtype),
        grid=(m // tm,),
        in_specs=[pl.BlockSpec((tm, d), lambda i: (i, 0)),
                  pl.BlockSpec((None, 1, d), lambda i: (l, 0, 0))],
        out_specs=pl.BlockSpec((tm, d), lambda i: (i, 0)),
        compiler_params=_params(("parallel",)),
        name="rmsnorm",
    )(x, g3)


def _matmul_kernel(*refs, n_parts, has_norm, has_scale, has_resid):
    a_refs, w_ref = refs[:n_parts], refs[n_parts]
    k = n_parts + 1
    g_ref = s_ref = r_ref = None
    if has_norm:
        g_ref = refs[k]; k += 1
    if has_scale:
        s_ref = refs[k]; k += 1
    if has_resid:
        r_ref = refs[k]; k += 1
    o_ref, wbf_ref = refs[k], refs[k + 1]

    @pl.when(pl.program_id(1) == 0)
    def _():
        wbf_ref[...] = w_ref[...].astype(BF16)

    acc = None
    k0 = 0
    for a_ref in a_refs:
        kp = a_ref.shape[1]
        a = a_ref[...]
        if has_norm:
            ms = jnp.mean(a * a, axis=-1, keepdims=True)
            a = (a * lax.rsqrt(ms + EPS) * g_ref[...]).astype(BF16)
        part = jnp.dot(a, wbf_ref[k0:k0 + kp, :], preferred_element_type=F32)
        acc = part if acc is None else acc + part
        k0 += kp
    if has_scale:
        acc = acc * s_ref[...]
    if has_resid:
        acc = acc + r_ref[...]
    o_ref[...] = acc.astype(o_ref.dtype)


def matmul(a_parts, w, l, out_dtype, norm_gain=None, colscale=None, resid=None,
           tm=512, tn=1024):
    if not isinstance(a_parts, (list, tuple)):
        a_parts = [a_parts]
    assert norm_gain is None or len(a_parts) == 1
    m = a_parts[0].shape[0]
    k = sum(a.shape[1] for a in a_parts)
    n = w.shape[-1]
    assert w.shape[-2] == k
    tm = min(tm, m)
    tn = min(tn, n)
    in_specs = [pl.BlockSpec((tm, a.shape[1]), lambda j, i: (i, 0)) for a in a_parts]
    in_specs.append(pl.BlockSpec((None, k, tn), lambda j, i: (l, 0, j)))
    args = list(a_parts) + [w]
    if norm_gain is not None:
        in_specs.append(pl.BlockSpec((None, 1, k), lambda j, i: (l, 0, 0)))
        args.append(norm_gain)
    if colscale is not None:
        in_specs.append(pl.BlockSpec((1, tn), lambda j, i: (0, j)))
        args.append(colscale)
    if resid is not None:
        in_specs.append(pl.BlockSpec((tm, tn), lambda j, i: (i, j)))
        args.append(resid)
    return pl.pallas_call(
        functools.partial(_matmul_kernel, n_parts=len(a_parts), has_norm=norm_gain is not None,
                          has_scale=colscale is not None, has_resid=resid is not None),
        out_shape=jax.ShapeDtypeStruct((m, n), out_dtype),
        grid=(n // tn, m // tm),
        in_specs=in_specs,
        out_specs=pl.BlockSpec((tm, tn), lambda j, i: (i, j)),
        scratch_shapes=[pltpu.VMEM((k, tn), BF16)],
        compiler_params=_params(("parallel", "arbitrary")),
        name="matmul",
    )(*args)


def _toeplitz(v, rows, cols):
    hh, n = v.shape
    assert n == rows + cols - 1
    vp = jnp.pad(v, ((0, 0), (0, 1)))
    skew = jnp.tile(vp, (1, rows))[:, :rows * n].reshape(hh, rows, n)
    return skew[:, :, rows - 1:]


def _attn_a_kernel(q_ref, k_ref, v_ref, bias_ref, o_ref):
    i = pl.program_id(2)
    start = pl.multiple_of(jnp.maximum(i * A_QB - LEFT_CHUNKS * CHUNK, 0), A_QB)
    k = k_ref[pl.ds(start, A_WIN), :]
    v = v_ref[pl.ds(start, A_WIN), :]
    s = lax.dot_general(q_ref[...], k, (((1,), (1,)), ((), ())),
                        preferred_element_type=F32)
    s = s + bias_ref[...]
    m = jnp.max(s, axis=-1, keepdims=True)
    p = jnp.exp(s - m)
    denom = jnp.sum(p, axis=-1, keepdims=True)
    o = jnp.dot(p.astype(BF16), v, preferred_element_type=F32)
    o_ref[...] = (o / denom).astype(o_ref.dtype)


def attn_a(proj, bias_a, l, batch):
    t = proj.shape[0]
    nq = SEQ // A_QB
    n_var = bias_a.shape[2]
    return pl.pallas_call(
        _attn_a_kernel,
        out_shape=jax.ShapeDtypeStruct((t, A_WIDTH), BF16),
        grid=(batch, A_HEADS, nq),
        in_specs=[
            pl.BlockSpec((A_QB, HEAD_DIM), lambda b, h, i: (b * nq + i, h)),
            pl.BlockSpec((SEQ, HEAD_DIM), lambda b, h, i: (b, A_HEADS + h)),
            pl.BlockSpec((SEQ, HEAD_DIM), lambda b, h, i: (b, 2 * A_HEADS + h)),
            pl.BlockSpec((None, None, None, A_QB, A_WIN),
                         lambda b, h, i: (l, h, jnp.minimum(i, n_var - 1), 0, 0)),
        ],
        out_specs=pl.BlockSpec((A_QB, HEAD_DIM), lambda b, h, i: (b * nq + i, h)),
        compiler_params=_params(("parallel", "parallel", "arbitrary")),
        name="attn_a",
    )(proj, proj, proj, bias_a)


def band_bias_tables(rel_tables):
    depth, _, heads = rel_tables.shape
    lead = LEFT_CHUNKS * CHUNK
    wide_w = A_WIN + lead
    r = np.arange(A_QB)[:, None]
    jj = np.arange(wide_w)[None, :]
    dchunk = (jj - lead) // CHUNK - r // CHUNK
    valid = (dchunk >= -LEFT_CHUNKS) & (dchunk <= 0)
    n = np.arange(A_QB + wide_w - 1)
    idx = np.clip(n - (A_QB - 1) - lead, -REL_CLIP, REL_CLIP) + REL_CLIP
    v = rel_tables.astype(F32)[:, idx, :].transpose(0, 2, 1).reshape(depth * heads, -1)
    wide = jnp.where(valid[None], _toeplitz(v, A_QB, wide_w), NEG_INF)
    offs = (lead, lead - A_QB, 0)
    out = jnp.stack([wide[:, :, o:o + A_WIN] for o in offs], axis=1)
    return out.reshape(depth, heads, len(offs), A_QB, A_WIN)


def _attn_b_kernel(far_ref, q_ref, k_ref, v_ref, bias_ref, lam_ref, g_ref, o_ref,
                   qqt_ref, vt_ref, m_ref, l_ref, acc_ref, *, lam_init):
    hp = pl.program_id(1)
    i = pl.program_id(2)
    qb = B_QB

    @pl.when(i == 0)
    def _():
        for hh in range(B_HP):
            for c in range(SEQ // qb):
                blk = v_ref[c * qb:(c + 1) * qb, hh * HEAD_DIM:(hh + 1) * HEAD_DIM]
                vt_ref[hh, c] = blk.astype(F32).T.astype(BF16)

    for hh in range(B_HP):
        qt = q_ref[:, hh * HEAD_DIM:(hh + 1) * HEAD_DIM].astype(F32).T
        row = lax.broadcasted_iota(jnp.int32, qt.shape, 0)
        qqt_ref[hh, :, 0:qb] = jnp.where(row < B_QK_DIM, qt, 0.0).astype(BF16)
        qqt_ref[hh, :, qb:2 * qb] = jnp.where(row >= B_QK_DIM, qt, 0.0).astype(BF16)
        m_ref[hh] = jnp.full((1, 2 * qb), NEG_INF, F32)
        l_ref[hh] = jnp.zeros((1, 2 * qb), F32)
        acc_ref[hh] = jnp.zeros((HEAD_DIM, 2 * qb), F32)

    def tile(j, d):
        off = pl.multiple_of(j * qb, qb)
        sts = [jnp.dot(k_ref[pl.ds(off, qb), hh * HEAD_DIM:(hh + 1) * HEAD_DIM], qqt_ref[hh],
                       preferred_element_type=F32) for hh in range(B_HP)]
        for hh in range(B_HP):
            st = sts[hh]
            if d is None:
                shift = far_ref[hp * B_HP + hh]
            else:
                b = bias_ref[hh, d]
                st = jnp.concatenate([st[:, 0:qb] + b, st[:, qb:2 * qb] + b], axis=1)
                shift = 0.0
            m_old = m_ref[hh]
            m_new = jnp.maximum(m_old, jnp.max(st, axis=0, keepdims=True) + shift)
            alpha = jnp.exp2(m_old - m_new)
            p = jnp.exp2(st - (m_new - shift))
            l_ref[hh] = alpha * l_ref[hh] + jnp.sum(p, axis=0, keepdims=True)
            acc_ref[hh] = alpha * acc_ref[hh] + jnp.dot(
                vt_ref[hh, j], p.astype(BF16), preferred_element_type=F32)
            m_ref[hh] = m_new

    def far_body(j, c):
        tile(j, None)
        return c

    lax.fori_loop(0, jnp.maximum(i - (B_NEAR - 1), 0), far_body, 0)

    for d in range(B_NEAR - 1, -1, -1):
        @pl.when(i >= d)
        def _(d=d):
            tile(i - d, d)

    lv = lam_ref[...]
    lam = (jnp.exp(jnp.sum(lv[0:1] * lv[1:2], axis=-1, keepdims=True))
           - jnp.exp(jnp.sum(lv[2:3] * lv[3:4], axis=-1, keepdims=True)) + lam_init)
    for hh in range(B_HP):
        o = acc_ref[hh] / l_ref[hh]
        od = o[:, 0:qb] - lam * o[:, qb:2 * qb]
        ms = jnp.mean(od * od, axis=0, keepdims=True)
        y = (od * lax.rsqrt(ms + EPS)).T * g_ref[...]
        o_ref[:, hh * HEAD_DIM:(hh + 1) * HEAD_DIM] = (y * (1.0 - lam_init)).astype(o_ref.dtype)


def attn_b(proj, far_bias, bias_b, lam_vecs, subln3, l, lam_init, batch):
    t = proj.shape[0]
    nq = SEQ // B_QB
    wb = B_HP * HEAD_DIM
    qcol = 3 * A_WIDTH // wb
    ncol = B_WIDTH // wb
    return pl.pallas_call(
        functools.partial(_attn_b_kernel, lam_init=lam_init),
        out_shape=jax.ShapeDtypeStruct((t, B_WIDTH), BF16),
        grid=(batch, B_HEADS // B_HP, nq),
        in_specs=[
            pl.BlockSpec(memory_space=pltpu.SMEM),
            pl.BlockSpec((B_QB, wb), lambda b, h, i: (b * nq + i, qcol + h)),
            pl.BlockSpec((SEQ, wb), lambda b, h, i: (b, qcol + ncol + h)),
            pl.BlockSpec((SEQ, wb), lambda b, h, i: (b, qcol + 2 * ncol + h)),
            pl.BlockSpec((B_HP, B_NEAR, B_QB, B_QB), lambda b, h, i: (h, 0, 0, 0)),
            pl.BlockSpec((None, 4, B_QK_DIM), lambda b, h, i: (l, 0, 0)),
            pl.BlockSpec((None, 1, HEAD_DIM), lambda b, h, i: (l, 0, 0)),
        ],
        out_specs=pl.BlockSpec((B_QB, wb), lambda b, h, i: (b * nq + i, h)),
        scratch_shapes=[pltpu.VMEM((B_HP, HEAD_DIM, 2 * B_QB), BF16),
                        pltpu.VMEM((B_HP, SEQ // B_QB, HEAD_DIM, B_QB), BF16),
                        pltpu.VMEM((B_HP, 1, 2 * B_QB), F32),
                        pltpu.VMEM((B_HP, 1, 2 * B_QB), F32),
                        pltpu.VMEM((B_HP, HEAD_DIM, 2 * B_QB), F32)],
        compiler_params=_params(("arbitrary", "arbitrary", "arbitrary")),
        name="attn_b",
    )(far_bias, proj, proj, proj, bias_b, lam_vecs, subln3)


def _t5_bucket(rel):
    half = T5_BUCKETS // 2
    max_exact = half // 2
    ret = jnp.where(rel > 0, half, 0)
    n = jnp.abs(rel)
    n_f = jnp.maximum(n, max_exact).astype(jnp.float32)
    large = max_exact + (jnp.log(n_f / max_exact) / math.log(T5_MAX_DIST / max_exact)
                         * (half - max_exact)).astype(jnp.int32)
    large = jnp.minimum(large, half - 1)
    return ret + jnp.where(n < max_exact, n, large)


def t5_bias_tables(t5_table):
    assert B_NEAR * B_QB - (B_QB - 1) >= T5_MAX_DIST
    lo = -(B_NEAR * B_QB)
    bias1d = t5_table.astype(F32)[_t5_bucket(jnp.arange(lo, B_QB))].T
    far = t5_table.astype(F32)[_t5_bucket(jnp.array([-(SEQ - 1)]))][0]
    u = np.arange(2 * B_QB - 1)
    kk = np.arange(B_QB)[:, None]
    qq = np.arange(B_QB)[None, :]
    tiles = []
    for d in range(B_NEAR):
        t = _toeplitz(bias1d[:, (B_QB - 1 - u - d * B_QB) - lo], B_QB, B_QB)
        if d == 0:
            t = jnp.where(((kk // CHUNK) <= (qq // CHUNK))[None], t * LOG2E, NEG_INF)
        else:
            t = t * LOG2E
        tiles.append(t)
    return far * LOG2E, jnp.stack(tiles, axis=1)


def _xattn_kernel(q_ref, k_ref, v_ref, o_ref):
    s = lax.dot_general(q_ref[...], k_ref[...], (((1,), (1,)), ((), ())),
                        preferred_element_type=F32)
    m = jnp.max(s, axis=-1, keepdims=True)
    p = jnp.exp(s - m)
    denom = jnp.sum(p, axis=-1, keepdims=True)
    o = jnp.dot(p.astype(BF16), v_ref[...], preferred_element_type=F32)
    o_ref[...] = (o / denom).astype(o_ref.dtype)


def xattn(q, kv, batch):
    t = q.shape[0]
    nq = SEQ // X_QB
    return pl.pallas_call(
        _xattn_kernel,
        out_shape=jax.ShapeDtypeStruct((t, D_MODEL), BF16),
        grid=(batch, X_HEADS, nq),
        in_specs=[
            pl.BlockSpec((X_QB, X_HEAD_DIM), lambda b, h, i: (b * nq + i, h)),
            pl.BlockSpec((MEM_LEN, X_HEAD_DIM), lambda b, h, i: (b, h)),
            pl.BlockSpec((MEM_LEN, X_HEAD_DIM), lambda b, h, i: (b, X_HEADS + h)),
        ],
        out_specs=pl.BlockSpec((X_QB, X_HEAD_DIM), lambda b, h, i: (b * nq + i, h)),
        compiler_params=_params(("parallel", "parallel", "parallel")),
        name="xattn",
    )(q, kv, kv)


def _pack_rows(v, out_ref):
    rows, width = v.shape
    half = width // 2
    lo = lax.bitcast_convert_type(v[:, :half].astype(BF16).astype(F32), jnp.uint32)
    hi = lax.bitcast_convert_type(v[:, half:].astype(BF16).astype(F32), jnp.uint32)
    u = (lo >> 16) | hi
    for c in range(half // LANES):
        out_ref[pl.ds(c, rows, stride=PACK_SUB), :] = u[:, c * LANES:(c + 1) * LANES]


def _unpack_rows(ref, rows):
    lo, hi = [], []
    for c in range(PACK_SUB):
        u = ref[pl.ds(c, rows, stride=PACK_SUB), :]
        lo.append(lax.bitcast_convert_type(u << 16, F32))
        hi.append(lax.bitcast_convert_type(u & jnp.uint32(0xFFFF0000), F32))
    return lo, hi


def _router_kernel(x_ref, g_ref, wr_ref, br_ref, route_ref, cnt_ref, hpk_ref, carry_ref, tri_ref):
    tm = x_ref.shape[0]

    @pl.when(pl.program_id(0) == 0)
    def _():
        carry_ref[...] = jnp.zeros(carry_ref.shape, F32)
        r = lax.broadcasted_iota(jnp.int32, (tm, tm), 0)
        c = lax.broadcasted_iota(jnp.int32, (tm, tm), 1)
        tri_ref[...] = jnp.where(c < r, 1.0, 0.0).astype(BF16)

    x = x_ref[...]
    ms = jnp.mean(x * x, axis=-1, keepdims=True)
    hn = x * lax.rsqrt(ms + EPS) * g_ref[...]
    _pack_rows(hn, hpk_ref)
    hb = hn.astype(BF16)
    logits = jnp.dot(hb, wr_ref[...].astype(BF16), preferred_element_type=F32) + br_ref[...]

    lane = lax.broadcasted_iota(jnp.int32, logits.shape, 1)
    lane_f = lane.astype(F32)
    big = jnp.float32(1e9)

    def top1(mask):
        v = jnp.where(mask, logits, -jnp.inf)
        vmax = jnp.max(v, axis=-1, keepdims=True)
        idx = jnp.min(jnp.where(mask & (v == vmax), lane_f, big), axis=-1, keepdims=True)
        return vmax, idx

    gmask = (lane >= N_EXPERTS) & (lane < N_EXPERTS + N_GROUPS)
    gmax, gidx = top1(gmask)
    gsum = jnp.sum(jnp.where(gmask, jnp.exp(logits - gmax), 0.0), axis=-1, keepdims=True)
    g_gate = 1.0 / gsum
    g_sel = gidx.astype(jnp.int32) - N_EXPERTS

    emask = (lane < N_EXPERTS) & ((lane // EXPERTS_PER_GROUP) == g_sel)
    v1, i1 = top1(emask)
    v2, i2 = top1(emask & (lane_f != i1))
    e2 = jnp.exp(v2 - v1)
    w1 = g_gate / (1.0 + e2)
    w2 = g_gate * e2 / (1.0 + e2)

    oh1 = jnp.where(lane_f == i1, 1.0, 0.0)
    oh2 = jnp.where(lane_f == i2, 1.0, 0.0)
    oh = oh1 + oh2
    before = jnp.dot(tri_ref[...], oh.astype(BF16), preferred_element_type=F32) + carry_ref[...]
    r1 = jnp.sum(oh1 * before, axis=-1, keepdims=True)
    r2 = jnp.sum(oh2 * before, axis=-1, keepdims=True)
    carry_ref[...] = carry_ref[...] + jnp.sum(oh, axis=0, keepdims=True)
    cnt_ref[...] = carry_ref[...]

    route = jnp.zeros(logits.shape, F32)
    for col, val in enumerate((i1, i2, w1, w2, r1, r2)):
        route = jnp.where(lane == col, val, route)
    route_ref[...] = route


def router(x, g3, wr, br, l, tm=512):
    t, d = x.shape
    return pl.pallas_call(
        _router_kernel,
        out_shape=(jax.ShapeDtypeStruct((t, LANES), F32),
                   jax.ShapeDtypeStruct((1, LANES), F32),
                   jax.ShapeDtypeStruct((t * PACK_SUB, LANES), jnp.uint32)),
        grid=(t // tm,),
        in_specs=[pl.BlockSpec((tm, d), lambda i: (i, 0)),
                  pl.BlockSpec((None, 1, d), lambda i: (l, 0, 0)),
                  pl.BlockSpec((None, d, LANES), lambda i: (l, 0, 0)),
                  pl.BlockSpec((None, 1, LANES), lambda i: (l, 0, 0))],
        out_specs=(pl.BlockSpec((tm, LANES), lambda i: (i, 0)),
                   pl.BlockSpec((1, LANES), lambda i: (0, 0)),
                   pl.BlockSpec((tm * PACK_SUB, LANES), lambda i: (i, 0))),
        scratch_shapes=[pltpu.VMEM((1, LANES), F32), pltpu.VMEM((tm, tm), BF16)],
        compiler_params=_params(("arbitrary",)),
        name="router",
    )(x, g3, wr, br)


def _max_tiles(t):
    return (2 * t) // E_TM + N_EXPERTS


def dispatch_plan(route, counts_row):
    t = route.shape[0]
    max_tiles = _max_tiles(t)
    ids = route[:, 0:2].astype(jnp.int32)
    ranks = route[:, 4:6].astype(jnp.int32)
    counts = counts_row[0, :N_EXPERTS].astype(jnp.int32)
    padded = ((counts + E_TM - 1) // E_TM) * E_TM
    ends = jnp.cumsum(padded)
    offs = ends - padded
    onehot = ids[:, :, None] == jnp.arange(N_EXPERTS, dtype=jnp.int32)
    dest = (jnp.sum(jnp.where(onehot, offs, 0), axis=-1) + ranks).reshape(2 * t)
    n_tiles = ends[-1] // E_TM
    tile_i = jnp.minimum(jnp.arange(max_tiles, dtype=jnp.int32), n_tiles - 1)
    tile_expert = jnp.sum(ends[None, :] <= tile_i[:, None] * E_TM, axis=1).astype(jnp.int32)
    tokens = jnp.arange(2 * t, dtype=jnp.int32) // 2
    row_token = jnp.zeros((max_tiles * E_TM,), jnp.int32).at[dest].set(
        tokens, unique_indices=True, mode="promise_in_bounds")
    return dest, row_token, tile_expert, n_tiles.reshape(1).astype(jnp.int32)


def _experts_kernel(te_ref, nt_ref, rt_ref, h_hbm, w1_ref, w3_ref, w2_ref, ys_ref,
                    xbuf0, xbuf1, sem, w1b, w3b, w2b, *, max_tiles):
    i = pl.program_id(0)
    nt = nt_ref[0]
    bufs = (xbuf0, xbuf1)

    def row_copy(tok, s, r):
        src = h_hbm.at[pl.ds(pl.multiple_of(tok * PACK_SUB, PACK_SUB), PACK_SUB)]
        return pltpu.make_async_copy(src, bufs[s].at[pl.ds(r * PACK_SUB, PACK_SUB)], sem.at[s])

    def tile_wait(s):
        pltpu.make_async_copy(h_hbm.at[pl.ds(0, E_TM * PACK_SUB)], bufs[s], sem.at[s]).wait()

    @pl.when(i == 0)
    def _():
        def body(r, c):
            src = h_hbm.at[pl.ds(pl.multiple_of(rt_ref[r] * PACK_SUB, PACK_SUB), PACK_SUB)]
            dst = xbuf0.at[pl.ds(pl.multiple_of(r * PACK_SUB, PACK_SUB), PACK_SUB)]
            pltpu.make_async_copy(src, dst, sem.at[0]).start()
            return c

        lax.fori_loop(0, E_TM, body, 0, unroll=8)

    def step(s):
        @pl.when((i == 0) | (te_ref[i] != te_ref[jnp.maximum(i - 1, 0)]))
        def _():
            w1b[...] = w1_ref[...].astype(BF16)
            w3b[...] = w3_ref[...].astype(BF16)
            w2b[...] = w2_ref[...].astype(BF16)

        tile_wait(s)

        base = jnp.minimum(i + 1, max_tiles - 1) * E_TM
        for r in range(E_TM):
            row_copy(rt_ref[base + r], 1 - s, r).start()

        lo, hi = _unpack_rows(bufs[s], E_TM)
        hb = jnp.concatenate([v.astype(BF16) for v in lo + hi], axis=1)
        a1 = jnp.dot(hb, w1b[...], preferred_element_type=F32)
        a3 = jnp.dot(hb, w3b[...], preferred_element_type=F32)
        hid = (a1 / (1.0 + jnp.exp(-a1))) * a3
        _pack_rows(jnp.dot(hid.astype(BF16), w2b[...], preferred_element_type=F32), ys_ref)

        @pl.when(i == nt - 1)
        def _():
            tile_wait(1 - s)

    for s in range(2):
        @pl.when((i < nt) & (i % 2 == s))
        def _(s=s):
            step(s)

    @pl.when(i >= nt)
    def _():
        ys_ref[...] = jnp.zeros(ys_ref.shape, ys_ref.dtype)


def experts(hpk, w1, w3, w2, l, tile_expert, n_tiles, row_token):
    d = w1.shape[-2]
    max_tiles = row_token.shape[0] // E_TM
    blk = E_TM * PACK_SUB
    grid_spec = pltpu.PrefetchScalarGridSpec(
        num_scalar_prefetch=3,
        grid=(max_tiles,),
        in_specs=[pl.BlockSpec(memory_space=pl.ANY),
                  pl.BlockSpec((None, None, d, D_EXPERT), lambda i, te, nt, rt: (l, te[i], 0, 0)),
                  pl.BlockSpec((None, None, d, D_EXPERT), lambda i, te, nt, rt: (l, te[i], 0, 0)),
                  pl.BlockSpec((None, None, D_EXPERT, d), lambda i, te, nt, rt: (l, te[i], 0, 0))],
        out_specs=pl.BlockSpec((blk, LANES), lambda i, te, nt, rt: (i, 0)),
        scratch_shapes=[pltpu.VMEM((blk, LANES), jnp.uint32),
                        pltpu.VMEM((blk, LANES), jnp.uint32),
                        pltpu.SemaphoreType.DMA((2,)),
                        pltpu.VMEM((d, D_EXPERT), BF16),
                        pltpu.VMEM((d, D_EXPERT), BF16),
                        pltpu.VMEM((D_EXPERT, d), BF16)],
    )
    return pl.pallas_call(
        functools.partial(_experts_kernel, max_tiles=max_tiles),
        out_shape=jax.ShapeDtypeStruct((max_tiles * blk, LANES), jnp.uint32),
        grid_spec=grid_spec,
        compiler_params=_params(("arbitrary",)),
        name="experts",
    )(tile_expert, n_tiles, row_token, hpk, w1, w3, w2)


def _combine_kernel(dest_ref, ys_hbm, x_ref, route_ref, o_ref, ybuf0, ybuf1, sem):
    i = pl.program_id(0)
    n = pl.num_programs(0)
    bufs = (ybuf0, ybuf1)

    def row_copy(row, s, k, r):
        src = ys_hbm.at[pl.ds(pl.multiple_of(row * PACK_SUB, PACK_SUB), PACK_SUB)]
        return pltpu.make_async_copy(src, bufs[s].at[k, pl.ds(r * PACK_SUB, PACK_SUB)], sem.at[s])

    def tile_wait(s):
        for k in range(2):
            pltpu.make_async_copy(ys_hbm.at[pl.ds(0, C_TM * PACK_SUB)], bufs[s].at[k],
                                  sem.at[s]).wait()

    @pl.when(i == 0)
    def _():
        def body(r, c):
            for k in range(2):
                src = ys_hbm.at[pl.ds(pl.multiple_of(dest_ref[2 * r + k] * PACK_SUB, PACK_SUB),
                                      PACK_SUB)]
                dst = ybuf0.at[k, pl.ds(pl.multiple_of(r * PACK_SUB, PACK_SUB), PACK_SUB)]
                pltpu.make_async_copy(src, dst, sem.at[0]).start()
            return c

        lax.fori_loop(0, C_TM, body, 0, unroll=4)

    def step(s):
        tile_wait(s)
        base = 2 * jnp.minimum(i + 1, n - 1) * C_TM
        for r in range(C_TM):
            for k in range(2):
                row_copy(dest_ref[base + 2 * r + k], 1 - s, k, r).start()

        route = route_ref[...]
        lane = lax.broadcasted_iota(jnp.int32, route.shape, 1)
        w1 = jnp.sum(jnp.where(lane == 2, route, 0.0), axis=-1, keepdims=True)
        w2 = jnp.sum(jnp.where(lane == 3, route, 0.0), axis=-1, keepdims=True)
        lo1, hi1 = _unpack_rows(bufs[s].at[0], C_TM)
        lo2, hi2 = _unpack_rows(bufs[s].at[1], C_TM)
        half = x_ref.shape[1] // 2
        for c in range(PACK_SUB):
            for off, y1, y2 in ((0, lo1, lo2), (half, hi1, hi2)):
                cols = slice(off + c * LANES, off + (c + 1) * LANES)
                o_ref[:, cols] = x_ref[:, cols] + w1 * y1[c] + w2 * y2[c]

        @pl.when(i == n - 1)
        def _():
            tile_wait(1 - s)

    for s in range(2):
        @pl.when(i % 2 == s)
        def _(s=s):
            step(s)


def combine(ys, x, route, dest):
    t, d = x.shape
    blk = C_TM * PACK_SUB
    grid_spec = pltpu.PrefetchScalarGridSpec(
        num_scalar_prefetch=1,
        grid=(t // C_TM,),
        in_specs=[pl.BlockSpec(memory_space=pl.ANY),
                  pl.BlockSpec((C_TM, d), lambda i, dest: (i, 0)),
                  pl.BlockSpec((C_TM, LANES), lambda i, dest: (i, 0))],
        out_specs=pl.BlockSpec((C_TM, d), lambda i, dest: (i, 0)),
        scratch_shapes=[pltpu.VMEM((2, blk, LANES), jnp.uint32),
                        pltpu.VMEM((2, blk, LANES), jnp.uint32),
                        pltpu.SemaphoreType.DMA((2,))],
    )
    return pl.pallas_call(
        _combine_kernel,
        out_shape=jax.ShapeDtypeStruct((t, d), F32),
        grid_spec=grid_spec,
        compiler_params=_params(("arbitrary",)),
        name="combine",
    )(dest, ys, x, route)


def kernel(x, mem, norm_mix, w_in, rel_bias_a, t5_table, diff_lambda, diff_subln, w_out,
           norm_cross, norm_mem, w_xq, w_xkv, w_xo, norm_ffn, w_group, b_group,
           w_expert, b_expert, w1, w3, w2, norm_final):
    batch, seq, d = x.shape
    t = batch * seq
    xf = x.reshape(t, d)
    memf = mem.reshape(batch * MEM_LEN, d)

    g_mix = norm_mix.reshape(DEPTH, 1, d)
    g_cross = norm_cross.reshape(DEPTH, 1, d)
    g_mem = norm_mem.reshape(DEPTH, 1, d)
    g_ffn = norm_ffn.reshape(DEPTH, 1, d)
    g_final = norm_final.reshape(1, 1, d)
    subln3 = diff_subln.reshape(DEPTH, 1, HEAD_DIM)

    in_scale = jnp.concatenate([
        jnp.full((A_WIDTH,), HEAD_DIM ** -0.5, F32), jnp.ones((2 * A_WIDTH,), F32),
        jnp.full((B_WIDTH,), B_QK_DIM ** -0.5 * LOG2E, F32), jnp.ones((2 * B_WIDTH,), F32)]
    ).reshape(1, IN_WIDTH)
    xq_scale = jnp.full((1, d), X_HEAD_DIM ** -0.5, F32)

    far_b, bias_b = t5_bias_tables(t5_table)
    bias_a = band_bias_tables(rel_bias_a)

    pad = LANES - N_EXPERTS - N_GROUPS
    wr = jnp.concatenate([w_expert, w_group, jnp.zeros((DEPTH, d, pad), F32)], axis=-1)
    br = jnp.concatenate([b_expert, b_group, jnp.zeros((DEPTH, pad), F32)],
                         axis=-1).reshape(DEPTH, 1, LANES)

    for l in range(DEPTH):
        lam_init = 0.8 - 0.6 * math.exp(-0.3 * l)
        proj = matmul(xf, w_in, l, BF16, norm_gain=g_mix, colscale=in_scale)
        oa = attn_a(proj, bias_a, l, batch)
        ob = attn_b(proj, far_b, bias_b, diff_lambda, subln3, l, lam_init, batch)
        xf = matmul([oa, ob], w_out, l, F32, resid=xf)
        q = matmul(xf, w_xq, l, BF16, norm_gain=g_cross, colscale=xq_scale, tn=2048)
        kv = matmul(memf, w_xkv, l, BF16, norm_gain=g_mem)
        xo = xattn(q, kv, batch)
        xf = matmul(xo, w_xo, l, F32, resid=xf)
        route, counts, hpk = router(xf, g_ffn, wr, br, l)
        dest, row_token, tile_expert, n_tiles = dispatch_plan(route, counts)
        ys = experts(hpk, w1, w3, w2, l, tile_expert, n_tiles, row_token)
        xf = combine(ys, xf, route, dest)
    out = rmsnorm(xf, g_final, 0, F32)
    return out.reshape(batch, seq, d)
```

```python
import functools
import math

import jax
import jax.numpy as jnp
import numpy as np
from jax import lax
from jax.experimental import pallas as pl
from jax.experimental.pallas import tpu as pltpu

D_MODEL = 2048
SEQ = 4096
DEPTH = 4
CHUNK = 64
LEFT_CHUNKS = 8
HEAD_DIM = 128
A_HEADS = 8
B_HEADS = 8
B_QK_DIM = 64
A_WIDTH = 1024
B_WIDTH = 1024
IN_WIDTH = 6144
REL_CLIP = 128
T5_BUCKETS = 32
T5_MAX_DIST = 512
MEM_LEN = 256
X_HEADS = 4
X_HEAD_DIM = 512
N_GROUPS = 4
EXPERTS_PER_GROUP = 8
N_EXPERTS = 32
D_EXPERT = 256
EPS = 1e-6
NEG_INF = -1e30

LANES = 128
VMEM_LIMIT = 56 * 1024 * 1024

BF16 = jnp.bfloat16
F32 = jnp.float32

A_QB = 256
A_WIN = A_QB + LEFT_CHUNKS * CHUNK
B_QB = 256
B_NEAR = 3
B_HP = 4
A_HP = 4
LOG2E = math.log2(math.e)
X_QB = 512

E_TM = 256
C_TM = 256
PACK_SUB = 8


def _params(sem):
    return pltpu.CompilerParams(dimension_semantics=sem, vmem_limit_bytes=VMEM_LIMIT)


def _rmsnorm_kernel(x_ref, g_ref, o_ref):
    x = x_ref[...]
    ms = jnp.mean(x * x, axis=-1, keepdims=True)
    o_ref[...] = (x * lax.rsqrt(ms + EPS) * g_ref[...]).astype(o_ref.dtype)


def rmsnorm(x, g3, l, out_dtype, tm=512):
    m, d = x.shape
    tm = min(tm, m)
    return pl.pallas_call(
        _rmsnorm_kernel,
        out_shape=jax.ShapeDtypeStruct((m, d), out_dtype),
        grid=(m // tm,),
        in_specs=[pl.BlockSpec((tm, d), lambda i: (i, 0)),
                  pl.BlockSpec((None, 1, d), lambda i: (l, 0, 0))],
        out_specs=pl.BlockSpec((tm, d), lambda i: (i, 0)),
        compiler_params=_params(("parallel",)),
        name="rmsnorm",
    )(x, g3)


def _matmul_kernel(*refs, n_parts, has_norm, has_scale, has_resid):
    a_refs, w_ref = refs[:n_parts], refs[n_parts]
    k = n_parts + 1
    g_ref = s_ref = r_ref = None
    if has_norm:
        g_ref = refs[k]; k += 1
    if has_scale:
        s_ref = refs[k]; k += 1
    if has_resid:
        r_ref = refs[k]; k += 1
    o_ref, wbf_ref = refs[k], refs[k + 1]

    @pl.when(pl.program_id(1) == 0)
    def _():
        wbf_ref[...] = w_ref[...].astype(BF16)

    acc = None
    k0 = 0
    for a_ref in a_refs:
        kp = a_ref.shape[1]
        a = a_ref[...]
        if has_norm:
            ms = jnp.mean(a * a, axis=-1, keepdims=True)
            a = (a * lax.rsqrt(ms + EPS) * g_ref[...]).astype(BF16)
        part = jnp.dot(a, wbf_ref[k0:k0 + kp, :], preferred_element_type=F32)
        acc = part if acc is None else acc + part
        k0 += kp
    if has_scale:
        acc = acc * s_ref[...]
    if has_resid:
        acc = acc + r_ref[...]
    o_ref[...] = acc.astype(o_ref.dtype)


def matmul(a_parts, w, l, out_dtype, norm_gain=None, colscale=None, resid=None,
           tm=512, tn=1024):
    if not isinstance(a_parts, (list, tuple)):
        a_parts = [a_parts]
    assert norm_gain is None or len(a_parts) == 1
    m = a_parts[0].shape[0]
    k = sum(a.shape[1] for a in a_parts)
    n = w.shape[-1]
    assert w.shape[-2] == k
    tm = min(tm, m)
    tn = min(tn, n)
    in_specs = [pl.BlockSpec((tm, a.shape[1]), lambda j, i: (i, 0)) for a in a_parts]
    in_specs.append(pl.BlockSpec((None, k, tn), lambda j, i: (l, 0, j)))
    args = list(a_parts) + [w]
    if norm_gain is not None:
        in_specs.append(pl.BlockSpec((None, 1, k), lambda j, i: (l, 0, 0)))
        args.append(norm_gain)
    if colscale is not None:
        in_specs.append(pl.BlockSpec((1, tn), lambda j, i: (0, j)))
        args.append(colscale)
    if resid is not None:
        in_specs.append(pl.BlockSpec((tm, tn), lambda j, i: (i, j)))
        args.append(resid)
    return pl.pallas_call(
        functools.partial(_matmul_kernel, n_parts=len(a_parts), has_norm=norm_gain is not None,
                          has_scale=colscale is not None, has_resid=resid is not None),
        out_shape=jax.ShapeDtypeStruct((m, n), out_dtype),
        grid=(n // tn, m // tm),
        in_specs=in_specs,
        out_specs=pl.BlockSpec((tm, tn), lambda j, i: (i, j)),
        scratch_shapes=[pltpu.VMEM((k, tn), BF16)],
        compiler_params=_params(("parallel", "arbitrary")),
        name="matmul",
    )(*args)


def _toeplitz(v, rows, cols):
    hh, n = v.shape
    assert n == rows + cols - 1
    vp = jnp.pad(v, ((0, 0), (0, 1)))
    skew = jnp.tile(vp, (1, rows))[:, :rows * n].reshape(hh, rows, n)
    return skew[:, :, rows - 1:]


def _attn_a_kernel(q_ref, k_ref, v_ref, b0_ref, b1_ref, b2_ref, o_ref, vt_ref, st_ref):
    i = pl.program_id(2)
    qb = A_QB
    n_kt = A_WIN // qb
    bias_refs = (b0_ref, b1_ref, b2_ref)

    @pl.when(i == 0)
    def _():
        for hh in range(A_HP):
            for c in range(SEQ // qb):
                blk = v_ref[c * qb:(c + 1) * qb, hh * HEAD_DIM:(hh + 1) * HEAD_DIM]
                vt_ref[hh, c] = blk.astype(F32).T.astype(BF16)

    t0 = jnp.maximum(i - (n_kt - 1), 0)
    start = pl.multiple_of(t0 * qb, qb)
    qts = [q_ref[:, hh * HEAD_DIM:(hh + 1) * HEAD_DIM].astype(F32).T.astype(BF16)
           for hh in range(A_HP)]
    for hh in range(A_HP):
        st_ref[hh] = jnp.dot(k_ref[pl.ds(start, A_WIN), hh * HEAD_DIM:(hh + 1) * HEAD_DIM],
                             qts[hh], preferred_element_type=F32)
    for hh in range(A_HP):
        st = st_ref[hh] + jnp.concatenate([r[hh] for r in bias_refs], axis=0)
        m = jnp.max(st, axis=0, keepdims=True)
        p = jnp.exp2(st - m)
        denom = jnp.sum(p, axis=0, keepdims=True)
        pb = p.astype(BF16)
        ot = None
        for u in range(n_kt):
            part = jnp.dot(vt_ref[hh, t0 + u], pb[u * qb:(u + 1) * qb],
                           preferred_element_type=F32)
            ot = part if ot is None else ot + part
        o_ref[:, hh * HEAD_DIM:(hh + 1) * HEAD_DIM] = (ot / denom).T.astype(o_ref.dtype)


def attn_a(proj, bias_a, l, batch):
    t = proj.shape[0]
    nq = SEQ // A_QB
    n_kt = A_WIN // A_QB
    wa = A_HP * HEAD_DIM
    ncol = A_WIDTH // wa

    def bias_spec(u):
        return pl.BlockSpec(
            (None, A_HP, None, A_QB, A_QB),
            lambda b, h, i: (l, h, (n_kt - 1) - jnp.minimum(i, n_kt - 1) + u, 0, 0))

    return pl.pallas_call(
        _attn_a_kernel,
        out_shape=jax.ShapeDtypeStruct((t, A_WIDTH), BF16),
        grid=(batch, A_HEADS // A_HP, nq),
        in_specs=[
            pl.BlockSpec((A_QB, wa), lambda b, h, i: (b * nq + i, h)),
            pl.BlockSpec((SEQ, wa), lambda b, h, i: (b, ncol + h)),
            pl.BlockSpec((SEQ, wa), lambda b, h, i: (b, 2 * ncol + h)),
            bias_spec(0), bias_spec(1), bias_spec(2),
        ],
        out_specs=pl.BlockSpec((A_QB, wa), lambda b, h, i: (b * nq + i, h)),
        scratch_shapes=[pltpu.VMEM((A_HP, SEQ // A_QB, HEAD_DIM, A_QB), BF16),
                        pltpu.VMEM((A_HP, A_WIN, A_QB), F32)],
        compiler_params=_params(("arbitrary", "arbitrary", "arbitrary")),
        name="attn_a",
    )(proj, proj, proj, bias_a, bias_a, bias_a)


def band_bias_tables(rel_tables):
    depth, _, heads = rel_tables.shape
    lead = LEFT_CHUNKS * CHUNK
    n_kt = A_WIN // A_QB
    wide_w = A_WIN + (n_kt - 1) * A_QB
    jj = np.arange(wide_w)[:, None]
    r = np.arange(A_QB)[None, :]
    dchunk = (jj - lead) // CHUNK - r // CHUNK
    valid = (dchunk >= -LEFT_CHUNKS) & (dchunk <= 0)
    n = np.arange(wide_w + A_QB - 1)
    idx = np.clip((wide_w - 1 - n) - lead, -REL_CLIP, REL_CLIP) + REL_CLIP
    v = rel_tables.astype(F32)[:, idx, :].transpose(0, 2, 1).reshape(depth * heads, -1)
    wide = jnp.where(valid[None], _toeplitz(v, wide_w, A_QB) * LOG2E, NEG_INF)
    return wide.reshape(depth, heads, wide_w // A_QB, A_QB, A_QB)


def _attn_b_kernel(far_ref, q_ref, k_ref, v_ref, bias_ref, lam_ref, g_ref, o_ref,
                   qqt_ref, vt_ref, m_ref, l_ref, acc_ref, st_ref, *, lam_init):
    hp = pl.program_id(1)
    i = pl.program_id(2)
    qb = B_QB

    @pl.when(i == 0)
    def _():
        for hh in range(B_HP):
            for c in range(SEQ // qb):
                blk = v_ref[c * qb:(c + 1) * qb, hh * HEAD_DIM:(hh + 1) * HEAD_DIM]
                vt_ref[hh, c] = blk.astype(F32).T.astype(BF16)

    for hh in range(B_HP):
        qt = q_ref[:, hh * HEAD_DIM:(hh + 1) * HEAD_DIM].astype(F32).T
        row = lax.broadcasted_iota(jnp.int32, qt.shape, 0)
        qqt_ref[hh, :, 0:qb] = jnp.where(row < B_QK_DIM, qt, 0.0).astype(BF16)
        qqt_ref[hh, :, qb:2 * qb] = jnp.where(row >= B_QK_DIM, qt, 0.0).astype(BF16)
        m_ref[hh] = jnp.full((1, 2 * qb), NEG_INF, F32)
        l_ref[hh] = jnp.zeros((1, 2 * qb), F32)
        acc_ref[hh] = jnp.zeros((HEAD_DIM, 2 * qb), F32)

    def tile(j, d, nk=1):
        off = pl.multiple_of(j * qb, qb)
        for hh in range(B_HP):
            st_ref[hh, 0:nk * qb] = jnp.dot(
                k_ref[pl.ds(off, nk * qb), hh * HEAD_DIM:(hh + 1) * HEAD_DIM], qqt_ref[hh],
                preferred_element_type=F32)
        for hh in range(B_HP):
            st = st_ref[hh, 0:nk * qb]
            if d is None:
                shift = far_ref[hp * B_HP + hh]
            else:
                b = jnp.concatenate([bias_ref[hh, du] for du in d], axis=0)
                st = jnp.concatenate([st[:, 0:qb] + b, st[:, qb:2 * qb] + b], axis=1)
                shift = 0.0
            m_old = m_ref[hh]
            m_new = jnp.maximum(m_old, jnp.max(st, axis=0, keepdims=True) + shift)
            alpha = jnp.exp2(m_old - m_new)
            p = jnp.exp2(st - (m_new - shift))
            l_ref[hh] = alpha * l_ref[hh] + jnp.sum(p, axis=0, keepdims=True)
            pb = p.astype(BF16)
            pv = None
            for u in range(nk):
                part = jnp.dot(vt_ref[hh, j + u], pb[u * qb:(u + 1) * qb],
                               preferred_element_type=F32)
                pv = part if pv is None else pv + part
            acc_ref[hh] = alpha * acc_ref[hh] + pv
            m_ref[hh] = m_new

    n_far = jnp.maximum(i - (B_NEAR - 1), 0)

    def far_body(jj, c):
        tile(2 * jj, None, nk=2)
        return c

    lax.fori_loop(0, n_far // 2, far_body, 0)

    @pl.when(n_far % 2 == 1)
    def _():
        tile(n_far - 1, None)

    assert B_NEAR == 3
    @pl.when(i >= 2)
    def _():
        tile(i - 2, (2, 1), nk=2)

    @pl.when(i == 1)
    def _():
        tile(0, (1,))

    tile(i, (0,))

    lv = lam_ref[...]
    lam = (jnp.exp(jnp.sum(lv[0:1] * lv[1:2], axis=-1, keepdims=True))
           - jnp.exp(jnp.sum(lv[2:3] * lv[3:4], axis=-1, keepdims=True)) + lam_init)
    for hh in range(B_HP):
        o = acc_ref[hh] / l_ref[hh]
        od = o[:, 0:qb] - lam * o[:, qb:2 * qb]
        ms = jnp.mean(od * od, axis=0, keepdims=True)
        y = (od * lax.rsqrt(ms + EPS)).T * g_ref[...]
        o_ref[:, hh * HEAD_DIM:(hh + 1) * HEAD_DIM] = (y * (1.0 - lam_init)).astype(o_ref.dtype)


def attn_b(proj, far_bias, bias_b, lam_vecs, subln3, l, lam_init, batch):
    t = proj.shape[0]
    nq = SEQ // B_QB
    wb = B_HP * HEAD_DIM
    qcol = 3 * A_WIDTH // wb
    ncol = B_WIDTH // wb
    return pl.pallas_call(
        functools.partial(_attn_b_kernel, lam_init=lam_init),
        out_shape=jax.ShapeDtypeStruct((t, B_WIDTH), BF16),
        grid=(batch, B_HEADS // B_HP, nq),
        in_specs=[
            pl.BlockSpec(memory_space=pltpu.SMEM),
            pl.BlockSpec((B_QB, wb), lambda b, h, i: (b * nq + i, qcol + h)),
            pl.BlockSpec((SEQ, wb), lambda b, h, i: (b, qcol + ncol + h)),
            pl.BlockSpec((SEQ, wb), lambda b, h, i: (b, qcol + 2 * ncol + h)),
            pl.BlockSpec((B_HP, B_NEAR, B_QB, B_QB), lambda b, h, i: (h, 0, 0, 0)),
            pl.BlockSpec((None, 4, B_QK_DIM), lambda b, h, i: (l, 0, 0)),
            pl.BlockSpec((None, 1, HEAD_DIM), lambda b, h, i: (l, 0, 0)),
        ],
        out_specs=pl.BlockSpec((B_QB, wb), lambda b, h, i: (b * nq + i, h)),
        scratch_shapes=[pltpu.VMEM((B_HP, HEAD_DIM, 2 * B_QB), BF16),
                        pltpu.VMEM((B_HP, SEQ // B_QB, HEAD_DIM, B_QB), BF16),
                        pltpu.VMEM((B_HP, 1, 2 * B_QB), F32),
                        pltpu.VMEM((B_HP, 1, 2 * B_QB), F32),
                        pltpu.VMEM((B_HP, HEAD_DIM, 2 * B_QB), F32),
                        pltpu.VMEM((B_HP, 2 * B_QB, 2 * B_QB), F32)],
        compiler_params=_params(("arbitrary", "arbitrary", "arbitrary")),
        name="attn_b",
    )(far_bias, proj, proj, proj, bias_b, lam_vecs, subln3)


def _t5_bucket(rel):
    half = T5_BUCKETS // 2
    max_exact = half // 2
    ret = jnp.where(rel > 0, half, 0)
    n = jnp.abs(rel)
    n_f = jnp.maximum(n, max_exact).astype(jnp.float32)
    large = max_exact + (jnp.log(n_f / max_exact) / math.log(T5_MAX_DIST / max_exact)
                         * (half - max_exact)).astype(jnp.int32)
    large = jnp.minimum(large, half - 1)
    return ret + jnp.where(n < max_exact, n, large)


def t5_bias_tables(t5_table):
    assert B_NEAR * B_QB - (B_QB - 1) >= T5_MAX_DIST
    lo = -(B_NEAR * B_QB)
    bias1d = t5_table.astype(F32)[_t5_bucket(jnp.arange(lo, B_QB))].T
    far = t5_table.astype(F32)[_t5_bucket(jnp.array([-(SEQ - 1)]))][0]
    u = np.arange(2 * B_QB - 1)
    kk = np.arange(B_QB)[:, None]
    qq = np.arange(B_QB)[None, :]
    tiles = []
    for d in range(B_NEAR):
        t = _toeplitz(bias1d[:, (B_QB - 1 - u - d * B_QB) - lo], B_QB, B_QB)
        if d == 0:
            t = jnp.where(((kk // CHUNK) <= (qq // CHUNK))[None], t * LOG2E, NEG_INF)
        else:
            t = t * LOG2E
        tiles.append(t)
    return far * LOG2E, jnp.stack(tiles, axis=1)


def _xattn_kernel(q_ref, k_ref, v_ref, o_ref):
    s = lax.dot_general(q_ref[...], k_ref[...], (((1,), (1,)), ((), ())),
                        preferred_element_type=F32)
    m = jnp.max(s, axis=-1, keepdims=True)
    p = jnp.exp(s - m)
    denom = jnp.sum(p, axis=-1, keepdims=True)
    o = jnp.dot(p.astype(BF16), v_ref[...], preferred_element_type=F32)
    o_ref[...] = (o / denom).astype(o_ref.dtype)


def xattn(q, kv, batch):
    t = q.shape[0]
    nq = SEQ // X_QB
    return pl.pallas_call(
        _xattn_kernel,
        out_shape=jax.ShapeDtypeStruct((t, D_MODEL), BF16),
        grid=(batch, X_HEADS, nq),
        in_specs=[
            pl.BlockSpec((X_QB, X_HEAD_DIM), lambda b, h, i: (b * nq + i, h)),
            pl.BlockSpec((MEM_LEN, X_HEAD_DIM), lambda b, h, i: (b, h)),
            pl.BlockSpec((MEM_LEN, X_HEAD_DIM), lambda b, h, i: (b, X_HEADS + h)),
        ],
        out_specs=pl.BlockSpec((X_QB, X_HEAD_DIM), lambda b, h, i: (b * nq + i, h)),
        compiler_params=_params(("parallel", "parallel", "parallel")),
        name="xattn",
    )(q, kv, kv)


def _pack_rows(v, out_ref):
    rows, width = v.shape
    half = width // 2
    lo = lax.bitcast_convert_type(v[:, :half].astype(BF16).astype(F32), jnp.uint32)
    hi = lax.bitcast_convert_type(v[:, half:].astype(BF16).astype(F32), jnp.uint32)
    u = (lo >> 16) | hi
    for c in range(half // LANES):
        out_ref[pl.ds(c, rows, stride=PACK_SUB), :] = u[:, c * LANES:(c + 1) * LANES]


def _unpack_rows(ref, rows):
    lo, hi = [], []
    for c in range(PACK_SUB):
        u = ref[pl.ds(c, rows, stride=PACK_SUB), :]
        lo.append(lax.bitcast_convert_type(u << 16, F32))
        hi.append(lax.bitcast_convert_type(u & jnp.uint32(0xFFFF0000), F32))
    return lo, hi


def _router_kernel(x_ref, g_ref, wr_ref, br_ref, route_ref, cnt_ref, hpk_ref, carry_ref, tri_ref):
    tm = x_ref.shape[0]

    @pl.when(pl.program_id(0) == 0)
    def _():
        carry_ref[...] = jnp.zeros(carry_ref.shape, F32)
        r = lax.broadcasted_iota(jnp.int32, (tm, tm), 0)
        c = lax.broadcasted_iota(jnp.int32, (tm, tm), 1)
        tri_ref[...] = jnp.where(c < r, 1.0, 0.0).astype(BF16)

    x = x_ref[...]
    ms = jnp.mean(x * x, axis=-1, keepdims=True)
    hn = x * lax.rsqrt(ms + EPS) * g_ref[...]
    _pack_rows(hn, hpk_ref)
    hb = hn.astype(BF16)
    logits = jnp.dot(hb, wr_ref[...].astype(BF16), preferred_element_type=F32) + br_ref[...]

    lane = lax.broadcasted_iota(jnp.int32, logits.shape, 1)
    lane_f = lane.astype(F32)
    big = jnp.float32(1e9)

    def top1(mask):
        v = jnp.where(mask, logits, -jnp.inf)
        vmax = jnp.max(v, axis=-1, keepdims=True)
        idx = jnp.min(jnp.where(mask & (v == vmax), lane_f, big), axis=-1, keepdims=True)
        return vmax, idx

    gmask = (lane >= N_EXPERTS) & (lane < N_EXPERTS + N_GROUPS)
    gmax, gidx = top1(gmask)
    gsum = jnp.sum(jnp.where(gmask, jnp.exp(logits - gmax), 0.0), axis=-1, keepdims=True)
    g_gate = 1.0 / gsum
    g_sel = gidx.astype(jnp.int32) - N_EXPERTS

    emask = (lane < N_EXPERTS) & ((lane // EXPERTS_PER_GROUP) == g_sel)
    v1, i1 = top1(emask)
    v2, i2 = top1(emask & (lane_f != i1))
    e2 = jnp.exp(v2 - v1)
    w1 = g_gate / (1.0 + e2)
    w2 = g_gate * e2 / (1.0 + e2)

    oh1 = jnp.where(lane_f == i1, 1.0, 0.0)
    oh2 = jnp.where(lane_f == i2, 1.0, 0.0)
    oh = oh1 + oh2
    before = jnp.dot(tri_ref[...], oh.astype(BF16), preferred_element_type=F32) + carry_ref[...]
    r1 = jnp.sum(oh1 * before, axis=-1, keepdims=True)
    r2 = jnp.sum(oh2 * before, axis=-1, keepdims=True)
    carry_ref[...] = carry_ref[...] + jnp.sum(oh, axis=0, keepdims=True)
    cnt_ref[...] = carry_ref[...]

    route = jnp.zeros(logits.shape, F32)
    for col, val in enumerate((i1, i2, w1, w2, r1, r2)):
        route = jnp.where(lane == col, val, route)
    route_ref[...] = route


def router(x, g3, wr, br, l, tm=512):
    t, d = x.shape
    return pl.pallas_call(
        _router_kernel,
        out_shape=(jax.ShapeDtypeStruct((t, LANES), F32),
                   jax.ShapeDtypeStruct((1, LANES), F32),
                   jax.ShapeDtypeStruct((t * PACK_SUB, LANES), jnp.uint32)),
        grid=(t // tm,),
        in_specs=[pl.BlockSpec((tm, d), lambda i: (i, 0)),
                  pl.BlockSpec((None, 1, d), lambda i: (l, 0, 0)),
                  pl.BlockSpec((None, d, LANES), lambda i: (l, 0, 0)),
                  pl.BlockSpec((None, 1, LANES), lambda i: (l, 0, 0))],
        out_specs=(pl.BlockSpec((tm, LANES), lambda i: (i, 0)),
                   pl.BlockSpec((1, LANES), lambda i: (0, 0)),
                   pl.BlockSpec((tm * PACK_SUB, LANES), lambda i: (i, 0))),
        scratch_shapes=[pltpu.VMEM((1, LANES), F32), pltpu.VMEM((tm, tm), BF16)],
        compiler_params=_params(("arbitrary",)),
        name="router",
    )(x, g3, wr, br)


def _max_tiles(t):
    return (2 * t) // E_TM + N_EXPERTS


def dispatch_plan(route, counts_row):
    t = route.shape[0]
    max_tiles = _max_tiles(t)
    ids = route[:, 0:2].astype(jnp.int32)
    ranks = route[:, 4:6].astype(jnp.int32)
    counts = counts_row[0, :N_EXPERTS].astype(jnp.int32)
    padded = ((counts + E_TM - 1) // E_TM) * E_TM
    ends = jnp.cumsum(padded)
    offs = ends - padded
    onehot = ids[:, :, None] == jnp.arange(N_EXPERTS, dtype=jnp.int32)
    dest = (jnp.sum(jnp.where(onehot, offs, 0), axis=-1) + ranks).reshape(2 * t)
    n_tiles = ends[-1] // E_TM
    tile_i = jnp.minimum(jnp.arange(max_tiles, dtype=jnp.int32), n_tiles - 1)
    tile_expert = jnp.sum(ends[None, :] <= tile_i[:, None] * E_TM, axis=1).astype(jnp.int32)
    tokens = jnp.arange(2 * t, dtype=jnp.int32) // 2
    row_token = jnp.zeros((max_tiles * E_TM,), jnp.int32).at[dest].set(
        tokens, unique_indices=True, mode="promise_in_bounds")
    return dest, row_token, tile_expert, n_tiles.reshape(1).astype(jnp.int32)


def _experts_kernel(te_ref, nt_ref, rt_ref, h_hbm, w1_ref, w3_ref, w2_ref, ys_ref,
                    xbuf0, xbuf1, sem, w1b, w3b, w2b, *, max_tiles):
    i = pl.program_id(0)
    nt = nt_ref[0]
    bufs = (xbuf0, xbuf1)

    def row_copy(tok, s, r):
        src = h_hbm.at[pl.ds(pl.multiple_of(tok * PACK_SUB, PACK_SUB), PACK_SUB)]
        return pltpu.make_async_copy(src, bufs[s].at[pl.ds(r * PACK_SUB, PACK_SUB)], sem.at[s])

    def tile_wait(s):
        pltpu.make_async_copy(h_hbm.at[pl.ds(0, E_TM * PACK_SUB)], bufs[s], sem.at[s]).wait()

    @pl.when(i == 0)
    def _():
        def body(r, c):
            src = h_hbm.at[pl.ds(pl.multiple_of(rt_ref[r] * PACK_SUB, PACK_SUB), PACK_SUB)]
            dst = xbuf0.at[pl.ds(pl.multiple_of(r * PACK_SUB, PACK_SUB), PACK_SUB)]
            pltpu.make_async_copy(src, dst, sem.at[0]).start()
            return c

        lax.fori_loop(0, E_TM, body, 0, unroll=8)

    def step(s):
        @pl.when((i == 0) | (te_ref[i] != te_ref[jnp.maximum(i - 1, 0)]))
        def _():
            w1b[...] = w1_ref[...].astype(BF16)
            w3b[...] = w3_ref[...].astype(BF16)
            w2b[...] = w2_ref[...].astype(BF16)

        tile_wait(s)

        base = jnp.minimum(i + 1, max_tiles - 1) * E_TM
        for r in range(E_TM):
            row_copy(rt_ref[base + r], 1 - s, r).start(priority=r % 2)

        lo, hi = _unpack_rows(bufs[s], E_TM)
        hb = jnp.concatenate([v.astype(BF16) for v in lo + hi], axis=1)
        a1 = jnp.dot(hb, w1b[...], preferred_element_type=F32)
        a3 = jnp.dot(hb, w3b[...], preferred_element_type=F32)
        hid = (a1 / (1.0 + jnp.exp(-a1))) * a3
        _pack_rows(jnp.dot(hid.astype(BF16), w2b[...], preferred_element_type=F32), ys_ref)

        @pl.when(i == nt - 1)
        def _():
            tile_wait(1 - s)

    for s in range(2):
        @pl.when((i < nt) & (i % 2 == s))
        def _(s=s):
            step(s)

    @pl.when(i >= nt)
    def _():
        ys_ref[...] = jnp.zeros(ys_ref.shape, ys_ref.dtype)


def experts(hpk, w1, w3, w2, l, tile_expert, n_tiles, row_token):
    d = w1.shape[-2]
    max_tiles = row_token.shape[0] // E_TM
    blk = E_TM * PACK_SUB
    grid_spec = pltpu.PrefetchScalarGridSpec(
        num_scalar_prefetch=3,
        grid=(max_tiles,),
        in_specs=[pl.BlockSpec(memory_space=pl.ANY),
                  pl.BlockSpec((None, None, d, D_EXPERT), lambda i, te, nt, rt: (l, te[i], 0, 0)),
                  pl.BlockSpec((None, None, d, D_EXPERT), lambda i, te, nt, rt: (l, te[i], 0, 0)),
                  pl.BlockSpec((None, None, D_EXPERT, d), lambda i, te, nt, rt: (l, te[i], 0, 0))],
        out_specs=pl.BlockSpec((blk, LANES), lambda i, te, nt, rt: (i, 0)),
        scratch_shapes=[pltpu.VMEM((blk, LANES), jnp.uint32),
                        pltpu.VMEM((blk, LANES), jnp.uint32),
                        pltpu.SemaphoreType.DMA((2,)),
                        pltpu.VMEM((d, D_EXPERT), BF16),
                        pltpu.VMEM((d, D_EXPERT), BF16),
                        pltpu.VMEM((D_EXPERT, d), BF16)],
    )
    return pl.pallas_call(
        functools.partial(_experts_kernel, max_tiles=max_tiles),
        out_shape=jax.ShapeDtypeStruct((max_tiles * blk, LANES), jnp.uint32),
        grid_spec=grid_spec,
        compiler_params=_params(("arbitrary",)),
        name="experts",
    )(tile_expert, n_tiles, row_token, hpk, w1, w3, w2)


def _combine_kernel(dest_ref, ys_hbm, x_ref, route_ref, o_ref, ybuf0, ybuf1, sem):
    i = pl.program_id(0)
    n = pl.num_programs(0)
    bufs = (ybuf0, ybuf1)

    def row_copy(row, s, k, r):
        src = ys_hbm.at[pl.ds(pl.multiple_of(row * PACK_SUB, PACK_SUB), PACK_SUB)]
        return pltpu.make_async_copy(src, bufs[s].at[k, pl.ds(r * PACK_SUB, PACK_SUB)], sem.at[s])

    def tile_wait(s):
        for k in range(2):
            pltpu.make_async_copy(ys_hbm.at[pl.ds(0, C_TM * PACK_SUB)], bufs[s].at[k],
                                  sem.at[s]).wait()

    @pl.when(i == 0)
    def _():
        def body(r, c):
            for k in range(2):
                src = ys_hbm.at[pl.ds(pl.multiple_of(dest_ref[2 * r + k] * PACK_SUB, PACK_SUB),
                                      PACK_SUB)]
                dst = ybuf0.at[k, pl.ds(pl.multiple_of(r * PACK_SUB, PACK_SUB), PACK_SUB)]
                pltpu.make_async_copy(src, dst, sem.at[0]).start()
            return c

        lax.fori_loop(0, C_TM, body, 0, unroll=4)

    def step(s):
        tile_wait(s)
        base = 2 * jnp.minimum(i + 1, n - 1) * C_TM
        for r in range(C_TM):
            for k in range(2):
                row_copy(dest_ref[base + 2 * r + k], 1 - s, k, r).start(priority=k)

        route = route_ref[...]
        lane = lax.broadcasted_iota(jnp.int32, route.shape, 1)
        w1 = jnp.sum(jnp.where(lane == 2, route, 0.0), axis=-1, keepdims=True)
        w2 = jnp.sum(jnp.where(lane == 3, route, 0.0), axis=-1, keepdims=True)
        lo1, hi1 = _unpack_rows(bufs[s].at[0], C_TM)
        lo2, hi2 = _unpack_rows(bufs[s].at[1], C_TM)
        half = x_ref.shape[1] // 2
        for c in range(PACK_SUB):
            for off, y1, y2 in ((0, lo1, lo2), (half, hi1, hi2)):
                cols = slice(off + c * LANES, off + (c + 1) * LANES)
                o_ref[:, cols] = x_ref[:, cols] + w1 * y1[c] + w2 * y2[c]

        @pl.when(i == n - 1)
        def _():
            tile_wait(1 - s)

    for s in range(2):
        @pl.when(i % 2 == s)
        def _(s=s):
            step(s)


def combine(ys, x, route, dest):
    t, d = x.shape
    blk = C_TM * PACK_SUB
    grid_spec = pltpu.PrefetchScalarGridSpec(
        num_scalar_prefetch=1,
        grid=(t // C_TM,),
        in_specs=[pl.BlockSpec(memory_space=pl.ANY),
                  pl.BlockSpec((C_TM, d), lambda i, dest: (i, 0)),
                  pl.BlockSpec((C_TM, LANES), lambda i, dest: (i, 0))],
        out_specs=pl.BlockSpec((C_TM, d), lambda i, dest: (i, 0)),
        scratch_shapes=[pltpu.VMEM((2, blk, LANES), jnp.uint32),
                        pltpu.VMEM((2, blk, LANES), jnp.uint32),
                        pltpu.SemaphoreType.DMA((2,))],
    )
    return pl.pallas_call(
        _combine_kernel,
        out_shape=jax.ShapeDtypeStruct((t, d), F32),
        grid_spec=grid_spec,
        compiler_params=_params(("arbitrary",)),
        name="combine",
    )(dest, ys, x, route)


def kernel(x, mem, norm_mix, w_in, rel_bias_a, t5_table, diff_lambda, diff_subln, w_out,
           norm_cross, norm_mem, w_xq, w_xkv, w_xo, norm_ffn, w_group, b_group,
           w_expert, b_expert, w1, w3, w2, norm_final):
    batch, seq, d = x.shape
    t = batch * seq
    xf = x.reshape(t, d)
    memf = mem.reshape(batch * MEM_LEN, d)

    g_mix = norm_mix.reshape(DEPTH, 1, d)
    g_cross = norm_cross.reshape(DEPTH, 1, d)
    g_mem = norm_mem.reshape(DEPTH, 1, d)
    g_ffn = norm_ffn.reshape(DEPTH, 1, d)
    g_final = norm_final.reshape(1, 1, d)
    subln3 = diff_subln.reshape(DEPTH, 1, HEAD_DIM)

    in_scale = jnp.concatenate([
        jnp.full((A_WIDTH,), HEAD_DIM ** -0.5 * LOG2E, F32), jnp.ones((2 * A_WIDTH,), F32),
        jnp.full((B_WIDTH,), B_QK_DIM ** -0.5 * LOG2E, F32), jnp.ones((2 * B_WIDTH,), F32)]
    ).reshape(1, IN_WIDTH)
    xq_scale = jnp.full((1, d), X_HEAD_DIM ** -0.5, F32)

    far_b, bias_b = t5_bias_tables(t5_table)
    bias_a = band_bias_tables(rel_bias_a)

    pad = LANES - N_EXPERTS - N_GROUPS
    wr = jnp.concatenate([w_expert, w_group, jnp.zeros((DEPTH, d, pad), F32)], axis=-1)
    br = jnp.concatenate([b_expert, b_group, jnp.zeros((DEPTH, pad), F32)],
                         axis=-1).reshape(DEPTH, 1, LANES)

    for l in range(DEPTH):
        lam_init = 0.8 - 0.6 * math.exp(-0.3 * l)
        proj = matmul(xf, w_in, l, BF16, norm_gain=g_mix, colscale=in_scale)
        oa = attn_a(proj, bias_a, l, batch)
        ob = attn_b(proj, far_b, bias_b, diff_lambda, subln3, l, lam_init, batch)
        xf = matmul([oa, ob], w_out, l, F32, resid=xf)
        q = matmul(xf, w_xq, l, BF16, norm_gain=g_cross, colscale=xq_scale, tn=2048)
        kv = matmul(memf, w_xkv, l, BF16, norm_gain=g_mem)
        xo = xattn(q, kv, batch)
        xf = matmul(xo, w_xo, l, F32, resid=xf)
        route, counts, hpk = router(xf, g_ffn, wr, br, l)
        dest, row_token, tile_expert, n_tiles = dispatch_plan(route, counts)
        ys = experts(hpk, w1, w3, w2, l, tile_expert, n_tiles, row_token)
        xf = combine(ys, xf, route, dest)
    out = rmsnorm(xf, g_final, 0, F32)
    return out.reshape(batch, seq, d)
```

```python
import functools
import math

import jax
import jax.numpy as jnp
import numpy as np
from jax import lax
from jax.experimental import pallas as pl
from jax.experimental.pallas import tpu as pltpu

D_MODEL = 2048
SEQ = 4096
DEPTH = 4
CHUNK = 64
LEFT_CHUNKS = 8
HEAD_DIM = 128
A_HEADS = 8
B_HEADS = 8
B_QK_DIM = 64
A_WIDTH = 1024
B_WIDTH = 1024
IN_WIDTH = 6144
REL_CLIP = 128
T5_BUCKETS = 32
T5_MAX_DIST = 512
MEM_LEN = 256
X_HEADS = 4
X_HEAD_DIM = 512
N_GROUPS = 4
EXPERTS_PER_GROUP = 8
N_EXPERTS = 32
D_EXPERT = 256
EPS = 1e-6
NEG_INF = -1e30

LANES = 128
VMEM_LIMIT = 56 * 1024 * 1024

BF16 = jnp.bfloat16
F32 = jnp.float32

A_QB = 256
A_WIN = A_QB + LEFT_CHUNKS * CHUNK
B_QB = 256
B_NEAR = 3
B_HP = 4
A_HP = 4
LOG2E = math.log2(math.e)
X_QB = 512

E_TM = 256
C_TM = 256
PACK_SUB = 8


def _params(sem):
    return pltpu.CompilerParams(dimension_semantics=sem, vmem_limit_bytes=VMEM_LIMIT)


def _rmsnorm_kernel(x_ref, g_ref, o_ref):
    x = x_ref[...]
    ms = jnp.mean(x * x, axis=-1, keepdims=True)
    o_ref[...] = (x * lax.rsqrt(ms + EPS) * g_ref[...]).astype(o_ref.dtype)


def rmsnorm(x, g3, l, out_dtype, tm=512):
    m, d = x.shape
    tm = min(tm, m)
    return pl.pallas_call(
        _rmsnorm_kernel,
        out_shape=jax.ShapeDtypeStruct((m, d), out_dtype),
        grid=(m // tm,),
        in_specs=[pl.BlockSpec((tm, d), lambda i: (i, 0)),
                  pl.BlockSpec((None, 1, d), lambda i: (l, 0, 0))],
        out_specs=pl.BlockSpec((tm, d), lambda i: (i, 0)),
        compiler_params=_params(("parallel",)),
        name="rmsnorm",
    )(x, g3)


def _matmul_kernel(*refs, n_parts, has_norm, has_scale, has_resid):
    a_refs, w_ref = refs[:n_parts], refs[n_parts]
    k = n_parts + 1
    g_ref = s_ref = r_ref = None
    if has_norm:
        g_ref = refs[k]; k += 1
    if has_scale:
        s_ref = refs[k]; k += 1
    if has_resid:
        r_ref = refs[k]; k += 1
    o_ref, wbf_ref = refs[k], refs[k + 1]

    @pl.when(pl.program_id(1) == 0)
    def _():
        wbf_ref[...] = w_ref[...].astype(BF16)

    acc = None
    k0 = 0
    for a_ref in a_refs:
        kp = a_ref.shape[1]
        a = a_ref[...]
        if has_norm:
            ms = jnp.mean(a * a, axis=-1, keepdims=True)
            a = (a * lax.rsqrt(ms + EPS) * g_ref[...]).astype(BF16)
        part = jnp.dot(a, wbf_ref[k0:k0 + kp, :], preferred_element_type=F32)
        acc = part if acc is None else acc + part
        k0 += kp
    if has_scale:
        acc = acc * s_ref[...]
    if has_resid:
        acc = acc + r_ref[...]
    o_ref[...] = acc.astype(o_ref.dtype)


def matmul(a_parts, w, l, out_dtype, norm_gain=None, colscale=None, resid=None,
           tm=512, tn=1024):
    if not isinstance(a_parts, (list, tuple)):
        a_parts = [a_parts]
    assert norm_gain is None or len(a_parts) == 1
    m = a_parts[0].shape[0]
    k = sum(a.shape[1] for a in a_parts)
    n = w.shape[-1]
    assert w.shape[-2] == k
    tm = min(tm, m)
    tn = min(tn, n)
    in_specs = [pl.BlockSpec((tm, a.shape[1]), lambda j, i: (i, 0)) for a in a_parts]
    in_specs.append(pl.BlockSpec((None, k, tn), lambda j, i: (l, 0, j)))
    args = list(a_parts) + [w]
    if norm_gain is not None:
        in_specs.append(pl.BlockSpec((None, 1, k), lambda j, i: (l, 0, 0)))
        args.append(norm_gain)
    if colscale is not None:
        in_specs.append(pl.BlockSpec((1, tn), lambda j, i: (0, j)))
        args.append(colscale)
    if resid is not None:
        in_specs.append(pl.BlockSpec((tm, tn), lambda j, i: (i, j)))
        args.append(resid)
    return pl.pallas_call(
        functools.partial(_matmul_kernel, n_parts=len(a_parts), has_norm=norm_gain is not None,
                          has_scale=colscale is not None, has_resid=resid is not None),
        out_shape=jax.ShapeDtypeStruct((m, n), out_dtype),
        grid=(n // tn, m // tm),
        in_specs=in_specs,
        out_specs=pl.BlockSpec((tm, tn), lambda j, i: (i, j)),
        scratch_shapes=[pltpu.VMEM((k, tn), BF16)],
        compiler_params=_params(("parallel", "arbitrary")),
        name="matmul",
    )(*args)


def _toeplitz(v, rows, cols):
    hh, n = v.shape
    assert n == rows + cols - 1
    vp = jnp.pad(v, ((0, 0), (0, 1)))
    skew = jnp.tile(vp, (1, rows))[:, :rows * n].reshape(hh, rows, n)
    return skew[:, :, rows - 1:]


def _attn_a_kernel(q_ref, k_ref, v_ref, b0_ref, b1_ref, b2_ref, o_ref, vt_ref, st_ref):
    i = pl.program_id(2)
    qb = A_QB
    n_kt = A_WIN // qb
    bias_refs = (b0_ref, b1_ref, b2_ref)

    @pl.when(i == 0)
    def _():
        for hh in range(A_HP):
            for c in range(SEQ // qb):
                blk = v_ref[c * qb:(c + 1) * qb, hh * HEAD_DIM:(hh + 1) * HEAD_DIM]
                vt_ref[hh, c] = blk.astype(F32).T.astype(BF16)

    t0 = jnp.maximum(i - (n_kt - 1), 0)
    start = pl.multiple_of(t0 * qb, qb)
    qts = [q_ref[:, hh * HEAD_DIM:(hh + 1) * HEAD_DIM].astype(F32).T.astype(BF16)
           for hh in range(A_HP)]
    for hh in range(A_HP):
        st_ref[hh] = jnp.dot(k_ref[pl.ds(start, A_WIN), hh * HEAD_DIM:(hh + 1) * HEAD_DIM],
                             qts[hh], preferred_element_type=F32)
    for hh in range(A_HP):
        st = st_ref[hh] + jnp.concatenate([r[hh] for r in bias_refs], axis=0)
        m = jnp.max(st, axis=0, keepdims=True)
        p = jnp.exp2(st - m)
        denom = jnp.sum(p, axis=0, keepdims=True)
        pb = p.astype(BF16)
        ot = None
        for u in range(n_kt):
            part = jnp.dot(vt_ref[hh, t0 + u], pb[u * qb:(u + 1) * qb],
                           preferred_element_type=F32)
            ot = part if ot is None else ot + part
        o_ref[:, hh * HEAD_DIM:(hh + 1) * HEAD_DIM] = (ot / denom).T.astype(o_ref.dtype)


def attn_a(proj, bias_a, l, batch):
    t = proj.shape[0]
    nq = SEQ // A_QB
    n_kt = A_WIN // A_QB
    wa = A_HP * HEAD_DIM
    ncol = A_WIDTH // wa

    def bias_spec(u):
        return pl.BlockSpec(
            (None, A_HP, None, A_QB, A_QB),
            lambda b, h, i: (l, h, (n_kt - 1) - jnp.minimum(i, n_kt - 1) + u, 0, 0))

    return pl.pallas_call(
        _attn_a_kernel,
        out_shape=jax.ShapeDtypeStruct((t, A_WIDTH), BF16),
        grid=(batch, A_HEADS // A_HP, nq),
        in_specs=[
            pl.BlockSpec((A_QB, wa), lambda b, h, i: (b * nq + i, h)),
            pl.BlockSpec((SEQ, wa), lambda b, h, i: (b, ncol + h)),
            pl.BlockSpec((SEQ, wa), lambda b, h, i: (b, 2 * ncol + h)),
            bias_spec(0), bias_spec(1), bias_spec(2),
        ],
        out_specs=pl.BlockSpec((A_QB, wa), lambda b, h, i: (b * nq + i, h)),
        scratch_shapes=[pltpu.VMEM((A_HP, SEQ // A_QB, HEAD_DIM, A_QB), BF16),
                        pltpu.VMEM((A_HP, A_WIN, A_QB), F32)],
        compiler_params=_params(("arbitrary", "arbitrary", "arbitrary")),
        name="attn_a",
    )(proj, proj, proj, bias_a, bias_a, bias_a)


def band_bias_tables(rel_tables):
    depth, _, heads = rel_tables.shape
    lead = LEFT_CHUNKS * CHUNK
    n_tiles = (A_WIN + lead) // A_QB
    kk = np.arange(A_QB)[:, None]
    r = np.arange(A_QB)[None, :]
    n = np.arange(2 * A_QB - 1)
    tables = rel_tables.astype(F32)
    tiles = []
    for t in range(n_tiles):
        idx = np.clip(t * A_QB + (A_QB - 1 - n) - lead, -REL_CLIP, REL_CLIP) + REL_CLIP
        v = tables[:, idx, :].transpose(0, 2, 1).reshape(depth * heads, -1)
        dchunk = (t * A_QB + kk - lead) // CHUNK - r // CHUNK
        valid = (dchunk >= -LEFT_CHUNKS) & (dchunk <= 0)
        tiles.append(jnp.where(valid[None], _toeplitz(v, A_QB, A_QB) * LOG2E, NEG_INF))
    return jnp.stack(tiles, axis=1).reshape(depth, heads, n_tiles, A_QB, A_QB)


def _attn_b_kernel(far_ref, q_ref, k_ref, v_ref, bias_ref, lam_ref, g_ref, o_ref,
                   qqt_ref, vt_ref, m_ref, l_ref, acc_ref, st_ref, *, lam_init):
    hp = pl.program_id(1)
    i = pl.program_id(2)
    qb = B_QB

    @pl.when(i == 0)
    def _():
        for hh in range(B_HP):
            for c in range(SEQ // qb):
                blk = v_ref[c * qb:(c + 1) * qb, hh * HEAD_DIM:(hh + 1) * HEAD_DIM]
                vt_ref[hh, c] = blk.astype(F32).T.astype(BF16)

    for hh in range(B_HP):
        qt = q_ref[:, hh * HEAD_DIM:(hh + 1) * HEAD_DIM].astype(F32).T
        row = lax.broadcasted_iota(jnp.int32, qt.shape, 0)
        qqt_ref[hh, :, 0:qb] = jnp.where(row < B_QK_DIM, qt, 0.0).astype(BF16)
        qqt_ref[hh, :, qb:2 * qb] = jnp.where(row >= B_QK_DIM, qt, 0.0).astype(BF16)
        m_ref[hh] = jnp.full((1, 2 * qb), NEG_INF, F32)
        l_ref[hh] = jnp.zeros((1, 2 * qb), F32)
        acc_ref[hh] = jnp.zeros((HEAD_DIM, 2 * qb), F32)

    def tile(j, d, nk=1):
        off = pl.multiple_of(j * qb, qb)
        for hh in range(B_HP):
            st_ref[hh, 0:nk * qb] = jnp.dot(
                k_ref[pl.ds(off, nk * qb), hh * HEAD_DIM:(hh + 1) * HEAD_DIM], qqt_ref[hh],
                preferred_element_type=F32)
        for hh in range(B_HP):
            st = st_ref[hh, 0:nk * qb]
            if d is None:
                shift = far_ref[hp * B_HP + hh]
            else:
                b = jnp.concatenate([bias_ref[hh, du] for du in d], axis=0)
                st = jnp.concatenate([st[:, 0:qb] + b, st[:, qb:2 * qb] + b], axis=1)
                shift = 0.0
            m_old = m_ref[hh]
            m_new = jnp.maximum(m_old, jnp.max(st, axis=0, keepdims=True) + shift)
            alpha = jnp.exp2(m_old - m_new)
            p = jnp.exp2(st - (m_new - shift))
            l_ref[hh] = alpha * l_ref[hh] + jnp.sum(p, axis=0, keepdims=True)
            pb = p.astype(BF16)
            pv = None
            for u in range(nk):
                part = jnp.dot(vt_ref[hh, j + u], pb[u * qb:(u + 1) * qb],
                               preferred_element_type=F32)
                pv = part if pv is None else pv + part
            acc_ref[hh] = alpha * acc_ref[hh] + pv
            m_ref[hh] = m_new

    n_far = jnp.maximum(i - (B_NEAR - 1), 0)

    def far_body(jj, c):
        tile(2 * jj, None, nk=2)
        return c

    lax.fori_loop(0, n_far // 2, far_body, 0)

    @pl.when(n_far % 2 == 1)
    def _():
        tile(n_far - 1, None)

    assert B_NEAR == 3
    @pl.when(i >= 2)
    def _():
        tile(i - 2, (2, 1), nk=2)

    @pl.when(i == 1)
    def _():
        tile(0, (1,))

    tile(i, (0,))

    lv = lam_ref[...]
    lam = (jnp.exp(jnp.sum(lv[0:1] * lv[1:2], axis=-1, keepdims=True))
           - jnp.exp(jnp.sum(lv[2:3] * lv[3:4], axis=-1, keepdims=True)) + lam_init)
    for hh in range(B_HP):
        o = acc_ref[hh] / l_ref[hh]
        od = o[:, 0:qb] - lam * o[:, qb:2 * qb]
        ms = jnp.mean(od * od, axis=0, keepdims=True)
        y = (od * lax.rsqrt(ms + EPS)).T * g_ref[...]
        o_ref[:, hh * HEAD_DIM:(hh + 1) * HEAD_DIM] = (y * (1.0 - lam_init)).astype(o_ref.dtype)


def attn_b(proj, far_bias, bias_b, lam_vecs, subln3, l, lam_init, batch):
    t = proj.shape[0]
    nq = SEQ // B_QB
    wb = B_HP * HEAD_DIM
    qcol = 3 * A_WIDTH // wb
    ncol = B_WIDTH // wb
    return pl.pallas_call(
        functools.partial(_attn_b_kernel, lam_init=lam_init),
        out_shape=jax.ShapeDtypeStruct((t, B_WIDTH), BF16),
        grid=(batch, B_HEADS // B_HP, nq),
        in_specs=[
            pl.BlockSpec(memory_space=pltpu.SMEM),
            pl.BlockSpec((B_QB, wb), lambda b, h, i: (b * nq + i, qcol + h)),
            pl.BlockSpec((SEQ, wb), lambda b, h, i: (b, qcol + ncol + h)),
            pl.BlockSpec((SEQ, wb), lambda b, h, i: (b, qcol + 2 * ncol + h)),
            pl.BlockSpec((B_HP, B_NEAR, B_QB, B_QB), lambda b, h, i: (h, 0, 0, 0)),
            pl.BlockSpec((None, 4, B_QK_DIM), lambda b, h, i: (l, 0, 0)),
            pl.BlockSpec((None, 1, HEAD_DIM), lambda b, h, i: (l, 0, 0)),
        ],
        out_specs=pl.BlockSpec((B_QB, wb), lambda b, h, i: (b * nq + i, h)),
        scratch_shapes=[pltpu.VMEM((B_HP, HEAD_DIM, 2 * B_QB), BF16),
                        pltpu.VMEM((B_HP, SEQ // B_QB, HEAD_DIM, B_QB), BF16),
                        pltpu.VMEM((B_HP, 1, 2 * B_QB), F32),
                        pltpu.VMEM((B_HP, 1, 2 * B_QB), F32),
                        pltpu.VMEM((B_HP, HEAD_DIM, 2 * B_QB), F32),
                        pltpu.VMEM((B_HP, 2 * B_QB, 2 * B_QB), F32)],
        compiler_params=_params(("arbitrary", "arbitrary", "arbitrary")),
        name="attn_b",
    )(far_bias, proj, proj, proj, bias_b, lam_vecs, subln3)


def _t5_bucket(rel):
    half = T5_BUCKETS // 2
    max_exact = half // 2
    ret = jnp.where(rel > 0, half, 0)
    n = jnp.abs(rel)
    n_f = jnp.maximum(n, max_exact).astype(jnp.float32)
    large = max_exact + (jnp.log(n_f / max_exact) / math.log(T5_MAX_DIST / max_exact)
                         * (half - max_exact)).astype(jnp.int32)
    large = jnp.minimum(large, half - 1)
    return ret + jnp.where(n < max_exact, n, large)


def t5_bias_tables(t5_table):
    assert B_NEAR * B_QB - (B_QB - 1) >= T5_MAX_DIST
    lo = -(B_NEAR * B_QB)
    bias1d = t5_table.astype(F32)[_t5_bucket(jnp.arange(lo, B_QB))].T
    far = t5_table.astype(F32)[_t5_bucket(jnp.array([-(SEQ - 1)]))][0]
    u = np.arange(2 * B_QB - 1)
    kk = np.arange(B_QB)[:, None]
    qq = np.arange(B_QB)[None, :]
    tiles = []
    for d in range(B_NEAR):
        t = _toeplitz(bias1d[:, (B_QB - 1 - u - d * B_QB) - lo], B_QB, B_QB)
        if d == 0:
            t = jnp.where(((kk // CHUNK) <= (qq // CHUNK))[None], t * LOG2E, NEG_INF)
        else:
            t = t * LOG2E
        tiles.append(t)
    return far * LOG2E, jnp.stack(tiles, axis=1)


def _xattn_kernel(q_ref, k_ref, v_ref, o_ref):
    s = lax.dot_general(q_ref[...], k_ref[...], (((1,), (1,)), ((), ())),
                        preferred_element_type=F32)
    m = jnp.max(s, axis=-1, keepdims=True)
    p = jnp.exp(s - m)
    denom = jnp.sum(p, axis=-1, keepdims=True)
    o = jnp.dot(p.astype(BF16), v_ref[...], preferred_element_type=F32)
    o_ref[...] = (o / denom).astype(o_ref.dtype)


def xattn(q, kv, batch):
    t = q.shape[0]
    nq = SEQ // X_QB
    return pl.pallas_call(
        _xattn_kernel,
        out_shape=jax.ShapeDtypeStruct((t, D_MODEL), BF16),
        grid=(batch, X_HEADS, nq),
        in_specs=[
            pl.BlockSpec((X_QB, X_HEAD_DIM), lambda b, h, i: (b * nq + i, h)),
            pl.BlockSpec((MEM_LEN, X_HEAD_DIM), lambda b, h, i: (b, h)),
            pl.BlockSpec((MEM_LEN, X_HEAD_DIM), lambda b, h, i: (b, X_HEADS + h)),
        ],
        out_specs=pl.BlockSpec((X_QB, X_HEAD_DIM), lambda b, h, i: (b * nq + i, h)),
        compiler_params=_params(("parallel", "parallel", "parallel")),
        name="xattn",
    )(q, kv, kv)


def _pack_rows(v, out_ref):
    rows, width = v.shape
    half = width // 2
    lo = lax.bitcast_convert_type(v[:, :half].astype(BF16).astype(F32), jnp.uint32)
    hi = lax.bitcast_convert_type(v[:, half:].astype(BF16).astype(F32), jnp.uint32)
    u = (lo >> 16) | hi
    for c in range(half // LANES):
        out_ref[pl.ds(c, rows, stride=PACK_SUB), :] = u[:, c * LANES:(c + 1) * LANES]


def _unpack_rows(ref, rows):
    lo, hi = [], []
    for c in range(PACK_SUB):
        u = ref[pl.ds(c, rows, stride=PACK_SUB), :]
        lo.append(lax.bitcast_convert_type(u << 16, F32))
        hi.append(lax.bitcast_convert_type(u & jnp.uint32(0xFFFF0000), F32))
    return lo, hi


def _router_kernel(x_ref, g_ref, wr_ref, br_ref, route_ref, cnt_ref, hpk_ref, carry_ref, tri_ref):
    tm = x_ref.shape[0]

    @pl.when(pl.program_id(0) == 0)
    def _():
        carry_ref[...] = jnp.zeros(carry_ref.shape, F32)
        r = lax.broadcasted_iota(jnp.int32, (tm, tm), 0)
        c = lax.broadcasted_iota(jnp.int32, (tm, tm), 1)
        tri_ref[...] = jnp.where(c < r, 1.0, 0.0).astype(BF16)

    x = x_ref[...]
    ms = jnp.mean(x * x, axis=-1, keepdims=True)
    hn = x * lax.rsqrt(ms + EPS) * g_ref[...]
    _pack_rows(hn, hpk_ref)
    hb = hn.astype(BF16)
    logits = jnp.dot(hb, wr_ref[...].astype(BF16), preferred_element_type=F32) + br_ref[...]

    lane = lax.broadcasted_iota(jnp.int32, logits.shape, 1)
    lane_f = lane.astype(F32)
    big = jnp.float32(1e9)

    def top1(mask):
        v = jnp.where(mask, logits, -jnp.inf)
        vmax = jnp.max(v, axis=-1, keepdims=True)
        idx = jnp.min(jnp.where(mask & (v == vmax), lane_f, big), axis=-1, keepdims=True)
        return vmax, idx

    gmask = (lane >= N_EXPERTS) & (lane < N_EXPERTS + N_GROUPS)
    gmax, gidx = top1(gmask)
    gsum = jnp.sum(jnp.where(gmask, jnp.exp(logits - gmax), 0.0), axis=-1, keepdims=True)
    g_gate = 1.0 / gsum
    g_sel = gidx.astype(jnp.int32) - N_EXPERTS

    emask = (lane < N_EXPERTS) & ((lane // EXPERTS_PER_GROUP) == g_sel)
    v1, i1 = top1(emask)
    v2, i2 = top1(emask & (lane_f != i1))
    e2 = jnp.exp(v2 - v1)
    w1 = g_gate / (1.0 + e2)
    w2 = g_gate * e2 / (1.0 + e2)

    oh1 = jnp.where(lane_f == i1, 1.0, 0.0)
    oh2 = jnp.where(lane_f == i2, 1.0, 0.0)
    oh = oh1 + oh2
    before = jnp.dot(tri_ref[...], oh.astype(BF16), preferred_element_type=F32) + carry_ref[...]
    r1 = jnp.sum(oh1 * before, axis=-1, keepdims=True)
    r2 = jnp.sum(oh2 * before, axis=-1, keepdims=True)
    carry_ref[...] = carry_ref[...] + jnp.sum(oh, axis=0, keepdims=True)
    cnt_ref[...] = carry_ref[...]

    route = jnp.zeros(logits.shape, F32)
    for col, val in enumerate((i1, i2, w1, w2, r1, r2)):
        route = jnp.where(lane == col, val, route)
    route_ref[...] = route


def router(x, g3, wr, br, l, tm=512):
    t, d = x.shape
    return pl.pallas_call(
        _router_kernel,
        out_shape=(jax.ShapeDtypeStruct((t, LANES), F32),
                   jax.ShapeDtypeStruct((1, LANES), F32),
                   jax.ShapeDtypeStruct((t * PACK_SUB, LANES), jnp.uint32)),
        grid=(t // tm,),
        in_specs=[pl.BlockSpec((tm, d), lambda i: (i, 0)),
                  pl.BlockSpec((None, 1, d), lambda i: (l, 0, 0)),
                  pl.BlockSpec((None, d, LANES), lambda i: (l, 0, 0)),
                  pl.BlockSpec((None, 1, LANES), lambda i: (l, 0, 0))],
        out_specs=(pl.BlockSpec((tm, LANES), lambda i: (i, 0)),
                   pl.BlockSpec((1, LANES), lambda i: (0, 0)),
                   pl.BlockSpec((tm * PACK_SUB, LANES), lambda i: (i, 0))),
        scratch_shapes=[pltpu.VMEM((1, LANES), F32), pltpu.VMEM((tm, tm), BF16)],
        compiler_params=_params(("arbitrary",)),
        name="router",
    )(x, g3, wr, br)


def _max_tiles(t):
    return (2 * t) // E_TM + N_EXPERTS


def dispatch_plan(route, counts_row):
    t = route.shape[0]
    max_tiles = _max_tiles(t)
    ids = route[:, 0:2].astype(jnp.int32)
    ranks = route[:, 4:6].astype(jnp.int32)
    counts = counts_row[0, :N_EXPERTS].astype(jnp.int32)
    padded = ((counts + E_TM - 1) // E_TM) * E_TM
    ends = jnp.cumsum(padded)
    offs = ends - padded
    onehot = ids[:, :, None] == jnp.arange(N_EXPERTS, dtype=jnp.int32)
    dest = (jnp.sum(jnp.where(onehot, offs, 0), axis=-1) + ranks).reshape(2 * t)
    n_tiles = ends[-1] // E_TM
    tile_i = jnp.minimum(jnp.arange(max_tiles, dtype=jnp.int32), n_tiles - 1)
    tile_expert = jnp.sum(ends[None, :] <= tile_i[:, None] * E_TM, axis=1).astype(jnp.int32)
    tokens = jnp.arange(2 * t, dtype=jnp.int32) // 2
    row_token = jnp.zeros((max_tiles * E_TM,), jnp.int32).at[dest].set(
        tokens, unique_indices=True, mode="promise_in_bounds")
    return dest, row_token, tile_expert, n_tiles.reshape(1).astype(jnp.int32)


def _experts_kernel(te_ref, nt_ref, rt_ref, h_hbm, w1_ref, w3_ref, w2_ref, ys_ref,
                    xbuf0, xbuf1, sem, w1b, w3b, w2b, *, max_tiles):
    i = pl.program_id(0)
    nt = nt_ref[0]
    bufs = (xbuf0, xbuf1)

    def row_copy(tok, s, r):
        src = h_hbm.at[pl.ds(pl.multiple_of(tok * PACK_SUB, PACK_SUB), PACK_SUB)]
        return pltpu.make_async_copy(src, bufs[s].at[pl.ds(r * PACK_SUB, PACK_SUB)], sem.at[s])

    def tile_wait(s):
        pltpu.make_async_copy(h_hbm.at[pl.ds(0, E_TM * PACK_SUB)], bufs[s], sem.at[s]).wait()

    @pl.when(i == 0)
    def _():
        def body(r, c):
            src = h_hbm.at[pl.ds(pl.multiple_of(rt_ref[r] * PACK_SUB, PACK_SUB), PACK_SUB)]
            dst = xbuf0.at[pl.ds(pl.multiple_of(r * PACK_SUB, PACK_SUB), PACK_SUB)]
            pltpu.make_async_copy(src, dst, sem.at[0]).start()
            return c

        lax.fori_loop(0, E_TM, body, 0, unroll=8)

    def step(s):
        @pl.when((i == 0) | (te_ref[i] != te_ref[jnp.maximum(i - 1, 0)]))
        def _():
            w1b[...] = w1_ref[...].astype(BF16)
            w3b[...] = w3_ref[...].astype(BF16)
            w2b[...] = w2_ref[...].astype(BF16)

        tile_wait(s)

        base = jnp.minimum(i + 1, max_tiles - 1) * E_TM
        for r in range(E_TM):
            row_copy(rt_ref[base + r], 1 - s, r).start(priority=r % 2)

        lo, hi = _unpack_rows(bufs[s], E_TM)
        hb = jnp.concatenate([v.astype(BF16) for v in lo + hi], axis=1)
        a1 = jnp.dot(hb, w1b[...], preferred_element_type=F32)
        a3 = jnp.dot(hb, w3b[...], preferred_element_type=F32)
        hid = (a1 / (1.0 + jnp.exp(-a1))) * a3
        _pack_rows(jnp.dot(hid.astype(BF16), w2b[...], preferred_element_type=F32), ys_ref)

        @pl.when(i == nt - 1)
        def _():
            tile_wait(1 - s)

    for s in range(2):
        @pl.when((i < nt) & (i % 2 == s))
        def _(s=s):
            step(s)

    @pl.when(i >= nt)
    def _():
        ys_ref[...] = jnp.zeros(ys_ref.shape, ys_ref.dtype)


def experts(hpk, w1, w3, w2, l, tile_expert, n_tiles, row_token):
    d = w1.shape[-2]
    max_tiles = row_token.shape[0] // E_TM
    blk = E_TM * PACK_SUB
    grid_spec = pltpu.PrefetchScalarGridSpec(
        num_scalar_prefetch=3,
        grid=(max_tiles,),
        in_specs=[pl.BlockSpec(memory_space=pl.ANY),
                  pl.BlockSpec((None, None, d, D_EXPERT), lambda i, te, nt, rt: (l, te[i], 0, 0)),
                  pl.BlockSpec((None, None, d, D_EXPERT), lambda i, te, nt, rt: (l, te[i], 0, 0)),
                  pl.BlockSpec((None, None, D_EXPERT, d), lambda i, te, nt, rt: (l, te[i], 0, 0))],
        out_specs=pl.BlockSpec((blk, LANES), lambda i, te, nt, rt: (i, 0)),
        scratch_shapes=[pltpu.VMEM((blk, LANES), jnp.uint32),
                        pltpu.VMEM((blk, LANES), jnp.uint32),
                        pltpu.SemaphoreType.DMA((2,)),
                        pltpu.VMEM((d, D_EXPERT), BF16),
                        pltpu.VMEM((d, D_EXPERT), BF16),
                        pltpu.VMEM((D_EXPERT, d), BF16)],
    )
    return pl.pallas_call(
        functools.partial(_experts_kernel, max_tiles=max_tiles),
        out_shape=jax.ShapeDtypeStruct((max_tiles * blk, LANES), jnp.uint32),
        grid_spec=grid_spec,
        compiler_params=_params(("arbitrary",)),
        name="experts",
    )(tile_expert, n_tiles, row_token, hpk, w1, w3, w2)


def _combine_kernel(dest_ref, ys_hbm, x_ref, route_ref, o_ref, ybuf0, ybuf1, sem):
    i = pl.program_id(0)
    n = pl.num_programs(0)
    bufs = (ybuf0, ybuf1)

    def row_copy(row, s, k, r):
        src = ys_hbm.at[pl.ds(pl.multiple_of(row * PACK_SUB, PACK_SUB), PACK_SUB)]
        return pltpu.make_async_copy(src, bufs[s].at[k, pl.ds(r * PACK_SUB, PACK_SUB)], sem.at[s])

    def tile_wait(s):
        for k in range(2):
            pltpu.make_async_copy(ys_hbm.at[pl.ds(0, C_TM * PACK_SUB)], bufs[s].at[k],
                                  sem.at[s]).wait()

    @pl.when(i == 0)
    def _():
        def body(r, c):
            for k in range(2):
                src = ys_hbm.at[pl.ds(pl.multiple_of(dest_ref[2 * r + k] * PACK_SUB, PACK_SUB),
                                      PACK_SUB)]
                dst = ybuf0.at[k, pl.ds(pl.multiple_of(r * PACK_SUB, PACK_SUB), PACK_SUB)]
                pltpu.make_async_copy(src, dst, sem.at[0]).start()
            return c

        lax.fori_loop(0, C_TM, body, 0, unroll=4)

    def step(s):
        tile_wait(s)
        base = 2 * jnp.minimum(i + 1, n - 1) * C_TM
        for r in range(C_TM):
            for k in range(2):
                row_copy(dest_ref[base + 2 * r + k], 1 - s, k, r).start(priority=k)

        route = route_ref[...]
        lane = lax.broadcasted_iota(jnp.int32, route.shape, 1)
        w1 = jnp.sum(jnp.where(lane == 2, route, 0.0), axis=-1, keepdims=True)
        w2 = jnp.sum(jnp.where(lane == 3, route, 0.0), axis=-1, keepdims=True)
        lo1, hi1 = _unpack_rows(bufs[s].at[0], C_TM)
        lo2, hi2 = _unpack_rows(bufs[s].at[1], C_TM)
        half = x_ref.shape[1] // 2
        for c in range(PACK_SUB):
            for off, y1, y2 in ((0, lo1, lo2), (half, hi1, hi2)):
                cols = slice(off + c * LANES, off + (c + 1) * LANES)
                o_ref[:, cols] = x_ref[:, cols] + w1 * y1[c] + w2 * y2[c]

        @pl.when(i == n - 1)
        def _():
            tile_wait(1 - s)

    for s in range(2):
        @pl.when(i % 2 == s)
        def _(s=s):
            step(s)


def combine(ys, x, route, dest):
    t, d = x.shape
    blk = C_TM * PACK_SUB
    grid_spec = pltpu.PrefetchScalarGridSpec(
        num_scalar_prefetch=1,
        grid=(t // C_TM,),
        in_specs=[pl.BlockSpec(memory_space=pl.ANY),
                  pl.BlockSpec((C_TM, d), lambda i, dest: (i, 0)),
                  pl.BlockSpec((C_TM, LANES), lambda i, dest: (i, 0))],
        out_specs=pl.BlockSpec((C_TM, d), lambda i, dest: (i, 0)),
        scratch_shapes=[pltpu.VMEM((2, blk, LANES), jnp.uint32),
                        pltpu.VMEM((2, blk, LANES), jnp.uint32),
                        pltpu.SemaphoreType.DMA((2,))],
    )
    return pl.pallas_call(
        _combine_kernel,
        out_shape=jax.ShapeDtypeStruct((t, d), F32),
        grid_spec=grid_spec,
        compiler_params=_params(("arbitrary",)),
        name="combine",
    )(dest, ys, x, route)


def kernel(x, mem, norm_mix, w_in, rel_bias_a, t5_table, diff_lambda, diff_subln, w_out,
           norm_cross, norm_mem, w_xq, w_xkv, w_xo, norm_ffn, w_group, b_group,
           w_expert, b_expert, w1, w3, w2, norm_final):
    batch, seq, d = x.shape
    t = batch * seq
    xf = x.reshape(t, d)
    memf = mem.reshape(batch * MEM_LEN, d)

    g_mix = norm_mix.reshape(DEPTH, 1, d)
    g_cross = norm_cross.reshape(DEPTH, 1, d)
    g_mem = norm_mem.reshape(DEPTH, 1, d)
    g_ffn = norm_ffn.reshape(DEPTH, 1, d)
    g_final = norm_final.reshape(1, 1, d)
    subln3 = diff_subln.reshape(DEPTH, 1, HEAD_DIM)

    in_scale = jnp.concatenate([
        jnp.full((A_WIDTH,), HEAD_DIM ** -0.5 * LOG2E, F32), jnp.ones((2 * A_WIDTH,), F32),
        jnp.full((B_WIDTH,), B_QK_DIM ** -0.5 * LOG2E, F32), jnp.ones((2 * B_WIDTH,), F32)]
    ).reshape(1, IN_WIDTH)
    xq_scale = jnp.full((1, d), X_HEAD_DIM ** -0.5, F32)

    far_b, bias_b = t5_bias_tables(t5_table)
    bias_a = band_bias_tables(rel_bias_a)

    pad = LANES - N_EXPERTS - N_GROUPS
    wr = jnp.concatenate([w_expert, w_group, jnp.zeros((DEPTH, d, pad), F32)], axis=-1)
    br = jnp.concatenate([b_expert, b_group, jnp.zeros((DEPTH, pad), F32)],
                         axis=-1).reshape(DEPTH, 1, LANES)

    for l in range(DEPTH):
        lam_init = 0.8 - 0.6 * math.exp(-0.3 * l)
        proj = matmul(xf, w_in, l, BF16, norm_gain=g_mix, colscale=in_scale, tm=1024)
        oa = attn_a(proj, bias_a, l, batch)
        ob = attn_b(proj, far_b, bias_b, diff_lambda, subln3, l, lam_init, batch)
        xf = matmul([oa, ob], w_out, l, F32, resid=xf)
        q = matmul(xf, w_xq, l, BF16, norm_gain=g_cross, colscale=xq_scale, tn=2048)
        kv = matmul(memf, w_xkv, l, BF16, norm_gain=g_mem)
        xo = xattn(q, kv, batch)
        xf = matmul(xo, w_xo, l, F32, resid=xf)
        route, counts, hpk = router(xf, g_ffn, wr, br, l)
        dest, row_token, tile_expert, n_tiles = dispatch_plan(route, counts)
        ys = experts(hpk, w1, w3, w2, l, tile_expert, n_tiles, row_token)
        xf = combine(ys, xf, route, dest)
    out = rmsnorm(xf, g_final, 0, F32)
    return out.reshape(batch, seq, d)
```

```python
import functools
import math

import jax
import jax.numpy as jnp
import numpy as np
from jax import lax
from jax.experimental import pallas as pl
from jax.experimental.pallas import tpu as pltpu

D_MODEL = 2048
SEQ = 4096
DEPTH = 4
CHUNK = 64
LEFT_CHUNKS = 8
HEAD_DIM = 128
A_HEADS = 8
B_HEADS = 8
B_QK_DIM = 64
A_WIDTH = 1024
B_WIDTH = 1024
IN_WIDTH = 6144
REL_CLIP = 128
T5_BUCKETS = 32
T5_MAX_DIST = 512
MEM_LEN = 256
X_HEADS = 4
X_HEAD_DIM = 512
N_GROUPS = 4
EXPERTS_PER_GROUP = 8
N_EXPERTS = 32
D_EXPERT = 256
EPS = 1e-6
NEG_INF = -1e30

LANES = 128
VMEM_LIMIT = 56 * 1024 * 1024

BF16 = jnp.bfloat16
F32 = jnp.float32

A_QB = 256
A_WIN = A_QB + LEFT_CHUNKS * CHUNK
B_QB = 256
B_NEAR = 3
B_HP = 4
A_HP = 4
LOG2E = math.log2(math.e)
X_QB = 512

E_TM = 256
C_TM = 256
PACK_SUB = 8


def _params(sem):
    return pltpu.CompilerParams(dimension_semantics=sem, vmem_limit_bytes=VMEM_LIMIT)


def _rmsnorm_kernel(x_ref, g_ref, o_ref):
    x = x_ref[...]
    ms = jnp.mean(x * x, axis=-1, keepdims=True)
    o_ref[...] = (x * lax.rsqrt(ms + EPS) * g_ref[...]).astype(o_ref.dtype)


def rmsnorm(x, g3, l, out_dtype, tm=512):
    m, d = x.shape
    tm = min(tm, m)
    return pl.pallas_call(
        _rmsnorm_kernel,
        out_shape=jax.ShapeDtypeStruct((m, d), out_dtype),
        grid=(m // tm,),
        in_specs=[pl.BlockSpec((tm, d), lambda i: (i, 0)),
                  pl.BlockSpec((None, 1, d), lambda i: (l, 0, 0))],
        out_specs=pl.BlockSpec((tm, d), lambda i: (i, 0)),
        compiler_params=_params(("parallel",)),
        name="rmsnorm",
    )(x, g3)


def _matmul_kernel(*refs, n_parts, has_norm, has_scale, has_resid):
    a_refs, w_ref = refs[:n_parts], refs[n_parts]
    k = n_parts + 1
    g_ref = s_ref = r_ref = None
    if has_norm:
        g_ref = refs[k]; k += 1
    if has_scale:
        s_ref = refs[k]; k += 1
    if has_resid:
        r_ref = refs[k]; k += 1
    o_ref, wbf_ref = refs[k], refs[k + 1]

    @pl.when(pl.program_id(1) == 0)
    def _():
        wbf_ref[...] = w_ref[...].astype(BF16)

    acc = None
    k0 = 0
    for a_ref in a_refs:
        kp = a_ref.shape[1]
        a = a_ref[...]
        if has_norm:
            ms = jnp.mean(a * a, axis=-1, keepdims=True)
            a = (a * lax.rsqrt(ms + EPS) * g_ref[...]).astype(BF16)
        part = jnp.dot(a, wbf_ref[k0:k0 + kp, :], preferred_element_type=F32)
        acc = part if acc is None else acc + part
        k0 += kp
    if has_scale:
        acc = acc * s_ref[...]
    if has_resid:
        acc = acc + r_ref[...]
    o_ref[...] = acc.astype(o_ref.dtype)


def matmul(a_parts, w, l, out_dtype, norm_gain=None, colscale=None, resid=None,
           tm=512, tn=1024):
    if not isinstance(a_parts, (list, tuple)):
        a_parts = [a_parts]
    assert norm_gain is None or len(a_parts) == 1
    m = a_parts[0].shape[0]
    k = sum(a.shape[1] for a in a_parts)
    n = w.shape[-1]
    assert w.shape[-2] == k
    tm = min(tm, m)
    tn = min(tn, n)
    in_specs = [pl.BlockSpec((tm, a.shape[1]), lambda j, i: (i, 0)) for a in a_parts]
    in_specs.append(pl.BlockSpec((None, k, tn), lambda j, i: (l, 0, j)))
    args = list(a_parts) + [w]
    if norm_gain is not None:
        in_specs.append(pl.BlockSpec((None, 1, k), lambda j, i: (l, 0, 0)))
        args.append(norm_gain)
    if colscale is not None:
        in_specs.append(pl.BlockSpec((1, tn), lambda j, i: (0, j)))
        args.append(colscale)
    if resid is not None:
        in_specs.append(pl.BlockSpec((tm, tn), lambda j, i: (i, j)))
        args.append(resid)
    return pl.pallas_call(
        functools.partial(_matmul_kernel, n_parts=len(a_parts), has_norm=norm_gain is not None,
                          has_scale=colscale is not None, has_resid=resid is not None),
        out_shape=jax.ShapeDtypeStruct((m, n), out_dtype),
        grid=(n // tn, m // tm),
        in_specs=in_specs,
        out_specs=pl.BlockSpec((tm, tn), lambda j, i: (i, j)),
        scratch_shapes=[pltpu.VMEM((k, tn), BF16)],
        compiler_params=_params(("parallel", "arbitrary")),
        name="matmul",
    )(*args)


def _toeplitz(v, rows, cols):
    hh, n = v.shape
    assert n == rows + cols - 1
    vp = jnp.pad(v, ((0, 0), (0, 1)))
    skew = jnp.tile(vp, (1, rows))[:, :rows * n].reshape(hh, rows, n)
    return skew[:, :, rows - 1:]


def _attn_a_kernel(q_ref, k_ref, v_ref, b0_ref, b1_ref, b2_ref, o_ref, vt_ref, st_ref):
    i = pl.program_id(2)
    qb = A_QB
    n_kt = A_WIN // qb
    bias_refs = (b0_ref, b1_ref, b2_ref)

    @pl.when(i == 0)
    def _():
        for hh in range(A_HP):
            for c in range(SEQ // qb):
                blk = v_ref[c * qb:(c + 1) * qb, hh * HEAD_DIM:(hh + 1) * HEAD_DIM]
                vt_ref[hh, c] = blk.astype(F32).T.astype(BF16)

    t0 = jnp.maximum(i - (n_kt - 1), 0)
    start = pl.multiple_of(t0 * qb, qb)
    qts = [q_ref[:, hh * HEAD_DIM:(hh + 1) * HEAD_DIM].astype(F32).T.astype(BF16)
           for hh in range(A_HP)]
    for hh in range(A_HP):
        st_ref[hh] = jnp.dot(k_ref[pl.ds(start, A_WIN), hh * HEAD_DIM:(hh + 1) * HEAD_DIM],
                             qts[hh], preferred_element_type=F32)
    for hh in range(A_HP):
        st = st_ref[hh] + jnp.concatenate([r[hh] for r in bias_refs], axis=0)
        m = jnp.max(st, axis=0, keepdims=True)
        p = jnp.exp2(st - m)
        denom = jnp.sum(p, axis=0, keepdims=True)
        pb = p.astype(BF16)
        ot = None
        for u in range(n_kt):
            part = jnp.dot(vt_ref[hh, t0 + u], pb[u * qb:(u + 1) * qb],
                           preferred_element_type=F32)
            ot = part if ot is None else ot + part
        o_ref[:, hh * HEAD_DIM:(hh + 1) * HEAD_DIM] = (ot / denom).T.astype(o_ref.dtype)


def attn_a(proj, bias_a, l, batch):
    t = proj.shape[0]
    nq = SEQ // A_QB
    n_kt = A_WIN // A_QB
    wa = A_HP * HEAD_DIM
    ncol = A_WIDTH // wa

    def bias_spec(u):
        return pl.BlockSpec(
            (None, A_HP, None, A_QB, A_QB),
            lambda b, h, i: (l, h, (n_kt - 1) - jnp.minimum(i, n_kt - 1) + u, 0, 0))

    return pl.pallas_call(
        _attn_a_kernel,
        out_shape=jax.ShapeDtypeStruct((t, A_WIDTH), BF16),
        grid=(batch, A_HEADS // A_HP, nq),
        in_specs=[
            pl.BlockSpec((A_QB, wa), lambda b, h, i: (b * nq + i, h)),
            pl.BlockSpec((SEQ, wa), lambda b, h, i: (b, ncol + h)),
            pl.BlockSpec((SEQ, wa), lambda b, h, i: (b, 2 * ncol + h)),
            bias_spec(0), bias_spec(1), bias_spec(2),
        ],
        out_specs=pl.BlockSpec((A_QB, wa), lambda b, h, i: (b * nq + i, h)),
        scratch_shapes=[pltpu.VMEM((A_HP, SEQ // A_QB, HEAD_DIM, A_QB), BF16),
                        pltpu.VMEM((A_HP, A_WIN, A_QB), F32)],
        compiler_params=_params(("arbitrary", "arbitrary", "arbitrary")),
        name="attn_a",
    )(proj, proj, proj, bias_a, bias_a, bias_a)


def band_bias_tables(rel_tables):
    depth, _, heads = rel_tables.shape
    lead = LEFT_CHUNKS * CHUNK
    n_tiles = (A_WIN + lead) // A_QB
    kk = np.arange(A_QB)[:, None]
    r = np.arange(A_QB)[None, :]
    n = np.arange(2 * A_QB - 1)
    tables = rel_tables.astype(F32)
    tiles = []
    for t in range(n_tiles):
        idx = np.clip(t * A_QB + (A_QB - 1 - n) - lead, -REL_CLIP, REL_CLIP) + REL_CLIP
        v = tables[:, idx, :].transpose(0, 2, 1).reshape(depth * heads, -1)
        dchunk = (t * A_QB + kk - lead) // CHUNK - r // CHUNK
        valid = (dchunk >= -LEFT_CHUNKS) & (dchunk <= 0)
        tiles.append(jnp.where(valid[None], _toeplitz(v, A_QB, A_QB) * LOG2E, NEG_INF))
    return jnp.stack(tiles, axis=1).reshape(depth, heads, n_tiles, A_QB, A_QB)


def _attn_b_kernel(far_ref, q_ref, k_ref, v_ref, bias_ref, lam_ref, g_ref, o_ref,
                   qqt_ref, vt_ref, m_ref, l_ref, acc_ref, st_ref, *, lam_init):
    hp = pl.program_id(1)
    i = pl.program_id(2)
    qb = B_QB

    @pl.when(i == 0)
    def _():
        for hh in range(B_HP):
            for c in range(SEQ // qb):
                blk = v_ref[c * qb:(c + 1) * qb, hh * HEAD_DIM:(hh + 1) * HEAD_DIM]
                vt_ref[hh, c] = blk.astype(F32).T.astype(BF16)

    for hh in range(B_HP):
        qt = q_ref[:, hh * HEAD_DIM:(hh + 1) * HEAD_DIM].astype(F32).T
        row = lax.broadcasted_iota(jnp.int32, qt.shape, 0)
        qqt_ref[hh, :, 0:qb] = jnp.where(row < B_QK_DIM, qt, 0.0).astype(BF16)
        qqt_ref[hh, :, qb:2 * qb] = jnp.where(row >= B_QK_DIM, qt, 0.0).astype(BF16)
        m_ref[hh] = jnp.full((1, 2 * qb), NEG_INF, F32)
        l_ref[hh] = jnp.zeros((1, 2 * qb), F32)
        acc_ref[hh] = jnp.zeros((HEAD_DIM, 2 * qb), F32)

    def tile(j, d, nk=1):
        off = pl.multiple_of(j * qb, qb)
        for hh in range(B_HP):
            st_ref[hh, 0:nk * qb] = jnp.dot(
                k_ref[pl.ds(off, nk * qb), hh * HEAD_DIM:(hh + 1) * HEAD_DIM], qqt_ref[hh],
                preferred_element_type=F32)
        for hh in range(B_HP):
            st = st_ref[hh, 0:nk * qb]
            if d is None:
                shift = far_ref[hp * B_HP + hh]
            else:
                b = jnp.concatenate([bias_ref[hh, du] for du in d], axis=0)
                st = jnp.concatenate([st[:, 0:qb] + b, st[:, qb:2 * qb] + b], axis=1)
                shift = 0.0
            m_old = m_ref[hh]
            m_new = jnp.maximum(m_old, jnp.max(st, axis=0, keepdims=True) + shift)
            alpha = jnp.exp2(m_old - m_new)
            p = jnp.exp2(st - (m_new - shift))
            l_ref[hh] = alpha * l_ref[hh] + jnp.sum(p, axis=0, keepdims=True)
            pb = p.astype(BF16)
            pv = None
            for u in range(nk):
                part = jnp.dot(vt_ref[hh, j + u], pb[u * qb:(u + 1) * qb],
                               preferred_element_type=F32)
                pv = part if pv is None else pv + part
            acc_ref[hh] = alpha * acc_ref[hh] + pv
            m_ref[hh] = m_new

    n_far = jnp.maximum(i - (B_NEAR - 1), 0)

    def far_body(jj, c):
        tile(2 * jj, None, nk=2)
        return c

    lax.fori_loop(0, n_far // 2, far_body, 0)

    @pl.when(n_far % 2 == 1)
    def _():
        tile(n_far - 1, None)

    assert B_NEAR == 3
    @pl.when(i >= 2)
    def _():
        tile(i - 2, (2, 1), nk=2)

    @pl.when(i == 1)
    def _():
        tile(0, (1,))

    tile(i, (0,))

    lv = lam_ref[...]
    lam = (jnp.exp(jnp.sum(lv[0:1] * lv[1:2], axis=-1, keepdims=True))
           - jnp.exp(jnp.sum(lv[2:3] * lv[3:4], axis=-1, keepdims=True)) + lam_init)
    for hh in range(B_HP):
        o = acc_ref[hh] / l_ref[hh]
        od = o[:, 0:qb] - lam * o[:, qb:2 * qb]
        ms = jnp.mean(od * od, axis=0, keepdims=True)
        y = (od * lax.rsqrt(ms + EPS)).T * g_ref[...]
        o_ref[:, hh * HEAD_DIM:(hh + 1) * HEAD_DIM] = (y * (1.0 - lam_init)).astype(o_ref.dtype)


def attn_b(proj, far_bias, bias_b, lam_vecs, subln3, l, lam_init, batch):
    t = proj.shape[0]
    nq = SEQ // B_QB
    wb = B_HP * HEAD_DIM
    qcol = 3 * A_WIDTH // wb
    ncol = B_WIDTH // wb
    return pl.pallas_call(
        functools.partial(_attn_b_kernel, lam_init=lam_init),
        out_shape=jax.ShapeDtypeStruct((t, B_WIDTH), BF16),
        grid=(batch, B_HEADS // B_HP, nq),
        in_specs=[
            pl.BlockSpec(memory_space=pltpu.SMEM),
            pl.BlockSpec((B_QB, wb), lambda b, h, i: (b * nq + i, qcol + h)),
            pl.BlockSpec((SEQ, wb), lambda b, h, i: (b, qcol + ncol + h)),
            pl.BlockSpec((SEQ, wb), lambda b, h, i: (b, qcol + 2 * ncol + h)),
            pl.BlockSpec((B_HP, B_NEAR, B_QB, B_QB), lambda b, h, i: (h, 0, 0, 0)),
            pl.BlockSpec((None, 4, B_QK_DIM), lambda b, h, i: (l, 0, 0)),
            pl.BlockSpec((None, 1, HEAD_DIM), lambda b, h, i: (l, 0, 0)),
        ],
        out_specs=pl.BlockSpec((B_QB, wb), lambda b, h, i: (b * nq + i, h)),
        scratch_shapes=[pltpu.VMEM((B_HP, HEAD_DIM, 2 * B_QB), BF16),
                        pltpu.VMEM((B_HP, SEQ // B_QB, HEAD_DIM, B_QB), BF16),
                        pltpu.VMEM((B_HP, 1, 2 * B_QB), F32),
                        pltpu.VMEM((B_HP, 1, 2 * B_QB), F32),
                        pltpu.VMEM((B_HP, HEAD_DIM, 2 * B_QB), F32),
                        pltpu.VMEM((B_HP, 2 * B_QB, 2 * B_QB), F32)],
        compiler_params=_params(("arbitrary", "arbitrary", "arbitrary")),
        name="attn_b",
    )(far_bias, proj, proj, proj, bias_b, lam_vecs, subln3)


def _t5_bucket(rel):
    half = T5_BUCKETS // 2
    max_exact = half // 2
    ret = jnp.where(rel > 0, half, 0)
    n = jnp.abs(rel)
    n_f = jnp.maximum(n, max_exact).astype(jnp.float32)
    large = max_exact + (jnp.log(n_f / max_exact) / math.log(T5_MAX_DIST / max_exact)
                         * (half - max_exact)).astype(jnp.int32)
    large = jnp.minimum(large, half - 1)
    return ret + jnp.where(n < max_exact, n, large)


def t5_bias_tables(t5_table):
    assert B_NEAR * B_QB - (B_QB - 1) >= T5_MAX_DIST
    lo = -(B_NEAR * B_QB)
    bias1d = t5_table.astype(F32)[_t5_bucket(jnp.arange(lo, B_QB))].T
    far = t5_table.astype(F32)[_t5_bucket(jnp.array([-(SEQ - 1)]))][0]
    u = np.arange(2 * B_QB - 1)
    kk = np.arange(B_QB)[:, None]
    qq = np.arange(B_QB)[None, :]
    tiles = []
    for d in range(B_NEAR):
        t = _toeplitz(bias1d[:, (B_QB - 1 - u - d * B_QB) - lo], B_QB, B_QB)
        if d == 0:
            t = jnp.where(((kk // CHUNK) <= (qq // CHUNK))[None], t * LOG2E, NEG_INF)
        else:
            t = t * LOG2E
        tiles.append(t)
    return far * LOG2E, jnp.stack(tiles, axis=1)


def _xattn_kernel(q_ref, k_ref, v_ref, o_ref, s_ref):
    dh = X_HEAD_DIM
    for h in range(X_HEADS):
        s_ref[h] = lax.dot_general(q_ref[:, h * dh:(h + 1) * dh], k_ref[:, h * dh:(h + 1) * dh],
                                   (((1,), (1,)), ((), ())), preferred_element_type=F32)
    for h in range(X_HEADS):
        s = s_ref[h]
        m = jnp.max(s, axis=-1, keepdims=True)
        p = jnp.exp(s - m)
        denom = jnp.sum(p, axis=-1, keepdims=True)
        o = jnp.dot(p.astype(BF16), v_ref[:, h * dh:(h + 1) * dh], preferred_element_type=F32)
        o_ref[:, h * dh:(h + 1) * dh] = (o / denom).astype(o_ref.dtype)


def xattn(q, kv, batch):
    t, d = q.shape
    nq = SEQ // X_QB
    return pl.pallas_call(
        _xattn_kernel,
        out_shape=jax.ShapeDtypeStruct((t, d), BF16),
        grid=(batch, nq),
        in_specs=[
            pl.BlockSpec((X_QB, d), lambda b, i: (b * nq + i, 0)),
            pl.BlockSpec((MEM_LEN, d), lambda b, i: (b, 0)),
            pl.BlockSpec((MEM_LEN, d), lambda b, i: (b, 1)),
        ],
        out_specs=pl.BlockSpec((X_QB, d), lambda b, i: (b * nq + i, 0)),
        scratch_shapes=[pltpu.VMEM((X_HEADS, X_QB, MEM_LEN), F32)],
        compiler_params=_params(("parallel", "parallel")),
        name="xattn",
    )(q, kv, kv)


def _pack_rows(v, out_ref):
    rows, width = v.shape
    half = width // 2
    lo = lax.bitcast_convert_type(v[:, :half].astype(BF16).astype(F32), jnp.uint32)
    hi = lax.bitcast_convert_type(v[:, half:].astype(BF16).astype(F32), jnp.uint32)
    u = (lo >> 16) | hi
    for c in range(half // LANES):
        out_ref[pl.ds(c, rows, stride=PACK_SUB), :] = u[:, c * LANES:(c + 1) * LANES]


def _unpack_rows(ref, rows):
    lo, hi = [], []
    for c in range(PACK_SUB):
        u = ref[pl.ds(c, rows, stride=PACK_SUB), :]
        lo.append(lax.bitcast_convert_type(u << 16, F32))
        hi.append(lax.bitcast_convert_type(u & jnp.uint32(0xFFFF0000), F32))
    return lo, hi


def _router_kernel(x_ref, g_ref, wr_ref, br_ref, route_ref, cnt_ref, hpk_ref, carry_ref, tri_ref):
    tm = x_ref.shape[0]

    @pl.when(pl.program_id(0) == 0)
    def _():
        carry_ref[...] = jnp.zeros(carry_ref.shape, F32)
        r = lax.broadcasted_iota(jnp.int32, (tm, tm), 0)
        c = lax.broadcasted_iota(jnp.int32, (tm, tm), 1)
        tri_ref[...] = jnp.where(c < r, 1.0, 0.0).astype(BF16)

    x = x_ref[...]
    ms = jnp.mean(x * x, axis=-1, keepdims=True)
    hn = x * lax.rsqrt(ms + EPS) * g_ref[...]
    _pack_rows(hn, hpk_ref)
    hb = hn.astype(BF16)
    logits = jnp.dot(hb, wr_ref[...].astype(BF16), preferred_element_type=F32) + br_ref[...]

    lane = lax.broadcasted_iota(jnp.int32, logits.shape, 1)
    lane_f = lane.astype(F32)
    big = jnp.float32(1e9)

    def top1(mask):
        v = jnp.where(mask, logits, -jnp.inf)
        vmax = jnp.max(v, axis=-1, keepdims=True)
        idx = jnp.min(jnp.where(mask & (v == vmax), lane_f, big), axis=-1, keepdims=True)
        return vmax, idx

    gmask = (lane >= N_EXPERTS) & (lane < N_EXPERTS + N_GROUPS)
    gmax, gidx = top1(gmask)
    gsum = jnp.sum(jnp.where(gmask, jnp.exp(logits - gmax), 0.0), axis=-1, keepdims=True)
    g_gate = 1.0 / gsum
    g_sel = gidx.astype(jnp.int32) - N_EXPERTS

    emask = (lane < N_EXPERTS) & ((lane // EXPERTS_PER_GROUP) == g_sel)
    v1, i1 = top1(emask)
    v2, i2 = top1(emask & (lane_f != i1))
    e2 = jnp.exp(v2 - v1)
    w1 = g_gate / (1.0 + e2)
    w2 = g_gate * e2 / (1.0 + e2)

    oh1 = jnp.where(lane_f == i1, 1.0, 0.0)
    oh2 = jnp.where(lane_f == i2, 1.0, 0.0)
    oh = oh1 + oh2
    before = jnp.dot(tri_ref[...], oh.astype(BF16), preferred_element_type=F32) + carry_ref[...]
    r1 = jnp.sum(oh1 * before, axis=-1, keepdims=True)
    r2 = jnp.sum(oh2 * before, axis=-1, keepdims=True)
    carry_ref[...] = carry_ref[...] + jnp.sum(oh, axis=0, keepdims=True)
    cnt_ref[...] = carry_ref[...]

    route = jnp.zeros(logits.shape, F32)
    for col, val in enumerate((i1, i2, w1, w2, r1, r2)):
        route = jnp.where(lane == col, val, route)
    route_ref[...] = route


def router(x, g3, wr, br, l, tm=512):
    t, d = x.shape
    return pl.pallas_call(
        _router_kernel,
        out_shape=(jax.ShapeDtypeStruct((t, LANES), F32),
                   jax.ShapeDtypeStruct((1, LANES), F32),
                   jax.ShapeDtypeStruct((t * PACK_SUB, LANES), jnp.uint32)),
        grid=(t // tm,),
        in_specs=[pl.BlockSpec((tm, d), lambda i: (i, 0)),
                  pl.BlockSpec((None, 1, d), lambda i: (l, 0, 0)),
                  pl.BlockSpec((None, d, LANES), lambda i: (l, 0, 0)),
                  pl.BlockSpec((None, 1, LANES), lambda i: (l, 0, 0))],
        out_specs=(pl.BlockSpec((tm, LANES), lambda i: (i, 0)),
                   pl.BlockSpec((1, LANES), lambda i: (0, 0)),
                   pl.BlockSpec((tm * PACK_SUB, LANES), lambda i: (i, 0))),
        scratch_shapes=[pltpu.VMEM((1, LANES), F32), pltpu.VMEM((tm, tm), BF16)],
        compiler_params=_params(("arbitrary",)),
        name="router",
    )(x, g3, wr, br)


def _max_tiles(t):
    return (2 * t) // E_TM + N_EXPERTS


def dispatch_plan(route, counts_row):
    t = route.shape[0]
    max_tiles = _max_tiles(t)
    ids = route[:, 0:2].astype(jnp.int32)
    ranks = route[:, 4:6].astype(jnp.int32)
    counts = counts_row[0, :N_EXPERTS].astype(jnp.int32)
    padded = ((counts + E_TM - 1) // E_TM) * E_TM
    ends = jnp.cumsum(padded)
    offs = ends - padded
    onehot = ids[:, :, None] == jnp.arange(N_EXPERTS, dtype=jnp.int32)
    dest = (jnp.sum(jnp.where(onehot, offs, 0), axis=-1) + ranks).reshape(2 * t)
    n_tiles = ends[-1] // E_TM
    tile_i = jnp.minimum(jnp.arange(max_tiles, dtype=jnp.int32), n_tiles - 1)
    tile_expert = jnp.sum(ends[None, :] <= tile_i[:, None] * E_TM, axis=1).astype(jnp.int32)
    tokens = jnp.arange(2 * t, dtype=jnp.int32) // 2
    row_token = jnp.zeros((max_tiles * E_TM,), jnp.int32).at[dest].set(
        tokens, unique_indices=True, mode="promise_in_bounds")
    return dest, row_token, tile_expert, n_tiles.reshape(1).astype(jnp.int32)


def _experts_kernel(te_ref, nt_ref, rt_ref, h_hbm, w1_ref, w3_ref, w2_ref, ys_ref,
                    xbuf0, xbuf1, xbuf2, sem, w1b, w3b, w2b, *, max_tiles):
    i = pl.program_id(0)
    nt = nt_ref[0]
    bufs = (xbuf0, xbuf1, xbuf2)
    ahead = len(bufs) - 1

    def row_copy(tok, s, r):
        src = h_hbm.at[pl.ds(pl.multiple_of(tok * PACK_SUB, PACK_SUB), PACK_SUB)]
        return pltpu.make_async_copy(src, bufs[s].at[pl.ds(r * PACK_SUB, PACK_SUB)], sem.at[s])

    def tile_wait(s):
        pltpu.make_async_copy(h_hbm.at[pl.ds(0, E_TM * PACK_SUB)], bufs[s], sem.at[s]).wait()

    @pl.when(i == 0)
    def _():
        for t in range(ahead):
            def body(r, c, t=t):
                src = h_hbm.at[pl.ds(pl.multiple_of(rt_ref[t * E_TM + r] * PACK_SUB, PACK_SUB),
                                     PACK_SUB)]
                dst = bufs[t].at[pl.ds(pl.multiple_of(r * PACK_SUB, PACK_SUB), PACK_SUB)]
                pltpu.make_async_copy(src, dst, sem.at[t]).start()
                return c

            lax.fori_loop(0, E_TM, body, 0, unroll=8)

    def step(s):
        @pl.when((i == 0) | (te_ref[i] != te_ref[jnp.maximum(i - 1, 0)]))
        def _():
            w1b[...] = w1_ref[...].astype(BF16)
            w3b[...] = w3_ref[...].astype(BF16)
            w2b[...] = w2_ref[...].astype(BF16)

        tile_wait(s)

        nxt = (s + ahead) % len(bufs)
        base = jnp.minimum(i + ahead, max_tiles - 1) * E_TM
        for r in range(E_TM):
            row_copy(rt_ref[base + r], nxt, r).start(priority=r % 2)

        lo, hi = _unpack_rows(bufs[s], E_TM)
        hb = jnp.concatenate([v.astype(BF16) for v in lo + hi], axis=1)
        a1 = jnp.dot(hb, w1b[...], preferred_element_type=F32)
        a3 = jnp.dot(hb, w3b[...], preferred_element_type=F32)
        hid = (a1 / (1.0 + jnp.exp(-a1))) * a3
        _pack_rows(jnp.dot(hid.astype(BF16), w2b[...], preferred_element_type=F32), ys_ref)

        @pl.when(i == nt - 1)
        def _():
            for t in range(1, len(bufs)):
                tile_wait((s + t) % len(bufs))

    for s in range(len(bufs)):
        @pl.when((i < nt) & (i % len(bufs) == s))
        def _(s=s):
            step(s)

    @pl.when(i >= nt)
    def _():
        ys_ref[...] = jnp.zeros(ys_ref.shape, ys_ref.dtype)


def experts(hpk, w1, w3, w2, l, tile_expert, n_tiles, row_token):
    d = w1.shape[-2]
    max_tiles = row_token.shape[0] // E_TM
    blk = E_TM * PACK_SUB
    grid_spec = pltpu.PrefetchScalarGridSpec(
        num_scalar_prefetch=3,
        grid=(max_tiles,),
        in_specs=[pl.BlockSpec(memory_space=pl.ANY),
                  pl.BlockSpec((None, None, d, D_EXPERT), lambda i, te, nt, rt: (l, te[i], 0, 0)),
                  pl.BlockSpec((None, None, d, D_EXPERT), lambda i, te, nt, rt: (l, te[i], 0, 0)),
                  pl.BlockSpec((None, None, D_EXPERT, d), lambda i, te, nt, rt: (l, te[i], 0, 0))],
        out_specs=pl.BlockSpec((blk, LANES), lambda i, te, nt, rt: (i, 0)),
        scratch_shapes=[pltpu.VMEM((blk, LANES), jnp.uint32),
                        pltpu.VMEM((blk, LANES), jnp.uint32),
                        pltpu.VMEM((blk, LANES), jnp.uint32),
                        pltpu.SemaphoreType.DMA((3,)),
                        pltpu.VMEM((d, D_EXPERT), BF16),
                        pltpu.VMEM((d, D_EXPERT), BF16),
                        pltpu.VMEM((D_EXPERT, d), BF16)],
    )
    return pl.pallas_call(
        functools.partial(_experts_kernel, max_tiles=max_tiles),
        out_shape=jax.ShapeDtypeStruct((max_tiles * blk, LANES), jnp.uint32),
        grid_spec=grid_spec,
        compiler_params=_params(("arbitrary",)),
        name="experts",
    )(tile_expert, n_tiles, row_token, hpk, w1, w3, w2)


def _combine_kernel(dest_ref, ys_hbm, x_ref, route_ref, o_ref, ybuf0, ybuf1, sem):
    i = pl.program_id(0)
    n = pl.num_programs(0)
    bufs = (ybuf0, ybuf1)

    def row_copy(row, s, k, r):
        src = ys_hbm.at[pl.ds(pl.multiple_of(row * PACK_SUB, PACK_SUB), PACK_SUB)]
        return pltpu.make_async_copy(src, bufs[s].at[k, pl.ds(r * PACK_SUB, PACK_SUB)], sem.at[s])

    def tile_wait(s):
        for k in range(2):
            pltpu.make_async_copy(ys_hbm.at[pl.ds(0, C_TM * PACK_SUB)], bufs[s].at[k],
                                  sem.at[s]).wait()

    @pl.when(i == 0)
    def _():
        def body(r, c):
            for k in range(2):
                src = ys_hbm.at[pl.ds(pl.multiple_of(dest_ref[2 * r + k] * PACK_SUB, PACK_SUB),
                                      PACK_SUB)]
                dst = ybuf0.at[k, pl.ds(pl.multiple_of(r * PACK_SUB, PACK_SUB), PACK_SUB)]
                pltpu.make_async_copy(src, dst, sem.at[0]).start()
            return c

        lax.fori_loop(0, C_TM, body, 0, unroll=4)

    def step(s):
        tile_wait(s)
        base = 2 * jnp.minimum(i + 1, n - 1) * C_TM
        for r in range(C_TM):
            for k in range(2):
                row_copy(dest_ref[base + 2 * r + k], 1 - s, k, r).start(priority=k)

        route = route_ref[...]
        lane = lax.broadcasted_iota(jnp.int32, route.shape, 1)
        w1 = jnp.sum(jnp.where(lane == 2, route, 0.0), axis=-1, keepdims=True)
        w2 = jnp.sum(jnp.where(lane == 3, route, 0.0), axis=-1, keepdims=True)
        lo1, hi1 = _unpack_rows(bufs[s].at[0], C_TM)
        lo2, hi2 = _unpack_rows(bufs[s].at[1], C_TM)
        half = x_ref.shape[1] // 2
        for c in range(PACK_SUB):
            for off, y1, y2 in ((0, lo1, lo2), (half, hi1, hi2)):
                cols = slice(off + c * LANES, off + (c + 1) * LANES)
                o_ref[:, cols] = x_ref[:, cols] + w1 * y1[c] + w2 * y2[c]

        @pl.when(i == n - 1)
        def _():
            tile_wait(1 - s)

    for s in range(2):
        @pl.when(i % 2 == s)
        def _(s=s):
            step(s)


def combine(ys, x, route, dest):
    t, d = x.shape
    blk = C_TM * PACK_SUB
    grid_spec = pltpu.PrefetchScalarGridSpec(
        num_scalar_prefetch=1,
        grid=(t // C_TM,),
        in_specs=[pl.BlockSpec(memory_space=pl.ANY),
                  pl.BlockSpec((C_TM, d), lambda i, dest: (i, 0)),
                  pl.BlockSpec((C_TM, LANES), lambda i, dest: (i, 0))],
        out_specs=pl.BlockSpec((C_TM, d), lambda i, dest: (i, 0)),
        scratch_shapes=[pltpu.VMEM((2, blk, LANES), jnp.uint32),
                        pltpu.VMEM((2, blk, LANES), jnp.uint32),
                        pltpu.SemaphoreType.DMA((2,))],
    )
    return pl.pallas_call(
        _combine_kernel,
        out_shape=jax.ShapeDtypeStruct((t, d), F32),
        grid_spec=grid_spec,
        compiler_params=_params(("arbitrary",)),
        name="combine",
    )(dest, ys, x, route)


def kernel(x, mem, norm_mix, w_in, rel_bias_a, t5_table, diff_lambda, diff_subln, w_out,
           norm_cross, norm_mem, w_xq, w_xkv, w_xo, norm_ffn, w_group, b_group,
           w_expert, b_expert, w1, w3, w2, norm_final):
    batch, seq, d = x.shape
    t = batch * seq
    xf = x.reshape(t, d)
    memf = mem.reshape(batch * MEM_LEN, d)

    g_mix = norm_mix.reshape(DEPTH, 1, d)
    g_cross = norm_cross.reshape(DEPTH, 1, d)
    g_mem = norm_mem.reshape(DEPTH, 1, d)
    g_ffn = norm_ffn.reshape(DEPTH, 1, d)
    g_final = norm_final.reshape(1, 1, d)
    subln3 = diff_subln.reshape(DEPTH, 1, HEAD_DIM)

    in_scale = jnp.concatenate([
        jnp.full((A_WIDTH,), HEAD_DIM ** -0.5 * LOG2E, F32), jnp.ones((2 * A_WIDTH,), F32),
        jnp.full((B_WIDTH,), B_QK_DIM ** -0.5 * LOG2E, F32), jnp.ones((2 * B_WIDTH,), F32)]
    ).reshape(1, IN_WIDTH)
    xq_scale = jnp.full((1, d), X_HEAD_DIM ** -0.5, F32)

    far_b, bias_b = t5_bias_tables(t5_table)
    bias_a = band_bias_tables(rel_bias_a)

    pad = LANES - N_EXPERTS - N_GROUPS
    wr = jnp.concatenate([w_expert, w_group, jnp.zeros((DEPTH, d, pad), F32)], axis=-1)
    br = jnp.concatenate([b_expert, b_group, jnp.zeros((DEPTH, pad), F32)],
                         axis=-1).reshape(DEPTH, 1, LANES)

    for l in range(DEPTH):
        lam_init = 0.8 - 0.6 * math.exp(-0.3 * l)
        proj = matmul(xf, w_in, l, BF16, norm_gain=g_mix, colscale=in_scale, tm=1024)
        oa = attn_a(proj, bias_a, l, batch)
        ob = attn_b(proj, far_b, bias_b, diff_lambda, subln3, l, lam_init, batch)
        xf = matmul([oa, ob], w_out, l, F32, resid=xf)
        q = matmul(xf, w_xq, l, BF16, norm_gain=g_cross, colscale=xq_scale, tn=2048)
        kv = matmul(memf, w_xkv, l, BF16, norm_gain=g_mem)
        xo = xattn(q, kv, batch)
        xf = matmul(xo, w_xo, l, F32, resid=xf)
        route, counts, hpk = router(xf, g_ffn, wr, br, l)
        dest, row_token, tile_expert, n_tiles = dispatch_plan(route, counts)
        ys = experts(hpk, w1, w3, w2, l, tile_expert, n_tiles, row_token)
        xf = combine(ys, xf, route, dest)
    out = rmsnorm(xf, g_final, 0, F32)
    return out.reshape(batch, seq, d)
```

```python
import functools
import math

import jax
import jax.numpy as jnp
import numpy as np
from jax import lax
from jax.experimental import pallas as pl
from jax.experimental.pallas import tpu as pltpu

D_MODEL = 2048
SEQ = 4096
DEPTH = 4
CHUNK = 64
LEFT_CHUNKS = 8
HEAD_DIM = 128
A_HEADS = 8
B_HEADS = 8
B_QK_DIM = 64
A_WIDTH = 1024
B_WIDTH = 1024
IN_WIDTH = 6144
REL_CLIP = 128
T5_BUCKETS = 32
T5_MAX_DIST = 512
MEM_LEN = 256
X_HEADS = 4
X_HEAD_DIM = 512
N_GROUPS = 4
EXPERTS_PER_GROUP = 8
N_EXPERTS = 32
D_EXPERT = 256
EPS = 1e-6
NEG_INF = -1e30

LANES = 128
VMEM_LIMIT = 56 * 1024 * 1024

BF16 = jnp.bfloat16
F32 = jnp.float32

A_QB = 256
A_WIN = A_QB + LEFT_CHUNKS * CHUNK
B_QB = 256
B_NEAR = 3
B_HP = 4
A_HP = 4
LOG2E = math.log2(math.e)
X_QB = 512

E_TM = 256
C_TM = 256
PACK_SUB = 8


def _params(sem):
    return pltpu.CompilerParams(dimension_semantics=sem, vmem_limit_bytes=VMEM_LIMIT)


def _rmsnorm_kernel(x_ref, g_ref, o_ref):
    x = x_ref[...]
    ms = jnp.mean(x * x, axis=-1, keepdims=True)
    o_ref[...] = (x * lax.rsqrt(ms + EPS) * g_ref[...]).astype(o_ref.dtype)


def rmsnorm(x, g3, l, out_dtype, tm=512):
    m, d = x.shape
    tm = min(tm, m)
    return pl.pallas_call(
        _rmsnorm_kernel,
        out_shape=jax.ShapeDtypeStruct((m, d), out_dtype),
        grid=(m // tm,),
        in_specs=[pl.BlockSpec((tm, d), lambda i: (i, 0)),
                  pl.BlockSpec((None, 1, d), lambda i: (l, 0, 0))],
        out_specs=pl.BlockSpec((tm, d), lambda i: (i, 0)),
        compiler_params=_params(("parallel",)),
        name="rmsnorm",
    )(x, g3)


def _matmul_kernel(*refs, n_parts, has_norm, has_scale, has_resid):
    a_refs, w_ref = refs[:n_parts], refs[n_parts]
    k = n_parts + 1
    g_ref = s_ref = r_ref = None
    if has_norm:
        g_ref = refs[k]; k += 1
    if has_scale:
        s_ref = refs[k]; k += 1
    if has_resid:
        r_ref = refs[k]; k += 1
    o_ref, wbf_ref = refs[k], refs[k + 1]

    @pl.when(pl.program_id(1) == 0)
    def _():
        wbf_ref[...] = w_ref[...].astype(BF16)

    acc = None
    k0 = 0
    for a_ref in a_refs:
        kp = a_ref.shape[1]
        a = a_ref[...]
        if has_norm:
            ms = jnp.mean(a * a, axis=-1, keepdims=True)
            a = (a * lax.rsqrt(ms + EPS) * g_ref[...]).astype(BF16)
        part = jnp.dot(a, wbf_ref[k0:k0 + kp, :], preferred_element_type=F32)
        acc = part if acc is None else acc + part
        k0 += kp
    if has_scale:
        acc = acc * s_ref[...]
    if has_resid:
        acc = acc + r_ref[...]
    o_ref[...] = acc.astype(o_ref.dtype)


def matmul(a_parts, w, l, out_dtype, norm_gain=None, colscale=None, resid=None,
           tm=512, tn=1024):
    if not isinstance(a_parts, (list, tuple)):
        a_parts = [a_parts]
    assert norm_gain is None or len(a_parts) == 1
    m = a_parts[0].shape[0]
    k = sum(a.shape[1] for a in a_parts)
    n = w.shape[-1]
    assert w.shape[-2] == k
    tm = min(tm, m)
    tn = min(tn, n)
    in_specs = [pl.BlockSpec((tm, a.shape[1]), lambda j, i: (i, 0)) for a in a_parts]
    in_specs.append(pl.BlockSpec((None, k, tn), lambda j, i: (l, 0, j)))
    args = list(a_parts) + [w]
    if norm_gain is not None:
        in_specs.append(pl.BlockSpec((None, 1, k), lambda j, i: (l, 0, 0)))
        args.append(norm_gain)
    if colscale is not None:
        in_specs.append(pl.BlockSpec((1, tn), lambda j, i: (0, j)))
        args.append(colscale)
    if resid is not None:
        in_specs.append(pl.BlockSpec((tm, tn), lambda j, i: (i, j)))
        args.append(resid)
    return pl.pallas_call(
        functools.partial(_matmul_kernel, n_parts=len(a_parts), has_norm=norm_gain is not None,
                          has_scale=colscale is not None, has_resid=resid is not None),
        out_shape=jax.ShapeDtypeStruct((m, n), out_dtype),
        grid=(n // tn, m // tm),
        in_specs=in_specs,
        out_specs=pl.BlockSpec((tm, tn), lambda j, i: (i, j)),
        scratch_shapes=[pltpu.VMEM((k, tn), BF16)],
        compiler_params=_params(("parallel", "arbitrary")),
        name="matmul",
    )(*args)


def _toeplitz(v, rows, cols):
    hh, n = v.shape
    assert n == rows + cols - 1
    vp = jnp.pad(v, ((0, 0), (0, 1)))
    skew = jnp.tile(vp, (1, rows))[:, :rows * n].reshape(hh, rows, n)
    return skew[:, :, rows - 1:]


def _attn_a_kernel(q_ref, k_ref, v_ref, b0_ref, b1_ref, b2_ref, o_ref, vt_ref, st_ref):
    i = pl.program_id(2)
    qb = A_QB
    n_kt = A_WIN // qb
    bias_refs = (b0_ref, b1_ref, b2_ref)

    @pl.when(i == 0)
    def _():
        for hh in range(A_HP):
            for c in range(SEQ // qb):
                blk = v_ref[c * qb:(c + 1) * qb, hh * HEAD_DIM:(hh + 1) * HEAD_DIM]
                vt_ref[hh, c] = blk.astype(F32).T.astype(BF16)

    t0 = jnp.maximum(i - (n_kt - 1), 0)
    start = pl.multiple_of(t0 * qb, qb)
    qts = [q_ref[:, hh * HEAD_DIM:(hh + 1) * HEAD_DIM].astype(F32).T.astype(BF16)
           for hh in range(A_HP)]
    for hh in range(A_HP):
        st_ref[hh] = jnp.dot(k_ref[pl.ds(start, A_WIN), hh * HEAD_DIM:(hh + 1) * HEAD_DIM],
                             qts[hh], preferred_element_type=F32)
    for hh in range(A_HP):
        st = st_ref[hh] + jnp.concatenate([r[hh] for r in bias_refs], axis=0)
        m = jnp.max(st, axis=0, keepdims=True)
        p = jnp.exp2(st - m)
        denom = jnp.sum(p, axis=0, keepdims=True)
        pb = p.astype(BF16)
        ot = None
        for u in range(n_kt):
            part = jnp.dot(vt_ref[hh, t0 + u], pb[u * qb:(u + 1) * qb],
                           preferred_element_type=F32)
            ot = part if ot is None else ot + part
        o_ref[:, hh * HEAD_DIM:(hh + 1) * HEAD_DIM] = (ot / denom).T.astype(o_ref.dtype)


def attn_a(proj, bias_a, l, batch):
    t = proj.shape[0]
    nq = SEQ // A_QB
    n_kt = A_WIN // A_QB
    wa = A_HP * HEAD_DIM
    ncol = A_WIDTH // wa

    def bias_spec(u):
        return pl.BlockSpec(
            (None, A_HP, None, A_QB, A_QB),
            lambda b, h, i: (l, h, (n_kt - 1) - jnp.minimum(i, n_kt - 1) + u, 0, 0))

    return pl.pallas_call(
        _attn_a_kernel,
        out_shape=jax.ShapeDtypeStruct((t, A_WIDTH), BF16),
        grid=(batch, A_HEADS // A_HP, nq),
        in_specs=[
            pl.BlockSpec((A_QB, wa), lambda b, h, i: (b * nq + i, h)),
            pl.BlockSpec((SEQ, wa), lambda b, h, i: (b, ncol + h)),
            pl.BlockSpec((SEQ, wa), lambda b, h, i: (b, 2 * ncol + h)),
            bias_spec(0), bias_spec(1), bias_spec(2),
        ],
        out_specs=pl.BlockSpec((A_QB, wa), lambda b, h, i: (b * nq + i, h)),
        scratch_shapes=[pltpu.VMEM((A_HP, SEQ // A_QB, HEAD_DIM, A_QB), BF16),
                        pltpu.VMEM((A_HP, A_WIN, A_QB), F32)],
        compiler_params=_params(("arbitrary", "arbitrary", "arbitrary")),
        name="attn_a",
    )(proj, proj, proj, bias_a, bias_a, bias_a)


def band_bias_tables(rel_tables):
    depth, _, heads = rel_tables.shape
    lead = LEFT_CHUNKS * CHUNK
    n_tiles = (A_WIN + lead) // A_QB
    kk = np.arange(A_QB)[:, None]
    r = np.arange(A_QB)[None, :]
    n = np.arange(2 * A_QB - 1)
    tables = rel_tables.astype(F32)
    tiles = []
    for t in range(n_tiles):
        idx = np.clip(t * A_QB + (A_QB - 1 - n) - lead, -REL_CLIP, REL_CLIP) + REL_CLIP
        v = tables[:, idx, :].transpose(0, 2, 1).reshape(depth * heads, -1)
        dchunk = (t * A_QB + kk - lead) // CHUNK - r // CHUNK
        valid = (dchunk >= -LEFT_CHUNKS) & (dchunk <= 0)
        tiles.append(jnp.where(valid[None], _toeplitz(v, A_QB, A_QB) * LOG2E, NEG_INF))
    return jnp.stack(tiles, axis=1).reshape(depth, heads, n_tiles, A_QB, A_QB)


def _attn_b_kernel(far_ref, q_ref, k_ref, v_ref, bias_ref, lam_ref, g_ref, o_ref,
                   qqt_ref, vt_ref, m_ref, l_ref, acc_ref, st_ref, *, lam_init):
    hp = pl.program_id(1)
    i = pl.program_id(2)
    qb = B_QB

    @pl.when(i == 0)
    def _():
        for hh in range(B_HP):
            for c in range(SEQ // qb):
                blk = v_ref[c * qb:(c + 1) * qb, hh * HEAD_DIM:(hh + 1) * HEAD_DIM]
                vt_ref[hh, c] = blk.astype(F32).T.astype(BF16)

    for hh in range(B_HP):
        qt = q_ref[:, hh * HEAD_DIM:(hh + 1) * HEAD_DIM].astype(F32).T
        row = lax.broadcasted_iota(jnp.int32, qt.shape, 0)
        qqt_ref[hh, :, 0:qb] = jnp.where(row < B_QK_DIM, qt, 0.0).astype(BF16)
        qqt_ref[hh, :, qb:2 * qb] = jnp.where(row >= B_QK_DIM, qt, 0.0).astype(BF16)
        m_ref[hh] = jnp.full((1, 2 * qb), NEG_INF, F32)
        l_ref[hh] = jnp.zeros((1, 2 * qb), F32)
        acc_ref[hh] = jnp.zeros((HEAD_DIM, 2 * qb), F32)

    def tile(j, d, nk=1):
        off = pl.multiple_of(j * qb, qb)
        for hh in range(B_HP):
            st_ref[hh, 0:nk * qb] = jnp.dot(
                k_ref[pl.ds(off, nk * qb), hh * HEAD_DIM:(hh + 1) * HEAD_DIM], qqt_ref[hh],
                preferred_element_type=F32)
        for hh in range(B_HP):
            st = st_ref[hh, 0:nk * qb]
            if d is None:
                shift = far_ref[hp * B_HP + hh]
            else:
                b = jnp.concatenate([bias_ref[hh, du] for du in d], axis=0)
                st = jnp.concatenate([st[:, 0:qb] + b, st[:, qb:2 * qb] + b], axis=1)
                shift = 0.0
            m_old = m_ref[hh]
            m_new = jnp.maximum(m_old, jnp.max(st, axis=0, keepdims=True) + shift)
            alpha = jnp.exp2(m_old - m_new)
            p = jnp.exp2(st - (m_new - shift))
            l_ref[hh] = alpha * l_ref[hh] + jnp.sum(p, axis=0, keepdims=True)
            pb = p.astype(BF16)
            pv = None
            for u in range(nk):
                part = jnp.dot(vt_ref[hh, j + u], pb[u * qb:(u + 1) * qb],
                               preferred_element_type=F32)
                pv = part if pv is None else pv + part
            acc_ref[hh] = alpha * acc_ref[hh] + pv
            m_ref[hh] = m_new

    n_far = jnp.maximum(i - (B_NEAR - 1), 0)

    def far_body(jj, c):
        tile(2 * jj, None, nk=2)
        return c

    lax.fori_loop(0, n_far // 2, far_body, 0)

    @pl.when(n_far % 2 == 1)
    def _():
        tile(n_far - 1, None)

    assert B_NEAR == 3
    @pl.when(i >= 2)
    def _():
        tile(i - 2, (2, 1), nk=2)

    @pl.when(i == 1)
    def _():
        tile(0, (1,))

    tile(i, (0,))

    lv = lam_ref[...]
    lam = (jnp.exp(jnp.sum(lv[0:1] * lv[1:2], axis=-1, keepdims=True))
           - jnp.exp(jnp.sum(lv[2:3] * lv[3:4], axis=-1, keepdims=True)) + lam_init)
    for hh in range(B_HP):
        o = acc_ref[hh] / l_ref[hh]
        od = o[:, 0:qb] - lam * o[:, qb:2 * qb]
        ms = jnp.mean(od * od, axis=0, keepdims=True)
        y = (od * lax.rsqrt(ms + EPS)).T * g_ref[...]
        o_ref[:, hh * HEAD_DIM:(hh + 1) * HEAD_DIM] = (y * (1.0 - lam_init)).astype(o_ref.dtype)


def attn_b(proj, far_bias, bias_b, lam_vecs, subln3, l, lam_init, batch):
    t = proj.shape[0]
    nq = SEQ // B_QB
    wb = B_HP * HEAD_DIM
    qcol = 3 * A_WIDTH // wb
    ncol = B_WIDTH // wb
    return pl.pallas_call(
        functools.partial(_attn_b_kernel, lam_init=lam_init),
        out_shape=jax.ShapeDtypeStruct((t, B_WIDTH), BF16),
        grid=(batch, B_HEADS // B_HP, nq),
        in_specs=[
            pl.BlockSpec(memory_space=pltpu.SMEM),
            pl.BlockSpec((B_QB, wb), lambda b, h, i: (b * nq + i, qcol + h)),
            pl.BlockSpec((SEQ, wb), lambda b, h, i: (b, qcol + ncol + h)),
            pl.BlockSpec((SEQ, wb), lambda b, h, i: (b, qcol + 2 * ncol + h)),
            pl.BlockSpec((B_HP, B_NEAR, B_QB, B_QB), lambda b, h, i: (h, 0, 0, 0)),
            pl.BlockSpec((None, 4, B_QK_DIM), lambda b, h, i: (l, 0, 0)),
            pl.BlockSpec((None, 1, HEAD_DIM), lambda b, h, i: (l, 0, 0)),
        ],
        out_specs=pl.BlockSpec((B_QB, wb), lambda b, h, i: (b * nq + i, h)),
        scratch_shapes=[pltpu.VMEM((B_HP, HEAD_DIM, 2 * B_QB), BF16),
                        pltpu.VMEM((B_HP, SEQ // B_QB, HEAD_DIM, B_QB), BF16),
                        pltpu.VMEM((B_HP, 1, 2 * B_QB), F32),
                        pltpu.VMEM((B_HP, 1, 2 * B_QB), F32),
                        pltpu.VMEM((B_HP, HEAD_DIM, 2 * B_QB), F32),
                        pltpu.VMEM((B_HP, 2 * B_QB, 2 * B_QB), F32)],
        compiler_params=_params(("arbitrary", "arbitrary", "arbitrary")),
        name="attn_b",
    )(far_bias, proj, proj, proj, bias_b, lam_vecs, subln3)


def _t5_bucket(rel):
    half = T5_BUCKETS // 2
    max_exact = half // 2
    ret = jnp.where(rel > 0, half, 0)
    n = jnp.abs(rel)
    n_f = jnp.maximum(n, max_exact).astype(jnp.float32)
    large = max_exact + (jnp.log(n_f / max_exact) / math.log(T5_MAX_DIST / max_exact)
                         * (half - max_exact)).astype(jnp.int32)
    large = jnp.minimum(large, half - 1)
    return ret + jnp.where(n < max_exact, n, large)


def t5_bias_tables(t5_table):
    assert B_NEAR * B_QB - (B_QB - 1) >= T5_MAX_DIST
    lo = -(B_NEAR * B_QB)
    bias1d = t5_table.astype(F32)[_t5_bucket(jnp.arange(lo, B_QB))].T
    far = t5_table.astype(F32)[_t5_bucket(jnp.array([-(SEQ - 1)]))][0]
    u = np.arange(2 * B_QB - 1)
    kk = np.arange(B_QB)[:, None]
    qq = np.arange(B_QB)[None, :]
    tiles = []
    for d in range(B_NEAR):
        t = _toeplitz(bias1d[:, (B_QB - 1 - u - d * B_QB) - lo], B_QB, B_QB)
        if d == 0:
            t = jnp.where(((kk // CHUNK) <= (qq // CHUNK))[None], t * LOG2E, NEG_INF)
        else:
            t = t * LOG2E
        tiles.append(t)
    return far * LOG2E, jnp.stack(tiles, axis=1)


def _xattn_kernel(q_ref, k_ref, v_ref, o_ref, s_ref):
    dh = X_HEAD_DIM
    for h in range(X_HEADS):
        s_ref[h] = lax.dot_general(q_ref[:, h * dh:(h + 1) * dh], k_ref[:, h * dh:(h + 1) * dh],
                                   (((1,), (1,)), ((), ())), preferred_element_type=F32)
    for h in range(X_HEADS):
        s = s_ref[h]
        m = jnp.max(s, axis=-1, keepdims=True)
        p = jnp.exp(s - m)
        denom = jnp.sum(p, axis=-1, keepdims=True)
        o = jnp.dot(p.astype(BF16), v_ref[:, h * dh:(h + 1) * dh], preferred_element_type=F32)
        o_ref[:, h * dh:(h + 1) * dh] = (o / denom).astype(o_ref.dtype)


def xattn(q, kv, batch):
    t, d = q.shape
    nq = SEQ // X_QB
    return pl.pallas_call(
        _xattn_kernel,
        out_shape=jax.ShapeDtypeStruct((t, d), BF16),
        grid=(batch, nq),
        in_specs=[
            pl.BlockSpec((X_QB, d), lambda b, i: (b * nq + i, 0)),
            pl.BlockSpec((MEM_LEN, d), lambda b, i: (b, 0)),
            pl.BlockSpec((MEM_LEN, d), lambda b, i: (b, 1)),
        ],
        out_specs=pl.BlockSpec((X_QB, d), lambda b, i: (b * nq + i, 0)),
        scratch_shapes=[pltpu.VMEM((X_HEADS, X_QB, MEM_LEN), F32)],
        compiler_params=_params(("parallel", "parallel")),
        name="xattn",
    )(q, kv, kv)


def _pack_rows(v, out_ref):
    rows, width = v.shape
    half = width // 2
    lo = lax.bitcast_convert_type(v[:, :half].astype(BF16).astype(F32), jnp.uint32)
    hi = lax.bitcast_convert_type(v[:, half:].astype(BF16).astype(F32), jnp.uint32)
    u = (lo >> 16) | hi
    for c in range(half // LANES):
        out_ref[pl.ds(c, rows, stride=PACK_SUB), :] = u[:, c * LANES:(c + 1) * LANES]


def _unpack_rows(ref, rows):
    lo, hi = [], []
    for c in range(PACK_SUB):
        u = ref[pl.ds(c, rows, stride=PACK_SUB), :]
        lo.append(lax.bitcast_convert_type(u << 16, F32))
        hi.append(lax.bitcast_convert_type(u & jnp.uint32(0xFFFF0000), F32))
    return lo, hi


def _router_kernel(x_ref, g_ref, wr_ref, br_ref, route_ref, cnt_ref, hpk_ref, carry_ref, tri_ref):
    tm = x_ref.shape[0]

    @pl.when(pl.program_id(0) == 0)
    def _():
        carry_ref[...] = jnp.zeros(carry_ref.shape, F32)
        r = lax.broadcasted_iota(jnp.int32, (tm, tm), 0)
        c = lax.broadcasted_iota(jnp.int32, (tm, tm), 1)
        tri_ref[...] = jnp.where(c < r, 1.0, 0.0).astype(BF16)

    x = x_ref[...]
    ms = jnp.mean(x * x, axis=-1, keepdims=True)
    hn = x * lax.rsqrt(ms + EPS) * g_ref[...]
    _pack_rows(hn, hpk_ref)
    hb = hn.astype(BF16)
    logits = jnp.dot(hb, wr_ref[...].astype(BF16), preferred_element_type=F32) + br_ref[...]

    lane = lax.broadcasted_iota(jnp.int32, logits.shape, 1)
    lane_f = lane.astype(F32)
    big = jnp.float32(1e9)

    def top1(mask):
        v = jnp.where(mask, logits, -jnp.inf)
        vmax = jnp.max(v, axis=-1, keepdims=True)
        idx = jnp.min(jnp.where(mask & (v == vmax), lane_f, big), axis=-1, keepdims=True)
        return vmax, idx

    gmask = (lane >= N_EXPERTS) & (lane < N_EXPERTS + N_GROUPS)
    gmax, gidx = top1(gmask)
    gsum = jnp.sum(jnp.where(gmask, jnp.exp(logits - gmax), 0.0), axis=-1, keepdims=True)
    g_gate = 1.0 / gsum
    g_sel = gidx.astype(jnp.int32) - N_EXPERTS

    emask = (lane < N_EXPERTS) & ((lane // EXPERTS_PER_GROUP) == g_sel)
    v1, i1 = top1(emask)
    v2, i2 = top1(emask & (lane_f != i1))
    e2 = jnp.exp(v2 - v1)
    w1 = g_gate / (1.0 + e2)
    w2 = g_gate * e2 / (1.0 + e2)

    oh1 = jnp.where(lane_f == i1, 1.0, 0.0)
    oh2 = jnp.where(lane_f == i2, 1.0, 0.0)
    oh = oh1 + oh2
    before = jnp.dot(tri_ref[...], oh.astype(BF16), preferred_element_type=F32) + carry_ref[...]
    r1 = jnp.sum(oh1 * before, axis=-1, keepdims=True)
    r2 = jnp.sum(oh2 * before, axis=-1, keepdims=True)
    carry_ref[...] = carry_ref[...] + jnp.sum(oh, axis=0, keepdims=True)
    cnt_ref[...] = carry_ref[...]

    route = jnp.zeros(logits.shape, F32)
    for col, val in enumerate((i1, i2, w1, w2, r1, r2)):
        route = jnp.where(lane == col, val, route)
    route_ref[...] = route


def router(x, g3, wr, br, l, tm=512):
    t, d = x.shape
    return pl.pallas_call(
        _router_kernel,
        out_shape=(jax.ShapeDtypeStruct((t, LANES), F32),
                   jax.ShapeDtypeStruct((1, LANES), F32),
                   jax.ShapeDtypeStruct((t * PACK_SUB, LANES), jnp.uint32)),
        grid=(t // tm,),
        in_specs=[pl.BlockSpec((tm, d), lambda i: (i, 0)),
                  pl.BlockSpec((None, 1, d), lambda i: (l, 0, 0)),
                  pl.BlockSpec((None, d, LANES), lambda i: (l, 0, 0)),
                  pl.BlockSpec((None, 1, LANES), lambda i: (l, 0, 0))],
        out_specs=(pl.BlockSpec((tm, LANES), lambda i: (i, 0)),
                   pl.BlockSpec((1, LANES), lambda i: (0, 0)),
                   pl.BlockSpec((tm * PACK_SUB, LANES), lambda i: (i, 0))),
        scratch_shapes=[pltpu.VMEM((1, LANES), F32), pltpu.VMEM((tm, tm), BF16)],
        compiler_params=_params(("arbitrary",)),
        name="router",
    )(x, g3, wr, br)


def _max_tiles(t):
    return (2 * t) // E_TM + N_EXPERTS


def dispatch_plan(route, counts_row):
    t = route.shape[0]
    max_tiles = _max_tiles(t)
    ids = route[:, 0:2].astype(jnp.int32)
    ranks = route[:, 4:6].astype(jnp.int32)
    counts = counts_row[0, :N_EXPERTS].astype(jnp.int32)
    padded = ((counts + E_TM - 1) // E_TM) * E_TM
    ends = jnp.cumsum(padded)
    offs = ends - padded
    onehot = ids[:, :, None] == jnp.arange(N_EXPERTS, dtype=jnp.int32)
    dest = (jnp.sum(jnp.where(onehot, offs, 0), axis=-1) + ranks).reshape(2 * t)
    n_tiles = ends[-1] // E_TM
    tile_i = jnp.minimum(jnp.arange(max_tiles, dtype=jnp.int32), n_tiles - 1)
    tile_expert = jnp.sum(ends[None, :] <= tile_i[:, None] * E_TM, axis=1).astype(jnp.int32)
    tokens = jnp.arange(2 * t, dtype=jnp.int32) // 2
    row_token = jnp.zeros((max_tiles * E_TM,), jnp.int32).at[dest].set(
        tokens, unique_indices=True, mode="promise_in_bounds")
    idx = jnp.arange(max_tiles, dtype=jnp.int32)
    first = (idx < n_tiles) & ((idx == 0) | (tile_expert != jnp.roll(tile_expert, 1)))
    slot = (jnp.cumsum(first) - 1) % 2
    first_pos = jnp.where(first, idx, max_tiles)
    after = jnp.flip(lax.cummin(jnp.flip(first_pos)))
    nxt_pos = jnp.concatenate([after[1:], jnp.full((1,), max_tiles, jnp.int32)])
    nxt = jnp.where(nxt_pos < max_tiles,
                    tile_expert[jnp.minimum(nxt_pos, max_tiles - 1)], -1)
    tile_plan = jnp.stack([first.astype(jnp.int32), slot.astype(jnp.int32),
                           nxt.astype(jnp.int32)])
    return dest, row_token, tile_expert, n_tiles.reshape(1).astype(jnp.int32), tile_plan


def _experts_kernel(te_ref, nt_ref, rt_ref, plan_ref, h_hbm, w1_hbm, w3_hbm, w2_hbm, ys_ref,
                    xbuf0, xbuf1, xbuf2, sem, w1f, w3f, w2f, wsem, w1b, w3b, w2b,
                    *, max_tiles, layer):
    i = pl.program_id(0)
    nt = nt_ref[0]
    bufs = (xbuf0, xbuf1, xbuf2)
    ahead = len(bufs) - 1

    def weight_copies(e, slot):
        return [pltpu.make_async_copy(w_hbm.at[layer, e], wf.at[slot], wsem.at[slot])
                for w_hbm, wf in ((w1_hbm, w1f), (w3_hbm, w3f), (w2_hbm, w2f))]

    def row_copy(tok, s, r):
        src = h_hbm.at[pl.ds(pl.multiple_of(tok * PACK_SUB, PACK_SUB), PACK_SUB)]
        return pltpu.make_async_copy(src, bufs[s].at[pl.ds(r * PACK_SUB, PACK_SUB)], sem.at[s])

    def tile_wait(s):
        pltpu.make_async_copy(h_hbm.at[pl.ds(0, E_TM * PACK_SUB)], bufs[s], sem.at[s]).wait()

    @pl.when(i == 0)
    def _():
        for c in weight_copies(te_ref[0], 0):
            c.start()
        for t in range(ahead):
            def body(r, c, t=t):
                src = h_hbm.at[pl.ds(pl.multiple_of(rt_ref[t * E_TM + r] * PACK_SUB, PACK_SUB),
                                     PACK_SUB)]
                dst = bufs[t].at[pl.ds(pl.multiple_of(r * PACK_SUB, PACK_SUB), PACK_SUB)]
                pltpu.make_async_copy(src, dst, sem.at[t]).start()
                return c

            lax.fori_loop(0, E_TM, body, 0, unroll=8)

    def step(s):
        @pl.when(plan_ref[0, i] == 1)
        def _():
            slot = plan_ref[1, i]
            for c in weight_copies(te_ref[i], slot):
                c.wait()
            w1b[...] = w1f[slot].astype(BF16)
            w3b[...] = w3f[slot].astype(BF16)
            w2b[...] = w2f[slot].astype(BF16)

            @pl.when(plan_ref[2, i] >= 0)
            def _():
                for c in weight_copies(plan_ref[2, i], 1 - slot):
                    c.start()

        tile_wait(s)

        nxt = (s + ahead) % len(bufs)
        base = jnp.minimum(i + ahead, max_tiles - 1) * E_TM
        for r in range(E_TM):
            row_copy(rt_ref[base + r], nxt, r).start(priority=r % 2)

        lo, hi = _unpack_rows(bufs[s], E_TM)
        hb = jnp.concatenate([v.astype(BF16) for v in lo + hi], axis=1)
        a1 = jnp.dot(hb, w1b[...], preferred_element_type=F32)
        a3 = jnp.dot(hb, w3b[...], preferred_element_type=F32)
        hid = (a1 / (1.0 + jnp.exp(-a1))) * a3
        _pack_rows(jnp.dot(hid.astype(BF16), w2b[...], preferred_element_type=F32), ys_ref)

        @pl.when(i == nt - 1)
        def _():
            for t in range(1, len(bufs)):
                tile_wait((s + t) % len(bufs))

    for s in range(len(bufs)):
        @pl.when((i < nt) & (i % len(bufs) == s))
        def _(s=s):
            step(s)

    @pl.when(i >= nt)
    def _():
        ys_ref[...] = jnp.zeros(ys_ref.shape, ys_ref.dtype)


def experts(hpk, w1, w3, w2, l, tile_expert, n_tiles, row_token, tile_plan):
    d = w1.shape[-2]
    max_tiles = row_token.shape[0] // E_TM
    blk = E_TM * PACK_SUB
    any_spec = pl.BlockSpec(memory_space=pl.ANY)
    grid_spec = pltpu.PrefetchScalarGridSpec(
        num_scalar_prefetch=4,
        grid=(max_tiles,),
        in_specs=[any_spec, any_spec, any_spec, any_spec],
        out_specs=pl.BlockSpec((blk, LANES), lambda i, te, nt, rt, plan: (i, 0)),
        scratch_shapes=[pltpu.VMEM((blk, LANES), jnp.uint32),
                        pltpu.VMEM((blk, LANES), jnp.uint32),
                        pltpu.VMEM((blk, LANES), jnp.uint32),
                        pltpu.SemaphoreType.DMA((3,)),
                        pltpu.VMEM((2, d, D_EXPERT), F32),
                        pltpu.VMEM((2, d, D_EXPERT), F32),
                        pltpu.VMEM((2, D_EXPERT, d), F32),
                        pltpu.SemaphoreType.DMA((2,)),
                        pltpu.VMEM((d, D_EXPERT), BF16),
                        pltpu.VMEM((d, D_EXPERT), BF16),
                        pltpu.VMEM((D_EXPERT, d), BF16)],
    )
    return pl.pallas_call(
        functools.partial(_experts_kernel, max_tiles=max_tiles, layer=l),
        out_shape=jax.ShapeDtypeStruct((max_tiles * blk, LANES), jnp.uint32),
        grid_spec=grid_spec,
        compiler_params=_params(("arbitrary",)),
        name="experts",
    )(tile_expert, n_tiles, row_token, tile_plan, hpk, w1, w3, w2)


def _combine_kernel(dest_ref, ys_hbm, x_ref, route_ref, o_ref, ybuf0, ybuf1, sem):
    i = pl.program_id(0)
    n = pl.num_programs(0)
    bufs = (ybuf0, ybuf1)

    def row_copy(row, s, k, r):
        src = ys_hbm.at[pl.ds(pl.multiple_of(row * PACK_SUB, PACK_SUB), PACK_SUB)]
        return pltpu.make_async_copy(src, bufs[s].at[k, pl.ds(r * PACK_SUB, PACK_SUB)], sem.at[s])

    def tile_wait(s):
        for k in range(2):
            pltpu.make_async_copy(ys_hbm.at[pl.ds(0, C_TM * PACK_SUB)], bufs[s].at[k],
                                  sem.at[s]).wait()

    @pl.when(i == 0)
    def _():
        def body(r, c):
            for k in range(2):
                src = ys_hbm.at[pl.ds(pl.multiple_of(dest_ref[2 * r + k] * PACK_SUB, PACK_SUB),
                                      PACK_SUB)]
                dst = ybuf0.at[k, pl.ds(pl.multiple_of(r * PACK_SUB, PACK_SUB), PACK_SUB)]
                pltpu.make_async_copy(src, dst, sem.at[0]).start()
            return c

        lax.fori_loop(0, C_TM, body, 0, unroll=4)

    def step(s):
        tile_wait(s)
        base = 2 * jnp.minimum(i + 1, n - 1) * C_TM
        for r in range(C_TM):
            for k in range(2):
                row_copy(dest_ref[base + 2 * r + k], 1 - s, k, r).start(priority=k)

        route = route_ref[...]
        lane = lax.broadcasted_iota(jnp.int32, route.shape, 1)
        w1 = jnp.sum(jnp.where(lane == 2, route, 0.0), axis=-1, keepdims=True)
        w2 = jnp.sum(jnp.where(lane == 3, route, 0.0), axis=-1, keepdims=True)
        lo1, hi1 = _unpack_rows(bufs[s].at[0], C_TM)
        lo2, hi2 = _unpack_rows(bufs[s].at[1], C_TM)
        half = x_ref.shape[1] // 2
        for c in range(PACK_SUB):
            for off, y1, y2 in ((0, lo1, lo2), (half, hi1, hi2)):
                cols = slice(off + c * LANES, off + (c + 1) * LANES)
                o_ref[:, cols] = x_ref[:, cols] + w1 * y1[c] + w2 * y2[c]

        @pl.when(i == n - 1)
        def _():
            tile_wait(1 - s)

    for s in range(2):
        @pl.when(i % 2 == s)
        def _(s=s):
            step(s)


def combine(ys, x, route, dest):
    t, d = x.shape
    blk = C_TM * PACK_SUB
    grid_spec = pltpu.PrefetchScalarGridSpec(
        num_scalar_prefetch=1,
        grid=(t // C_TM,),
        in_specs=[pl.BlockSpec(memory_space=pl.ANY),
                  pl.BlockSpec((C_TM, d), lambda i, dest: (i, 0)),
                  pl.BlockSpec((C_TM, LANES), lambda i, dest: (i, 0))],
        out_specs=pl.BlockSpec((C_TM, d), lambda i, dest: (i, 0)),
        scratch_shapes=[pltpu.VMEM((2, blk, LANES), jnp.uint32),
                        pltpu.VMEM((2, blk, LANES), jnp.uint32),
                        pltpu.SemaphoreType.DMA((2,))],
    )
    return pl.pallas_call(
        _combine_kernel,
        out_shape=jax.ShapeDtypeStruct((t, d), F32),
        grid_spec=grid_spec,
        compiler_params=_params(("arbitrary",)),
        name="combine",
    )(dest, ys, x, route)


def kernel(x, mem, norm_mix, w_in, rel_bias_a, t5_table, diff_lambda, diff_subln, w_out,
           norm_cross, norm_mem, w_xq, w_xkv, w_xo, norm_ffn, w_group, b_group,
           w_expert, b_expert, w1, w3, w2, norm_final):
    batch, seq, d = x.shape
    t = batch * seq
    xf = x.reshape(t, d)
    memf = mem.reshape(batch * MEM_LEN, d)

    g_mix = norm_mix.reshape(DEPTH, 1, d)
    g_cross = norm_cross.reshape(DEPTH, 1, d)
    g_mem = norm_mem.reshape(DEPTH, 1, d)
    g_ffn = norm_ffn.reshape(DEPTH, 1, d)
    g_final = norm_final.reshape(1, 1, d)
    subln3 = diff_subln.reshape(DEPTH, 1, HEAD_DIM)

    in_scale = jnp.concatenate([
        jnp.full((A_WIDTH,), HEAD_DIM ** -0.5 * LOG2E, F32), jnp.ones((2 * A_WIDTH,), F32),
        jnp.full((B_WIDTH,), B_QK_DIM ** -0.5 * LOG2E, F32), jnp.ones((2 * B_WIDTH,), F32)]
    ).reshape(1, IN_WIDTH)
    xq_scale = jnp.full((1, d), X_HEAD_DIM ** -0.5, F32)

    far_b, bias_b = t5_bias_tables(t5_table)
    bias_a = band_bias_tables(rel_bias_a)

    pad = LANES - N_EXPERTS - N_GROUPS
    wr = jnp.concatenate([w_expert, w_group, jnp.zeros((DEPTH, d, pad), F32)], axis=-1)
    br = jnp.concatenate([b_expert, b_group, jnp.zeros((DEPTH, pad), F32)],
                         axis=-1).reshape(DEPTH, 1, LANES)

    for l in range(DEPTH):
        lam_init = 0.8 - 0.6 * math.exp(-0.3 * l)
        proj = matmul(xf, w_in, l, BF16, norm_gain=g_mix, colscale=in_scale, tm=1024)
        oa = attn_a(proj, bias_a, l, batch)
        ob = attn_b(proj, far_b, bias_b, diff_lambda, subln3, l, lam_init, batch)
        xf = matmul([oa, ob], w_out, l, F32, resid=xf)
        q = matmul(xf, w_xq, l, BF16, norm_gain=g_cross, colscale=xq_scale, tn=2048)
        kv = matmul(memf, w_xkv, l, BF16, norm_gain=g_mem)
        xo = xattn(q, kv, batch)
        xf = matmul(xo, w_xo, l, F32, resid=xf)
        route, counts, hpk = router(xf, g_ffn, wr, br, l)
        dest, row_token, tile_expert, n_tiles, tile_plan = dispatch_plan(route, counts)
        ys = experts(hpk, w1, w3, w2, l, tile_expert, n_tiles, row_token, tile_plan)
        xf = combine(ys, xf, route, dest)
    out = rmsnorm(xf, g_final, 0, F32)
    return out.reshape(batch, seq, d)
```

```python
import functools
import math

import jax
import jax.numpy as jnp
import numpy as np
from jax import lax
from jax.experimental import pallas as pl
from jax.experimental.pallas import tpu as pltpu

D_MODEL = 2048
SEQ = 4096
DEPTH = 4
CHUNK = 64
LEFT_CHUNKS = 8
HEAD_DIM = 128
A_HEADS = 8
B_HEADS = 8
B_QK_DIM = 64
A_WIDTH = 1024
B_WIDTH = 1024
IN_WIDTH = 6144
REL_CLIP = 128
T5_BUCKETS = 32
T5_MAX_DIST = 512
MEM_LEN = 256
X_HEADS = 4
X_HEAD_DIM = 512
N_GROUPS = 4
EXPERTS_PER_GROUP = 8
N_EXPERTS = 32
D_EXPERT = 256
EPS = 1e-6
NEG_INF = -1e30

LANES = 128
VMEM_LIMIT = 56 * 1024 * 1024

BF16 = jnp.bfloat16
F32 = jnp.float32

A_QB = 256
A_WIN = A_QB + LEFT_CHUNKS * CHUNK
B_QB = 256
B_NEAR = 3
B_HP = 4
A_HP = 4
LOG2E = math.log2(math.e)
X_QB = 512

E_TM = 256
C_TM = 256
D_TM = 256
PACK_SUB = 8


def _params(sem):
    return pltpu.CompilerParams(dimension_semantics=sem, vmem_limit_bytes=VMEM_LIMIT)


def _rmsnorm_kernel(x_ref, g_ref, o_ref):
    x = x_ref[...]
    ms = jnp.mean(x * x, axis=-1, keepdims=True)
    o_ref[...] = (x * lax.rsqrt(ms + EPS) * g_ref[...]).astype(o_ref.dtype)


def rmsnorm(x, g3, l, out_dtype, tm=512):
    m, d = x.shape
    tm = min(tm, m)
    return pl.pallas_call(
        _rmsnorm_kernel,
        out_shape=jax.ShapeDtypeStruct((m, d), out_dtype),
        grid=(m // tm,),
        in_specs=[pl.BlockSpec((tm, d), lambda i: (i, 0)),
                  pl.BlockSpec((None, 1, d), lambda i: (l, 0, 0))],
        out_specs=pl.BlockSpec((tm, d), lambda i: (i, 0)),
        compiler_params=_params(("parallel",)),
        name="rmsnorm",
    )(x, g3)


def _matmul_kernel(*refs, n_parts, has_norm, has_scale, has_resid):
    a_refs, w_ref = refs[:n_parts], refs[n_parts]
    k = n_parts + 1
    g_ref = s_ref = r_ref = None
    if has_norm:
        g_ref = refs[k]; k += 1
    if has_scale:
        s_ref = refs[k]; k += 1
    if has_resid:
        r_ref = refs[k]; k += 1
    o_ref, wbf_ref = refs[k], refs[k + 1]

    @pl.when(pl.program_id(1) == 0)
    def _():
        wbf_ref[...] = w_ref[...].astype(BF16)

    acc = None
    k0 = 0
    for a_ref in a_refs:
        kp = a_ref.shape[1]
        a = a_ref[...]
        if has_norm:
            ms = jnp.mean(a * a, axis=-1, keepdims=True)
            a = (a * lax.rsqrt(ms + EPS) * g_ref[...]).astype(BF16)
        part = jnp.dot(a, wbf_ref[k0:k0 + kp, :], preferred_element_type=F32)
        acc = part if acc is None else acc + part
        k0 += kp
    if has_scale:
        acc = acc * s_ref[...]
    if has_resid:
        acc = acc + r_ref[...]
    o_ref[...] = acc.astype(o_ref.dtype)


def matmul(a_parts, w, l, out_dtype, norm_gain=None, colscale=None, resid=None,
           tm=512, tn=1024):
    if not isinstance(a_parts, (list, tuple)):
        a_parts = [a_parts]
    assert norm_gain is None or len(a_parts) == 1
    m = a_parts[0].shape[0]
    k = sum(a.shape[1] for a in a_parts)
    n = w.shape[-1]
    assert w.shape[-2] == k
    tm = min(tm, m)
    tn = min(tn, n)
    in_specs = [pl.BlockSpec((tm, a.shape[1]), lambda j, i: (i, 0)) for a in a_parts]
    in_specs.append(pl.BlockSpec((None, k, tn), lambda j, i: (l, 0, j)))
    args = list(a_parts) + [w]
    if norm_gain is not None:
        in_specs.append(pl.BlockSpec((None, 1, k), lambda j, i: (l, 0, 0)))
        args.append(norm_gain)
    if colscale is not None:
        in_specs.append(pl.BlockSpec((1, tn), lambda j, i: (0, j)))
        args.append(colscale)
    if resid is not None:
        in_specs.append(pl.BlockSpec((tm, tn), lambda j, i: (i, j)))
        args.append(resid)
    return pl.pallas_call(
        functools.partial(_matmul_kernel, n_parts=len(a_parts), has_norm=norm_gain is not None,
                          has_scale=colscale is not None, has_resid=resid is not None),
        out_shape=jax.ShapeDtypeStruct((m, n), out_dtype),
        grid=(n // tn, m // tm),
        in_specs=in_specs,
        out_specs=pl.BlockSpec((tm, tn), lambda j, i: (i, j)),
        scratch_shapes=[pltpu.VMEM((k, tn), BF16)],
        compiler_params=_params(("parallel", "arbitrary")),
        name="matmul",
    )(*args)


def _toeplitz(v, rows, cols):
    hh, n = v.shape
    assert n == rows + cols - 1
    vp = jnp.pad(v, ((0, 0), (0, 1)))
    skew = jnp.tile(vp, (1, rows))[:, :rows * n].reshape(hh, rows, n)
    return skew[:, :, rows - 1:]


def _attn_a_kernel(q_ref, k_ref, v_ref, b0_ref, b1_ref, b2_ref, o_ref, vt_ref, st_ref):
    i = pl.program_id(2)
    qb = A_QB
    n_kt = A_WIN // qb
    bias_refs = (b0_ref, b1_ref, b2_ref)

    @pl.when(i == 0)
    def _():
        for hh in range(A_HP):
            for c in range(SEQ // qb):
                blk = v_ref[c * qb:(c + 1) * qb, hh * HEAD_DIM:(hh + 1) * HEAD_DIM]
                vt_ref[hh, c] = blk.astype(F32).T.astype(BF16)

    t0 = jnp.maximum(i - (n_kt - 1), 0)
    start = pl.multiple_of(t0 * qb, qb)
    qts = [q_ref[:, hh * HEAD_DIM:(hh + 1) * HEAD_DIM].astype(F32).T.astype(BF16)
           for hh in range(A_HP)]
    for hh in range(A_HP):
        st_ref[hh] = jnp.dot(k_ref[pl.ds(start, A_WIN), hh * HEAD_DIM:(hh + 1) * HEAD_DIM],
                             qts[hh], preferred_element_type=F32)
    for hh in range(A_HP):
        st = st_ref[hh] + jnp.concatenate([r[hh] for r in bias_refs], axis=0)
        m = jnp.max(st, axis=0, keepdims=True)
        p = jnp.exp2(st - m)
        denom = jnp.sum(p, axis=0, keepdims=True)
        pb = p.astype(BF16)
        ot = None
        for u in range(n_kt):
            part = jnp.dot(vt_ref[hh, t0 + u], pb[u * qb:(u + 1) * qb],
                           preferred_element_type=F32)
            ot = part if ot is None else ot + part
        o_ref[:, hh * HEAD_DIM:(hh + 1) * HEAD_DIM] = (ot / denom).T.astype(o_ref.dtype)


def attn_a(proj, bias_a, l, batch):
    t = proj.shape[0]
    nq = SEQ // A_QB
    n_kt = A_WIN // A_QB
    wa = A_HP * HEAD_DIM
    ncol = A_WIDTH // wa

    def bias_spec(u):
        return pl.BlockSpec(
            (None, A_HP, None, A_QB, A_QB),
            lambda b, h, i: (l, h, (n_kt - 1) - jnp.minimum(i, n_kt - 1) + u, 0, 0))

    return pl.pallas_call(
        _attn_a_kernel,
        out_shape=jax.ShapeDtypeStruct((t, A_WIDTH), BF16),
        grid=(batch, A_HEADS // A_HP, nq),
        in_specs=[
            pl.BlockSpec((A_QB, wa), lambda b, h, i: (b * nq + i, h)),
            pl.BlockSpec((SEQ, wa), lambda b, h, i: (b, ncol + h)),
            pl.BlockSpec((SEQ, wa), lambda b, h, i: (b, 2 * ncol + h)),
            bias_spec(0), bias_spec(1), bias_spec(2),
        ],
        out_specs=pl.BlockSpec((A_QB, wa), lambda b, h, i: (b * nq + i, h)),
        scratch_shapes=[pltpu.VMEM((A_HP, SEQ // A_QB, HEAD_DIM, A_QB), BF16),
                        pltpu.VMEM((A_HP, A_WIN, A_QB), F32)],
        compiler_params=_params(("arbitrary", "arbitrary", "arbitrary")),
        name="attn_a",
    )(proj, proj, proj, bias_a, bias_a, bias_a)


def band_bias_tables(rel_tables):
    depth, _, heads = rel_tables.shape
    lead = LEFT_CHUNKS * CHUNK
    n_tiles = (A_WIN + lead) // A_QB
    kk = np.arange(A_QB)[:, None]
    r = np.arange(A_QB)[None, :]
    n = np.arange(2 * A_QB - 1)
    tables = rel_tables.astype(F32)
    tiles = []
    for t in range(n_tiles):
        idx = np.clip(t * A_QB + (A_QB - 1 - n) - lead, -REL_CLIP, REL_CLIP) + REL_CLIP
        v = tables[:, idx, :].transpose(0, 2, 1).reshape(depth * heads, -1)
        dchunk = (t * A_QB + kk - lead) // CHUNK - r // CHUNK
        valid = (dchunk >= -LEFT_CHUNKS) & (dchunk <= 0)
        tiles.append(jnp.where(valid[None], _toeplitz(v, A_QB, A_QB) * LOG2E, NEG_INF))
    return jnp.stack(tiles, axis=1).reshape(depth, heads, n_tiles, A_QB, A_QB)


def _attn_b_kernel(far_ref, q_ref, k_ref, v_ref, bias_ref, lam_ref, g_ref, o_ref,
                   qqt_ref, vt_ref, m_ref, l_ref, acc_ref, st_ref, *, lam_init):
    hp = pl.program_id(1)
    i = pl.program_id(2)
    qb = B_QB

    @pl.when(i == 0)
    def _():
        for hh in range(B_HP):
            for c in range(SEQ // qb):
                blk = v_ref[c * qb:(c + 1) * qb, hh * HEAD_DIM:(hh + 1) * HEAD_DIM]
                vt_ref[hh, c] = blk.astype(F32).T.astype(BF16)

    for hh in range(B_HP):
        qt = q_ref[:, hh * HEAD_DIM:(hh + 1) * HEAD_DIM].astype(F32).T
        row = lax.broadcasted_iota(jnp.int32, qt.shape, 0)
        qqt_ref[hh, :, 0:qb] = jnp.where(row < B_QK_DIM, qt, 0.0).astype(BF16)
        qqt_ref[hh, :, qb:2 * qb] = jnp.where(row >= B_QK_DIM, qt, 0.0).astype(BF16)
        m_ref[hh] = jnp.full((1, 2 * qb), NEG_INF, F32)
        l_ref[hh] = jnp.zeros((1, 2 * qb), F32)
        acc_ref[hh] = jnp.zeros((HEAD_DIM, 2 * qb), F32)

    def tile(j, d, nk=1):
        off = pl.multiple_of(j * qb, qb)
        for hh in range(B_HP):
            st_ref[hh, 0:nk * qb] = jnp.dot(
                k_ref[pl.ds(off, nk * qb), hh * HEAD_DIM:(hh + 1) * HEAD_DIM], qqt_ref[hh],
                preferred_element_type=F32)
        for hh in range(B_HP):
            st = st_ref[hh, 0:nk * qb]
            if d is None:
                shift = far_ref[hp * B_HP + hh]
            else:
                b = jnp.concatenate([bias_ref[hh, du] for du in d], axis=0)
                st = jnp.concatenate([st[:, 0:qb] + b, st[:, qb:2 * qb] + b], axis=1)
                shift = 0.0
            m_old = m_ref[hh]
            m_new = jnp.maximum(m_old, jnp.max(st, axis=0, keepdims=True) + shift)
            alpha = jnp.exp2(m_old - m_new)
            p = jnp.exp2(st - (m_new - shift))
            l_ref[hh] = alpha * l_ref[hh] + jnp.sum(p, axis=0, keepdims=True)
            pb = p.astype(BF16)
            pv = None
            for u in range(nk):
                part = jnp.dot(vt_ref[hh, j + u], pb[u * qb:(u + 1) * qb],
                               preferred_element_type=F32)
                pv = part if pv is None else pv + part
            acc_ref[hh] = alpha * acc_ref[hh] + pv
            m_ref[hh] = m_new

    n_far = jnp.maximum(i - (B_NEAR - 1), 0)

    def far_body(jj, c):
        tile(2 * jj, None, nk=2)
        return c

    lax.fori_loop(0, n_far // 2, far_body, 0)

    @pl.when(n_far % 2 == 1)
    def _():
        tile(n_far - 1, None)

    assert B_NEAR == 3
    @pl.when(i >= 2)
    def _():
        tile(i - 2, (2, 1), nk=2)

    @pl.when(i == 1)
    def _():
        tile(0, (1,))

    tile(i, (0,))

    lv = lam_ref[...]
    lam = (jnp.exp(jnp.sum(lv[0:1] * lv[1:2], axis=-1, keepdims=True))
           - jnp.exp(jnp.sum(lv[2:3] * lv[3:4], axis=-1, keepdims=True)) + lam_init)
    for hh in range(B_HP):
        o = acc_ref[hh] / l_ref[hh]
        od = o[:, 0:qb] - lam * o[:, qb:2 * qb]
        ms = jnp.mean(od * od, axis=0, keepdims=True)
        y = (od * lax.rsqrt(ms + EPS)).T * g_ref[...]
        o_ref[:, hh * HEAD_DIM:(hh + 1) * HEAD_DIM] = (y * (1.0 - lam_init)).astype(o_ref.dtype)


def attn_b(proj, far_bias, bias_b, lam_vecs, subln3, l, lam_init, batch):
    t = proj.shape[0]
    nq = SEQ // B_QB
    wb = B_HP * HEAD_DIM
    qcol = 3 * A_WIDTH // wb
    ncol = B_WIDTH // wb
    return pl.pallas_call(
        functools.partial(_attn_b_kernel, lam_init=lam_init),
        out_shape=jax.ShapeDtypeStruct((t, B_WIDTH), BF16),
        grid=(batch, B_HEADS // B_HP, nq),
        in_specs=[
            pl.BlockSpec(memory_space=pltpu.SMEM),
            pl.BlockSpec((B_QB, wb), lambda b, h, i: (b * nq + i, qcol + h)),
            pl.BlockSpec((SEQ, wb), lambda b, h, i: (b, qcol + ncol + h)),
            pl.BlockSpec((SEQ, wb), lambda b, h, i: (b, qcol + 2 * ncol + h)),
            pl.BlockSpec((B_HP, B_NEAR, B_QB, B_QB), lambda b, h, i: (h, 0, 0, 0)),
            pl.BlockSpec((None, 4, B_QK_DIM), lambda b, h, i: (l, 0, 0)),
            pl.BlockSpec((None, 1, HEAD_DIM), lambda b, h, i: (l, 0, 0)),
        ],
        out_specs=pl.BlockSpec((B_QB, wb), lambda b, h, i: (b * nq + i, h)),
        scratch_shapes=[pltpu.VMEM((B_HP, HEAD_DIM, 2 * B_QB), BF16),
                        pltpu.VMEM((B_HP, SEQ // B_QB, HEAD_DIM, B_QB), BF16),
                        pltpu.VMEM((B_HP, 1, 2 * B_QB), F32),
                        pltpu.VMEM((B_HP, 1, 2 * B_QB), F32),
                        pltpu.VMEM((B_HP, HEAD_DIM, 2 * B_QB), F32),
                        pltpu.VMEM((B_HP, 2 * B_QB, 2 * B_QB), F32)],
        compiler_params=_params(("arbitrary", "arbitrary", "arbitrary")),
        name="attn_b",
    )(far_bias, proj, proj, proj, bias_b, lam_vecs, subln3)


def _t5_bucket(rel):
    half = T5_BUCKETS // 2
    max_exact = half // 2
    ret = jnp.where(rel > 0, half, 0)
    n = jnp.abs(rel)
    n_f = jnp.maximum(n, max_exact).astype(jnp.float32)
    large = max_exact + (jnp.log(n_f / max_exact) / math.log(T5_MAX_DIST / max_exact)
                         * (half - max_exact)).astype(jnp.int32)
    large = jnp.minimum(large, half - 1)
    return ret + jnp.where(n < max_exact, n, large)


def t5_bias_tables(t5_table):
    assert B_NEAR * B_QB - (B_QB - 1) >= T5_MAX_DIST
    lo = -(B_NEAR * B_QB)
    bias1d = t5_table.astype(F32)[_t5_bucket(jnp.arange(lo, B_QB))].T
    far = t5_table.astype(F32)[_t5_bucket(jnp.array([-(SEQ - 1)]))][0]
    u = np.arange(2 * B_QB - 1)
    kk = np.arange(B_QB)[:, None]
    qq = np.arange(B_QB)[None, :]
    tiles = []
    for d in range(B_NEAR):
        t = _toeplitz(bias1d[:, (B_QB - 1 - u - d * B_QB) - lo], B_QB, B_QB)
        if d == 0:
            t = jnp.where(((kk // CHUNK) <= (qq // CHUNK))[None], t * LOG2E, NEG_INF)
        else:
            t = t * LOG2E
        tiles.append(t)
    return far * LOG2E, jnp.stack(tiles, axis=1)


def _xattn_kernel(q_ref, k_ref, v_ref, o_ref, s_ref):
    dh = X_HEAD_DIM
    for h in range(X_HEADS):
        s_ref[h] = lax.dot_general(q_ref[:, h * dh:(h + 1) * dh], k_ref[:, h * dh:(h + 1) * dh],
                                   (((1,), (1,)), ((), ())), preferred_element_type=F32)
    for h in range(X_HEADS):
        s = s_ref[h]
        m = jnp.max(s, axis=-1, keepdims=True)
        p = jnp.exp(s - m)
        denom = jnp.sum(p, axis=-1, keepdims=True)
        o = jnp.dot(p.astype(BF16), v_ref[:, h * dh:(h + 1) * dh], preferred_element_type=F32)
        o_ref[:, h * dh:(h + 1) * dh] = (o / denom).astype(o_ref.dtype)


def xattn(q, kv, batch):
    t, d = q.shape
    nq = SEQ // X_QB
    return pl.pallas_call(
        _xattn_kernel,
        out_shape=jax.ShapeDtypeStruct((t, d), BF16),
        grid=(batch, nq),
        in_specs=[
            pl.BlockSpec((X_QB, d), lambda b, i: (b * nq + i, 0)),
            pl.BlockSpec((MEM_LEN, d), lambda b, i: (b, 0)),
            pl.BlockSpec((MEM_LEN, d), lambda b, i: (b, 1)),
        ],
        out_specs=pl.BlockSpec((X_QB, d), lambda b, i: (b * nq + i, 0)),
        scratch_shapes=[pltpu.VMEM((X_HEADS, X_QB, MEM_LEN), F32)],
        compiler_params=_params(("parallel", "parallel")),
        name="xattn",
    )(q, kv, kv)


def _pack_rows(v, out_ref):
    rows, width = v.shape
    half = width // 2
    lo = lax.bitcast_convert_type(v[:, :half].astype(BF16).astype(F32), jnp.uint32)
    hi = lax.bitcast_convert_type(v[:, half:].astype(BF16).astype(F32), jnp.uint32)
    u = (lo >> 16) | hi
    for c in range(half // LANES):
        out_ref[pl.ds(c, rows, stride=PACK_SUB), :] = u[:, c * LANES:(c + 1) * LANES]


def _unpack_rows(ref, rows):
    lo, hi = [], []
    for c in range(PACK_SUB):
        u = ref[pl.ds(c, rows, stride=PACK_SUB), :]
        lo.append(lax.bitcast_convert_type(u << 16, F32))
        hi.append(lax.bitcast_convert_type(u & jnp.uint32(0xFFFF0000), F32))
    return lo, hi


def _router_kernel(x_ref, g_ref, wr_ref, br_ref, route_ref, cnt_ref, hpk_ref, carry_ref, tri_ref):
    tm = x_ref.shape[0]

    @pl.when(pl.program_id(0) == 0)
    def _():
        carry_ref[...] = jnp.zeros(carry_ref.shape, F32)
        r = lax.broadcasted_iota(jnp.int32, (tm, tm), 0)
        c = lax.broadcasted_iota(jnp.int32, (tm, tm), 1)
        tri_ref[...] = jnp.where(c < r, 1.0, 0.0).astype(BF16)

    x = x_ref[...]
    ms = jnp.mean(x * x, axis=-1, keepdims=True)
    hn = x * lax.rsqrt(ms + EPS) * g_ref[...]
    _pack_rows(hn, hpk_ref)
    hb = hn.astype(BF16)
    logits = jnp.dot(hb, wr_ref[...].astype(BF16), preferred_element_type=F32) + br_ref[...]

    lane = lax.broadcasted_iota(jnp.int32, logits.shape, 1)
    lane_f = lane.astype(F32)
    big = jnp.float32(1e9)

    def top1(mask):
        v = jnp.where(mask, logits, -jnp.inf)
        vmax = jnp.max(v, axis=-1, keepdims=True)
        idx = jnp.min(jnp.where(mask & (v == vmax), lane_f, big), axis=-1, keepdims=True)
        return vmax, idx

    gmask = (lane >= N_EXPERTS) & (lane < N_EXPERTS + N_GROUPS)
    gmax, gidx = top1(gmask)
    gsum = jnp.sum(jnp.where(gmask, jnp.exp(logits - gmax), 0.0), axis=-1, keepdims=True)
    g_gate = 1.0 / gsum
    g_sel = gidx.astype(jnp.int32) - N_EXPERTS

    emask = (lane < N_EXPERTS) & ((lane // EXPERTS_PER_GROUP) == g_sel)
    v1, i1 = top1(emask)
    v2, i2 = top1(emask & (lane_f != i1))
    e2 = jnp.exp(v2 - v1)
    w1 = g_gate / (1.0 + e2)
    w2 = g_gate * e2 / (1.0 + e2)

    oh1 = jnp.where(lane_f == i1, 1.0, 0.0)
    oh2 = jnp.where(lane_f == i2, 1.0, 0.0)
    oh = oh1 + oh2
    before = jnp.dot(tri_ref[...], oh.astype(BF16), preferred_element_type=F32) + carry_ref[...]
    r1 = jnp.sum(oh1 * before, axis=-1, keepdims=True)
    r2 = jnp.sum(oh2 * before, axis=-1, keepdims=True)
    carry_ref[...] = carry_ref[...] + jnp.sum(oh, axis=0, keepdims=True)
    cnt_ref[...] = carry_ref[...]

    route = jnp.zeros(logits.shape, F32)
    for col, val in enumerate((i1, i2, w1, w2, r1, r2)):
        route = jnp.where(lane == col, val, route)
    route_ref[...] = route


def router(x, g3, wr, br, l, tm=512):
    t, d = x.shape
    return pl.pallas_call(
        _router_kernel,
        out_shape=(jax.ShapeDtypeStruct((t, LANES), F32),
                   jax.ShapeDtypeStruct((1, LANES), F32),
                   jax.ShapeDtypeStruct((t * PACK_SUB, LANES), jnp.uint32)),
        grid=(t // tm,),
        in_specs=[pl.BlockSpec((tm, d), lambda i: (i, 0)),
                  pl.BlockSpec((None, 1, d), lambda i: (l, 0, 0)),
                  pl.BlockSpec((None, d, LANES), lambda i: (l, 0, 0)),
                  pl.BlockSpec((None, 1, LANES), lambda i: (l, 0, 0))],
        out_specs=(pl.BlockSpec((tm, LANES), lambda i: (i, 0)),
                   pl.BlockSpec((1, LANES), lambda i: (0, 0)),
                   pl.BlockSpec((tm * PACK_SUB, LANES), lambda i: (i, 0))),
        scratch_shapes=[pltpu.VMEM((1, LANES), F32), pltpu.VMEM((tm, tm), BF16)],
        compiler_params=_params(("arbitrary",)),
        name="router",
    )(x, g3, wr, br)


def _max_tiles(t):
    return (2 * t) // E_TM + N_EXPERTS


def dispatch_plan(route, counts_row):
    t = route.shape[0]
    max_tiles = _max_tiles(t)
    ids = route[:, 0:2].astype(jnp.int32)
    ranks = route[:, 4:6].astype(jnp.int32)
    counts = counts_row[0, :N_EXPERTS].astype(jnp.int32)
    padded = ((counts + E_TM - 1) // E_TM) * E_TM
    ends = jnp.cumsum(padded)
    offs = ends - padded
    onehot = ids[:, :, None] == jnp.arange(N_EXPERTS, dtype=jnp.int32)
    dest = (jnp.sum(jnp.where(onehot, offs, 0), axis=-1) + ranks).reshape(2 * t)
    n_tiles = ends[-1] // E_TM
    idx = jnp.arange(max_tiles, dtype=jnp.int32)
    tile_i = jnp.minimum(idx, n_tiles - 1)
    tile_expert = jnp.sum(ends[None, :] <= tile_i[:, None] * E_TM, axis=1).astype(jnp.int32)
    used_end = (offs + counts)[tile_expert]
    tile_rows = jnp.clip(used_end - tile_i * E_TM, 0, E_TM).astype(jnp.int32)
    first = (idx < n_tiles) & ((idx == 0) | (tile_expert != jnp.roll(tile_expert, 1)))
    slot = (jnp.cumsum(first) - 1) % 2
    first_pos = jnp.where(first, idx, max_tiles)
    after = jnp.flip(lax.cummin(jnp.flip(first_pos)))
    nxt_pos = jnp.concatenate([after[1:], jnp.full((1,), max_tiles, jnp.int32)])
    nxt = jnp.where(nxt_pos < max_tiles,
                    tile_expert[jnp.minimum(nxt_pos, max_tiles - 1)], -1)
    tile_plan = jnp.stack([first.astype(jnp.int32), slot.astype(jnp.int32),
                           nxt.astype(jnp.int32)])
    return dest, tile_expert, tile_rows, n_tiles.reshape(1).astype(jnp.int32), tile_plan


def _dispatch_kernel(dest_ref, h_ref, xs_hbm, stage0, stage1, sem):
    i = pl.program_id(0)
    n = pl.num_programs(0)
    stages = (stage0, stage1)

    def drain(s):
        for _ in range(2):
            pltpu.make_async_copy(stages[s], xs_hbm.at[pl.ds(0, D_TM * PACK_SUB)],
                                  sem.at[s]).wait()

    def step(s):
        @pl.when(i >= 2)
        def _():
            drain(s)

        stages[s][...] = h_ref[...]
        base = 2 * i * D_TM
        for r in range(D_TM):
            for k in range(2):
                row = dest_ref[base + 2 * r + k]
                dst = xs_hbm.at[pl.ds(pl.multiple_of(row * PACK_SUB, PACK_SUB), PACK_SUB)]
                pltpu.make_async_copy(stages[s].at[pl.ds(r * PACK_SUB, PACK_SUB)], dst,
                                      sem.at[s]).start(priority=k)

        @pl.when(i == n - 1)
        def _():
            drain(s)

            @pl.when(n >= 2)
            def _():
                drain(1 - s)

    for s in range(2):
        @pl.when(i % 2 == s)
        def _(s=s):
            step(s)


def dispatch(hpk, dest, n_rows):
    t = dest.shape[0] // 2
    blk = D_TM * PACK_SUB
    grid_spec = pltpu.PrefetchScalarGridSpec(
        num_scalar_prefetch=1,
        grid=(t // D_TM,),
        in_specs=[pl.BlockSpec((blk, LANES), lambda i, dest: (i, 0))],
        out_specs=pl.BlockSpec(memory_space=pl.ANY),
        scratch_shapes=[pltpu.VMEM((blk, LANES), jnp.uint32),
                        pltpu.VMEM((blk, LANES), jnp.uint32),
                        pltpu.SemaphoreType.DMA((2,))],
    )
    return pl.pallas_call(
        _dispatch_kernel,
        out_shape=jax.ShapeDtypeStruct((n_rows * PACK_SUB, LANES), jnp.uint32),
        grid_spec=grid_spec,
        compiler_params=_params(("arbitrary",)),
        name="dispatch",
    )(dest, hpk)


def _experts_kernel(te_ref, nt_ref, rows_ref, plan_ref, xs_ref, w1_hbm, w3_hbm, w2_hbm, ys_ref,
                    w1f, w3f, w2f, wsem, w1b, w3b, w2b, *, layer):
    i = pl.program_id(0)
    nt = nt_ref[0]

    def weight_copies(e, slot):
        return [pltpu.make_async_copy(w_hbm.at[layer, e], wf.at[slot], wsem.at[slot])
                for w_hbm, wf in ((w1_hbm, w1f), (w3_hbm, w3f), (w2_hbm, w2f))]

    @pl.when(i == 0)
    def _():
        for c in weight_copies(te_ref[0], 0):
            c.start()

    @pl.when(i < nt)
    def _():
        @pl.when(plan_ref[0, i] == 1)
        def _():
            slot = plan_ref[1, i]
            for c in weight_copies(te_ref[i], slot):
                c.wait()
            w1b[...] = w1f[slot].astype(BF16)
            w3b[...] = w3f[slot].astype(BF16)
            w2b[...] = w2f[slot].astype(BF16)

            @pl.when(plan_ref[2, i] >= 0)
            def _():
                for c in weight_copies(plan_ref[2, i], 1 - slot):
                    c.start()

        lo, hi = _unpack_rows(xs_ref, E_TM)
        hb = jnp.concatenate([v.astype(BF16) for v in lo + hi], axis=1)
        row = lax.broadcasted_iota(jnp.int32, hb.shape, 0)
        hb = jnp.where(row < rows_ref[i], hb, jnp.zeros_like(hb))
        a1 = jnp.dot(hb, w1b[...], preferred_element_type=F32)
        a3 = jnp.dot(hb, w3b[...], preferred_element_type=F32)
        hid = (a1 / (1.0 + jnp.exp(-a1))) * a3
        _pack_rows(jnp.dot(hid.astype(BF16), w2b[...], preferred_element_type=F32), ys_ref)

    @pl.when(i >= nt)
    def _():
        ys_ref[...] = jnp.zeros(ys_ref.shape, ys_ref.dtype)


def experts(xs, w1, w3, w2, l, tile_expert, n_tiles, tile_rows, tile_plan):
    d = w1.shape[-2]
    max_tiles = tile_expert.shape[0]
    blk = E_TM * PACK_SUB
    any_spec = pl.BlockSpec(memory_space=pl.ANY)
    grid_spec = pltpu.PrefetchScalarGridSpec(
        num_scalar_prefetch=4,
        grid=(max_tiles,),
        in_specs=[pl.BlockSpec((blk, LANES),
                               lambda i, te, nt, rows, plan: (jnp.minimum(i, nt[0] - 1), 0)),
                  any_spec, any_spec, any_spec],
        out_specs=pl.BlockSpec((blk, LANES), lambda i, te, nt, rows, plan: (i, 0)),
        scratch_shapes=[pltpu.VMEM((2, d, D_EXPERT), F32),
                        pltpu.VMEM((2, d, D_EXPERT), F32),
                        pltpu.VMEM((2, D_EXPERT, d), F32),
                        pltpu.SemaphoreType.DMA((2,)),
                        pltpu.VMEM((d, D_EXPERT), BF16),
                        pltpu.VMEM((d, D_EXPERT), BF16),
                        pltpu.VMEM((D_EXPERT, d), BF16)],
    )
    return pl.pallas_call(
        functools.partial(_experts_kernel, layer=l),
        out_shape=jax.ShapeDtypeStruct((max_tiles * blk, LANES), jnp.uint32),
        grid_spec=grid_spec,
        compiler_params=_params(("arbitrary",)),
        name="experts",
    )(tile_expert, n_tiles, tile_rows, tile_plan, xs, w1, w3, w2)


def _combine_kernel(dest_ref, ys_hbm, x_ref, route_ref, o_ref, ybuf0, ybuf1, sem):
    i = pl.program_id(0)
    n = pl.num_programs(0)
    bufs = (ybuf0, ybuf1)

    def row_copy(row, s, k, r):
        src = ys_hbm.at[pl.ds(pl.multiple_of(row * PACK_SUB, PACK_SUB), PACK_SUB)]
        return pltpu.make_async_copy(src, bufs[s].at[k, pl.ds(r * PACK_SUB, PACK_SUB)], sem.at[s])

    def tile_wait(s):
        for k in range(2):
            pltpu.make_async_copy(ys_hbm.at[pl.ds(0, C_TM * PACK_SUB)], bufs[s].at[k],
                                  sem.at[s]).wait()

    @pl.when(i == 0)
    def _():
        def body(r, c):
            for k in range(2):
                src = ys_hbm.at[pl.ds(pl.multiple_of(dest_ref[2 * r + k] * PACK_SUB, PACK_SUB),
                                      PACK_SUB)]
                dst = ybuf0.at[k, pl.ds(pl.multiple_of(r * PACK_SUB, PACK_SUB), PACK_SUB)]
                pltpu.make_async_copy(src, dst, sem.at[0]).start()
            return c

        lax.fori_loop(0, C_TM, body, 0, unroll=4)

    def step(s):
        tile_wait(s)
        base = 2 * jnp.minimum(i + 1, n - 1) * C_TM
        for r in range(C_TM):
            for k in range(2):
                row_copy(dest_ref[base + 2 * r + k], 1 - s, k, r).start(priority=k)

        route = route_ref[...]
        lane = lax.broadcasted_iota(jnp.int32, route.shape, 1)
        w1 = jnp.sum(jnp.where(lane == 2, route, 0.0), axis=-1, keepdims=True)
        w2 = jnp.sum(jnp.where(lane == 3, route, 0.0), axis=-1, keepdims=True)
        lo1, hi1 = _unpack_rows(bufs[s].at[0], C_TM)
        lo2, hi2 = _unpack_rows(bufs[s].at[1], C_TM)
        half = x_ref.shape[1] // 2
        for c in range(PACK_SUB):
            for off, y1, y2 in ((0, lo1, lo2), (half, hi1, hi2)):
                cols = slice(off + c * LANES, off + (c + 1) * LANES)
                o_ref[:, cols] = x_ref[:, cols] + w1 * y1[c] + w2 * y2[c]

        @pl.when(i == n - 1)
        def _():
            tile_wait(1 - s)

    for s in range(2):
        @pl.when(i % 2 == s)
        def _(s=s):
            step(s)


def combine(ys, x, route, dest):
    t, d = x.shape
    blk = C_TM * PACK_SUB
    grid_spec = pltpu.PrefetchScalarGridSpec(
        num_scalar_prefetch=1,
        grid=(t // C_TM,),
        in_specs=[pl.BlockSpec(memory_space=pl.ANY),
                  pl.BlockSpec((C_TM, d), lambda i, dest: (i, 0)),
                  pl.BlockSpec((C_TM, LANES), lambda i, dest: (i, 0))],
        out_specs=pl.BlockSpec((C_TM, d), lambda i, dest: (i, 0)),
        scratch_shapes=[pltpu.VMEM((2, blk, LANES), jnp.uint32),
                        pltpu.VMEM((2, blk, LANES), jnp.uint32),
                        pltpu.SemaphoreType.DMA((2,))],
    )
    return pl.pallas_call(
        _combine_kernel,
        out_shape=jax.ShapeDtypeStruct((t, d), F32),
        grid_spec=grid_spec,
        compiler_params=_params(("arbitrary",)),
        name="combine",
    )(dest, ys, x, route)


def kernel(x, mem, norm_mix, w_in, rel_bias_a, t5_table, diff_lambda, diff_subln, w_out,
           norm_cross, norm_mem, w_xq, w_xkv, w_xo, norm_ffn, w_group, b_group,
           w_expert, b_expert, w1, w3, w2, norm_final):
    batch, seq, d = x.shape
    t = batch * seq
    xf = x.reshape(t, d)
    memf = mem.reshape(batch * MEM_LEN, d)

    g_mix = norm_mix.reshape(DEPTH, 1, d)
    g_cross = norm_cross.reshape(DEPTH, 1, d)
    g_mem = norm_mem.reshape(DEPTH, 1, d)
    g_ffn = norm_ffn.reshape(DEPTH, 1, d)
    g_final = norm_final.reshape(1, 1, d)
    subln3 = diff_subln.reshape(DEPTH, 1, HEAD_DIM)

    in_scale = jnp.concatenate([
        jnp.full((A_WIDTH,), HEAD_DIM ** -0.5 * LOG2E, F32), jnp.ones((2 * A_WIDTH,), F32),
        jnp.full((B_WIDTH,), B_QK_DIM ** -0.5 * LOG2E, F32), jnp.ones((2 * B_WIDTH,), F32)]
    ).reshape(1, IN_WIDTH)
    xq_scale = jnp.full((1, d), X_HEAD_DIM ** -0.5, F32)

    far_b, bias_b = t5_bias_tables(t5_table)
    bias_a = band_bias_tables(rel_bias_a)

    pad = LANES - N_EXPERTS - N_GROUPS
    wr = jnp.concatenate([w_expert, w_group, jnp.zeros((DEPTH, d, pad), F32)], axis=-1)
    br = jnp.concatenate([b_expert, b_group, jnp.zeros((DEPTH, pad), F32)],
                         axis=-1).reshape(DEPTH, 1, LANES)

    for l in range(DEPTH):
        lam_init = 0.8 - 0.6 * math.exp(-0.3 * l)
        proj = matmul(xf, w_in, l, BF16, norm_gain=g_mix, colscale=in_scale, tm=1024)
        oa = attn_a(proj, bias_a, l, batch)
        ob = attn_b(proj, far_b, bias_b, diff_lambda, subln3, l, lam_init, batch)
        xf = matmul([oa, ob], w_out, l, F32, resid=xf)
        q = matmul(xf, w_xq, l, BF16, norm_gain=g_cross, colscale=xq_scale, tn=2048)
        kv = matmul(memf, w_xkv, l, BF16, norm_gain=g_mem)
        xo = xattn(q, kv, batch)
        xf = matmul(xo, w_xo, l, F32, resid=xf)
        route, counts, hpk = router(xf, g_ffn, wr, br, l)
        dest, tile_expert, tile_rows, n_tiles, tile_plan = dispatch_plan(route, counts)
        xs = dispatch(hpk, dest, _max_tiles(t) * E_TM)
        ys = experts(xs, w1, w3, w2, l, tile_expert, n_tiles, tile_rows, tile_plan)
        xf = combine(ys, xf, route, dest)
    out = rmsnorm(xf, g_final, 0, F32)
    return out.reshape(batch, seq, d)
```

```python
import functools
import math

import jax
import jax.numpy as jnp
import numpy as np
from jax import lax
from jax.experimental import pallas as pl
from jax.experimental.pallas import tpu as pltpu

D_MODEL = 2048
SEQ = 4096
DEPTH = 4
CHUNK = 64
LEFT_CHUNKS = 8
HEAD_DIM = 128
A_HEADS = 8
B_HEADS = 8
B_QK_DIM = 64
A_WIDTH = 1024
B_WIDTH = 1024
IN_WIDTH = 6144
REL_CLIP = 128
T5_BUCKETS = 32
T5_MAX_DIST = 512
MEM_LEN = 256
X_HEADS = 4
X_HEAD_DIM = 512
N_GROUPS = 4
EXPERTS_PER_GROUP = 8
N_EXPERTS = 32
D_EXPERT = 256
EPS = 1e-6
NEG_INF = -1e30

LANES = 128
VMEM_LIMIT = 56 * 1024 * 1024

BF16 = jnp.bfloat16
F32 = jnp.float32

A_QB = 256
A_WIN = A_QB + LEFT_CHUNKS * CHUNK
B_QB = 256
B_NEAR = 3
B_HP = 4
A_HP = 4
LOG2E = math.log2(math.e)
X_QB = 512

E_TM = 256
C_TM = 256
D_TM = 256
PACK_SUB = 8


def _params(sem):
    return pltpu.CompilerParams(dimension_semantics=sem, vmem_limit_bytes=VMEM_LIMIT)


def _rmsnorm_kernel(x_ref, g_ref, o_ref):
    x = x_ref[...]
    ms = jnp.mean(x * x, axis=-1, keepdims=True)
    o_ref[...] = (x * lax.rsqrt(ms + EPS) * g_ref[...]).astype(o_ref.dtype)


def rmsnorm(x, g3, l, out_dtype, tm=512):
    m, d = x.shape
    tm = min(tm, m)
    return pl.pallas_call(
        _rmsnorm_kernel,
        out_shape=jax.ShapeDtypeStruct((m, d), out_dtype),
        grid=(m // tm,),
        in_specs=[pl.BlockSpec((tm, d), lambda i: (i, 0)),
                  pl.BlockSpec((None, 1, d), lambda i: (l, 0, 0))],
        out_specs=pl.BlockSpec((tm, d), lambda i: (i, 0)),
        compiler_params=_params(("parallel",)),
        name="rmsnorm",
    )(x, g3)


def _matmul_kernel(*refs, n_parts, has_norm, has_scale, has_resid):
    a_refs, w_ref = refs[:n_parts], refs[n_parts]
    k = n_parts + 1
    g_ref = s_ref = r_ref = None
    if has_norm:
        g_ref = refs[k]; k += 1
    if has_scale:
        s_ref = refs[k]; k += 1
    if has_resid:
        r_ref = refs[k]; k += 1
    o_ref, wbf_ref = refs[k], refs[k + 1]

    @pl.when(pl.program_id(1) == 0)
    def _():
        wbf_ref[...] = w_ref[...].astype(BF16)

    acc = None
    k0 = 0
    for a_ref in a_refs:
        kp = a_ref.shape[1]
        a = a_ref[...]
        if has_norm:
            ms = jnp.mean(a * a, axis=-1, keepdims=True)
            a = (a * lax.rsqrt(ms + EPS) * g_ref[...]).astype(BF16)
        part = jnp.dot(a, wbf_ref[k0:k0 + kp, :], preferred_element_type=F32)
        acc = part if acc is None else acc + part
        k0 += kp
    if has_scale:
        acc = acc * s_ref[...]
    if has_resid:
        acc = acc + r_ref[...]
    o_ref[...] = acc.astype(o_ref.dtype)


def matmul(a_parts, w, l, out_dtype, norm_gain=None, colscale=None, resid=None,
           tm=512, tn=1024):
    if not isinstance(a_parts, (list, tuple)):
        a_parts = [a_parts]
    assert norm_gain is None or len(a_parts) == 1
    m = a_parts[0].shape[0]
    k = sum(a.shape[1] for a in a_parts)
    n = w.shape[-1]
    assert w.shape[-2] == k
    tm = min(tm, m)
    tn = min(tn, n)
    in_specs = [pl.BlockSpec((tm, a.shape[1]), lambda j, i: (i, 0)) for a in a_parts]
    in_specs.append(pl.BlockSpec((None, k, tn), lambda j, i: (l, 0, j)))
    args = list(a_parts) + [w]
    if norm_gain is not None:
        in_specs.append(pl.BlockSpec((None, 1, k), lambda j, i: (l, 0, 0)))
        args.append(norm_gain)
    if colscale is not None:
        in_specs.append(pl.BlockSpec((1, tn), lambda j, i: (0, j)))
        args.append(colscale)
    if resid is not None:
        in_specs.append(pl.BlockSpec((tm, tn), lambda j, i: (i, j)))
        args.append(resid)
    return pl.pallas_call(
        functools.partial(_matmul_kernel, n_parts=len(a_parts), has_norm=norm_gain is not None,
                          has_scale=colscale is not None, has_resid=resid is not None),
        out_shape=jax.ShapeDtypeStruct((m, n), out_dtype),
        grid=(n // tn, m // tm),
        in_specs=in_specs,
        out_specs=pl.BlockSpec((tm, tn), lambda j, i: (i, j)),
        scratch_shapes=[pltpu.VMEM((k, tn), BF16)],
        compiler_params=_params(("parallel", "arbitrary")),
        name="matmul",
    )(*args)


def _toeplitz(v, rows, cols):
    hh, n = v.shape
    assert n == rows + cols - 1
    vp = jnp.pad(v, ((0, 0), (0, 1)))
    skew = jnp.tile(vp, (1, rows))[:, :rows * n].reshape(hh, rows, n)
    return skew[:, :, rows - 1:]


def _attn_a_kernel(q_ref, k_ref, v_ref, b0_ref, b1_ref, b2_ref, o_ref, vt_ref, st_ref):
    i = pl.program_id(2)
    qb = A_QB
    n_kt = A_WIN // qb
    bias_refs = (b0_ref, b1_ref, b2_ref)

    @pl.when(i == 0)
    def _():
        for hh in range(A_HP):
            for c in range(SEQ // qb):
                blk = v_ref[c * qb:(c + 1) * qb, hh * HEAD_DIM:(hh + 1) * HEAD_DIM]
                vt_ref[hh, c] = blk.astype(F32).T.astype(BF16)

    t0 = jnp.maximum(i - (n_kt - 1), 0)
    start = pl.multiple_of(t0 * qb, qb)
    qts = [q_ref[:, hh * HEAD_DIM:(hh + 1) * HEAD_DIM].astype(F32).T.astype(BF16)
           for hh in range(A_HP)]
    for hh in range(A_HP):
        st_ref[hh] = jnp.dot(k_ref[pl.ds(start, A_WIN), hh * HEAD_DIM:(hh + 1) * HEAD_DIM],
                             qts[hh], preferred_element_type=F32)
    for hh in range(A_HP):
        st = st_ref[hh] + jnp.concatenate([r[hh] for r in bias_refs], axis=0)
        m = jnp.max(st, axis=0, keepdims=True)
        p = jnp.exp2(st - m)
        denom = jnp.sum(p, axis=0, keepdims=True)
        pb = p.astype(BF16)
        ot = None
        for u in range(n_kt):
            part = jnp.dot(vt_ref[hh, t0 + u], pb[u * qb:(u + 1) * qb],
                           preferred_element_type=F32)
            ot = part if ot is None else ot + part
        o_ref[:, hh * HEAD_DIM:(hh + 1) * HEAD_DIM] = (ot / denom).T.astype(o_ref.dtype)


def attn_a(proj, bias_a, l, batch):
    t = proj.shape[0]
    nq = SEQ // A_QB
    n_kt = A_WIN // A_QB
    wa = A_HP * HEAD_DIM
    ncol = A_WIDTH // wa

    def bias_spec(u):
        return pl.BlockSpec(
            (None, A_HP, None, A_QB, A_QB),
            lambda b, h, i: (l, h, (n_kt - 1) - jnp.minimum(i, n_kt - 1) + u, 0, 0))

    return pl.pallas_call(
        _attn_a_kernel,
        out_shape=jax.ShapeDtypeStruct((t, A_WIDTH), BF16),
        grid=(batch, A_HEADS // A_HP, nq),
        in_specs=[
            pl.BlockSpec((A_QB, wa), lambda b, h, i: (b * nq + i, h)),
            pl.BlockSpec((SEQ, wa), lambda b, h, i: (b, ncol + h)),
            pl.BlockSpec((SEQ, wa), lambda b, h, i: (b, 2 * ncol + h)),
            bias_spec(0), bias_spec(1), bias_spec(2),
        ],
        out_specs=pl.BlockSpec((A_QB, wa), lambda b, h, i: (b * nq + i, h)),
        scratch_shapes=[pltpu.VMEM((A_HP, SEQ // A_QB, HEAD_DIM, A_QB), BF16),
                        pltpu.VMEM((A_HP, A_WIN, A_QB), F32)],
        compiler_params=_params(("arbitrary", "arbitrary", "arbitrary")),
        name="attn_a",
    )(proj, proj, proj, bias_a, bias_a, bias_a)


def band_bias_tables(rel_tables):
    depth, _, heads = rel_tables.shape
    lead = LEFT_CHUNKS * CHUNK
    n_tiles = (A_WIN + lead) // A_QB
    kk = np.arange(A_QB)[:, None]
    r = np.arange(A_QB)[None, :]
    n = np.arange(2 * A_QB - 1)
    tables = rel_tables.astype(F32)
    tiles = []
    for t in range(n_tiles):
        idx = np.clip(t * A_QB + (A_QB - 1 - n) - lead, -REL_CLIP, REL_CLIP) + REL_CLIP
        v = tables[:, idx, :].transpose(0, 2, 1).reshape(depth * heads, -1)
        dchunk = (t * A_QB + kk - lead) // CHUNK - r // CHUNK
        valid = (dchunk >= -LEFT_CHUNKS) & (dchunk <= 0)
        tiles.append(jnp.where(valid[None], _toeplitz(v, A_QB, A_QB) * LOG2E, NEG_INF))
    return jnp.stack(tiles, axis=1).reshape(depth, heads, n_tiles, A_QB, A_QB)


def _attn_b_kernel(far_ref, q_ref, k_ref, v_ref, bias_ref, lam_ref, g_ref, o_ref,
                   qqt_ref, vt_ref, m_ref, l_ref, acc_ref, st_ref, *, lam_init):
    hp = pl.program_id(1)
    i = pl.program_id(2)
    qb = B_QB

    @pl.when(i == 0)
    def _():
        for hh in range(B_HP):
            for c in range(SEQ // qb):
                blk = v_ref[c * qb:(c + 1) * qb, hh * HEAD_DIM:(hh + 1) * HEAD_DIM]
                vt_ref[hh, c] = blk.astype(F32).T.astype(BF16)

    for hh in range(B_HP):
        qt = q_ref[:, hh * HEAD_DIM:(hh + 1) * HEAD_DIM].astype(F32).T
        row = lax.broadcasted_iota(jnp.int32, qt.shape, 0)
        qqt_ref[hh, :, 0:qb] = jnp.where(row < B_QK_DIM, qt, 0.0).astype(BF16)
        qqt_ref[hh, :, qb:2 * qb] = jnp.where(row >= B_QK_DIM, qt, 0.0).astype(BF16)
        m_ref[hh] = jnp.full((1, 2 * qb), NEG_INF, F32)
        l_ref[hh] = jnp.zeros((1, 2 * qb), F32)
        acc_ref[hh] = jnp.zeros((HEAD_DIM, 2 * qb), F32)

    def tile(j, d, nk=1):
        off = pl.multiple_of(j * qb, qb)
        for hh in range(B_HP):
            st_ref[hh, 0:nk * qb] = jnp.dot(
                k_ref[pl.ds(off, nk * qb), hh * HEAD_DIM:(hh + 1) * HEAD_DIM], qqt_ref[hh],
                preferred_element_type=F32)
        for hh in range(B_HP):
            st = st_ref[hh, 0:nk * qb]
            if d is None:
                shift = far_ref[hp * B_HP + hh]
            else:
                b = jnp.concatenate([bias_ref[hh, du] for du in d], axis=0)
                st = jnp.concatenate([st[:, 0:qb] + b, st[:, qb:2 * qb] + b], axis=1)
                shift = 0.0
            m_old = m_ref[hh]
            m_new = jnp.maximum(m_old, jnp.max(st, axis=0, keepdims=True) + shift)
            alpha = jnp.exp2(m_old - m_new)
            p = jnp.exp2(st - (m_new - shift))
            l_ref[hh] = alpha * l_ref[hh] + jnp.sum(p, axis=0, keepdims=True)
            pb = p.astype(BF16)
            pv = None
            for u in range(nk):
                part = jnp.dot(vt_ref[hh, j + u], pb[u * qb:(u + 1) * qb],
                               preferred_element_type=F32)
                pv = part if pv is None else pv + part
            acc_ref[hh] = alpha * acc_ref[hh] + pv
            m_ref[hh] = m_new

    n_far = jnp.maximum(i - (B_NEAR - 1), 0)

    def far_body(jj, c):
        tile(2 * jj, None, nk=2)
        return c

    lax.fori_loop(0, n_far // 2, far_body, 0)

    @pl.when(n_far % 2 == 1)
    def _():
        tile(n_far - 1, None)

    assert B_NEAR == 3
    @pl.when(i >= 2)
    def _():
        tile(i - 2, (2, 1), nk=2)

    @pl.when(i == 1)
    def _():
        tile(0, (1,))

    tile(i, (0,))

    lv = lam_ref[...]
    lam = (jnp.exp(jnp.sum(lv[0:1] * lv[1:2], axis=-1, keepdims=True))
           - jnp.exp(jnp.sum(lv[2:3] * lv[3:4], axis=-1, keepdims=True)) + lam_init)
    for hh in range(B_HP):
        o = acc_ref[hh] / l_ref[hh]
        od = o[:, 0:qb] - lam * o[:, qb:2 * qb]
        ms = jnp.mean(od * od, axis=0, keepdims=True)
        y = (od * lax.rsqrt(ms + EPS)).T * g_ref[...]
        o_ref[:, hh * HEAD_DIM:(hh + 1) * HEAD_DIM] = (y * (1.0 - lam_init)).astype(o_ref.dtype)


def attn_b(proj, far_bias, bias_b, lam_vecs, subln3, l, lam_init, batch):
    t = proj.shape[0]
    nq = SEQ // B_QB
    wb = B_HP * HEAD_DIM
    qcol = 3 * A_WIDTH // wb
    ncol = B_WIDTH // wb
    return pl.pallas_call(
        functools.partial(_attn_b_kernel, lam_init=lam_init),
        out_shape=jax.ShapeDtypeStruct((t, B_WIDTH), BF16),
        grid=(batch, B_HEADS // B_HP, nq),
        in_specs=[
            pl.BlockSpec(memory_space=pltpu.SMEM),
            pl.BlockSpec((B_QB, wb), lambda b, h, i: (b * nq + i, qcol + h)),
            pl.BlockSpec((SEQ, wb), lambda b, h, i: (b, qcol + ncol + h)),
            pl.BlockSpec((SEQ, wb), lambda b, h, i: (b, qcol + 2 * ncol + h)),
            pl.BlockSpec((B_HP, B_NEAR, B_QB, B_QB), lambda b, h, i: (h, 0, 0, 0)),
            pl.BlockSpec((None, 4, B_QK_DIM), lambda b, h, i: (l, 0, 0)),
            pl.BlockSpec((None, 1, HEAD_DIM), lambda b, h, i: (l, 0, 0)),
        ],
        out_specs=pl.BlockSpec((B_QB, wb), lambda b, h, i: (b * nq + i, h)),
        scratch_shapes=[pltpu.VMEM((B_HP, HEAD_DIM, 2 * B_QB), BF16),
                        pltpu.VMEM((B_HP, SEQ // B_QB, HEAD_DIM, B_QB), BF16),
                        pltpu.VMEM((B_HP, 1, 2 * B_QB), F32),
                        pltpu.VMEM((B_HP, 1, 2 * B_QB), F32),
                        pltpu.VMEM((B_HP, HEAD_DIM, 2 * B_QB), F32),
                        pltpu.VMEM((B_HP, 2 * B_QB, 2 * B_QB), F32)],
        compiler_params=_params(("arbitrary", "arbitrary", "arbitrary")),
        name="attn_b",
    )(far_bias, proj, proj, proj, bias_b, lam_vecs, subln3)


def _t5_bucket(rel):
    half = T5_BUCKETS // 2
    max_exact = half // 2
    ret = jnp.where(rel > 0, half, 0)
    n = jnp.abs(rel)
    n_f = jnp.maximum(n, max_exact).astype(jnp.float32)
    large = max_exact + (jnp.log(n_f / max_exact) / math.log(T5_MAX_DIST / max_exact)
                         * (half - max_exact)).astype(jnp.int32)
    large = jnp.minimum(large, half - 1)
    return ret + jnp.where(n < max_exact, n, large)


def t5_bias_tables(t5_table):
    assert B_NEAR * B_QB - (B_QB - 1) >= T5_MAX_DIST
    lo = -(B_NEAR * B_QB)
    bias1d = t5_table.astype(F32)[_t5_bucket(jnp.arange(lo, B_QB))].T
    far = t5_table.astype(F32)[_t5_bucket(jnp.array([-(SEQ - 1)]))][0]
    u = np.arange(2 * B_QB - 1)
    kk = np.arange(B_QB)[:, None]
    qq = np.arange(B_QB)[None, :]
    tiles = []
    for d in range(B_NEAR):
        t = _toeplitz(bias1d[:, (B_QB - 1 - u - d * B_QB) - lo], B_QB, B_QB)
        if d == 0:
            t = jnp.where(((kk // CHUNK) <= (qq // CHUNK))[None], t * LOG2E, NEG_INF)
        else:
            t = t * LOG2E
        tiles.append(t)
    return far * LOG2E, jnp.stack(tiles, axis=1)


def _xattn_kernel(q_ref, k_ref, v_ref, o_ref, s_ref):
    dh = X_HEAD_DIM
    for h in range(X_HEADS):
        s_ref[h] = lax.dot_general(q_ref[:, h * dh:(h + 1) * dh], k_ref[:, h * dh:(h + 1) * dh],
                                   (((1,), (1,)), ((), ())), preferred_element_type=F32)
    for h in range(X_HEADS):
        s = s_ref[h]
        m = jnp.max(s, axis=-1, keepdims=True)
        p = jnp.exp(s - m)
        denom = jnp.sum(p, axis=-1, keepdims=True)
        o = jnp.dot(p.astype(BF16), v_ref[:, h * dh:(h + 1) * dh], preferred_element_type=F32)
        o_ref[:, h * dh:(h + 1) * dh] = (o / denom).astype(o_ref.dtype)


def xattn(q, kv, batch):
    t, d = q.shape
    nq = SEQ // X_QB
    return pl.pallas_call(
        _xattn_kernel,
        out_shape=jax.ShapeDtypeStruct((t, d), BF16),
        grid=(batch, nq),
        in_specs=[
            pl.BlockSpec((X_QB, d), lambda b, i: (b * nq + i, 0)),
            pl.BlockSpec((MEM_LEN, d), lambda b, i: (b, 0)),
            pl.BlockSpec((MEM_LEN, d), lambda b, i: (b, 1)),
        ],
        out_specs=pl.BlockSpec((X_QB, d), lambda b, i: (b * nq + i, 0)),
        scratch_shapes=[pltpu.VMEM((X_HEADS, X_QB, MEM_LEN), F32)],
        compiler_params=_params(("parallel", "parallel")),
        name="xattn",
    )(q, kv, kv)


def _pack_rows(v, out_ref):
    rows, width = v.shape
    half = width // 2
    lo = lax.bitcast_convert_type(v[:, :half].astype(BF16).astype(F32), jnp.uint32)
    hi = lax.bitcast_convert_type(v[:, half:].astype(BF16).astype(F32), jnp.uint32)
    u = (lo >> 16) | hi
    for c in range(half // LANES):
        out_ref[pl.ds(c, rows, stride=PACK_SUB), :] = u[:, c * LANES:(c + 1) * LANES]


def _unpack_rows(ref, rows):
    lo, hi = [], []
    for c in range(PACK_SUB):
        u = ref[pl.ds(c, rows, stride=PACK_SUB), :]
        lo.append(lax.bitcast_convert_type(u << 16, F32))
        hi.append(lax.bitcast_convert_type(u & jnp.uint32(0xFFFF0000), F32))
    return lo, hi


def _router_kernel(x_ref, g_ref, wr_ref, br_ref, route_ref, cnt_ref, hpk_ref, carry_ref, tri_ref):
    tm = x_ref.shape[0]

    @pl.when(pl.program_id(0) == 0)
    def _():
        carry_ref[...] = jnp.zeros(carry_ref.shape, F32)
        r = lax.broadcasted_iota(jnp.int32, (tm, tm), 0)
        c = lax.broadcasted_iota(jnp.int32, (tm, tm), 1)
        tri_ref[...] = jnp.where(c < r, 1.0, 0.0).astype(BF16)

    x = x_ref[...]
    ms = jnp.mean(x * x, axis=-1, keepdims=True)
    hn = x * lax.rsqrt(ms + EPS) * g_ref[...]
    _pack_rows(hn, hpk_ref)
    hb = hn.astype(BF16)
    logits = jnp.dot(hb, wr_ref[...].astype(BF16), preferred_element_type=F32) + br_ref[...]

    lane = lax.broadcasted_iota(jnp.int32, logits.shape, 1)
    lane_f = lane.astype(F32)
    big = jnp.float32(1e9)

    def top1(mask):
        v = jnp.where(mask, logits, -jnp.inf)
        vmax = jnp.max(v, axis=-1, keepdims=True)
        idx = jnp.min(jnp.where(mask & (v == vmax), lane_f, big), axis=-1, keepdims=True)
        return vmax, idx

    gmask = (lane >= N_EXPERTS) & (lane < N_EXPERTS + N_GROUPS)
    gmax, gidx = top1(gmask)
    gsum = jnp.sum(jnp.where(gmask, jnp.exp(logits - gmax), 0.0), axis=-1, keepdims=True)
    g_gate = 1.0 / gsum
    g_sel = gidx.astype(jnp.int32) - N_EXPERTS

    emask = (lane < N_EXPERTS) & ((lane // EXPERTS_PER_GROUP) == g_sel)
    v1, i1 = top1(emask)
    v2, i2 = top1(emask & (lane_f != i1))
    e2 = jnp.exp(v2 - v1)
    w1 = g_gate / (1.0 + e2)
    w2 = g_gate * e2 / (1.0 + e2)

    oh1 = jnp.where(lane_f == i1, 1.0, 0.0)
    oh2 = jnp.where(lane_f == i2, 1.0, 0.0)
    oh = oh1 + oh2
    before = jnp.dot(tri_ref[...], oh.astype(BF16), preferred_element_type=F32) + carry_ref[...]
    r1 = jnp.sum(oh1 * before, axis=-1, keepdims=True)
    r2 = jnp.sum(oh2 * before, axis=-1, keepdims=True)
    carry_ref[...] = carry_ref[...] + jnp.sum(oh, axis=0, keepdims=True)
    cnt_ref[...] = carry_ref[...]

    route = jnp.zeros(logits.shape, F32)
    for col, val in enumerate((i1, i2, w1, w2, r1, r2)):
        route = jnp.where(lane == col, val, route)
    route_ref[...] = route


def router(x, g3, wr, br, l, tm=512):
    t, d = x.shape
    return pl.pallas_call(
        _router_kernel,
        out_shape=(jax.ShapeDtypeStruct((t, LANES), F32),
                   jax.ShapeDtypeStruct((1, LANES), F32),
                   jax.ShapeDtypeStruct((t * PACK_SUB, LANES), jnp.uint32)),
        grid=(t // tm,),
        in_specs=[pl.BlockSpec((tm, d), lambda i: (i, 0)),
                  pl.BlockSpec((None, 1, d), lambda i: (l, 0, 0)),
                  pl.BlockSpec((None, d, LANES), lambda i: (l, 0, 0)),
                  pl.BlockSpec((None, 1, LANES), lambda i: (l, 0, 0))],
        out_specs=(pl.BlockSpec((tm, LANES), lambda i: (i, 0)),
                   pl.BlockSpec((1, LANES), lambda i: (0, 0)),
                   pl.BlockSpec((tm * PACK_SUB, LANES), lambda i: (i, 0))),
        scratch_shapes=[pltpu.VMEM((1, LANES), F32), pltpu.VMEM((tm, tm), BF16)],
        compiler_params=_params(("arbitrary",)),
        name="router",
    )(x, g3, wr, br)


def _max_tiles(t):
    return (2 * t) // E_TM + N_EXPERTS


def dispatch_plan(route, counts_row):
    t = route.shape[0]
    max_tiles = _max_tiles(t)
    ids = route[:, 0:2].astype(jnp.int32)
    ranks = route[:, 4:6].astype(jnp.int32)
    counts = counts_row[0, :N_EXPERTS].astype(jnp.int32)
    padded = ((counts + E_TM - 1) // E_TM) * E_TM
    ends = jnp.cumsum(padded)
    offs = ends - padded
    onehot = ids[:, :, None] == jnp.arange(N_EXPERTS, dtype=jnp.int32)
    dest = (jnp.sum(jnp.where(onehot, offs, 0), axis=-1) + ranks).reshape(2 * t)
    n_tiles = ends[-1] // E_TM
    idx = jnp.arange(max_tiles, dtype=jnp.int32)
    tile_i = jnp.minimum(idx, n_tiles - 1)
    tile_expert = jnp.sum(ends[None, :] <= tile_i[:, None] * E_TM, axis=1).astype(jnp.int32)
    used_end = (offs + counts)[tile_expert]
    tile_rows = jnp.clip(used_end - tile_i * E_TM, 0, E_TM).astype(jnp.int32)
    tile_fill = ((idx >= n_tiles) | (tile_rows < E_TM)).astype(jnp.int32)
    first = (idx < n_tiles) & ((idx == 0) | (tile_expert != jnp.roll(tile_expert, 1)))
    slot = (jnp.cumsum(first) - 1) % 2
    first_pos = jnp.where(first, idx, max_tiles)
    after = jnp.flip(lax.cummin(jnp.flip(first_pos)))
    nxt_pos = jnp.concatenate([after[1:], jnp.full((1,), max_tiles, jnp.int32)])
    nxt = jnp.where(nxt_pos < max_tiles,
                    tile_expert[jnp.minimum(nxt_pos, max_tiles - 1)], -1)
    tile_plan = jnp.stack([first.astype(jnp.int32), slot.astype(jnp.int32),
                           nxt.astype(jnp.int32)])
    return dest, tile_expert, tile_fill, n_tiles.reshape(1).astype(jnp.int32), tile_plan


def _dispatch_kernel(dest_ref, fill_ref, h_ref, xs_hbm, stage0, stage1, sem, zbuf, zsem):
    i = pl.program_id(0)
    n = pl.num_programs(0)
    stages = (stage0, stage1)

    @pl.when(i == 0)
    def _():
        zbuf[...] = jnp.zeros(zbuf.shape, zbuf.dtype)
        blk = E_TM * PACK_SUB

        def fill_copy(j):
            return pltpu.make_async_copy(zbuf, xs_hbm.at[pl.ds(j * blk, blk)], zsem)

        for j in range(fill_ref.shape[0]):
            @pl.when(fill_ref[j] == 1)
            def _(j=j):
                fill_copy(j).start()
        for j in range(fill_ref.shape[0]):
            @pl.when(fill_ref[j] == 1)
            def _(j=j):
                fill_copy(j).wait()

    def drain(s):
        for _ in range(2):
            pltpu.make_async_copy(stages[s], xs_hbm.at[pl.ds(0, D_TM * PACK_SUB)],
                                  sem.at[s]).wait()

    def step(s):
        @pl.when(i >= 2)
        def _():
            drain(s)

        stages[s][...] = h_ref[...]
        base = 2 * i * D_TM
        for r in range(D_TM):
            for k in range(2):
                row = dest_ref[base + 2 * r + k]
                dst = xs_hbm.at[pl.ds(pl.multiple_of(row * PACK_SUB, PACK_SUB), PACK_SUB)]
                pltpu.make_async_copy(stages[s].at[pl.ds(r * PACK_SUB, PACK_SUB)], dst,
                                      sem.at[s]).start(priority=k)

        @pl.when(i == n - 1)
        def _():
            drain(s)

            @pl.when(n >= 2)
            def _():
                drain(1 - s)

    for s in range(2):
        @pl.when(i % 2 == s)
        def _(s=s):
            step(s)


def dispatch(hpk, dest, tile_fill):
    t = dest.shape[0] // 2
    blk = D_TM * PACK_SUB
    grid_spec = pltpu.PrefetchScalarGridSpec(
        num_scalar_prefetch=2,
        grid=(t // D_TM,),
        in_specs=[pl.BlockSpec((blk, LANES), lambda i, dest, fill: (i, 0))],
        out_specs=pl.BlockSpec(memory_space=pl.ANY),
        scratch_shapes=[pltpu.VMEM((blk, LANES), jnp.uint32),
                        pltpu.VMEM((blk, LANES), jnp.uint32),
                        pltpu.SemaphoreType.DMA((2,)),
                        pltpu.VMEM((E_TM * PACK_SUB, LANES), jnp.uint32),
                        pltpu.SemaphoreType.DMA(())],
    )
    return pl.pallas_call(
        _dispatch_kernel,
        out_shape=jax.ShapeDtypeStruct((tile_fill.shape[0] * E_TM * PACK_SUB, LANES), jnp.uint32),
        grid_spec=grid_spec,
        compiler_params=_params(("arbitrary",)),
        name="dispatch",
    )(dest, tile_fill, hpk)


def _experts_kernel(te_ref, nt_ref, plan_ref, xs_ref, w1_hbm, w3_hbm, w2_hbm, ys_ref,
                    w1f, w3f, w2f, wsem, w1b, w3b, w2b, *, layer):
    i = pl.program_id(0)
    nt = nt_ref[0]

    def weight_copies(e, slot):
        return [pltpu.make_async_copy(w_hbm.at[layer, e], wf.at[slot], wsem.at[slot])
                for w_hbm, wf in ((w1_hbm, w1f), (w3_hbm, w3f), (w2_hbm, w2f))]

    @pl.when(i == 0)
    def _():
        for c in weight_copies(te_ref[0], 0):
            c.start()

    @pl.when(i < nt)
    def _():
        @pl.when(plan_ref[0, i] == 1)
        def _():
            slot = plan_ref[1, i]
            for c in weight_copies(te_ref[i], slot):
                c.wait()
            w1b[...] = w1f[slot].astype(BF16)
            w3b[...] = w3f[slot].astype(BF16)
            w2b[...] = w2f[slot].astype(BF16)

            @pl.when(plan_ref[2, i] >= 0)
            def _():
                for c in weight_copies(plan_ref[2, i], 1 - slot):
                    c.start()

        lo, hi = _unpack_rows(xs_ref, E_TM)
        hb = jnp.concatenate([v.astype(BF16) for v in lo + hi], axis=1)
        a1 = jnp.dot(hb, w1b[...], preferred_element_type=F32)
        a3 = jnp.dot(hb, w3b[...], preferred_element_type=F32)
        hid = (a1 / (1.0 + jnp.exp(-a1))) * a3
        _pack_rows(jnp.dot(hid.astype(BF16), w2b[...], preferred_element_type=F32), ys_ref)

    @pl.when(i >= nt)
    def _():
        ys_ref[...] = jnp.zeros(ys_ref.shape, ys_ref.dtype)


def experts(xs, w1, w3, w2, l, tile_expert, n_tiles, tile_plan):
    d = w1.shape[-2]
    max_tiles = tile_expert.shape[0]
    blk = E_TM * PACK_SUB
    any_spec = pl.BlockSpec(memory_space=pl.ANY)
    grid_spec = pltpu.PrefetchScalarGridSpec(
        num_scalar_prefetch=3,
        grid=(max_tiles,),
        in_specs=[pl.BlockSpec((blk, LANES),
                               lambda i, te, nt, plan: (jnp.minimum(i, nt[0] - 1), 0)),
                  any_spec, any_spec, any_spec],
        out_specs=pl.BlockSpec((blk, LANES), lambda i, te, nt, plan: (i, 0)),
        scratch_shapes=[pltpu.VMEM((2, d, D_EXPERT), F32),
                        pltpu.VMEM((2, d, D_EXPERT), F32),
                        pltpu.VMEM((2, D_EXPERT, d), F32),
                        pltpu.SemaphoreType.DMA((2,)),
                        pltpu.VMEM((d, D_EXPERT), BF16),
                        pltpu.VMEM((d, D_EXPERT), BF16),
                        pltpu.VMEM((D_EXPERT, d), BF16)],
    )
    return pl.pallas_call(
        functools.partial(_experts_kernel, layer=l),
        out_shape=jax.ShapeDtypeStruct((max_tiles * blk, LANES), jnp.uint32),
        grid_spec=grid_spec,
        compiler_params=_params(("arbitrary",)),
        name="experts",
    )(tile_expert, n_tiles, tile_plan, xs, w1, w3, w2)


def _combine_kernel(dest_ref, ys_hbm, x_ref, route_ref, o_ref, ybuf0, ybuf1, sem):
    i = pl.program_id(0)
    n = pl.num_programs(0)
    bufs = (ybuf0, ybuf1)

    def row_copy(row, s, k, r):
        src = ys_hbm.at[pl.ds(pl.multiple_of(row * PACK_SUB, PACK_SUB), PACK_SUB)]
        return pltpu.make_async_copy(src, bufs[s].at[k, pl.ds(r * PACK_SUB, PACK_SUB)], sem.at[s])

    def tile_wait(s):
        for k in range(2):
            pltpu.make_async_copy(ys_hbm.at[pl.ds(0, C_TM * PACK_SUB)], bufs[s].at[k],
                                  sem.at[s]).wait()

    @pl.when(i == 0)
    def _():
        def body(r, c):
            for k in range(2):
                src = ys_hbm.at[pl.ds(pl.multiple_of(dest_ref[2 * r + k] * PACK_SUB, PACK_SUB),
                                      PACK_SUB)]
                dst = ybuf0.at[k, pl.ds(pl.multiple_of(r * PACK_SUB, PACK_SUB), PACK_SUB)]
                pltpu.make_async_copy(src, dst, sem.at[0]).start()
            return c

        lax.fori_loop(0, C_TM, body, 0, unroll=4)

    def step(s):
        tile_wait(s)
        base = 2 * jnp.minimum(i + 1, n - 1) * C_TM
        for r in range(C_TM):
            for k in range(2):
                row_copy(dest_ref[base + 2 * r + k], 1 - s, k, r).start(priority=k)

        route = route_ref[...]
        lane = lax.broadcasted_iota(jnp.int32, route.shape, 1)
        w1 = jnp.sum(jnp.where(lane == 2, route, 0.0), axis=-1, keepdims=True)
        w2 = jnp.sum(jnp.where(lane == 3, route, 0.0), axis=-1, keepdims=True)
        lo1, hi1 = _unpack_rows(bufs[s].at[0], C_TM)
        lo2, hi2 = _unpack_rows(bufs[s].at[1], C_TM)
        half = x_ref.shape[1] // 2
        for c in range(PACK_SUB):
            for off, y1, y2 in ((0, lo1, lo2), (half, hi1, hi2)):
                cols = slice(off + c * LANES, off + (c + 1) * LANES)
                o_ref[:, cols] = x_ref[:, cols] + w1 * y1[c] + w2 * y2[c]

        @pl.when(i == n - 1)
        def _():
            tile_wait(1 - s)

    for s in range(2):
        @pl.when(i % 2 == s)
        def _(s=s):
            step(s)


def combine(ys, x, route, dest):
    t, d = x.shape
    blk = C_TM * PACK_SUB
    grid_spec = pltpu.PrefetchScalarGridSpec(
        num_scalar_prefetch=1,
        grid=(t // C_TM,),
        in_specs=[pl.BlockSpec(memory_space=pl.ANY),
                  pl.BlockSpec((C_TM, d), lambda i, dest: (i, 0)),
                  pl.BlockSpec((C_TM, LANES), lambda i, dest: (i, 0))],
        out_specs=pl.BlockSpec((C_TM, d), lambda i, dest: (i, 0)),
        scratch_shapes=[pltpu.VMEM((2, blk, LANES), jnp.uint32),
                        pltpu.VMEM((2, blk, LANES), jnp.uint32),
                        pltpu.SemaphoreType.DMA((2,))],
    )
    return pl.pallas_call(
        _combine_kernel,
        out_shape=jax.ShapeDtypeStruct((t, d), F32),
        grid_spec=grid_spec,
        compiler_params=_params(("arbitrary",)),
        name="combine",
    )(dest, ys, x, route)


def kernel(x, mem, norm_mix, w_in, rel_bias_a, t5_table, diff_lambda, diff_subln, w_out,
           norm_cross, norm_mem, w_xq, w_xkv, w_xo, norm_ffn, w_group, b_group,
           w_expert, b_expert, w1, w3, w2, norm_final):
    batch, seq, d = x.shape
    t = batch * seq
    xf = x.reshape(t, d)
    memf = mem.reshape(batch * MEM_LEN, d)

    g_mix = norm_mix.reshape(DEPTH, 1, d)
    g_cross = norm_cross.reshape(DEPTH, 1, d)
    g_mem = norm_mem.reshape(DEPTH, 1, d)
    g_ffn = norm_ffn.reshape(DEPTH, 1, d)
    g_final = norm_final.reshape(1, 1, d)
    subln3 = diff_subln.reshape(DEPTH, 1, HEAD_DIM)

    in_scale = jnp.concatenate([
        jnp.full((A_WIDTH,), HEAD_DIM ** -0.5 * LOG2E, F32), jnp.ones((2 * A_WIDTH,), F32),
        jnp.full((B_WIDTH,), B_QK_DIM ** -0.5 * LOG2E, F32), jnp.ones((2 * B_WIDTH,), F32)]
    ).reshape(1, IN_WIDTH)
    xq_scale = jnp.full((1, d), X_HEAD_DIM ** -0.5, F32)

    far_b, bias_b = t5_bias_tables(t5_table)
    bias_a = band_bias_tables(rel_bias_a)

    pad = LANES - N_EXPERTS - N_GROUPS
    wr = jnp.concatenate([w_expert, w_group, jnp.zeros((DEPTH, d, pad), F32)], axis=-1)
    br = jnp.concatenate([b_expert, b_group, jnp.zeros((DEPTH, pad), F32)],
                         axis=-1).reshape(DEPTH, 1, LANES)

    for l in range(DEPTH):
        lam_init = 0.8 - 0.6 * math.exp(-0.3 * l)
        proj = matmul(xf, w_in, l, BF16, norm_gain=g_mix, colscale=in_scale, tm=1024)
        oa = attn_a(proj, bias_a, l, batch)
        ob = attn_b(proj, far_b, bias_b, diff_lambda, subln3, l, lam_init, batch)
        xf = matmul([oa, ob], w_out, l, F32, resid=xf)
        q = matmul(xf, w_xq, l, BF16, norm_gain=g_cross, colscale=xq_scale, tn=2048)
        kv = matmul(memf, w_xkv, l, BF16, norm_gain=g_mem)
        xo = xattn(q, kv, batch)
        xf = matmul(xo, w_xo, l, F32, resid=xf)
        route, counts, hpk = router(xf, g_ffn, wr, br, l)
        dest, tile_expert, tile_fill, n_tiles, tile_plan = dispatch_plan(route, counts)
        xs = dispatch(hpk, dest, tile_fill)
        ys = experts(xs, w1, w3, w2, l, tile_expert, n_tiles, tile_plan)
        xf = combine(ys, xf, route, dest)
    out = rmsnorm(xf, g_final, 0, F32)
    return out.reshape(batch, seq, d)
```

```python
import functools
import math

import jax
import jax.numpy as jnp
import numpy as np
from jax import lax
from jax.experimental import pallas as pl
from jax.experimental.pallas import tpu as pltpu

D_MODEL = 2048
SEQ = 4096
DEPTH = 4
CHUNK = 64
LEFT_CHUNKS = 8
HEAD_DIM = 128
A_HEADS = 8
B_HEADS = 8
B_QK_DIM = 64
A_WIDTH = 1024
B_WIDTH = 1024
IN_WIDTH = 6144
REL_CLIP = 128
T5_BUCKETS = 32
T5_MAX_DIST = 512
MEM_LEN = 256
X_HEADS = 4
X_HEAD_DIM = 512
N_GROUPS = 4
EXPERTS_PER_GROUP = 8
N_EXPERTS = 32
D_EXPERT = 256
EPS = 1e-6
NEG_INF = -1e30

LANES = 128
VMEM_LIMIT = 56 * 1024 * 1024

BF16 = jnp.bfloat16
F32 = jnp.float32

A_QB = 256
A_WIN = A_QB + LEFT_CHUNKS * CHUNK
B_QB = 256
B_NEAR = 3
B_HP = 4
B_FAR_NK = 4
A_HP = 4
LOG2E = math.log2(math.e)
X_QB = 512

E_TM = 256
C_TM = 256
D_TM = 256
PACK_SUB = 8


def _params(sem):
    return pltpu.CompilerParams(dimension_semantics=sem, vmem_limit_bytes=VMEM_LIMIT)


def _rmsnorm_kernel(x_ref, g_ref, o_ref):
    x = x_ref[...]
    ms = jnp.mean(x * x, axis=-1, keepdims=True)
    o_ref[...] = (x * lax.rsqrt(ms + EPS) * g_ref[...]).astype(o_ref.dtype)


def rmsnorm(x, g3, l, out_dtype, tm=512):
    m, d = x.shape
    tm = min(tm, m)
    return pl.pallas_call(
        _rmsnorm_kernel,
        out_shape=jax.ShapeDtypeStruct((m, d), out_dtype),
        grid=(m // tm,),
        in_specs=[pl.BlockSpec((tm, d), lambda i: (i, 0)),
                  pl.BlockSpec((None, 1, d), lambda i: (l, 0, 0))],
        out_specs=pl.BlockSpec((tm, d), lambda i: (i, 0)),
        compiler_params=_params(("parallel",)),
        name="rmsnorm",
    )(x, g3)


def _matmul_kernel(*refs, n_parts, has_norm, has_scale, has_resid):
    a_refs, w_ref = refs[:n_parts], refs[n_parts]
    k = n_parts + 1
    g_ref = s_ref = r_ref = None
    if has_norm:
        g_ref = refs[k]; k += 1
    if has_scale:
        s_ref = refs[k]; k += 1
    if has_resid:
        r_ref = refs[k]; k += 1
    o_ref, wbf_ref = refs[k], refs[k + 1]

    @pl.when(pl.program_id(1) == 0)
    def _():
        wbf_ref[...] = w_ref[...].astype(BF16)

    acc = None
    k0 = 0
    for a_ref in a_refs:
        kp = a_ref.shape[1]
        a = a_ref[...]
        if has_norm:
            ms = jnp.mean(a * a, axis=-1, keepdims=True)
            a = (a * lax.rsqrt(ms + EPS) * g_ref[...]).astype(BF16)
        part = jnp.dot(a, wbf_ref[k0:k0 + kp, :], preferred_element_type=F32)
        acc = part if acc is None else acc + part
        k0 += kp
    if has_scale:
        acc = acc * s_ref[...]
    if has_resid:
        acc = acc + r_ref[...]
    o_ref[...] = acc.astype(o_ref.dtype)


def matmul(a_parts, w, l, out_dtype, norm_gain=None, colscale=None, resid=None,
           tm=512, tn=1024):
    if not isinstance(a_parts, (list, tuple)):
        a_parts = [a_parts]
    assert norm_gain is None or len(a_parts) == 1
    m = a_parts[0].shape[0]
    k = sum(a.shape[1] for a in a_parts)
    n = w.shape[-1]
    assert w.shape[-2] == k
    tm = min(tm, m)
    tn = min(tn, n)
    in_specs = [pl.BlockSpec((tm, a.shape[1]), lambda j, i: (i, 0)) for a in a_parts]
    in_specs.append(pl.BlockSpec((None, k, tn), lambda j, i: (l, 0, j)))
    args = list(a_parts) + [w]
    if norm_gain is not None:
        in_specs.append(pl.BlockSpec((None, 1, k), lambda j, i: (l, 0, 0)))
        args.append(norm_gain)
    if colscale is not None:
        in_specs.append(pl.BlockSpec((1, tn), lambda j, i: (0, j)))
        args.append(colscale)
    if resid is not None:
        in_specs.append(pl.BlockSpec((tm, tn), lambda j, i: (i, j)))
        args.append(resid)
    return pl.pallas_call(
        functools.partial(_matmul_kernel, n_parts=len(a_parts), has_norm=norm_gain is not None,
                          has_scale=colscale is not None, has_resid=resid is not None),
        out_shape=jax.ShapeDtypeStruct((m, n), out_dtype),
        grid=(n // tn, m // tm),
        in_specs=in_specs,
        out_specs=pl.BlockSpec((tm, tn), lambda j, i: (i, j)),
        scratch_shapes=[pltpu.VMEM((k, tn), BF16)],
        compiler_params=_params(("parallel", "arbitrary")),
        name="matmul",
    )(*args)


def _toeplitz(v, rows, cols):
    hh, n = v.shape
    assert n == rows + cols - 1
    vp = jnp.pad(v, ((0, 0), (0, 1)))
    skew = jnp.tile(vp, (1, rows))[:, :rows * n].reshape(hh, rows, n)
    return skew[:, :, rows - 1:]


def _attn_a_kernel(q_ref, k_ref, v_ref, b0_ref, b1_ref, b2_ref, o_ref, vt_ref, st_ref):
    i = pl.program_id(2)
    qb = A_QB
    n_kt = A_WIN // qb
    bias_refs = (b0_ref, b1_ref, b2_ref)

    @pl.when(i == 0)
    def _():
        for hh in range(A_HP):
            for c in range(SEQ // qb):
                blk = v_ref[c * qb:(c + 1) * qb, hh * HEAD_DIM:(hh + 1) * HEAD_DIM]
                vt_ref[hh, c] = blk.astype(F32).T.astype(BF16)

    t0 = jnp.maximum(i - (n_kt - 1), 0)
    start = pl.multiple_of(t0 * qb, qb)
    qts = [q_ref[:, hh * HEAD_DIM:(hh + 1) * HEAD_DIM].astype(F32).T.astype(BF16)
           for hh in range(A_HP)]
    for hh in range(A_HP):
        st_ref[hh] = jnp.dot(k_ref[pl.ds(start, A_WIN), hh * HEAD_DIM:(hh + 1) * HEAD_DIM],
                             qts[hh], preferred_element_type=F32)
    for hh in range(A_HP):
        st = st_ref[hh] + jnp.concatenate([r[hh] for r in bias_refs], axis=0)
        m = jnp.max(st, axis=0, keepdims=True)
        p = jnp.exp2(st - m)
        denom = jnp.sum(p, axis=0, keepdims=True)
        pb = p.astype(BF16)
        ot = None
        for u in range(n_kt):
            part = jnp.dot(vt_ref[hh, t0 + u], pb[u * qb:(u + 1) * qb],
                           preferred_element_type=F32)
            ot = part if ot is None else ot + part
        o_ref[:, hh * HEAD_DIM:(hh + 1) * HEAD_DIM] = (ot / denom).T.astype(o_ref.dtype)


def attn_a(proj, bias_a, l, batch):
    t = proj.shape[0]
    nq = SEQ // A_QB
    n_kt = A_WIN // A_QB
    wa = A_HP * HEAD_DIM
    ncol = A_WIDTH // wa

    def bias_spec(u):
        return pl.BlockSpec(
            (None, A_HP, None, A_QB, A_QB),
            lambda b, h, i: (l, h, (n_kt - 1) - jnp.minimum(i, n_kt - 1) + u, 0, 0))

    return pl.pallas_call(
        _attn_a_kernel,
        out_shape=jax.ShapeDtypeStruct((t, A_WIDTH), BF16),
        grid=(batch, A_HEADS // A_HP, nq),
        in_specs=[
            pl.BlockSpec((A_QB, wa), lambda b, h, i: (b * nq + i, h)),
            pl.BlockSpec((SEQ, wa), lambda b, h, i: (b, ncol + h)),
            pl.BlockSpec((SEQ, wa), lambda b, h, i: (b, 2 * ncol + h)),
            bias_spec(0), bias_spec(1), bias_spec(2),
        ],
        out_specs=pl.BlockSpec((A_QB, wa), lambda b, h, i: (b * nq + i, h)),
        scratch_shapes=[pltpu.VMEM((A_HP, SEQ // A_QB, HEAD_DIM, A_QB), BF16),
                        pltpu.VMEM((A_HP, A_WIN, A_QB), F32)],
        compiler_params=_params(("arbitrary", "arbitrary", "arbitrary")),
        name="attn_a",
    )(proj, proj, proj, bias_a, bias_a, bias_a)


def band_bias_tables(rel_tables):
    depth, _, heads = rel_tables.shape
    lead = LEFT_CHUNKS * CHUNK
    n_tiles = (A_WIN + lead) // A_QB
    kk = np.arange(A_QB)[:, None]
    r = np.arange(A_QB)[None, :]
    n = np.arange(2 * A_QB - 1)
    tables = rel_tables.astype(F32)
    tiles = []
    for t in range(n_tiles):
        idx = np.clip(t * A_QB + (A_QB - 1 - n) - lead, -REL_CLIP, REL_CLIP) + REL_CLIP
        v = tables[:, idx, :].transpose(0, 2, 1).reshape(depth * heads, -1)
        dchunk = (t * A_QB + kk - lead) // CHUNK - r // CHUNK
        valid = (dchunk >= -LEFT_CHUNKS) & (dchunk <= 0)
        tiles.append(jnp.where(valid[None], _toeplitz(v, A_QB, A_QB) * LOG2E, NEG_INF))
    return jnp.stack(tiles, axis=1).reshape(depth, heads, n_tiles, A_QB, A_QB)


def _attn_b_kernel(far_ref, q_ref, k_ref, v_ref, bias_ref, lam_ref, g_ref, o_ref,
                   qqt_ref, vt_ref, m_ref, l_ref, acc_ref, st_ref, *, lam_init):
    hp = pl.program_id(1)
    i = pl.program_id(2)
    qb = B_QB

    @pl.when(i == 0)
    def _():
        for hh in range(B_HP):
            for c in range(SEQ // qb):
                blk = v_ref[c * qb:(c + 1) * qb, hh * HEAD_DIM:(hh + 1) * HEAD_DIM]
                vt_ref[hh, c] = blk.astype(F32).T.astype(BF16)

    for hh in range(B_HP):
        qt = q_ref[:, hh * HEAD_DIM:(hh + 1) * HEAD_DIM].astype(F32).T
        row = lax.broadcasted_iota(jnp.int32, qt.shape, 0)
        qqt_ref[hh, :, 0:qb] = jnp.where(row < B_QK_DIM, qt, 0.0).astype(BF16)
        qqt_ref[hh, :, qb:2 * qb] = jnp.where(row >= B_QK_DIM, qt, 0.0).astype(BF16)
        m_ref[hh] = jnp.full((1, 2 * qb), NEG_INF, F32)
        l_ref[hh] = jnp.zeros((1, 2 * qb), F32)
        acc_ref[hh] = jnp.zeros((HEAD_DIM, 2 * qb), F32)

    def tile(j, d, nk=1):
        off = pl.multiple_of(j * qb, qb)
        for hh in range(B_HP):
            st_ref[hh, 0:nk * qb] = jnp.dot(
                k_ref[pl.ds(off, nk * qb), hh * HEAD_DIM:(hh + 1) * HEAD_DIM], qqt_ref[hh],
                preferred_element_type=F32)
        for hh in range(B_HP):
            st = st_ref[hh, 0:nk * qb]
            if d is None:
                shift = far_ref[hp * B_HP + hh]
            else:
                b = jnp.concatenate([bias_ref[hh, du] for du in d], axis=0)
                st = jnp.concatenate([st[:, 0:qb] + b, st[:, qb:2 * qb] + b], axis=1)
                shift = 0.0
            m_old = m_ref[hh]
            m_new = jnp.maximum(m_old, jnp.max(st, axis=0, keepdims=True) + shift)
            alpha = jnp.exp2(m_old - m_new)
            p = jnp.exp2(st - (m_new - shift))
            l_ref[hh] = alpha * l_ref[hh] + jnp.sum(p, axis=0, keepdims=True)
            pb = p.astype(BF16)
            pv = None
            for u in range(nk):
                part = jnp.dot(vt_ref[hh, j + u], pb[u * qb:(u + 1) * qb],
                               preferred_element_type=F32)
                pv = part if pv is None else pv + part
            acc_ref[hh] = alpha * acc_ref[hh] + pv
            m_ref[hh] = m_new

    n_far = jnp.maximum(i - (B_NEAR - 1), 0)

    def far_body(jj, c):
        tile(B_FAR_NK * jj, None, nk=B_FAR_NK)
        return c

    n_quads = n_far // B_FAR_NK
    lax.fori_loop(0, n_quads, far_body, 0)
    rem = n_far - B_FAR_NK * n_quads
    assert B_FAR_NK == 4

    @pl.when(rem >= 2)
    def _():
        tile(B_FAR_NK * n_quads, None, nk=2)

    @pl.when(rem % 2 == 1)
    def _():
        tile(n_far - 1, None)

    assert B_NEAR == 3
    @pl.when(i >= 2)
    def _():
        tile(i - 2, (2, 1, 0), nk=3)

    @pl.when(i == 1)
    def _():
        tile(0, (1, 0), nk=2)

    @pl.when(i == 0)
    def _():
        tile(0, (0,))

    lv = lam_ref[...]
    lam = (jnp.exp(jnp.sum(lv[0:1] * lv[1:2], axis=-1, keepdims=True))
           - jnp.exp(jnp.sum(lv[2:3] * lv[3:4], axis=-1, keepdims=True)) + lam_init)
    for hh in range(B_HP):
        o = acc_ref[hh] / l_ref[hh]
        od = o[:, 0:qb] - lam * o[:, qb:2 * qb]
        ms = jnp.mean(od * od, axis=0, keepdims=True)
        y = (od * lax.rsqrt(ms + EPS)).T * g_ref[...]
        o_ref[:, hh * HEAD_DIM:(hh + 1) * HEAD_DIM] = (y * (1.0 - lam_init)).astype(o_ref.dtype)


def attn_b(proj, far_bias, bias_b, lam_vecs, subln3, l, lam_init, batch):
    t = proj.shape[0]
    nq = SEQ // B_QB
    wb = B_HP * HEAD_DIM
    qcol = 3 * A_WIDTH // wb
    ncol = B_WIDTH // wb
    return pl.pallas_call(
        functools.partial(_attn_b_kernel, lam_init=lam_init),
        out_shape=jax.ShapeDtypeStruct((t, B_WIDTH), BF16),
        grid=(batch, B_HEADS // B_HP, nq),
        in_specs=[
            pl.BlockSpec(memory_space=pltpu.SMEM),
            pl.BlockSpec((B_QB, wb), lambda b, h, i: (b * nq + i, qcol + h)),
            pl.BlockSpec((SEQ, wb), lambda b, h, i: (b, qcol + ncol + h)),
            pl.BlockSpec((SEQ, wb), lambda b, h, i: (b, qcol + 2 * ncol + h)),
            pl.BlockSpec((B_HP, B_NEAR, B_QB, B_QB), lambda b, h, i: (h, 0, 0, 0)),
            pl.BlockSpec((None, 4, B_QK_DIM), lambda b, h, i: (l, 0, 0)),
            pl.BlockSpec((None, 1, HEAD_DIM), lambda b, h, i: (l, 0, 0)),
        ],
        out_specs=pl.BlockSpec((B_QB, wb), lambda b, h, i: (b * nq + i, h)),
        scratch_shapes=[pltpu.VMEM((B_HP, HEAD_DIM, 2 * B_QB), BF16),
                        pltpu.VMEM((B_HP, SEQ // B_QB, HEAD_DIM, B_QB), BF16),
                        pltpu.VMEM((B_HP, 1, 2 * B_QB), F32),
                        pltpu.VMEM((B_HP, 1, 2 * B_QB), F32),
                        pltpu.VMEM((B_HP, HEAD_DIM, 2 * B_QB), F32),
                        pltpu.VMEM((B_HP, B_FAR_NK * B_QB, 2 * B_QB), F32)],
        compiler_params=_params(("arbitrary", "arbitrary", "arbitrary")),
        name="attn_b",
    )(far_bias, proj, proj, proj, bias_b, lam_vecs, subln3)


def _t5_bucket(rel):
    half = T5_BUCKETS // 2
    max_exact = half // 2
    ret = jnp.where(rel > 0, half, 0)
    n = jnp.abs(rel)
    n_f = jnp.maximum(n, max_exact).astype(jnp.float32)
    large = max_exact + (jnp.log(n_f / max_exact) / math.log(T5_MAX_DIST / max_exact)
                         * (half - max_exact)).astype(jnp.int32)
    large = jnp.minimum(large, half - 1)
    return ret + jnp.where(n < max_exact, n, large)


def t5_bias_tables(t5_table):
    assert B_NEAR * B_QB - (B_QB - 1) >= T5_MAX_DIST
    lo = -(B_NEAR * B_QB)
    bias1d = t5_table.astype(F32)[_t5_bucket(jnp.arange(lo, B_QB))].T
    far = t5_table.astype(F32)[_t5_bucket(jnp.array([-(SEQ - 1)]))][0]
    u = np.arange(2 * B_QB - 1)
    kk = np.arange(B_QB)[:, None]
    qq = np.arange(B_QB)[None, :]
    tiles = []
    for d in range(B_NEAR):
        t = _toeplitz(bias1d[:, (B_QB - 1 - u - d * B_QB) - lo], B_QB, B_QB)
        if d == 0:
            t = jnp.where(((kk // CHUNK) <= (qq // CHUNK))[None], t * LOG2E, NEG_INF)
        else:
            t = t * LOG2E
        tiles.append(t)
    return far * LOG2E, jnp.stack(tiles, axis=1)


def _xattn_kernel(q_ref, k_ref, v_ref, o_ref, s_ref):
    dh = X_HEAD_DIM
    for h in range(X_HEADS):
        s_ref[h] = lax.dot_general(q_ref[:, h * dh:(h + 1) * dh], k_ref[:, h * dh:(h + 1) * dh],
                                   (((1,), (1,)), ((), ())), preferred_element_type=F32)
    for h in range(X_HEADS):
        s = s_ref[h]
        m = jnp.max(s, axis=-1, keepdims=True)
        p = jnp.exp(s - m)
        denom = jnp.sum(p, axis=-1, keepdims=True)
        o = jnp.dot(p.astype(BF16), v_ref[:, h * dh:(h + 1) * dh], preferred_element_type=F32)
        o_ref[:, h * dh:(h + 1) * dh] = (o / denom).astype(o_ref.dtype)


def xattn(q, kv, batch):
    t, d = q.shape
    nq = SEQ // X_QB
    return pl.pallas_call(
        _xattn_kernel,
        out_shape=jax.ShapeDtypeStruct((t, d), BF16),
        grid=(batch, nq),
        in_specs=[
            pl.BlockSpec((X_QB, d), lambda b, i: (b * nq + i, 0)),
            pl.BlockSpec((MEM_LEN, d), lambda b, i: (b, 0)),
            pl.BlockSpec((MEM_LEN, d), lambda b, i: (b, 1)),
        ],
        out_specs=pl.BlockSpec((X_QB, d), lambda b, i: (b * nq + i, 0)),
        scratch_shapes=[pltpu.VMEM((X_HEADS, X_QB, MEM_LEN), F32)],
        compiler_params=_params(("parallel", "parallel")),
        name="xattn",
    )(q, kv, kv)


def _pack_rows(v, out_ref):
    rows, width = v.shape
    half = width // 2
    lo = lax.bitcast_convert_type(v[:, :half].astype(BF16).astype(F32), jnp.uint32)
    hi = lax.bitcast_convert_type(v[:, half:].astype(BF16).astype(F32), jnp.uint32)
    u = (lo >> 16) | hi
    for c in range(half // LANES):
        out_ref[pl.ds(c, rows, stride=PACK_SUB), :] = u[:, c * LANES:(c + 1) * LANES]


def _unpack_rows(ref, rows):
    lo, hi = [], []
    for c in range(PACK_SUB):
        u = ref[pl.ds(c, rows, stride=PACK_SUB), :]
        lo.append(lax.bitcast_convert_type(u << 16, F32))
        hi.append(lax.bitcast_convert_type(u & jnp.uint32(0xFFFF0000), F32))
    return lo, hi


def _router_kernel(x_ref, g_ref, wr_ref, br_ref, route_ref, cnt_ref, hpk_ref, carry_ref, tri_ref):
    tm = x_ref.shape[0]

    @pl.when(pl.program_id(0) == 0)
    def _():
        carry_ref[...] = jnp.zeros(carry_ref.shape, F32)
        r = lax.broadcasted_iota(jnp.int32, (tm, tm), 0)
        c = lax.broadcasted_iota(jnp.int32, (tm, tm), 1)
        tri_ref[...] = jnp.where(c < r, 1.0, 0.0).astype(BF16)

    x = x_ref[...]
    ms = jnp.mean(x * x, axis=-1, keepdims=True)
    hn = x * lax.rsqrt(ms + EPS) * g_ref[...]
    _pack_rows(hn, hpk_ref)
    hb = hn.astype(BF16)
    logits = jnp.dot(hb, wr_ref[...].astype(BF16), preferred_element_type=F32) + br_ref[...]

    lane = lax.broadcasted_iota(jnp.int32, logits.shape, 1)
    lane_f = lane.astype(F32)
    big = jnp.float32(1e9)

    def top1(mask):
        v = jnp.where(mask, logits, -jnp.inf)
        vmax = jnp.max(v, axis=-1, keepdims=True)
        idx = jnp.min(jnp.where(mask & (v == vmax), lane_f, big), axis=-1, keepdims=True)
        return vmax, idx

    gmask = (lane >= N_EXPERTS) & (lane < N_EXPERTS + N_GROUPS)
    gmax, gidx = top1(gmask)
    gsum = jnp.sum(jnp.where(gmask, jnp.exp(logits - gmax), 0.0), axis=-1, keepdims=True)
    g_gate = 1.0 / gsum
    g_sel = gidx.astype(jnp.int32) - N_EXPERTS

    emask = (lane < N_EXPERTS) & ((lane // EXPERTS_PER_GROUP) == g_sel)
    v1, i1 = top1(emask)
    v2, i2 = top1(emask & (lane_f != i1))
    e2 = jnp.exp(v2 - v1)
    w1 = g_gate / (1.0 + e2)
    w2 = g_gate * e2 / (1.0 + e2)

    oh1 = jnp.where(lane_f == i1, 1.0, 0.0)
    oh2 = jnp.where(lane_f == i2, 1.0, 0.0)
    oh = oh1 + oh2
    before = jnp.dot(tri_ref[...], oh.astype(BF16), preferred_element_type=F32) + carry_ref[...]
    r1 = jnp.sum(oh1 * before, axis=-1, keepdims=True)
    r2 = jnp.sum(oh2 * before, axis=-1, keepdims=True)
    carry_ref[...] = carry_ref[...] + jnp.sum(oh, axis=0, keepdims=True)
    cnt_ref[...] = carry_ref[...]

    route = jnp.zeros(logits.shape, F32)
    for col, val in enumerate((i1, i2, w1, w2, r1, r2)):
        route = jnp.where(lane == col, val, route)
    route_ref[...] = route


def router(x, g3, wr, br, l, tm=512):
    t, d = x.shape
    return pl.pallas_call(
        _router_kernel,
        out_shape=(jax.ShapeDtypeStruct((t, LANES), F32),
                   jax.ShapeDtypeStruct((1, LANES), F32),
                   jax.ShapeDtypeStruct((t * PACK_SUB, LANES), jnp.uint32)),
        grid=(t // tm,),
        in_specs=[pl.BlockSpec((tm, d), lambda i: (i, 0)),
                  pl.BlockSpec((None, 1, d), lambda i: (l, 0, 0)),
                  pl.BlockSpec((None, d, LANES), lambda i: (l, 0, 0)),
                  pl.BlockSpec((None, 1, LANES), lambda i: (l, 0, 0))],
        out_specs=(pl.BlockSpec((tm, LANES), lambda i: (i, 0)),
                   pl.BlockSpec((1, LANES), lambda i: (0, 0)),
                   pl.BlockSpec((tm * PACK_SUB, LANES), lambda i: (i, 0))),
        scratch_shapes=[pltpu.VMEM((1, LANES), F32), pltpu.VMEM((tm, tm), BF16)],
        compiler_params=_params(("arbitrary",)),
        name="router",
    )(x, g3, wr, br)


def _max_tiles(t):
    return (2 * t) // E_TM + N_EXPERTS


def dispatch_plan(route, counts_row):
    t = route.shape[0]
    max_tiles = _max_tiles(t)
    ids = route[:, 0:2].astype(jnp.int32)
    ranks = route[:, 4:6].astype(jnp.int32)
    counts = counts_row[0, :N_EXPERTS].astype(jnp.int32)
    padded = ((counts + E_TM - 1) // E_TM) * E_TM
    ends = jnp.cumsum(padded)
    offs = ends - padded
    onehot = ids[:, :, None] == jnp.arange(N_EXPERTS, dtype=jnp.int32)
    dest = (jnp.sum(jnp.where(onehot, offs, 0), axis=-1) + ranks).reshape(2 * t)
    n_tiles = ends[-1] // E_TM
    idx = jnp.arange(max_tiles, dtype=jnp.int32)
    tile_i = jnp.minimum(idx, n_tiles - 1)
    tile_expert = jnp.sum(ends[None, :] <= tile_i[:, None] * E_TM, axis=1).astype(jnp.int32)
    used_end = (offs + counts)[tile_expert]
    tile_rows = jnp.clip(used_end - tile_i * E_TM, 0, E_TM).astype(jnp.int32)
    tile_fill = ((idx >= n_tiles) | (tile_rows < E_TM)).astype(jnp.int32)
    first = (idx < n_tiles) & ((idx == 0) | (tile_expert != jnp.roll(tile_expert, 1)))
    slot = (jnp.cumsum(first) - 1) % 2
    first_pos = jnp.where(first, idx, max_tiles)
    after = jnp.flip(lax.cummin(jnp.flip(first_pos)))
    nxt_pos = jnp.concatenate([after[1:], jnp.full((1,), max_tiles, jnp.int32)])
    nxt = jnp.where(nxt_pos < max_tiles,
                    tile_expert[jnp.minimum(nxt_pos, max_tiles - 1)], -1)
    tile_plan = jnp.stack([first.astype(jnp.int32), slot.astype(jnp.int32),
                           nxt.astype(jnp.int32)])
    return dest, tile_expert, tile_fill, n_tiles.reshape(1).astype(jnp.int32), tile_plan


def _dispatch_kernel(dest_ref, fill_ref, h_ref, xs_hbm, stage0, stage1, sem, zbuf, zsem):
    i = pl.program_id(0)
    n = pl.num_programs(0)
    stages = (stage0, stage1)

    @pl.when(i == 0)
    def _():
        zbuf[...] = jnp.zeros(zbuf.shape, zbuf.dtype)
        blk = E_TM * PACK_SUB

        def fill_copy(j):
            return pltpu.make_async_copy(zbuf, xs_hbm.at[pl.ds(j * blk, blk)], zsem)

        for j in range(fill_ref.shape[0]):
            @pl.when(fill_ref[j] == 1)
            def _(j=j):
                fill_copy(j).start()
        for j in range(fill_ref.shape[0]):
            @pl.when(fill_ref[j] == 1)
            def _(j=j):
                fill_copy(j).wait()

    def drain(s):
        for _ in range(2):
            pltpu.make_async_copy(stages[s], xs_hbm.at[pl.ds(0, D_TM * PACK_SUB)],
                                  sem.at[s]).wait()

    def step(s):
        @pl.when(i >= 2)
        def _():
            drain(s)

        stages[s][...] = h_ref[...]
        base = 2 * i * D_TM
        for r in range(D_TM):
            for k in range(2):
                row = dest_ref[base + 2 * r + k]
                dst = xs_hbm.at[pl.ds(pl.multiple_of(row * PACK_SUB, PACK_SUB), PACK_SUB)]
                pltpu.make_async_copy(stages[s].at[pl.ds(r * PACK_SUB, PACK_SUB)], dst,
                                      sem.at[s]).start(priority=k)

        @pl.when(i == n - 1)
        def _():
            drain(s)

            @pl.when(n >= 2)
            def _():
                drain(1 - s)

    for s in range(2):
        @pl.when(i % 2 == s)
        def _(s=s):
            step(s)


def dispatch(hpk, dest, tile_fill):
    t = dest.shape[0] // 2
    blk = D_TM * PACK_SUB
    grid_spec = pltpu.PrefetchScalarGridSpec(
        num_scalar_prefetch=2,
        grid=(t // D_TM,),
        in_specs=[pl.BlockSpec((blk, LANES), lambda i, dest, fill: (i, 0))],
        out_specs=pl.BlockSpec(memory_space=pl.ANY),
        scratch_shapes=[pltpu.VMEM((blk, LANES), jnp.uint32),
                        pltpu.VMEM((blk, LANES), jnp.uint32),
                        pltpu.SemaphoreType.DMA((2,)),
                        pltpu.VMEM((E_TM * PACK_SUB, LANES), jnp.uint32),
                        pltpu.SemaphoreType.DMA(())],
    )
    return pl.pallas_call(
        _dispatch_kernel,
        out_shape=jax.ShapeDtypeStruct((tile_fill.shape[0] * E_TM * PACK_SUB, LANES), jnp.uint32),
        grid_spec=grid_spec,
        compiler_params=_params(("arbitrary",)),
        name="dispatch",
    )(dest, tile_fill, hpk)


def _experts_kernel(te_ref, nt_ref, plan_ref, xs_ref, w1_hbm, w3_hbm, w2_hbm, ys_ref,
                    w1f, w3f, w2f, wsem, w1b, w3b, w2b, *, layer):
    i = pl.program_id(0)
    nt = nt_ref[0]

    def weight_copies(e, slot):
        return [pltpu.make_async_copy(w_hbm.at[layer, e], wf.at[slot], wsem.at[slot])
                for w_hbm, wf in ((w1_hbm, w1f), (w3_hbm, w3f), (w2_hbm, w2f))]

    @pl.when(i == 0)
    def _():
        for c in weight_copies(te_ref[0], 0):
            c.start()

    @pl.when(i < nt)
    def _():
        @pl.when(plan_ref[0, i] == 1)
        def _():
            slot = plan_ref[1, i]
            for c in weight_copies(te_ref[i], slot):
                c.wait()
            w1b[...] = w1f[slot].astype(BF16)
            w3b[...] = w3f[slot].astype(BF16)
            w2b[...] = w2f[slot].astype(BF16)

            @pl.when(plan_ref[2, i] >= 0)
            def _():
                for c in weight_copies(plan_ref[2, i], 1 - slot):
                    c.start()

        lo, hi = _unpack_rows(xs_ref, E_TM)
        hb = jnp.concatenate([v.astype(BF16) for v in lo + hi], axis=1)
        a1 = jnp.dot(hb, w1b[...], preferred_element_type=F32)
        a3 = jnp.dot(hb, w3b[...], preferred_element_type=F32)
        hid = (a1 / (1.0 + jnp.exp(-a1))) * a3
        _pack_rows(jnp.dot(hid.astype(BF16), w2b[...], preferred_element_type=F32), ys_ref)

    @pl.when(i >= nt)
    def _():
        ys_ref[...] = jnp.zeros(ys_ref.shape, ys_ref.dtype)


def experts(xs, w1, w3, w2, l, tile_expert, n_tiles, tile_plan):
    d = w1.shape[-2]
    max_tiles = tile_expert.shape[0]
    blk = E_TM * PACK_SUB
    any_spec = pl.BlockSpec(memory_space=pl.ANY)
    grid_spec = pltpu.PrefetchScalarGridSpec(
        num_scalar_prefetch=3,
        grid=(max_tiles,),
        in_specs=[pl.BlockSpec((blk, LANES),
                               lambda i, te, nt, plan: (jnp.minimum(i, nt[0] - 1), 0)),
                  any_spec, any_spec, any_spec],
        out_specs=pl.BlockSpec((blk, LANES), lambda i, te, nt, plan: (i, 0)),
        scratch_shapes=[pltpu.VMEM((2, d, D_EXPERT), F32),
                        pltpu.VMEM((2, d, D_EXPERT), F32),
                        pltpu.VMEM((2, D_EXPERT, d), F32),
                        pltpu.SemaphoreType.DMA((2,)),
                        pltpu.VMEM((d, D_EXPERT), BF16),
                        pltpu.VMEM((d, D_EXPERT), BF16),
                        pltpu.VMEM((D_EXPERT, d), BF16)],
    )
    return pl.pallas_call(
        functools.partial(_experts_kernel, layer=l),
        out_shape=jax.ShapeDtypeStruct((max_tiles * blk, LANES), jnp.uint32),
        grid_spec=grid_spec,
        compiler_params=_params(("arbitrary",)),
        name="experts",
    )(tile_expert, n_tiles, tile_plan, xs, w1, w3, w2)


def _combine_kernel(dest_ref, ys_hbm, x_ref, route_ref, o_ref, ybuf0, ybuf1, sem):
    i = pl.program_id(0)
    n = pl.num_programs(0)
    bufs = (ybuf0, ybuf1)

    def row_copy(row, s, k, r):
        src = ys_hbm.at[pl.ds(pl.multiple_of(row * PACK_SUB, PACK_SUB), PACK_SUB)]
        return pltpu.make_async_copy(src, bufs[s].at[k, pl.ds(r * PACK_SUB, PACK_SUB)], sem.at[s])

    def tile_wait(s):
        for k in range(2):
            pltpu.make_async_copy(ys_hbm.at[pl.ds(0, C_TM * PACK_SUB)], bufs[s].at[k],
                                  sem.at[s]).wait()

    @pl.when(i == 0)
    def _():
        def body(r, c):
            for k in range(2):
                src = ys_hbm.at[pl.ds(pl.multiple_of(dest_ref[2 * r + k] * PACK_SUB, PACK_SUB),
                                      PACK_SUB)]
                dst = ybuf0.at[k, pl.ds(pl.multiple_of(r * PACK_SUB, PACK_SUB), PACK_SUB)]
                pltpu.make_async_copy(src, dst, sem.at[0]).start()
            return c

        lax.fori_loop(0, C_TM, body, 0, unroll=4)

    def step(s):
        tile_wait(s)
        base = 2 * jnp.minimum(i + 1, n - 1) * C_TM
        for r in range(C_TM):
            for k in range(2):
                row_copy(dest_ref[base + 2 * r + k], 1 - s, k, r).start(priority=k)

        route = route_ref[...]
        lane = lax.broadcasted_iota(jnp.int32, route.shape, 1)
        w1 = jnp.sum(jnp.where(lane == 2, route, 0.0), axis=-1, keepdims=True)
        w2 = jnp.sum(jnp.where(lane == 3, route, 0.0), axis=-1, keepdims=True)
        lo1, hi1 = _unpack_rows(bufs[s].at[0], C_TM)
        lo2, hi2 = _unpack_rows(bufs[s].at[1], C_TM)
        half = x_ref.shape[1] // 2
        for c in range(PACK_SUB):
            for off, y1, y2 in ((0, lo1, lo2), (half, hi1, hi2)):
                cols = slice(off + c * LANES, off + (c + 1) * LANES)
                o_ref[:, cols] = x_ref[:, cols] + w1 * y1[c] + w2 * y2[c]

        @pl.when(i == n - 1)
        def _():
            tile_wait(1 - s)

    for s in range(2):
        @pl.when(i % 2 == s)
        def _(s=s):
            step(s)


def combine(ys, x, route, dest):
    t, d = x.shape
    blk = C_TM * PACK_SUB
    grid_spec = pltpu.PrefetchScalarGridSpec(
        num_scalar_prefetch=1,
        grid=(t // C_TM,),
        in_specs=[pl.BlockSpec(memory_space=pl.ANY),
                  pl.BlockSpec((C_TM, d), lambda i, dest: (i, 0)),
                  pl.BlockSpec((C_TM, LANES), lambda i, dest: (i, 0))],
        out_specs=pl.BlockSpec((C_TM, d), lambda i, dest: (i, 0)),
        scratch_shapes=[pltpu.VMEM((2, blk, LANES), jnp.uint32),
                        pltpu.VMEM((2, blk, LANES), jnp.uint32),
                        pltpu.SemaphoreType.DMA((2,))],
    )
    return pl.pallas_call(
        _combine_kernel,
        out_shape=jax.ShapeDtypeStruct((t, d), F32),
        grid_spec=grid_spec,
        compiler_params=_params(("arbitrary",)),
        name="combine",
    )(dest, ys, x, route)


def kernel(x, mem, norm_mix, w_in, rel_bias_a, t5_table, diff_lambda, diff_subln, w_out,
           norm_cross, norm_mem, w_xq, w_xkv, w_xo, norm_ffn, w_group, b_group,
           w_expert, b_expert, w1, w3, w2, norm_final):
    batch, seq, d = x.shape
    t = batch * seq
    xf = x.reshape(t, d)
    memf = mem.reshape(batch * MEM_LEN, d)

    g_mix = norm_mix.reshape(DEPTH, 1, d)
    g_cross = norm_cross.reshape(DEPTH, 1, d)
    g_mem = norm_mem.reshape(DEPTH, 1, d)
    g_ffn = norm_ffn.reshape(DEPTH, 1, d)
    g_final = norm_final.reshape(1, 1, d)
    subln3 = diff_subln.reshape(DEPTH, 1, HEAD_DIM)

    in_scale = jnp.concatenate([
        jnp.full((A_WIDTH,), HEAD_DIM ** -0.5 * LOG2E, F32), jnp.ones((2 * A_WIDTH,), F32),
        jnp.full((B_WIDTH,), B_QK_DIM ** -0.5 * LOG2E, F32), jnp.ones((2 * B_WIDTH,), F32)]
    ).reshape(1, IN_WIDTH)
    xq_scale = jnp.full((1, d), X_HEAD_DIM ** -0.5, F32)

    far_b, bias_b = t5_bias_tables(t5_table)
    bias_a = band_bias_tables(rel_bias_a)

    pad = LANES - N_EXPERTS - N_GROUPS
    wr = jnp.concatenate([w_expert, w_group, jnp.zeros((DEPTH, d, pad), F32)], axis=-1)
    br = jnp.concatenate([b_expert, b_group, jnp.zeros((DEPTH, pad), F32)],
                         axis=-1).reshape(DEPTH, 1, LANES)

    for l in range(DEPTH):
        lam_init = 0.8 - 0.6 * math.exp(-0.3 * l)
        proj = matmul(xf, w_in, l, BF16, norm_gain=g_mix, colscale=in_scale, tm=1024)
        oa = attn_a(proj, bias_a, l, batch)
        ob = attn_b(proj, far_b, bias_b, diff_lambda, subln3, l, lam_init, batch)
        xf = matmul([oa, ob], w_out, l, F32, resid=xf)
        q = matmul(xf, w_xq, l, BF16, norm_gain=g_cross, colscale=xq_scale, tn=2048)
        kv = matmul(memf, w_xkv, l, BF16, norm_gain=g_mem)
        xo = xattn(q, kv, batch)
        xf = matmul(xo, w_xo, l, F32, resid=xf)
        route, counts, hpk = router(xf, g_ffn, wr, br, l)
        dest, tile_expert, tile_fill, n_tiles, tile_plan = dispatch_plan(route, counts)
        xs = dispatch(hpk, dest, tile_fill)
        ys = experts(xs, w1, w3, w2, l, tile_expert, n_tiles, tile_plan)
        xf = combine(ys, xf, route, dest)
    out = rmsnorm(xf, g_final, 0, F32)
    return out.reshape(batch, seq, d)
```

```python
import functools
import math

import jax
import jax.numpy as jnp
import numpy as np
from jax import lax
from jax.experimental import pallas as pl
from jax.experimental.pallas import tpu as pltpu

D_MODEL = 2048
SEQ = 4096
DEPTH = 4
CHUNK = 64
LEFT_CHUNKS = 8
HEAD_DIM = 128
A_HEADS = 8
B_HEADS = 8
B_QK_DIM = 64
A_WIDTH = 1024
B_WIDTH = 1024
IN_WIDTH = 6144
REL_CLIP = 128
T5_BUCKETS = 32
T5_MAX_DIST = 512
MEM_LEN = 256
X_HEADS = 4
X_HEAD_DIM = 512
N_GROUPS = 4
EXPERTS_PER_GROUP = 8
N_EXPERTS = 32
D_EXPERT = 256
EPS = 1e-6
NEG_INF = -1e30

LANES = 128
VMEM_LIMIT = 56 * 1024 * 1024

BF16 = jnp.bfloat16
F32 = jnp.float32

A_QB = 256
A_WIN = A_QB + LEFT_CHUNKS * CHUNK
B_QB = 256
B_NEAR = 3
B_HP = 4
B_FAR_NK = 4
A_HP = 4
LOG2E = math.log2(math.e)
X_QB = 512

E_TM = 256
C_TM = 256
D_TM = 256
FILL_PADDED, FILL_TAIL = 1, 2
PACK_SUB = 8


def _params(sem):
    return pltpu.CompilerParams(dimension_semantics=sem, vmem_limit_bytes=VMEM_LIMIT)


def _rmsnorm_kernel(x_ref, g_ref, o_ref):
    x = x_ref[...]
    ms = jnp.mean(x * x, axis=-1, keepdims=True)
    o_ref[...] = (x * lax.rsqrt(ms + EPS) * g_ref[...]).astype(o_ref.dtype)


def rmsnorm(x, g3, l, out_dtype, tm=512):
    m, d = x.shape
    tm = min(tm, m)
    return pl.pallas_call(
        _rmsnorm_kernel,
        out_shape=jax.ShapeDtypeStruct((m, d), out_dtype),
        grid=(m // tm,),
        in_specs=[pl.BlockSpec((tm, d), lambda i: (i, 0)),
                  pl.BlockSpec((None, 1, d), lambda i: (l, 0, 0))],
        out_specs=pl.BlockSpec((tm, d), lambda i: (i, 0)),
        compiler_params=_params(("parallel",)),
        name="rmsnorm",
    )(x, g3)


def _matmul_kernel(*refs, n_parts, has_norm, has_scale, has_resid):
    a_refs, w_ref = refs[:n_parts], refs[n_parts]
    k = n_parts + 1
    g_ref = s_ref = r_ref = None
    if has_norm:
        g_ref = refs[k]; k += 1
    if has_scale:
        s_ref = refs[k]; k += 1
    if has_resid:
        r_ref = refs[k]; k += 1
    o_ref, wbf_ref = refs[k], refs[k + 1]

    @pl.when(pl.program_id(1) == 0)
    def _():
        wbf_ref[...] = w_ref[...].astype(BF16)

    acc = None
    k0 = 0
    for a_ref in a_refs:
        kp = a_ref.shape[1]
        a = a_ref[...]
        if has_norm:
            ms = jnp.mean(a * a, axis=-1, keepdims=True)
            a = (a * lax.rsqrt(ms + EPS) * g_ref[...]).astype(BF16)
        part = jnp.dot(a, wbf_ref[k0:k0 + kp, :], preferred_element_type=F32)
        acc = part if acc is None else acc + part
        k0 += kp
    if has_scale:
        acc = acc * s_ref[...]
    if has_resid:
        acc = acc + r_ref[...]
    o_ref[...] = acc.astype(o_ref.dtype)


def matmul(a_parts, w, l, out_dtype, norm_gain=None, colscale=None, resid=None,
           tm=512, tn=1024):
    if not isinstance(a_parts, (list, tuple)):
        a_parts = [a_parts]
    assert norm_gain is None or len(a_parts) == 1
    m = a_parts[0].shape[0]
    k = sum(a.shape[1] for a in a_parts)
    n = w.shape[-1]
    assert w.shape[-2] == k
    tm = min(tm, m)
    tn = min(tn, n)
    in_specs = [pl.BlockSpec((tm, a.shape[1]), lambda j, i: (i, 0)) for a in a_parts]
    in_specs.append(pl.BlockSpec((None, k, tn), lambda j, i: (l, 0, j)))
    args = list(a_parts) + [w]
    if norm_gain is not None:
        in_specs.append(pl.BlockSpec((None, 1, k), lambda j, i: (l, 0, 0)))
        args.append(norm_gain)
    if colscale is not None:
        in_specs.append(pl.BlockSpec((1, tn), lambda j, i: (0, j)))
        args.append(colscale)
    if resid is not None:
        in_specs.append(pl.BlockSpec((tm, tn), lambda j, i: (i, j)))
        args.append(resid)
    return pl.pallas_call(
        functools.partial(_matmul_kernel, n_parts=len(a_parts), has_norm=norm_gain is not None,
                          has_scale=colscale is not None, has_resid=resid is not None),
        out_shape=jax.ShapeDtypeStruct((m, n), out_dtype),
        grid=(n // tn, m // tm),
        in_specs=in_specs,
        out_specs=pl.BlockSpec((tm, tn), lambda j, i: (i, j)),
        scratch_shapes=[pltpu.VMEM((k, tn), BF16)],
        compiler_params=_params(("parallel", "arbitrary")),
        name="matmul",
    )(*args)


def _toeplitz(v, rows, cols):
    hh, n = v.shape
    assert n == rows + cols - 1
    vp = jnp.pad(v, ((0, 0), (0, 1)))
    skew = jnp.tile(vp, (1, rows))[:, :rows * n].reshape(hh, rows, n)
    return skew[:, :, rows - 1:]


def _attn_a_kernel(q_ref, k_ref, v_ref, b0_ref, b1_ref, b2_ref, o_ref, vt_ref, st_ref):
    i = pl.program_id(2)
    qb = A_QB
    n_kt = A_WIN // qb
    bias_refs = (b0_ref, b1_ref, b2_ref)

    @pl.when(i == 0)
    def _():
        for hh in range(A_HP):
            for c in range(SEQ // qb):
                blk = v_ref[c * qb:(c + 1) * qb, hh * HEAD_DIM:(hh + 1) * HEAD_DIM]
                vt_ref[hh, c] = blk.astype(F32).T.astype(BF16)

    t0 = jnp.maximum(i - (n_kt - 1), 0)
    start = pl.multiple_of(t0 * qb, qb)
    qts = [q_ref[:, hh * HEAD_DIM:(hh + 1) * HEAD_DIM].astype(F32).T.astype(BF16)
           for hh in range(A_HP)]
    for hh in range(A_HP):
        st_ref[hh] = jnp.dot(k_ref[pl.ds(start, A_WIN), hh * HEAD_DIM:(hh + 1) * HEAD_DIM],
                             qts[hh], preferred_element_type=F32)
    for hh in range(A_HP):
        st = st_ref[hh] + jnp.concatenate([r[hh] for r in bias_refs], axis=0)
        m = jnp.max(st, axis=0, keepdims=True)
        p = jnp.exp2(st - m)
        denom = jnp.sum(p, axis=0, keepdims=True)
        pb = p.astype(BF16)
        ot = None
        for u in range(n_kt):
            part = jnp.dot(vt_ref[hh, t0 + u], pb[u * qb:(u + 1) * qb],
                           preferred_element_type=F32)
            ot = part if ot is None else ot + part
        o_ref[:, hh * HEAD_DIM:(hh + 1) * HEAD_DIM] = (ot / denom).T.astype(o_ref.dtype)


def attn_a(proj, bias_a, l, batch):
    t = proj.shape[0]
    nq = SEQ // A_QB
    n_kt = A_WIN // A_QB
    wa = A_HP * HEAD_DIM
    ncol = A_WIDTH // wa

    def bias_spec(u):
        return pl.BlockSpec(
            (None, A_HP, None, A_QB, A_QB),
            lambda b, h, i: (l, h, (n_kt - 1) - jnp.minimum(i, n_kt - 1) + u, 0, 0))

    return pl.pallas_call(
        _attn_a_kernel,
        out_shape=jax.ShapeDtypeStruct((t, A_WIDTH), BF16),
        grid=(batch, A_HEADS // A_HP, nq),
        in_specs=[
            pl.BlockSpec((A_QB, wa), lambda b, h, i: (b * nq + i, h)),
            pl.BlockSpec((SEQ, wa), lambda b, h, i: (b, ncol + h)),
            pl.BlockSpec((SEQ, wa), lambda b, h, i: (b, 2 * ncol + h)),
            bias_spec(0), bias_spec(1), bias_spec(2),
        ],
        out_specs=pl.BlockSpec((A_QB, wa), lambda b, h, i: (b * nq + i, h)),
        scratch_shapes=[pltpu.VMEM((A_HP, SEQ // A_QB, HEAD_DIM, A_QB), BF16),
                        pltpu.VMEM((A_HP, A_WIN, A_QB), F32)],
        compiler_params=_params(("arbitrary", "arbitrary", "arbitrary")),
        name="attn_a",
    )(proj, proj, proj, bias_a, bias_a, bias_a)


def band_bias_tables(rel_tables):
    depth, _, heads = rel_tables.shape
    lead = LEFT_CHUNKS * CHUNK
    n_tiles = (A_WIN + lead) // A_QB
    kk = np.arange(A_QB)[:, None]
    r = np.arange(A_QB)[None, :]
    n = np.arange(2 * A_QB - 1)
    tables = rel_tables.astype(F32)
    tiles = []
    for t in range(n_tiles):
        idx = np.clip(t * A_QB + (A_QB - 1 - n) - lead, -REL_CLIP, REL_CLIP) + REL_CLIP
        v = tables[:, idx, :].transpose(0, 2, 1).reshape(depth * heads, -1)
        dchunk = (t * A_QB + kk - lead) // CHUNK - r // CHUNK
        valid = (dchunk >= -LEFT_CHUNKS) & (dchunk <= 0)
        tiles.append(jnp.where(valid[None], _toeplitz(v, A_QB, A_QB) * LOG2E, NEG_INF))
    return jnp.stack(tiles, axis=1).reshape(depth, heads, n_tiles, A_QB, A_QB)


def _attn_b_kernel(far_ref, q_ref, k_ref, v_ref, bias_ref, lam_ref, g_ref, o_ref,
                   qqt_ref, vt_ref, m_ref, l_ref, acc_ref, st_ref, *, lam_init):
    hp = pl.program_id(1)
    i = pl.program_id(2)
    qb = B_QB

    @pl.when(i == 0)
    def _():
        for hh in range(B_HP):
            for c in range(SEQ // qb):
                blk = v_ref[c * qb:(c + 1) * qb, hh * HEAD_DIM:(hh + 1) * HEAD_DIM]
                vt_ref[hh, c] = blk.astype(F32).T.astype(BF16)

    for hh in range(B_HP):
        qt = q_ref[:, hh * HEAD_DIM:(hh + 1) * HEAD_DIM].astype(F32).T
        row = lax.broadcasted_iota(jnp.int32, qt.shape, 0)
        qqt_ref[hh, :, 0:qb] = jnp.where(row < B_QK_DIM, qt, 0.0).astype(BF16)
        qqt_ref[hh, :, qb:2 * qb] = jnp.where(row >= B_QK_DIM, qt, 0.0).astype(BF16)
        m_ref[hh] = jnp.full((1, 2 * qb), NEG_INF, F32)
        l_ref[hh] = jnp.zeros((1, 2 * qb), F32)
        acc_ref[hh] = jnp.zeros((HEAD_DIM, 2 * qb), F32)

    def tile(j, d, nk=1):
        off = pl.multiple_of(j * qb, qb)
        for hh in range(B_HP):
            st_ref[hh, 0:nk * qb] = jnp.dot(
                k_ref[pl.ds(off, nk * qb), hh * HEAD_DIM:(hh + 1) * HEAD_DIM], qqt_ref[hh],
                preferred_element_type=F32)
        for hh in range(B_HP):
            st = st_ref[hh, 0:nk * qb]
            if d is None:
                shift = far_ref[hp * B_HP + hh]
            else:
                b = jnp.concatenate([bias_ref[hh, du] for du in d], axis=0)
                st = jnp.concatenate([st[:, 0:qb] + b, st[:, qb:2 * qb] + b], axis=1)
                shift = 0.0
            m_old = m_ref[hh]
            m_new = jnp.maximum(m_old, jnp.max(st, axis=0, keepdims=True) + shift)
            alpha = jnp.exp2(m_old - m_new)
            p = jnp.exp2(st - (m_new - shift))
            l_ref[hh] = alpha * l_ref[hh] + jnp.sum(p, axis=0, keepdims=True)
            pb = p.astype(BF16)
            pv = None
            for u in range(nk):
                part = jnp.dot(vt_ref[hh, j + u], pb[u * qb:(u + 1) * qb],
                               preferred_element_type=F32)
                pv = part if pv is None else pv + part
            acc_ref[hh] = alpha * acc_ref[hh] + pv
            m_ref[hh] = m_new

    n_far = jnp.maximum(i - (B_NEAR - 1), 0)

    def far_body(jj, c):
        tile(B_FAR_NK * jj, None, nk=B_FAR_NK)
        return c

    n_quads = n_far // B_FAR_NK
    lax.fori_loop(0, n_quads, far_body, 0)
    rem = n_far - B_FAR_NK * n_quads
    assert B_FAR_NK == 4

    @pl.when(rem >= 2)
    def _():
        tile(B_FAR_NK * n_quads, None, nk=2)

    @pl.when(rem % 2 == 1)
    def _():
        tile(n_far - 1, None)

    assert B_NEAR == 3
    @pl.when(i >= 2)
    def _():
        tile(i - 2, (2, 1, 0), nk=3)

    @pl.when(i == 1)
    def _():
        tile(0, (1, 0), nk=2)

    @pl.when(i == 0)
    def _():
        tile(0, (0,))

    lv = lam_ref[...]
    lam = (jnp.exp(jnp.sum(lv[0:1] * lv[1:2], axis=-1, keepdims=True))
           - jnp.exp(jnp.sum(lv[2:3] * lv[3:4], axis=-1, keepdims=True)) + lam_init)
    for hh in range(B_HP):
        o = acc_ref[hh] / l_ref[hh]
        od = o[:, 0:qb] - lam * o[:, qb:2 * qb]
        ms = jnp.mean(od * od, axis=0, keepdims=True)
        y = (od * lax.rsqrt(ms + EPS)).T * g_ref[...]
        o_ref[:, hh * HEAD_DIM:(hh + 1) * HEAD_DIM] = (y * (1.0 - lam_init)).astype(o_ref.dtype)


def attn_b(proj, far_bias, bias_b, lam_vecs, subln3, l, lam_init, batch):
    t = proj.shape[0]
    nq = SEQ // B_QB
    wb = B_HP * HEAD_DIM
    qcol = 3 * A_WIDTH // wb
    ncol = B_WIDTH // wb
    return pl.pallas_call(
        functools.partial(_attn_b_kernel, lam_init=lam_init),
        out_shape=jax.ShapeDtypeStruct((t, B_WIDTH), BF16),
        grid=(batch, B_HEADS // B_HP, nq),
        in_specs=[
            pl.BlockSpec(memory_space=pltpu.SMEM),
            pl.BlockSpec((B_QB, wb), lambda b, h, i: (b * nq + i, qcol + h)),
            pl.BlockSpec((SEQ, wb), lambda b, h, i: (b, qcol + ncol + h)),
            pl.BlockSpec((SEQ, wb), lambda b, h, i: (b, qcol + 2 * ncol + h)),
            pl.BlockSpec((B_HP, B_NEAR, B_QB, B_QB), lambda b, h, i: (h, 0, 0, 0)),
            pl.BlockSpec((None, 4, B_QK_DIM), lambda b, h, i: (l, 0, 0)),
            pl.BlockSpec((None, 1, HEAD_DIM), lambda b, h, i: (l, 0, 0)),
        ],
        out_specs=pl.BlockSpec((B_QB, wb), lambda b, h, i: (b * nq + i, h)),
        scratch_shapes=[pltpu.VMEM((B_HP, HEAD_DIM, 2 * B_QB), BF16),
                        pltpu.VMEM((B_HP, SEQ // B_QB, HEAD_DIM, B_QB), BF16),
                        pltpu.VMEM((B_HP, 1, 2 * B_QB), F32),
                        pltpu.VMEM((B_HP, 1, 2 * B_QB), F32),
                        pltpu.VMEM((B_HP, HEAD_DIM, 2 * B_QB), F32),
                        pltpu.VMEM((B_HP, B_FAR_NK * B_QB, 2 * B_QB), F32)],
        compiler_params=_params(("arbitrary", "arbitrary", "arbitrary")),
        name="attn_b",
    )(far_bias, proj, proj, proj, bias_b, lam_vecs, subln3)


def _t5_bucket(rel):
    half = T5_BUCKETS // 2
    max_exact = half // 2
    ret = jnp.where(rel > 0, half, 0)
    n = jnp.abs(rel)
    n_f = jnp.maximum(n, max_exact).astype(jnp.float32)
    large = max_exact + (jnp.log(n_f / max_exact) / math.log(T5_MAX_DIST / max_exact)
                         * (half - max_exact)).astype(jnp.int32)
    large = jnp.minimum(large, half - 1)
    return ret + jnp.where(n < max_exact, n, large)


def t5_bias_tables(t5_table):
    assert B_NEAR * B_QB - (B_QB - 1) >= T5_MAX_DIST
    lo = -(B_NEAR * B_QB)
    bias1d = t5_table.astype(F32)[_t5_bucket(jnp.arange(lo, B_QB))].T
    far = t5_table.astype(F32)[_t5_bucket(jnp.array([-(SEQ - 1)]))][0]
    u = np.arange(2 * B_QB - 1)
    kk = np.arange(B_QB)[:, None]
    qq = np.arange(B_QB)[None, :]
    tiles = []
    for d in range(B_NEAR):
        t = _toeplitz(bias1d[:, (B_QB - 1 - u - d * B_QB) - lo], B_QB, B_QB)
        if d == 0:
            t = jnp.where(((kk // CHUNK) <= (qq // CHUNK))[None], t * LOG2E, NEG_INF)
        else:
            t = t * LOG2E
        tiles.append(t)
    return far * LOG2E, jnp.stack(tiles, axis=1)


def _xattn_kernel(q_ref, k_ref, v_ref, o_ref, s_ref):
    dh = X_HEAD_DIM
    for h in range(X_HEADS):
        s_ref[h] = lax.dot_general(q_ref[:, h * dh:(h + 1) * dh], k_ref[:, h * dh:(h + 1) * dh],
                                   (((1,), (1,)), ((), ())), preferred_element_type=F32)
    for h in range(X_HEADS):
        s = s_ref[h]
        m = jnp.max(s, axis=-1, keepdims=True)
        p = jnp.exp(s - m)
        denom = jnp.sum(p, axis=-1, keepdims=True)
        o = jnp.dot(p.astype(BF16), v_ref[:, h * dh:(h + 1) * dh], preferred_element_type=F32)
        o_ref[:, h * dh:(h + 1) * dh] = (o / denom).astype(o_ref.dtype)


def xattn(q, kv, batch):
    t, d = q.shape
    nq = SEQ // X_QB
    return pl.pallas_call(
        _xattn_kernel,
        out_shape=jax.ShapeDtypeStruct((t, d), BF16),
        grid=(batch, nq),
        in_specs=[
            pl.BlockSpec((X_QB, d), lambda b, i: (b * nq + i, 0)),
            pl.BlockSpec((MEM_LEN, d), lambda b, i: (b, 0)),
            pl.BlockSpec((MEM_LEN, d), lambda b, i: (b, 1)),
        ],
        out_specs=pl.BlockSpec((X_QB, d), lambda b, i: (b * nq + i, 0)),
        scratch_shapes=[pltpu.VMEM((X_HEADS, X_QB, MEM_LEN), F32)],
        compiler_params=_params(("parallel", "parallel")),
        name="xattn",
    )(q, kv, kv)


def _pack_rows(v, out_ref):
    rows, width = v.shape
    half = width // 2
    lo = lax.bitcast_convert_type(v[:, :half].astype(BF16).astype(F32), jnp.uint32)
    hi = lax.bitcast_convert_type(v[:, half:].astype(BF16).astype(F32), jnp.uint32)
    u = (lo >> 16) | hi
    for c in range(half // LANES):
        out_ref[pl.ds(c, rows, stride=PACK_SUB), :] = u[:, c * LANES:(c + 1) * LANES]


def _unpack_rows(ref, rows):
    lo, hi = [], []
    for c in range(PACK_SUB):
        u = ref[pl.ds(c, rows, stride=PACK_SUB), :]
        lo.append(lax.bitcast_convert_type(u << 16, F32))
        hi.append(lax.bitcast_convert_type(u & jnp.uint32(0xFFFF0000), F32))
    return lo, hi


def _router_kernel(x_ref, g_ref, wr_ref, br_ref, route_ref, cnt_ref, hpk_ref, carry_ref, tri_ref):
    tm = x_ref.shape[0]

    @pl.when(pl.program_id(0) == 0)
    def _():
        carry_ref[...] = jnp.zeros(carry_ref.shape, F32)
        r = lax.broadcasted_iota(jnp.int32, (tm, tm), 0)
        c = lax.broadcasted_iota(jnp.int32, (tm, tm), 1)
        tri_ref[...] = jnp.where(c < r, 1.0, 0.0).astype(BF16)

    x = x_ref[...]
    ms = jnp.mean(x * x, axis=-1, keepdims=True)
    hn = x * lax.rsqrt(ms + EPS) * g_ref[...]
    _pack_rows(hn, hpk_ref)
    hb = hn.astype(BF16)
    logits = jnp.dot(hb, wr_ref[...].astype(BF16), preferred_element_type=F32) + br_ref[...]

    lane = lax.broadcasted_iota(jnp.int32, logits.shape, 1)
    lane_f = lane.astype(F32)
    big = jnp.float32(1e9)

    def top1(mask):
        v = jnp.where(mask, logits, -jnp.inf)
        vmax = jnp.max(v, axis=-1, keepdims=True)
        idx = jnp.min(jnp.where(mask & (v == vmax), lane_f, big), axis=-1, keepdims=True)
        return vmax, idx

    gmask = (lane >= N_EXPERTS) & (lane < N_EXPERTS + N_GROUPS)
    gmax, gidx = top1(gmask)
    gsum = jnp.sum(jnp.where(gmask, jnp.exp(logits - gmax), 0.0), axis=-1, keepdims=True)
    g_gate = 1.0 / gsum
    g_sel = gidx.astype(jnp.int32) - N_EXPERTS

    emask = (lane < N_EXPERTS) & ((lane // EXPERTS_PER_GROUP) == g_sel)
    v1, i1 = top1(emask)
    v2, i2 = top1(emask & (lane_f != i1))
    e2 = jnp.exp(v2 - v1)
    w1 = g_gate / (1.0 + e2)
    w2 = g_gate * e2 / (1.0 + e2)

    oh1 = jnp.where(lane_f == i1, 1.0, 0.0)
    oh2 = jnp.where(lane_f == i2, 1.0, 0.0)
    oh = oh1 + oh2
    before = jnp.dot(tri_ref[...], oh.astype(BF16), preferred_element_type=F32) + carry_ref[...]
    r1 = jnp.sum(oh1 * before, axis=-1, keepdims=True)
    r2 = jnp.sum(oh2 * before, axis=-1, keepdims=True)
    carry_ref[...] = carry_ref[...] + jnp.sum(oh, axis=0, keepdims=True)
    cnt_ref[...] = carry_ref[...]

    route = jnp.zeros(logits.shape, F32)
    for col, val in enumerate((i1, i2, w1, w2, r1, r2)):
        route = jnp.where(lane == col, val, route)
    route_ref[...] = route


def router(x, g3, wr, br, l, tm=512):
    t, d = x.shape
    return pl.pallas_call(
        _router_kernel,
        out_shape=(jax.ShapeDtypeStruct((t, LANES), F32),
                   jax.ShapeDtypeStruct((1, LANES), F32),
                   jax.ShapeDtypeStruct((t * PACK_SUB, LANES), jnp.uint32)),
        grid=(t // tm,),
        in_specs=[pl.BlockSpec((tm, d), lambda i: (i, 0)),
                  pl.BlockSpec((None, 1, d), lambda i: (l, 0, 0)),
                  pl.BlockSpec((None, d, LANES), lambda i: (l, 0, 0)),
                  pl.BlockSpec((None, 1, LANES), lambda i: (l, 0, 0))],
        out_specs=(pl.BlockSpec((tm, LANES), lambda i: (i, 0)),
                   pl.BlockSpec((1, LANES), lambda i: (0, 0)),
                   pl.BlockSpec((tm * PACK_SUB, LANES), lambda i: (i, 0))),
        scratch_shapes=[pltpu.VMEM((1, LANES), F32), pltpu.VMEM((tm, tm), BF16)],
        compiler_params=_params(("arbitrary",)),
        name="router",
    )(x, g3, wr, br)


def _max_tiles(t):
    return (2 * t) // E_TM + N_EXPERTS


def dispatch_plan(route, counts_row):
    t = route.shape[0]
    max_tiles = _max_tiles(t)
    ids = route[:, 0:2].astype(jnp.int32)
    ranks = route[:, 4:6].astype(jnp.int32)
    counts = counts_row[0, :N_EXPERTS].astype(jnp.int32)
    padded = ((counts + E_TM - 1) // E_TM) * E_TM
    ends = jnp.cumsum(padded)
    offs = ends - padded
    onehot = ids[:, :, None] == jnp.arange(N_EXPERTS, dtype=jnp.int32)
    dest = (jnp.sum(jnp.where(onehot, offs, 0), axis=-1) + ranks).reshape(2 * t)
    n_tiles = ends[-1] // E_TM
    idx = jnp.arange(max_tiles, dtype=jnp.int32)
    tile_i = jnp.minimum(idx, n_tiles - 1)
    tile_expert = jnp.sum(ends[None, :] <= tile_i[:, None] * E_TM, axis=1).astype(jnp.int32)
    used_end = (offs + counts)[tile_expert]
    tile_rows = jnp.clip(used_end - tile_i * E_TM, 0, E_TM).astype(jnp.int32)
    tile_fill = jnp.where(idx >= n_tiles, FILL_TAIL,
                          jnp.where(tile_rows < E_TM, FILL_PADDED, 0)).astype(jnp.int32)
    first = (idx < n_tiles) & ((idx == 0) | (tile_expert != jnp.roll(tile_expert, 1)))
    slot = (jnp.cumsum(first) - 1) % 2
    first_pos = jnp.where(first, idx, max_tiles)
    after = jnp.flip(lax.cummin(jnp.flip(first_pos)))
    nxt_pos = jnp.concatenate([after[1:], jnp.full((1,), max_tiles, jnp.int32)])
    nxt = jnp.where(nxt_pos < max_tiles,
                    tile_expert[jnp.minimum(nxt_pos, max_tiles - 1)], -1)
    tile_plan = jnp.stack([first.astype(jnp.int32), slot.astype(jnp.int32),
                           nxt.astype(jnp.int32)])
    return dest, tile_expert, tile_fill, n_tiles.reshape(1).astype(jnp.int32), tile_plan


def _dispatch_kernel(dest_ref, fill_ref, h_ref, xs_hbm, stage0, stage1, sem, zbuf, zsem):
    i = pl.program_id(0)
    n = pl.num_programs(0)
    stages = (stage0, stage1)

    def fill_copy(j, kind):
        blk = E_TM * PACK_SUB
        return pltpu.make_async_copy(zbuf, xs_hbm.at[pl.ds(j * blk, blk)], zsem.at[kind - 1])

    def fill_each(kind, fn):
        for j in range(fill_ref.shape[0]):
            @pl.when(fill_ref[j] == kind)
            def _(j=j):
                fn(fill_copy(j, kind))

    @pl.when(i == 0)
    def _():
        zbuf[...] = jnp.zeros(zbuf.shape, zbuf.dtype)
        fill_each(FILL_PADDED, lambda c: c.start())
        fill_each(FILL_TAIL, lambda c: c.start())
        fill_each(FILL_PADDED, lambda c: c.wait())

    def drain(s):
        for _ in range(2):
            pltpu.make_async_copy(stages[s], xs_hbm.at[pl.ds(0, D_TM * PACK_SUB)],
                                  sem.at[s]).wait()

    def step(s):
        @pl.when(i >= 2)
        def _():
            drain(s)

        stages[s][...] = h_ref[...]
        base = 2 * i * D_TM
        for r in range(D_TM):
            for k in range(2):
                row = dest_ref[base + 2 * r + k]
                dst = xs_hbm.at[pl.ds(pl.multiple_of(row * PACK_SUB, PACK_SUB), PACK_SUB)]
                pltpu.make_async_copy(stages[s].at[pl.ds(r * PACK_SUB, PACK_SUB)], dst,
                                      sem.at[s]).start(priority=k)

        @pl.when(i == n - 1)
        def _():
            drain(s)

            @pl.when(n >= 2)
            def _():
                drain(1 - s)

    for s in range(2):
        @pl.when(i % 2 == s)
        def _(s=s):
            step(s)

    @pl.when(i == n - 1)
    def _():
        fill_each(FILL_TAIL, lambda c: c.wait())


def dispatch(hpk, dest, tile_fill):
    t = dest.shape[0] // 2
    blk = D_TM * PACK_SUB
    grid_spec = pltpu.PrefetchScalarGridSpec(
        num_scalar_prefetch=2,
        grid=(t // D_TM,),
        in_specs=[pl.BlockSpec((blk, LANES), lambda i, dest, fill: (i, 0))],
        out_specs=pl.BlockSpec(memory_space=pl.ANY),
        scratch_shapes=[pltpu.VMEM((blk, LANES), jnp.uint32),
                        pltpu.VMEM((blk, LANES), jnp.uint32),
                        pltpu.SemaphoreType.DMA((2,)),
                        pltpu.VMEM((E_TM * PACK_SUB, LANES), jnp.uint32),
                        pltpu.SemaphoreType.DMA((2,))],
    )
    return pl.pallas_call(
        _dispatch_kernel,
        out_shape=jax.ShapeDtypeStruct((tile_fill.shape[0] * E_TM * PACK_SUB, LANES), jnp.uint32),
        grid_spec=grid_spec,
        compiler_params=_params(("arbitrary",)),
        name="dispatch",
    )(dest, tile_fill, hpk)


def _experts_kernel(te_ref, nt_ref, plan_ref, xs_ref, w1_hbm, w3_hbm, w2_hbm, ys_ref,
                    w1f, w3f, w2f, wsem, w1b, w3b, w2b, *, layer):
    i = pl.program_id(0)
    nt = nt_ref[0]

    def weight_copies(e, slot):
        return [pltpu.make_async_copy(w_hbm.at[layer, e], wf.at[slot], wsem.at[slot])
                for w_hbm, wf in ((w1_hbm, w1f), (w3_hbm, w3f), (w2_hbm, w2f))]

    @pl.when(i == 0)
    def _():
        for c in weight_copies(te_ref[0], 0):
            c.start()

    @pl.when(i < nt)
    def _():
        @pl.when(plan_ref[0, i] == 1)
        def _():
            slot = plan_ref[1, i]
            for c in weight_copies(te_ref[i], slot):
                c.wait()
            w1b[...] = w1f[slot].astype(BF16)
            w3b[...] = w3f[slot].astype(BF16)
            w2b[...] = w2f[slot].astype(BF16)

            @pl.when(plan_ref[2, i] >= 0)
            def _():
                for c in weight_copies(plan_ref[2, i], 1 - slot):
                    c.start()

        lo, hi = _unpack_rows(xs_ref, E_TM)
        hb = jnp.concatenate([v.astype(BF16) for v in lo + hi], axis=1)
        a1 = jnp.dot(hb, w1b[...], preferred_element_type=F32)
        a3 = jnp.dot(hb, w3b[...], preferred_element_type=F32)
        hid = (a1 / (1.0 + jnp.exp(-a1))) * a3
        _pack_rows(jnp.dot(hid.astype(BF16), w2b[...], preferred_element_type=F32), ys_ref)

    @pl.when(i >= nt)
    def _():
        ys_ref[...] = jnp.zeros(ys_ref.shape, ys_ref.dtype)


def experts(xs, w1, w3, w2, l, tile_expert, n_tiles, tile_plan):
    d = w1.shape[-2]
    max_tiles = tile_expert.shape[0]
    blk = E_TM * PACK_SUB
    any_spec = pl.BlockSpec(memory_space=pl.ANY)
    grid_spec = pltpu.PrefetchScalarGridSpec(
        num_scalar_prefetch=3,
        grid=(max_tiles,),
        in_specs=[pl.BlockSpec((blk, LANES),
                               lambda i, te, nt, plan: (jnp.minimum(i, nt[0] - 1), 0)),
                  any_spec, any_spec, any_spec],
        out_specs=pl.BlockSpec((blk, LANES), lambda i, te, nt, plan: (i, 0)),
        scratch_shapes=[pltpu.VMEM((2, d, D_EXPERT), F32),
                        pltpu.VMEM((2, d, D_EXPERT), F32),
                        pltpu.VMEM((2, D_EXPERT, d), F32),
                        pltpu.SemaphoreType.DMA((2,)),
                        pltpu.VMEM((d, D_EXPERT), BF16),
                        pltpu.VMEM((d, D_EXPERT), BF16),
                        pltpu.VMEM((D_EXPERT, d), BF16)],
    )
    return pl.pallas_call(
        functools.partial(_experts_kernel, layer=l),
        out_shape=jax.ShapeDtypeStruct((max_tiles * blk, LANES), jnp.uint32),
        grid_spec=grid_spec,
        compiler_params=_params(("arbitrary",)),
        name="experts",
    )(tile_expert, n_tiles, tile_plan, xs, w1, w3, w2)


def _combine_kernel(dest_ref, ys_hbm, x_ref, route_ref, o_ref, ybuf0, ybuf1, sem):
    i = pl.program_id(0)
    n = pl.num_programs(0)
    bufs = (ybuf0, ybuf1)

    def row_copy(row, s, k, r):
        src = ys_hbm.at[pl.ds(pl.multiple_of(row * PACK_SUB, PACK_SUB), PACK_SUB)]
        return pltpu.make_async_copy(src, bufs[s].at[k, pl.ds(r * PACK_SUB, PACK_SUB)], sem.at[s])

    def tile_wait(s):
        for k in range(2):
            pltpu.make_async_copy(ys_hbm.at[pl.ds(0, C_TM * PACK_SUB)], bufs[s].at[k],
                                  sem.at[s]).wait()

    @pl.when(i == 0)
    def _():
        def body(r, c):
            for k in range(2):
                src = ys_hbm.at[pl.ds(pl.multiple_of(dest_ref[2 * r + k] * PACK_SUB, PACK_SUB),
                                      PACK_SUB)]
                dst = ybuf0.at[k, pl.ds(pl.multiple_of(r * PACK_SUB, PACK_SUB), PACK_SUB)]
                pltpu.make_async_copy(src, dst, sem.at[0]).start()
            return c

        lax.fori_loop(0, C_TM, body, 0, unroll=4)

    def step(s):
        tile_wait(s)
        base = 2 * jnp.minimum(i + 1, n - 1) * C_TM
        for r in range(C_TM):
            for k in range(2):
                row_copy(dest_ref[base + 2 * r + k], 1 - s, k, r).start(priority=k)

        route = route_ref[...]
        lane = lax.broadcasted_iota(jnp.int32, route.shape, 1)
        w1 = jnp.sum(jnp.where(lane == 2, route, 0.0), axis=-1, keepdims=True)
        w2 = jnp.sum(jnp.where(lane == 3, route, 0.0), axis=-1, keepdims=True)
        lo1, hi1 = _unpack_rows(bufs[s].at[0], C_TM)
        lo2, hi2 = _unpack_rows(bufs[s].at[1], C_TM)
        half = x_ref.shape[1] // 2
        for c in range(PACK_SUB):
            for off, y1, y2 in ((0, lo1, lo2), (half, hi1, hi2)):
                cols = slice(off + c * LANES, off + (c + 1) * LANES)
                o_ref[:, cols] = x_ref[:, cols] + w1 * y1[c] + w2 * y2[c]

        @pl.when(i == n - 1)
        def _():
            tile_wait(1 - s)

    for s in range(2):
        @pl.when(i % 2 == s)
        def _(s=s):
            step(s)


def combine(ys, x, route, dest):
    t, d = x.shape
    blk = C_TM * PACK_SUB
    grid_spec = pltpu.PrefetchScalarGridSpec(
        num_scalar_prefetch=1,
        grid=(t // C_TM,),
        in_specs=[pl.BlockSpec(memory_space=pl.ANY),
                  pl.BlockSpec((C_TM, d), lambda i, dest: (i, 0)),
                  pl.BlockSpec((C_TM, LANES), lambda i, dest: (i, 0))],
        out_specs=pl.BlockSpec((C_TM, d), lambda i, dest: (i, 0)),
        scratch_shapes=[pltpu.VMEM((2, blk, LANES), jnp.uint32),
                        pltpu.VMEM((2, blk, LANES), jnp.uint32),
                        pltpu.SemaphoreType.DMA((2,))],
    )
    return pl.pallas_call(
        _combine_kernel,
        out_shape=jax.ShapeDtypeStruct((t, d), F32),
        grid_spec=grid_spec,
        compiler_params=_params(("arbitrary",)),
        name="combine",
    )(dest, ys, x, route)


def kernel(x, mem, norm_mix, w_in, rel_bias_a, t5_table, diff_lambda, diff_subln, w_out,
           norm_cross, norm_mem, w_xq, w_xkv, w_xo, norm_ffn, w_group, b_group,
           w_expert, b_expert, w1, w3, w2, norm_final):
    batch, seq, d = x.shape
    t = batch * seq
    xf = x.reshape(t, d)
    memf = mem.reshape(batch * MEM_LEN, d)

    g_mix = norm_mix.reshape(DEPTH, 1, d)
    g_cross = norm_cross.reshape(DEPTH, 1, d)
    g_mem = norm_mem.reshape(DEPTH, 1, d)
    g_ffn = norm_ffn.reshape(DEPTH, 1, d)
    g_final = norm_final.reshape(1, 1, d)
    subln3 = diff_subln.reshape(DEPTH, 1, HEAD_DIM)

    in_scale = jnp.concatenate([
        jnp.full((A_WIDTH,), HEAD_DIM ** -0.5 * LOG2E, F32), jnp.ones((2 * A_WIDTH,), F32),
        jnp.full((B_WIDTH,), B_QK_DIM ** -0.5 * LOG2E, F32), jnp.ones((2 * B_WIDTH,), F32)]
    ).reshape(1, IN_WIDTH)
    xq_scale = jnp.full((1, d), X_HEAD_DIM ** -0.5, F32)

    far_b, bias_b = t5_bias_tables(t5_table)
    bias_a = band_bias_tables(rel_bias_a)

    pad = LANES - N_EXPERTS - N_GROUPS
    wr = jnp.concatenate([w_expert, w_group, jnp.zeros((DEPTH, d, pad), F32)], axis=-1)
    br = jnp.concatenate([b_expert, b_group, jnp.zeros((DEPTH, pad), F32)],
                         axis=-1).reshape(DEPTH, 1, LANES)

    for l in range(DEPTH):
        lam_init = 0.8 - 0.6 * math.exp(-0.3 * l)
        proj = matmul(xf, w_in, l, BF16, norm_gain=g_mix, colscale=in_scale, tm=1024)
        oa = attn_a(proj, bias_a, l, batch)
        ob = attn_b(proj, far_b, bias_b, diff_lambda, subln3, l, lam_init, batch)
        xf = matmul([oa, ob], w_out, l, F32, resid=xf, tm=1024)
        q = matmul(xf, w_xq, l, BF16, norm_gain=g_cross, colscale=xq_scale, tn=2048)
        kv = matmul(memf, w_xkv, l, BF16, norm_gain=g_mem)
        xo = xattn(q, kv, batch)
        xf = matmul(xo, w_xo, l, F32, resid=xf, tm=1024)
        route, counts, hpk = router(xf, g_ffn, wr, br, l)
        dest, tile_expert, tile_fill, n_tiles, tile_plan = dispatch_plan(route, counts)
        xs = dispatch(hpk, dest, tile_fill)
        ys = experts(xs, w1, w3, w2, l, tile_expert, n_tiles, tile_plan)
        xf = combine(ys, xf, route, dest)
    out = rmsnorm(xf, g_final, 0, F32)
    return out.reshape(batch, seq, d)
```

```python
import functools
import math

import jax
import jax.numpy as jnp
import numpy as np
from jax import lax
from jax.experimental import pallas as pl
from jax.experimental.pallas import tpu as pltpu

D_MODEL = 2048
SEQ = 4096
DEPTH = 4
CHUNK = 64
LEFT_CHUNKS = 8
HEAD_DIM = 128
A_HEADS = 8
B_HEADS = 8
B_QK_DIM = 64
A_WIDTH = 1024
B_WIDTH = 1024
IN_WIDTH = 6144
REL_CLIP = 128
T5_BUCKETS = 32
T5_MAX_DIST = 512
MEM_LEN = 256
X_HEADS = 4
X_HEAD_DIM = 512
N_GROUPS = 4
EXPERTS_PER_GROUP = 8
N_EXPERTS = 32
D_EXPERT = 256
EPS = 1e-6
NEG_INF = -1e30

LANES = 128
VMEM_LIMIT = 56 * 1024 * 1024

BF16 = jnp.bfloat16
F32 = jnp.float32

A_QB = 256
A_WIN = A_QB + LEFT_CHUNKS * CHUNK
B_QB = 256
B_NEAR = 3
B_HP = 4
B_FAR_NK = 4
A_HP = 4
LOG2E = math.log2(math.e)
X_QB = 512

E_TM = 256
C_TM = 512
D_TM = 512
FILL_PADDED, FILL_TAIL = 1, 2
PACK_SUB = 8


def _params(sem):
    return pltpu.CompilerParams(dimension_semantics=sem, vmem_limit_bytes=VMEM_LIMIT)


def _rmsnorm_kernel(x_ref, g_ref, o_ref):
    x = x_ref[...]
    ms = jnp.mean(x * x, axis=-1, keepdims=True)
    o_ref[...] = (x * lax.rsqrt(ms + EPS) * g_ref[...]).astype(o_ref.dtype)


def rmsnorm(x, g3, l, out_dtype, tm=512):
    m, d = x.shape
    tm = min(tm, m)
    return pl.pallas_call(
        _rmsnorm_kernel,
        out_shape=jax.ShapeDtypeStruct((m, d), out_dtype),
        grid=(m // tm,),
        in_specs=[pl.BlockSpec((tm, d), lambda i: (i, 0)),
                  pl.BlockSpec((None, 1, d), lambda i: (l, 0, 0))],
        out_specs=pl.BlockSpec((tm, d), lambda i: (i, 0)),
        compiler_params=_params(("parallel",)),
        name="rmsnorm",
    )(x, g3)


def _matmul_kernel(*refs, n_parts, has_norm, has_scale, has_resid):
    a_refs, w_ref = refs[:n_parts], refs[n_parts]
    k = n_parts + 1
    g_ref = s_ref = r_ref = None
    if has_norm:
        g_ref = refs[k]; k += 1
    if has_scale:
        s_ref = refs[k]; k += 1
    if has_resid:
        r_ref = refs[k]; k += 1
    o_ref, wbf_ref = refs[k], refs[k + 1]

    @pl.when(pl.program_id(1) == 0)
    def _():
        wbf_ref[...] = w_ref[...].astype(BF16)

    acc = None
    k0 = 0
    for a_ref in a_refs:
        kp = a_ref.shape[1]
        a = a_ref[...]
        if has_norm:
            ms = jnp.mean(a * a, axis=-1, keepdims=True)
            a = (a * lax.rsqrt(ms + EPS) * g_ref[...]).astype(BF16)
        part = jnp.dot(a, wbf_ref[k0:k0 + kp, :], preferred_element_type=F32)
        acc = part if acc is None else acc + part
        k0 += kp
    if has_scale:
        acc = acc * s_ref[...]
    if has_resid:
        acc = acc + r_ref[...]
    o_ref[...] = acc.astype(o_ref.dtype)


def matmul(a_parts, w, l, out_dtype, norm_gain=None, colscale=None, resid=None,
           tm=512, tn=1024):
    if not isinstance(a_parts, (list, tuple)):
        a_parts = [a_parts]
    assert norm_gain is None or len(a_parts) == 1
    m = a_parts[0].shape[0]
    k = sum(a.shape[1] for a in a_parts)
    n = w.shape[-1]
    assert w.shape[-2] == k
    tm = min(tm, m)
    tn = min(tn, n)
    in_specs = [pl.BlockSpec((tm, a.shape[1]), lambda j, i: (i, 0)) for a in a_parts]
    in_specs.append(pl.BlockSpec((None, k, tn), lambda j, i: (l, 0, j)))
    args = list(a_parts) + [w]
    if norm_gain is not None:
        in_specs.append(pl.BlockSpec((None, 1, k), lambda j, i: (l, 0, 0)))
        args.append(norm_gain)
    if colscale is not None:
        in_specs.append(pl.BlockSpec((1, tn), lambda j, i: (0, j)))
        args.append(colscale)
    if resid is not None:
        in_specs.append(pl.BlockSpec((tm, tn), lambda j, i: (i, j)))
        args.append(resid)
    return pl.pallas_call(
        functools.partial(_matmul_kernel, n_parts=len(a_parts), has_norm=norm_gain is not None,
                          has_scale=colscale is not None, has_resid=resid is not None),
        out_shape=jax.ShapeDtypeStruct((m, n), out_dtype),
        grid=(n // tn, m // tm),
        in_specs=in_specs,
        out_specs=pl.BlockSpec((tm, tn), lambda j, i: (i, j)),
        scratch_shapes=[pltpu.VMEM((k, tn), BF16)],
        compiler_params=_params(("parallel", "arbitrary")),
        name="matmul",
    )(*args)


def _toeplitz(v, rows, cols):
    hh, n = v.shape
    assert n == rows + cols - 1
    vp = jnp.pad(v, ((0, 0), (0, 1)))
    skew = jnp.tile(vp, (1, rows))[:, :rows * n].reshape(hh, rows, n)
    return skew[:, :, rows - 1:]


def _attn_a_kernel(q_ref, k_ref, v_ref, b0_ref, b1_ref, b2_ref, o_ref, vt_ref, st_ref):
    i = pl.program_id(2)
    qb = A_QB
    n_kt = A_WIN // qb
    bias_refs = (b0_ref, b1_ref, b2_ref)

    @pl.when(i == 0)
    def _():
        for hh in range(A_HP):
            for c in range(SEQ // qb):
                blk = v_ref[c * qb:(c + 1) * qb, hh * HEAD_DIM:(hh + 1) * HEAD_DIM]
                vt_ref[hh, c] = blk.astype(F32).T.astype(BF16)

    t0 = jnp.maximum(i - (n_kt - 1), 0)
    start = pl.multiple_of(t0 * qb, qb)
    qts = [q_ref[:, hh * HEAD_DIM:(hh + 1) * HEAD_DIM].astype(F32).T.astype(BF16)
           for hh in range(A_HP)]
    for hh in range(A_HP):
        st_ref[hh] = jnp.dot(k_ref[pl.ds(start, A_WIN), hh * HEAD_DIM:(hh + 1) * HEAD_DIM],
                             qts[hh], preferred_element_type=F32)
    for hh in range(A_HP):
        st = st_ref[hh] + jnp.concatenate([r[hh] for r in bias_refs], axis=0)
        m = jnp.max(st, axis=0, keepdims=True)
        p = jnp.exp2(st - m)
        denom = jnp.sum(p, axis=0, keepdims=True)
        pb = p.astype(BF16)
        ot = None
        for u in range(n_kt):
            part = jnp.dot(vt_ref[hh, t0 + u], pb[u * qb:(u + 1) * qb],
                           preferred_element_type=F32)
            ot = part if ot is None else ot + part
        o_ref[:, hh * HEAD_DIM:(hh + 1) * HEAD_DIM] = (ot / denom).T.astype(o_ref.dtype)


def attn_a(proj, bias_a, l, batch):
    t = proj.shape[0]
    nq = SEQ // A_QB
    n_kt = A_WIN // A_QB
    wa = A_HP * HEAD_DIM
    ncol = A_WIDTH // wa

    def bias_spec(u):
        return pl.BlockSpec(
            (None, A_HP, None, A_QB, A_QB),
            lambda b, h, i: (l, h, (n_kt - 1) - jnp.minimum(i, n_kt - 1) + u, 0, 0))

    return pl.pallas_call(
        _attn_a_kernel,
        out_shape=jax.ShapeDtypeStruct((t, A_WIDTH), BF16),
        grid=(batch, A_HEADS // A_HP, nq),
        in_specs=[
            pl.BlockSpec((A_QB, wa), lambda b, h, i: (b * nq + i, h)),
            pl.BlockSpec((SEQ, wa), lambda b, h, i: (b, ncol + h)),
            pl.BlockSpec((SEQ, wa), lambda b, h, i: (b, 2 * ncol + h)),
            bias_spec(0), bias_spec(1), bias_spec(2),
        ],
        out_specs=pl.BlockSpec((A_QB, wa), lambda b, h, i: (b * nq + i, h)),
        scratch_shapes=[pltpu.VMEM((A_HP, SEQ // A_QB, HEAD_DIM, A_QB), BF16),
                        pltpu.VMEM((A_HP, A_WIN, A_QB), F32)],
        compiler_params=_params(("arbitrary", "arbitrary", "arbitrary")),
        name="attn_a",
    )(proj, proj, proj, bias_a, bias_a, bias_a)


def band_bias_tables(rel_tables):
    depth, _, heads = rel_tables.shape
    lead = LEFT_CHUNKS * CHUNK
    n_tiles = (A_WIN + lead) // A_QB
    kk = np.arange(A_QB)[:, None]
    r = np.arange(A_QB)[None, :]
    n = np.arange(2 * A_QB - 1)
    tables = rel_tables.astype(F32)
    tiles = []
    for t in range(n_tiles):
        idx = np.clip(t * A_QB + (A_QB - 1 - n) - lead, -REL_CLIP, REL_CLIP) + REL_CLIP
        v = tables[:, idx, :].transpose(0, 2, 1).reshape(depth * heads, -1)
        dchunk = (t * A_QB + kk - lead) // CHUNK - r // CHUNK
        valid = (dchunk >= -LEFT_CHUNKS) & (dchunk <= 0)
        tiles.append(jnp.where(valid[None], _toeplitz(v, A_QB, A_QB) * LOG2E, NEG_INF))
    return jnp.stack(tiles, axis=1).reshape(depth, heads, n_tiles, A_QB, A_QB)


def _attn_b_kernel(far_ref, q_ref, k_ref, v_ref, bias_ref, lam_ref, g_ref, o_ref,
                   qqt_ref, vt_ref, m_ref, l_ref, acc_ref, st_ref, *, lam_init):
    hp = pl.program_id(1)
    i = pl.program_id(2)
    qb = B_QB

    @pl.when(i == 0)
    def _():
        for hh in range(B_HP):
            for c in range(SEQ // qb):
                blk = v_ref[c * qb:(c + 1) * qb, hh * HEAD_DIM:(hh + 1) * HEAD_DIM]
                vt_ref[hh, c] = blk.astype(F32).T.astype(BF16)

    for hh in range(B_HP):
        qt = q_ref[:, hh * HEAD_DIM:(hh + 1) * HEAD_DIM].astype(F32).T
        row = lax.broadcasted_iota(jnp.int32, qt.shape, 0)
        qqt_ref[hh, :, 0:qb] = jnp.where(row < B_QK_DIM, qt, 0.0).astype(BF16)
        qqt_ref[hh, :, qb:2 * qb] = jnp.where(row >= B_QK_DIM, qt, 0.0).astype(BF16)
        m_ref[hh] = jnp.full((1, 2 * qb), NEG_INF, F32)
        l_ref[hh] = jnp.zeros((1, 2 * qb), F32)
        acc_ref[hh] = jnp.zeros((HEAD_DIM, 2 * qb), F32)

    def tile(j, d, nk=1):
        off = pl.multiple_of(j * qb, qb)
        for hh in range(B_HP):
            st_ref[hh, 0:nk * qb] = jnp.dot(
                k_ref[pl.ds(off, nk * qb), hh * HEAD_DIM:(hh + 1) * HEAD_DIM], qqt_ref[hh],
                preferred_element_type=F32)
        for hh in range(B_HP):
            st = st_ref[hh, 0:nk * qb]
            if d is None:
                shift = far_ref[hp * B_HP + hh]
            else:
                b = jnp.concatenate([bias_ref[hh, du] for du in d], axis=0)
                st = jnp.concatenate([st[:, 0:qb] + b, st[:, qb:2 * qb] + b], axis=1)
                shift = 0.0
            m_old = m_ref[hh]
            m_new = jnp.maximum(m_old, jnp.max(st, axis=0, keepdims=True) + shift)
            alpha = jnp.exp2(m_old - m_new)
            p = jnp.exp2(st - (m_new - shift))
            l_ref[hh] = alpha * l_ref[hh] + jnp.sum(p, axis=0, keepdims=True)
            pb = p.astype(BF16)
            pv = None
            for u in range(nk):
                part = jnp.dot(vt_ref[hh, j + u], pb[u * qb:(u + 1) * qb],
                               preferred_element_type=F32)
                pv = part if pv is None else pv + part
            acc_ref[hh] = alpha * acc_ref[hh] + pv
            m_ref[hh] = m_new

    n_far = jnp.maximum(i - (B_NEAR - 1), 0)

    def far_body(jj, c):
        tile(B_FAR_NK * jj, None, nk=B_FAR_NK)
        return c

    n_quads = n_far // B_FAR_NK
    lax.fori_loop(0, n_quads, far_body, 0)
    rem = n_far - B_FAR_NK * n_quads
    assert B_FAR_NK == 4

    @pl.when(rem >= 2)
    def _():
        tile(B_FAR_NK * n_quads, None, nk=2)

    @pl.when(rem % 2 == 1)
    def _():
        tile(n_far - 1, None)

    assert B_NEAR == 3
    @pl.when(i >= 2)
    def _():
        tile(i - 2, (2, 1, 0), nk=3)

    @pl.when(i == 1)
    def _():
        tile(0, (1, 0), nk=2)

    @pl.when(i == 0)
    def _():
        tile(0, (0,))

    lv = lam_ref[...]
    lam = (jnp.exp(jnp.sum(lv[0:1] * lv[1:2], axis=-1, keepdims=True))
           - jnp.exp(jnp.sum(lv[2:3] * lv[3:4], axis=-1, keepdims=True)) + lam_init)
    for hh in range(B_HP):
        o = acc_ref[hh] / l_ref[hh]
        od = o[:, 0:qb] - lam * o[:, qb:2 * qb]
        ms = jnp.mean(od * od, axis=0, keepdims=True)
        y = (od * lax.rsqrt(ms + EPS)).T * g_ref[...]
        o_ref[:, hh * HEAD_DIM:(hh + 1) * HEAD_DIM] = (y * (1.0 - lam_init)).astype(o_ref.dtype)


def attn_b(proj, far_bias, bias_b, lam_vecs, subln3, l, lam_init, batch):
    t = proj.shape[0]
    nq = SEQ // B_QB
    wb = B_HP * HEAD_DIM
    qcol = 3 * A_WIDTH // wb
    ncol = B_WIDTH // wb
    return pl.pallas_call(
        functools.partial(_attn_b_kernel, lam_init=lam_init),
        out_shape=jax.ShapeDtypeStruct((t, B_WIDTH), BF16),
        grid=(batch, B_HEADS // B_HP, nq),
        in_specs=[
            pl.BlockSpec(memory_space=pltpu.SMEM),
            pl.BlockSpec((B_QB, wb), lambda b, h, i: (b * nq + i, qcol + h)),
            pl.BlockSpec((SEQ, wb), lambda b, h, i: (b, qcol + ncol + h)),
            pl.BlockSpec((SEQ, wb), lambda b, h, i: (b, qcol + 2 * ncol + h)),
            pl.BlockSpec((B_HP, B_NEAR, B_QB, B_QB), lambda b, h, i: (h, 0, 0, 0)),
            pl.BlockSpec((None, 4, B_QK_DIM), lambda b, h, i: (l, 0, 0)),
            pl.BlockSpec((None, 1, HEAD_DIM), lambda b, h, i: (l, 0, 0)),
        ],
        out_specs=pl.BlockSpec((B_QB, wb), lambda b, h, i: (b * nq + i, h)),
        scratch_shapes=[pltpu.VMEM((B_HP, HEAD_DIM, 2 * B_QB), BF16),
                        pltpu.VMEM((B_HP, SEQ // B_QB, HEAD_DIM, B_QB), BF16),
                        pltpu.VMEM((B_HP, 1, 2 * B_QB), F32),
                        pltpu.VMEM((B_HP, 1, 2 * B_QB), F32),
                        pltpu.VMEM((B_HP, HEAD_DIM, 2 * B_QB), F32),
                        pltpu.VMEM((B_HP, B_FAR_NK * B_QB, 2 * B_QB), F32)],
        compiler_params=_params(("arbitrary", "arbitrary", "arbitrary")),
        name="attn_b",
    )(far_bias, proj, proj, proj, bias_b, lam_vecs, subln3)


def _t5_bucket(rel):
    half = T5_BUCKETS // 2
    max_exact = half // 2
    ret = jnp.where(rel > 0, half, 0)
    n = jnp.abs(rel)
    n_f = jnp.maximum(n, max_exact).astype(jnp.float32)
    large = max_exact + (jnp.log(n_f / max_exact) / math.log(T5_MAX_DIST / max_exact)
                         * (half - max_exact)).astype(jnp.int32)
    large = jnp.minimum(large, half - 1)
    return ret + jnp.where(n < max_exact, n, large)


def t5_bias_tables(t5_table):
    assert B_NEAR * B_QB - (B_QB - 1) >= T5_MAX_DIST
    lo = -(B_NEAR * B_QB)
    bias1d = t5_table.astype(F32)[_t5_bucket(jnp.arange(lo, B_QB))].T
    far = t5_table.astype(F32)[_t5_bucket(jnp.array([-(SEQ - 1)]))][0]
    u = np.arange(2 * B_QB - 1)
    kk = np.arange(B_QB)[:, None]
    qq = np.arange(B_QB)[None, :]
    tiles = []
    for d in range(B_NEAR):
        t = _toeplitz(bias1d[:, (B_QB - 1 - u - d * B_QB) - lo], B_QB, B_QB)
        if d == 0:
            t = jnp.where(((kk // CHUNK) <= (qq // CHUNK))[None], t * LOG2E, NEG_INF)
        else:
            t = t * LOG2E
        tiles.append(t)
    return far * LOG2E, jnp.stack(tiles, axis=1)


def _xattn_kernel(q_ref, k_ref, v_ref, o_ref, s_ref):
    dh = X_HEAD_DIM
    for h in range(X_HEADS):
        s_ref[h] = lax.dot_general(q_ref[:, h * dh:(h + 1) * dh], k_ref[:, h * dh:(h + 1) * dh],
                                   (((1,), (1,)), ((), ())), preferred_element_type=F32)
    for h in range(X_HEADS):
        s = s_ref[h]
        m = jnp.max(s, axis=-1, keepdims=True)
        p = jnp.exp(s - m)
        denom = jnp.sum(p, axis=-1, keepdims=True)
        o = jnp.dot(p.astype(BF16), v_ref[:, h * dh:(h + 1) * dh], preferred_element_type=F32)
        o_ref[:, h * dh:(h + 1) * dh] = (o / denom).astype(o_ref.dtype)


def xattn(q, kv, batch):
    t, d = q.shape
    nq = SEQ // X_QB
    return pl.pallas_call(
        _xattn_kernel,
        out_shape=jax.ShapeDtypeStruct((t, d), BF16),
        grid=(batch, nq),
        in_specs=[
            pl.BlockSpec((X_QB, d), lambda b, i: (b * nq + i, 0)),
            pl.BlockSpec((MEM_LEN, d), lambda b, i: (b, 0)),
            pl.BlockSpec((MEM_LEN, d), lambda b, i: (b, 1)),
        ],
        out_specs=pl.BlockSpec((X_QB, d), lambda b, i: (b * nq + i, 0)),
        scratch_shapes=[pltpu.VMEM((X_HEADS, X_QB, MEM_LEN), F32)],
        compiler_params=_params(("parallel", "parallel")),
        name="xattn",
    )(q, kv, kv)


def _pack_rows(v, out_ref):
    rows, width = v.shape
    half = width // 2
    lo = lax.bitcast_convert_type(v[:, :half].astype(BF16).astype(F32), jnp.uint32)
    hi = lax.bitcast_convert_type(v[:, half:].astype(BF16).astype(F32), jnp.uint32)
    u = (lo >> 16) | hi
    for c in range(half // LANES):
        out_ref[pl.ds(c, rows, stride=PACK_SUB), :] = u[:, c * LANES:(c + 1) * LANES]


def _unpack_rows(ref, rows):
    lo, hi = [], []
    for c in range(PACK_SUB):
        u = ref[pl.ds(c, rows, stride=PACK_SUB), :]
        lo.append(lax.bitcast_convert_type(u << 16, F32))
        hi.append(lax.bitcast_convert_type(u & jnp.uint32(0xFFFF0000), F32))
    return lo, hi


def _router_kernel(x_ref, g_ref, wr_ref, br_ref, route_ref, cnt_ref, hpk_ref, carry_ref, tri_ref):
    tm = x_ref.shape[0]

    @pl.when(pl.program_id(0) == 0)
    def _():
        carry_ref[...] = jnp.zeros(carry_ref.shape, F32)
        r = lax.broadcasted_iota(jnp.int32, (tm, tm), 0)
        c = lax.broadcasted_iota(jnp.int32, (tm, tm), 1)
        tri_ref[...] = jnp.where(c < r, 1.0, 0.0).astype(BF16)

    x = x_ref[...]
    ms = jnp.mean(x * x, axis=-1, keepdims=True)
    hn = x * lax.rsqrt(ms + EPS) * g_ref[...]
    _pack_rows(hn, hpk_ref)
    hb = hn.astype(BF16)
    logits = jnp.dot(hb, wr_ref[...].astype(BF16), preferred_element_type=F32) + br_ref[...]

    lane = lax.broadcasted_iota(jnp.int32, logits.shape, 1)
    lane_f = lane.astype(F32)
    big = jnp.float32(1e9)

    def top1(mask):
        v = jnp.where(mask, logits, -jnp.inf)
        vmax = jnp.max(v, axis=-1, keepdims=True)
        idx = jnp.min(jnp.where(mask & (v == vmax), lane_f, big), axis=-1, keepdims=True)
        return vmax, idx

    gmask = (lane >= N_EXPERTS) & (lane < N_EXPERTS + N_GROUPS)
    gmax, gidx = top1(gmask)
    gsum = jnp.sum(jnp.where(gmask, jnp.exp(logits - gmax), 0.0), axis=-1, keepdims=True)
    g_gate = 1.0 / gsum
    g_sel = gidx.astype(jnp.int32) - N_EXPERTS

    emask = (lane < N_EXPERTS) & ((lane // EXPERTS_PER_GROUP) == g_sel)
    v1, i1 = top1(emask)
    v2, i2 = top1(emask & (lane_f != i1))
    e2 = jnp.exp(v2 - v1)
    w1 = g_gate / (1.0 + e2)
    w2 = g_gate * e2 / (1.0 + e2)

    oh1 = jnp.where(lane_f == i1, 1.0, 0.0)
    oh2 = jnp.where(lane_f == i2, 1.0, 0.0)
    oh = oh1 + oh2
    before = jnp.dot(tri_ref[...], oh.astype(BF16), preferred_element_type=F32) + carry_ref[...]
    r1 = jnp.sum(oh1 * before, axis=-1, keepdims=True)
    r2 = jnp.sum(oh2 * before, axis=-1, keepdims=True)
    carry_ref[...] = carry_ref[...] + jnp.sum(oh, axis=0, keepdims=True)
    cnt_ref[...] = carry_ref[...]

    route = jnp.zeros(logits.shape, F32)
    for col, val in enumerate((i1, i2, w1, w2, r1, r2)):
        route = jnp.where(lane == col, val, route)
    route_ref[...] = route


def router(x, g3, wr, br, l, tm=512):
    t, d = x.shape
    return pl.pallas_call(
        _router_kernel,
        out_shape=(jax.ShapeDtypeStruct((t, LANES), F32),
                   jax.ShapeDtypeStruct((1, LANES), F32),
                   jax.ShapeDtypeStruct((t * PACK_SUB, LANES), jnp.uint32)),
        grid=(t // tm,),
        in_specs=[pl.BlockSpec((tm, d), lambda i: (i, 0)),
                  pl.BlockSpec((None, 1, d), lambda i: (l, 0, 0)),
                  pl.BlockSpec((None, d, LANES), lambda i: (l, 0, 0)),
                  pl.BlockSpec((None, 1, LANES), lambda i: (l, 0, 0))],
        out_specs=(pl.BlockSpec((tm, LANES), lambda i: (i, 0)),
                   pl.BlockSpec((1, LANES), lambda i: (0, 0)),
                   pl.BlockSpec((tm * PACK_SUB, LANES), lambda i: (i, 0))),
        scratch_shapes=[pltpu.VMEM((1, LANES), F32), pltpu.VMEM((tm, tm), BF16)],
        compiler_params=_params(("arbitrary",)),
        name="router",
    )(x, g3, wr, br)


def _max_tiles(t):
    return (2 * t) // E_TM + N_EXPERTS


def dispatch_plan(route, counts_row):
    t = route.shape[0]
    max_tiles = _max_tiles(t)
    ids = route[:, 0:2].astype(jnp.int32)
    ranks = route[:, 4:6].astype(jnp.int32)
    counts = counts_row[0, :N_EXPERTS].astype(jnp.int32)
    padded = ((counts + E_TM - 1) // E_TM) * E_TM
    ends = jnp.cumsum(padded)
    offs = ends - padded
    onehot = ids[:, :, None] == jnp.arange(N_EXPERTS, dtype=jnp.int32)
    dest = (jnp.sum(jnp.where(onehot, offs, 0), axis=-1) + ranks).reshape(2 * t)
    n_tiles = ends[-1] // E_TM
    idx = jnp.arange(max_tiles, dtype=jnp.int32)
    tile_i = jnp.minimum(idx, n_tiles - 1)
    tile_expert = jnp.sum(ends[None, :] <= tile_i[:, None] * E_TM, axis=1).astype(jnp.int32)
    used_end = (offs + counts)[tile_expert]
    tile_rows = jnp.clip(used_end - tile_i * E_TM, 0, E_TM).astype(jnp.int32)
    tile_fill = jnp.where(idx >= n_tiles, FILL_TAIL,
                          jnp.where(tile_rows < E_TM, FILL_PADDED, 0)).astype(jnp.int32)
    first = (idx < n_tiles) & ((idx == 0) | (tile_expert != jnp.roll(tile_expert, 1)))
    slot = (jnp.cumsum(first) - 1) % 2
    first_pos = jnp.where(first, idx, max_tiles)
    after = jnp.flip(lax.cummin(jnp.flip(first_pos)))
    nxt_pos = jnp.concatenate([after[1:], jnp.full((1,), max_tiles, jnp.int32)])
    nxt = jnp.where(nxt_pos < max_tiles,
                    tile_expert[jnp.minimum(nxt_pos, max_tiles - 1)], -1)
    tile_plan = jnp.stack([first.astype(jnp.int32), slot.astype(jnp.int32),
                           nxt.astype(jnp.int32)])
    return dest, tile_expert, tile_fill, n_tiles.reshape(1).astype(jnp.int32), tile_plan


def _dispatch_kernel(dest_ref, fill_ref, h_ref, xs_hbm, stage0, stage1, sem, zbuf, zsem):
    i = pl.program_id(0)
    n = pl.num_programs(0)
    stages = (stage0, stage1)

    def fill_copy(j, kind):
        blk = E_TM * PACK_SUB
        return pltpu.make_async_copy(zbuf, xs_hbm.at[pl.ds(j * blk, blk)], zsem.at[kind - 1])

    def fill_each(kind, fn):
        for j in range(fill_ref.shape[0]):
            @pl.when(fill_ref[j] == kind)
            def _(j=j):
                fn(fill_copy(j, kind))

    @pl.when(i == 0)
    def _():
        zbuf[...] = jnp.zeros(zbuf.shape, zbuf.dtype)
        fill_each(FILL_PADDED, lambda c: c.start())
        fill_each(FILL_TAIL, lambda c: c.start())
        fill_each(FILL_PADDED, lambda c: c.wait())

    def drain(s):
        for _ in range(2):
            pltpu.make_async_copy(stages[s], xs_hbm.at[pl.ds(0, D_TM * PACK_SUB)],
                                  sem.at[s]).wait()

    def step(s):
        @pl.when(i >= 2)
        def _():
            drain(s)

        stages[s][...] = h_ref[...]
        base = 2 * i * D_TM
        for r in range(D_TM):
            for k in range(2):
                row = dest_ref[base + 2 * r + k]
                dst = xs_hbm.at[pl.ds(pl.multiple_of(row * PACK_SUB, PACK_SUB), PACK_SUB)]
                pltpu.make_async_copy(stages[s].at[pl.ds(r * PACK_SUB, PACK_SUB)], dst,
                                      sem.at[s]).start(priority=k)

        @pl.when(i == n - 1)
        def _():
            drain(s)

            @pl.when(n >= 2)
            def _():
                drain(1 - s)

    for s in range(2):
        @pl.when(i % 2 == s)
        def _(s=s):
            step(s)

    @pl.when(i == n - 1)
    def _():
        fill_each(FILL_TAIL, lambda c: c.wait())


def dispatch(hpk, dest, tile_fill):
    t = dest.shape[0] // 2
    blk = D_TM * PACK_SUB
    grid_spec = pltpu.PrefetchScalarGridSpec(
        num_scalar_prefetch=2,
        grid=(t // D_TM,),
        in_specs=[pl.BlockSpec((blk, LANES), lambda i, dest, fill: (i, 0))],
        out_specs=pl.BlockSpec(memory_space=pl.ANY),
        scratch_shapes=[pltpu.VMEM((blk, LANES), jnp.uint32),
                        pltpu.VMEM((blk, LANES), jnp.uint32),
                        pltpu.SemaphoreType.DMA((2,)),
                        pltpu.VMEM((E_TM * PACK_SUB, LANES), jnp.uint32),
                        pltpu.SemaphoreType.DMA((2,))],
    )
    return pl.pallas_call(
        _dispatch_kernel,
        out_shape=jax.ShapeDtypeStruct((tile_fill.shape[0] * E_TM * PACK_SUB, LANES), jnp.uint32),
        grid_spec=grid_spec,
        compiler_params=_params(("arbitrary",)),
        name="dispatch",
    )(dest, tile_fill, hpk)


def _experts_kernel(te_ref, nt_ref, plan_ref, xs_ref, w1_hbm, w3_hbm, w2_hbm, ys_ref,
                    w1f, w3f, w2f, wsem, w1b, w3b, w2b, *, layer):
    i = pl.program_id(0)
    nt = nt_ref[0]

    def weight_copies(e, slot):
        return [pltpu.make_async_copy(w_hbm.at[layer, e], wf.at[slot], wsem.at[slot])
                for w_hbm, wf in ((w1_hbm, w1f), (w3_hbm, w3f), (w2_hbm, w2f))]

    @pl.when(i == 0)
    def _():
        for c in weight_copies(te_ref[0], 0):
            c.start()

    @pl.when(i < nt)
    def _():
        @pl.when(plan_ref[0, i] == 1)
        def _():
            slot = plan_ref[1, i]
            for c in weight_copies(te_ref[i], slot):
                c.wait()
            w1b[...] = w1f[slot].astype(BF16)
            w3b[...] = w3f[slot].astype(BF16)
            w2b[...] = w2f[slot].astype(BF16)

            @pl.when(plan_ref[2, i] >= 0)
            def _():
                for c in weight_copies(plan_ref[2, i], 1 - slot):
                    c.start()

        lo, hi = _unpack_rows(xs_ref, E_TM)
        hb = jnp.concatenate([v.astype(BF16) for v in lo + hi], axis=1)
        a1 = jnp.dot(hb, w1b[...], preferred_element_type=F32)
        a3 = jnp.dot(hb, w3b[...], preferred_element_type=F32)
        hid = (a1 / (1.0 + jnp.exp(-a1))) * a3
        _pack_rows(jnp.dot(hid.astype(BF16), w2b[...], preferred_element_type=F32), ys_ref)

    @pl.when(i >= nt)
    def _():
        ys_ref[...] = jnp.zeros(ys_ref.shape, ys_ref.dtype)


def experts(xs, w1, w3, w2, l, tile_expert, n_tiles, tile_plan):
    d = w1.shape[-2]
    max_tiles = tile_expert.shape[0]
    blk = E_TM * PACK_SUB
    any_spec = pl.BlockSpec(memory_space=pl.ANY)
    grid_spec = pltpu.PrefetchScalarGridSpec(
        num_scalar_prefetch=3,
        grid=(max_tiles,),
        in_specs=[pl.BlockSpec((blk, LANES),
                               lambda i, te, nt, plan: (jnp.minimum(i, nt[0] - 1), 0)),
                  any_spec, any_spec, any_spec],
        out_specs=pl.BlockSpec((blk, LANES), lambda i, te, nt, plan: (i, 0)),
        scratch_shapes=[pltpu.VMEM((2, d, D_EXPERT), F32),
                        pltpu.VMEM((2, d, D_EXPERT), F32),
                        pltpu.VMEM((2, D_EXPERT, d), F32),
                        pltpu.SemaphoreType.DMA((2,)),
                        pltpu.VMEM((d, D_EXPERT), BF16),
                        pltpu.VMEM((d, D_EXPERT), BF16),
                        pltpu.VMEM((D_EXPERT, d), BF16)],
    )
    return pl.pallas_call(
        functools.partial(_experts_kernel, layer=l),
        out_shape=jax.ShapeDtypeStruct((max_tiles * blk, LANES), jnp.uint32),
        grid_spec=grid_spec,
        compiler_params=_params(("arbitrary",)),
        name="experts",
    )(tile_expert, n_tiles, tile_plan, xs, w1, w3, w2)


def _combine_kernel(dest_ref, ys_hbm, x_ref, route_ref, o_ref, ybuf0, ybuf1, sem):
    i = pl.program_id(0)
    n = pl.num_programs(0)
    bufs = (ybuf0, ybuf1)

    def row_copy(row, s, k, r):
        src = ys_hbm.at[pl.ds(pl.multiple_of(row * PACK_SUB, PACK_SUB), PACK_SUB)]
        return pltpu.make_async_copy(src, bufs[s].at[k, pl.ds(r * PACK_SUB, PACK_SUB)], sem.at[s])

    def tile_wait(s):
        for k in range(2):
            pltpu.make_async_copy(ys_hbm.at[pl.ds(0, C_TM * PACK_SUB)], bufs[s].at[k],
                                  sem.at[s]).wait()

    @pl.when(i == 0)
    def _():
        def body(r, c):
            for k in range(2):
                src = ys_hbm.at[pl.ds(pl.multiple_of(dest_ref[2 * r + k] * PACK_SUB, PACK_SUB),
                                      PACK_SUB)]
                dst = ybuf0.at[k, pl.ds(pl.multiple_of(r * PACK_SUB, PACK_SUB), PACK_SUB)]
                pltpu.make_async_copy(src, dst, sem.at[0]).start()
            return c

        lax.fori_loop(0, C_TM, body, 0, unroll=4)

    def step(s):
        tile_wait(s)
        base = 2 * jnp.minimum(i + 1, n - 1) * C_TM
        for r in range(C_TM):
            for k in range(2):
                row_copy(dest_ref[base + 2 * r + k], 1 - s, k, r).start(priority=k)

        route = route_ref[...]
        lane = lax.broadcasted_iota(jnp.int32, route.shape, 1)
        w1 = jnp.sum(jnp.where(lane == 2, route, 0.0), axis=-1, keepdims=True)
        w2 = jnp.sum(jnp.where(lane == 3, route, 0.0), axis=-1, keepdims=True)
        lo1, hi1 = _unpack_rows(bufs[s].at[0], C_TM)
        lo2, hi2 = _unpack_rows(bufs[s].at[1], C_TM)
        half = x_ref.shape[1] // 2
        for c in range(PACK_SUB):
            for off, y1, y2 in ((0, lo1, lo2), (half, hi1, hi2)):
                cols = slice(off + c * LANES, off + (c + 1) * LANES)
                o_ref[:, cols] = x_ref[:, cols] + w1 * y1[c] + w2 * y2[c]

        @pl.when(i == n - 1)
        def _():
            tile_wait(1 - s)

    for s in range(2):
        @pl.when(i % 2 == s)
        def _(s=s):
            step(s)


def combine(ys, x, route, dest):
    t, d = x.shape
    blk = C_TM * PACK_SUB
    grid_spec = pltpu.PrefetchScalarGridSpec(
        num_scalar_prefetch=1,
        grid=(t // C_TM,),
        in_specs=[pl.BlockSpec(memory_space=pl.ANY),
                  pl.BlockSpec((C_TM, d), lambda i, dest: (i, 0)),
                  pl.BlockSpec((C_TM, LANES), lambda i, dest: (i, 0))],
        out_specs=pl.BlockSpec((C_TM, d), lambda i, dest: (i, 0)),
        scratch_shapes=[pltpu.VMEM((2, blk, LANES), jnp.uint32),
                        pltpu.VMEM((2, blk, LANES), jnp.uint32),
                        pltpu.SemaphoreType.DMA((2,))],
    )
    return pl.pallas_call(
        _combine_kernel,
        out_shape=jax.ShapeDtypeStruct((t, d), F32),
        grid_spec=grid_spec,
        compiler_params=_params(("arbitrary",)),
        name="combine",
    )(dest, ys, x, route)


def kernel(x, mem, norm_mix, w_in, rel_bias_a, t5_table, diff_lambda, diff_subln, w_out,
           norm_cross, norm_mem, w_xq, w_xkv, w_xo, norm_ffn, w_group, b_group,
           w_expert, b_expert, w1, w3, w2, norm_final):
    batch, seq, d = x.shape
    t = batch * seq
    xf = x.reshape(t, d)
    memf = mem.reshape(batch * MEM_LEN, d)

    g_mix = norm_mix.reshape(DEPTH, 1, d)
    g_cross = norm_cross.reshape(DEPTH, 1, d)
    g_mem = norm_mem.reshape(DEPTH, 1, d)
    g_ffn = norm_ffn.reshape(DEPTH, 1, d)
    g_final = norm_final.reshape(1, 1, d)
    subln3 = diff_subln.reshape(DEPTH, 1, HEAD_DIM)

    in_scale = jnp.concatenate([
        jnp.full((A_WIDTH,), HEAD_DIM ** -0.5 * LOG2E, F32), jnp.ones((2 * A_WIDTH,), F32),
        jnp.full((B_WIDTH,), B_QK_DIM ** -0.5 * LOG2E, F32), jnp.ones((2 * B_WIDTH,), F32)]
    ).reshape(1, IN_WIDTH)
    xq_scale = jnp.full((1, d), X_HEAD_DIM ** -0.5, F32)

    far_b, bias_b = t5_bias_tables(t5_table)
    bias_a = band_bias_tables(rel_bias_a)

    pad = LANES - N_EXPERTS - N_GROUPS
    wr = jnp.concatenate([w_expert, w_group, jnp.zeros((DEPTH, d, pad), F32)], axis=-1)
    br = jnp.concatenate([b_expert, b_group, jnp.zeros((DEPTH, pad), F32)],
                         axis=-1).reshape(DEPTH, 1, LANES)

    for l in range(DEPTH):
        lam_init = 0.8 - 0.6 * math.exp(-0.3 * l)
        proj = matmul(xf, w_in, l, BF16, norm_gain=g_mix, colscale=in_scale, tm=1024)
        oa = attn_a(proj, bias_a, l, batch)
        ob = attn_b(proj, far_b, bias_b, diff_lambda, subln3, l, lam_init, batch)
        xf = matmul([oa, ob], w_out, l, F32, resid=xf, tm=1024)
        q = matmul(xf, w_xq, l, BF16, norm_gain=g_cross, colscale=xq_scale, tn=2048)
        kv = matmul(memf, w_xkv, l, BF16, norm_gain=g_mem)
        xo = xattn(q, kv, batch)
        xf = matmul(xo, w_xo, l, F32, resid=xf, tm=1024)
        route, counts, hpk = router(xf, g_ffn, wr, br, l)
        dest, tile_expert, tile_fill, n_tiles, tile_plan = dispatch_plan(route, counts)
        xs = dispatch(hpk, dest, tile_fill)
        ys = experts(xs, w1, w3, w2, l, tile_expert, n_tiles, tile_plan)
        xf = combine(ys, xf, route, dest)
    out = rmsnorm(xf, g_final, 0, F32)
    return out.reshape(batch, seq, d)
```

```python
import functools
import math

import jax
import jax.numpy as jnp
import numpy as np
from jax import lax
from jax.experimental import pallas as pl
from jax.experimental.pallas import tpu as pltpu

D_MODEL = 2048
SEQ = 4096
DEPTH = 4
CHUNK = 64
LEFT_CHUNKS = 8
HEAD_DIM = 128
A_HEADS = 8
B_HEADS = 8
B_QK_DIM = 64
A_WIDTH = 1024
B_WIDTH = 1024
IN_WIDTH = 6144
REL_CLIP = 128
T5_BUCKETS = 32
T5_MAX_DIST = 512
MEM_LEN = 256
X_HEADS = 4
X_HEAD_DIM = 512
N_GROUPS = 4
EXPERTS_PER_GROUP = 8
N_EXPERTS = 32
D_EXPERT = 256
EPS = 1e-6
NEG_INF = -1e30

LANES = 128
VMEM_LIMIT = 56 * 1024 * 1024

BF16 = jnp.bfloat16
F32 = jnp.float32

A_QB = 256
A_WIN = A_QB + LEFT_CHUNKS * CHUNK
B_QB = 256
B_NEAR = 3
B_HP = 4
B_FAR_NK = 4
A_HP = 4
LOG2E = math.log2(math.e)
X_QB = 512

E_TM = 256
C_TM = 512
D_TM = 512
FILL_PADDED, FILL_TAIL = 1, 2
PACK_SUB = 8


def _params(sem):
    return pltpu.CompilerParams(dimension_semantics=sem, vmem_limit_bytes=VMEM_LIMIT)


def _matmul_kernel(*refs, n_parts, has_norm, has_scale, has_resid):
    a_refs, w_ref = refs[:n_parts], refs[n_parts]
    k = n_parts + 1
    g_ref = s_ref = r_ref = None
    if has_norm:
        g_ref = refs[k]; k += 1
    if has_scale:
        s_ref = refs[k]; k += 1
    if has_resid:
        r_ref = refs[k]; k += 1
    o_ref, wbf_ref = refs[k], refs[k + 1]

    @pl.when(pl.program_id(1) == 0)
    def _():
        wbf_ref[...] = w_ref[...].astype(BF16)

    acc = None
    k0 = 0
    for a_ref in a_refs:
        kp = a_ref.shape[1]
        a = a_ref[...]
        if has_norm:
            ms = jnp.mean(a * a, axis=-1, keepdims=True)
            a = (a * lax.rsqrt(ms + EPS) * g_ref[...]).astype(BF16)
        part = jnp.dot(a, wbf_ref[k0:k0 + kp, :], preferred_element_type=F32)
        acc = part if acc is None else acc + part
        k0 += kp
    if has_scale:
        acc = acc * s_ref[...]
    if has_resid:
        acc = acc + r_ref[...]
    o_ref[...] = acc.astype(o_ref.dtype)


def matmul(a_parts, w, l, out_dtype, norm_gain=None, colscale=None, resid=None,
           tm=512, tn=1024):
    if not isinstance(a_parts, (list, tuple)):
        a_parts = [a_parts]
    assert norm_gain is None or len(a_parts) == 1
    m = a_parts[0].shape[0]
    k = sum(a.shape[1] for a in a_parts)
    n = w.shape[-1]
    assert w.shape[-2] == k
    tm = min(tm, m)
    tn = min(tn, n)
    in_specs = [pl.BlockSpec((tm, a.shape[1]), lambda j, i: (i, 0)) for a in a_parts]
    in_specs.append(pl.BlockSpec((None, k, tn), lambda j, i: (l, 0, j)))
    args = list(a_parts) + [w]
    if norm_gain is not None:
        in_specs.append(pl.BlockSpec((None, 1, k), lambda j, i: (l, 0, 0)))
        args.append(norm_gain)
    if colscale is not None:
        in_specs.append(pl.BlockSpec((1, tn), lambda j, i: (0, j)))
        args.append(colscale)
    if resid is not None:
        in_specs.append(pl.BlockSpec((tm, tn), lambda j, i: (i, j)))
        args.append(resid)
    return pl.pallas_call(
        functools.partial(_matmul_kernel, n_parts=len(a_parts), has_norm=norm_gain is not None,
                          has_scale=colscale is not None, has_resid=resid is not None),
        out_shape=jax.ShapeDtypeStruct((m, n), out_dtype),
        grid=(n // tn, m // tm),
        in_specs=in_specs,
        out_specs=pl.BlockSpec((tm, tn), lambda j, i: (i, j)),
        scratch_shapes=[pltpu.VMEM((k, tn), BF16)],
        compiler_params=_params(("parallel", "arbitrary")),
        name="matmul",
    )(*args)


def _toeplitz(v, rows, cols):
    hh, n = v.shape
    assert n == rows + cols - 1
    vp = jnp.pad(v, ((0, 0), (0, 1)))
    skew = jnp.tile(vp, (1, rows))[:, :rows * n].reshape(hh, rows, n)
    return skew[:, :, rows - 1:]


def _attn_a_kernel(q_ref, k_ref, v_ref, b0_ref, b1_ref, b2_ref, o_ref, vt_ref, st_ref):
    i = pl.program_id(2)
    qb = A_QB
    n_kt = A_WIN // qb
    bias_refs = (b0_ref, b1_ref, b2_ref)

    @pl.when(i == 0)
    def _():
        for hh in range(A_HP):
            for c in range(SEQ // qb):
                blk = v_ref[c * qb:(c + 1) * qb, hh * HEAD_DIM:(hh + 1) * HEAD_DIM]
                vt_ref[hh, c] = blk.astype(F32).T.astype(BF16)

    t0 = jnp.maximum(i - (n_kt - 1), 0)
    start = pl.multiple_of(t0 * qb, qb)
    qts = [q_ref[:, hh * HEAD_DIM:(hh + 1) * HEAD_DIM].astype(F32).T.astype(BF16)
           for hh in range(A_HP)]
    for hh in range(A_HP):
        st_ref[hh] = jnp.dot(k_ref[pl.ds(start, A_WIN), hh * HEAD_DIM:(hh + 1) * HEAD_DIM],
                             qts[hh], preferred_element_type=F32)
    for hh in range(A_HP):
        st = st_ref[hh] + jnp.concatenate([r[hh] for r in bias_refs], axis=0)
        m = jnp.max(st, axis=0, keepdims=True)
        p = jnp.exp2(st - m)
        denom = jnp.sum(p, axis=0, keepdims=True)
        pb = p.astype(BF16)
        ot = None
        for u in range(n_kt):
            part = jnp.dot(vt_ref[hh, t0 + u], pb[u * qb:(u + 1) * qb],
                           preferred_element_type=F32)
            ot = part if ot is None else ot + part
        o_ref[:, hh * HEAD_DIM:(hh + 1) * HEAD_DIM] = (ot / denom).T.astype(o_ref.dtype)


def attn_a(proj, bias_a, l, batch):
    t = proj.shape[0]
    nq = SEQ // A_QB
    n_kt = A_WIN // A_QB
    wa = A_HP * HEAD_DIM
    ncol = A_WIDTH // wa

    def bias_spec(u):
        return pl.BlockSpec(
            (None, A_HP, None, A_QB, A_QB),
            lambda b, h, i: (l, h, (n_kt - 1) - jnp.minimum(i, n_kt - 1) + u, 0, 0))

    return pl.pallas_call(
        _attn_a_kernel,
        out_shape=jax.ShapeDtypeStruct((t, A_WIDTH), BF16),
        grid=(batch, A_HEADS // A_HP, nq),
        in_specs=[
            pl.BlockSpec((A_QB, wa), lambda b, h, i: (b * nq + i, h)),
            pl.BlockSpec((SEQ, wa), lambda b, h, i: (b, ncol + h)),
            pl.BlockSpec((SEQ, wa), lambda b, h, i: (b, 2 * ncol + h)),
            bias_spec(0), bias_spec(1), bias_spec(2),
        ],
        out_specs=pl.BlockSpec((A_QB, wa), lambda b, h, i: (b * nq + i, h)),
        scratch_shapes=[pltpu.VMEM((A_HP, SEQ // A_QB, HEAD_DIM, A_QB), BF16),
                        pltpu.VMEM((A_HP, A_WIN, A_QB), F32)],
        compiler_params=_params(("arbitrary", "arbitrary", "arbitrary")),
        name="attn_a",
    )(proj, proj, proj, bias_a, bias_a, bias_a)


def band_bias_tables(rel_tables):
    depth, _, heads = rel_tables.shape
    lead = LEFT_CHUNKS * CHUNK
    n_tiles = (A_WIN + lead) // A_QB
    kk = np.arange(A_QB)[:, None]
    r = np.arange(A_QB)[None, :]
    n = np.arange(2 * A_QB - 1)
    tables = rel_tables.astype(F32)
    tiles = []
    for t in range(n_tiles):
        idx = np.clip(t * A_QB + (A_QB - 1 - n) - lead, -REL_CLIP, REL_CLIP) + REL_CLIP
        v = tables[:, idx, :].transpose(0, 2, 1).reshape(depth * heads, -1)
        dchunk = (t * A_QB + kk - lead) // CHUNK - r // CHUNK
        valid = (dchunk >= -LEFT_CHUNKS) & (dchunk <= 0)
        tiles.append(jnp.where(valid[None], _toeplitz(v, A_QB, A_QB) * LOG2E, NEG_INF))
    return jnp.stack(tiles, axis=1).reshape(depth, heads, n_tiles, A_QB, A_QB)


def _attn_b_kernel(far_ref, q_ref, k_ref, v_ref, bias_ref, lam_ref, g_ref, o_ref,
                   qqt_ref, vt_ref, m_ref, l_ref, acc_ref, st_ref, *, lam_init):
    hp = pl.program_id(1)
    i = pl.program_id(2)
    qb = B_QB

    @pl.when(i == 0)
    def _():
        for hh in range(B_HP):
            for c in range(SEQ // qb):
                blk = v_ref[c * qb:(c + 1) * qb, hh * HEAD_DIM:(hh + 1) * HEAD_DIM]
                vt_ref[hh, c] = blk.astype(F32).T.astype(BF16)

    for hh in range(B_HP):
        qt = q_ref[:, hh * HEAD_DIM:(hh + 1) * HEAD_DIM].astype(F32).T
        row = lax.broadcasted_iota(jnp.int32, qt.shape, 0)
        qqt_ref[hh, :, 0:qb] = jnp.where(row < B_QK_DIM, qt, 0.0).astype(BF16)
        qqt_ref[hh, :, qb:2 * qb] = jnp.where(row >= B_QK_DIM, qt, 0.0).astype(BF16)
        m_ref[hh] = jnp.full((1, 2 * qb), NEG_INF, F32)
        l_ref[hh] = jnp.zeros((1, 2 * qb), F32)
        acc_ref[hh] = jnp.zeros((HEAD_DIM, 2 * qb), F32)

    def tile(j, d, nk=1):
        off = pl.multiple_of(j * qb, qb)
        for hh in range(B_HP):
            st_ref[hh, 0:nk * qb] = jnp.dot(
                k_ref[pl.ds(off, nk * qb), hh * HEAD_DIM:(hh + 1) * HEAD_DIM], qqt_ref[hh],
                preferred_element_type=F32)
        for hh in range(B_HP):
            st = st_ref[hh, 0:nk * qb]
            if d is None:
                shift = far_ref[hp * B_HP + hh]
            else:
                b = jnp.concatenate([bias_ref[hh, du] for du in d], axis=0)
                st = jnp.concatenate([st[:, 0:qb] + b, st[:, qb:2 * qb] + b], axis=1)
                shift = 0.0
            m_old = m_ref[hh]
            m_new = jnp.maximum(m_old, jnp.max(st, axis=0, keepdims=True) + shift)
            alpha = jnp.exp2(m_old - m_new)
            p = jnp.exp2(st - (m_new - shift))
            l_ref[hh] = alpha * l_ref[hh] + jnp.sum(p, axis=0, keepdims=True)
            pb = p.astype(BF16)
            pv = None
            for u in range(nk):
                part = jnp.dot(vt_ref[hh, j + u], pb[u * qb:(u + 1) * qb],
                               preferred_element_type=F32)
                pv = part if pv is None else pv + part
            acc_ref[hh] = alpha * acc_ref[hh] + pv
            m_ref[hh] = m_new

    n_far = jnp.maximum(i - (B_NEAR - 1), 0)

    def far_body(jj, c):
        tile(B_FAR_NK * jj, None, nk=B_FAR_NK)
        return c

    n_quads = n_far // B_FAR_NK
    lax.fori_loop(0, n_quads, far_body, 0)
    rem = n_far - B_FAR_NK * n_quads
    assert B_FAR_NK == 4

    @pl.when(rem >= 2)
    def _():
        tile(B_FAR_NK * n_quads, None, nk=2)

    @pl.when(rem % 2 == 1)
    def _():
        tile(n_far - 1, None)

    assert B_NEAR == 3
    @pl.when(i >= 2)
    def _():
        tile(i - 2, (2, 1, 0), nk=3)

    @pl.when(i == 1)
    def _():
        tile(0, (1, 0), nk=2)

    @pl.when(i == 0)
    def _():
        tile(0, (0,))

    lv = lam_ref[...]
    lam = (jnp.exp(jnp.sum(lv[0:1] * lv[1:2], axis=-1, keepdims=True))
           - jnp.exp(jnp.sum(lv[2:3] * lv[3:4], axis=-1, keepdims=True)) + lam_init)
    for hh in range(B_HP):
        o = acc_ref[hh] / l_ref[hh]
        od = o[:, 0:qb] - lam * o[:, qb:2 * qb]
        ms = jnp.mean(od * od, axis=0, keepdims=True)
        y = (od * lax.rsqrt(ms + EPS)).T * g_ref[...]
        o_ref[:, hh * HEAD_DIM:(hh + 1) * HEAD_DIM] = (y * (1.0 - lam_init)).astype(o_ref.dtype)


def attn_b(proj, far_bias, bias_b, lam_vecs, subln3, l, lam_init, batch):
    t = proj.shape[0]
    nq = SEQ // B_QB
    wb = B_HP * HEAD_DIM
    qcol = 3 * A_WIDTH // wb
    ncol = B_WIDTH // wb
    return pl.pallas_call(
        functools.partial(_attn_b_kernel, lam_init=lam_init),
        out_shape=jax.ShapeDtypeStruct((t, B_WIDTH), BF16),
        grid=(batch, B_HEADS // B_HP, nq),
        in_specs=[
            pl.BlockSpec(memory_space=pltpu.SMEM),
            pl.BlockSpec((B_QB, wb), lambda b, h, i: (b * nq + i, qcol + h)),
            pl.BlockSpec((SEQ, wb), lambda b, h, i: (b, qcol + ncol + h)),
            pl.BlockSpec((SEQ, wb), lambda b, h, i: (b, qcol + 2 * ncol + h)),
            pl.BlockSpec((B_HP, B_NEAR, B_QB, B_QB), lambda b, h, i: (h, 0, 0, 0)),
            pl.BlockSpec((None, 4, B_QK_DIM), lambda b, h, i: (l, 0, 0)),
            pl.BlockSpec((None, 1, HEAD_DIM), lambda b, h, i: (l, 0, 0)),
        ],
        out_specs=pl.BlockSpec((B_QB, wb), lambda b, h, i: (b * nq + i, h)),
        scratch_shapes=[pltpu.VMEM((B_HP, HEAD_DIM, 2 * B_QB), BF16),
                        pltpu.VMEM((B_HP, SEQ // B_QB, HEAD_DIM, B_QB), BF16),
                        pltpu.VMEM((B_HP, 1, 2 * B_QB), F32),
                        pltpu.VMEM((B_HP, 1, 2 * B_QB), F32),
                        pltpu.VMEM((B_HP, HEAD_DIM, 2 * B_QB), F32),
                        pltpu.VMEM((B_HP, B_FAR_NK * B_QB, 2 * B_QB), F32)],
        compiler_params=_params(("arbitrary", "arbitrary", "arbitrary")),
        name="attn_b",
    )(far_bias, proj, proj, proj, bias_b, lam_vecs, subln3)


def _t5_bucket(rel):
    half = T5_BUCKETS // 2
    max_exact = half // 2
    ret = jnp.where(rel > 0, half, 0)
    n = jnp.abs(rel)
    n_f = jnp.maximum(n, max_exact).astype(jnp.float32)
    large = max_exact + (jnp.log(n_f / max_exact) / math.log(T5_MAX_DIST / max_exact)
                         * (half - max_exact)).astype(jnp.int32)
    large = jnp.minimum(large, half - 1)
    return ret + jnp.where(n < max_exact, n, large)


def t5_bias_tables(t5_table):
    assert B_NEAR * B_QB - (B_QB - 1) >= T5_MAX_DIST
    lo = -(B_NEAR * B_QB)
    bias1d = t5_table.astype(F32)[_t5_bucket(jnp.arange(lo, B_QB))].T
    far = t5_table.astype(F32)[_t5_bucket(jnp.array([-(SEQ - 1)]))][0]
    u = np.arange(2 * B_QB - 1)
    kk = np.arange(B_QB)[:, None]
    qq = np.arange(B_QB)[None, :]
    tiles = []
    for d in range(B_NEAR):
        t = _toeplitz(bias1d[:, (B_QB - 1 - u - d * B_QB) - lo], B_QB, B_QB)
        if d == 0:
            t = jnp.where(((kk // CHUNK) <= (qq // CHUNK))[None], t * LOG2E, NEG_INF)
        else:
            t = t * LOG2E
        tiles.append(t)
    return far * LOG2E, jnp.stack(tiles, axis=1)


def _xattn_kernel(q_ref, k_ref, v_ref, o_ref, s_ref):
    dh = X_HEAD_DIM
    for h in range(X_HEADS):
        s_ref[h] = lax.dot_general(q_ref[:, h * dh:(h + 1) * dh], k_ref[:, h * dh:(h + 1) * dh],
                                   (((1,), (1,)), ((), ())), preferred_element_type=F32)
    for h in range(X_HEADS):
        s = s_ref[h]
        m = jnp.max(s, axis=-1, keepdims=True)
        p = jnp.exp(s - m)
        denom = jnp.sum(p, axis=-1, keepdims=True)
        o = jnp.dot(p.astype(BF16), v_ref[:, h * dh:(h + 1) * dh], preferred_element_type=F32)
        o_ref[:, h * dh:(h + 1) * dh] = (o / denom).astype(o_ref.dtype)


def xattn(q, kv, batch):
    t, d = q.shape
    nq = SEQ // X_QB
    return pl.pallas_call(
        _xattn_kernel,
        out_shape=jax.ShapeDtypeStruct((t, d), BF16),
        grid=(batch, nq),
        in_specs=[
            pl.BlockSpec((X_QB, d), lambda b, i: (b * nq + i, 0)),
            pl.BlockSpec((MEM_LEN, d), lambda b, i: (b, 0)),
            pl.BlockSpec((MEM_LEN, d), lambda b, i: (b, 1)),
        ],
        out_specs=pl.BlockSpec((X_QB, d), lambda b, i: (b * nq + i, 0)),
        scratch_shapes=[pltpu.VMEM((X_HEADS, X_QB, MEM_LEN), F32)],
        compiler_params=_params(("parallel", "parallel")),
        name="xattn",
    )(q, kv, kv)


def _pack_rows(v, out_ref):
    rows, width = v.shape
    half = width // 2
    lo = lax.bitcast_convert_type(v[:, :half].astype(BF16).astype(F32), jnp.uint32)
    hi = lax.bitcast_convert_type(v[:, half:].astype(BF16).astype(F32), jnp.uint32)
    u = (lo >> 16) | hi
    for c in range(half // LANES):
        out_ref[pl.ds(c, rows, stride=PACK_SUB), :] = u[:, c * LANES:(c + 1) * LANES]


def _unpack_rows(ref, rows):
    lo, hi = [], []
    for c in range(PACK_SUB):
        u = ref[pl.ds(c, rows, stride=PACK_SUB), :]
        lo.append(lax.bitcast_convert_type(u << 16, F32))
        hi.append(lax.bitcast_convert_type(u & jnp.uint32(0xFFFF0000), F32))
    return lo, hi


def _router_kernel(x_ref, g_ref, wr_ref, br_ref, route_ref, cnt_ref, hpk_ref, carry_ref, tri_ref,
                   wrb_ref):
    tm = x_ref.shape[0]

    @pl.when(pl.program_id(0) == 0)
    def _():
        carry_ref[...] = jnp.zeros(carry_ref.shape, F32)
        wrb_ref[...] = wr_ref[...].astype(BF16)
        r = lax.broadcasted_iota(jnp.int32, (tm, tm), 0)
        c = lax.broadcasted_iota(jnp.int32, (tm, tm), 1)
        tri_ref[...] = jnp.where(c < r, 1.0, 0.0).astype(BF16)

    x = x_ref[...]
    ms = jnp.mean(x * x, axis=-1, keepdims=True)
    hn = x * lax.rsqrt(ms + EPS) * g_ref[...]
    _pack_rows(hn, hpk_ref)
    hb = hn.astype(BF16)
    logits = jnp.dot(hb, wrb_ref[...], preferred_element_type=F32) + br_ref[...]

    lane = lax.broadcasted_iota(jnp.int32, logits.shape, 1)
    lane_f = lane.astype(F32)
    big = jnp.float32(1e9)

    def top1(mask):
        v = jnp.where(mask, logits, -jnp.inf)
        vmax = jnp.max(v, axis=-1, keepdims=True)
        idx = jnp.min(jnp.where(mask & (v == vmax), lane_f, big), axis=-1, keepdims=True)
        return vmax, idx

    gmask = (lane >= N_EXPERTS) & (lane < N_EXPERTS + N_GROUPS)
    gmax, gidx = top1(gmask)
    gsum = jnp.sum(jnp.where(gmask, jnp.exp(logits - gmax), 0.0), axis=-1, keepdims=True)
    g_gate = 1.0 / gsum
    g_sel = gidx.astype(jnp.int32) - N_EXPERTS

    emask = (lane < N_EXPERTS) & ((lane // EXPERTS_PER_GROUP) == g_sel)
    v1, i1 = top1(emask)
    v2, i2 = top1(emask & (lane_f != i1))
    e2 = jnp.exp(v2 - v1)
    w1 = g_gate / (1.0 + e2)
    w2 = g_gate * e2 / (1.0 + e2)

    oh1 = jnp.where(lane_f == i1, 1.0, 0.0)
    oh2 = jnp.where(lane_f == i2, 1.0, 0.0)
    oh = oh1 + oh2
    before = jnp.dot(tri_ref[...], oh.astype(BF16), preferred_element_type=F32) + carry_ref[...]
    r1 = jnp.sum(oh1 * before, axis=-1, keepdims=True)
    r2 = jnp.sum(oh2 * before, axis=-1, keepdims=True)
    carry_ref[...] = carry_ref[...] + jnp.sum(oh, axis=0, keepdims=True)
    cnt_ref[...] = carry_ref[...]

    route = jnp.zeros(logits.shape, F32)
    for col, val in enumerate((i1, i2, w1, w2, r1, r2)):
        route = jnp.where(lane == col, val, route)
    route_ref[...] = route


def router(x, g3, wr, br, l, tm=512):
    t, d = x.shape
    return pl.pallas_call(
        _router_kernel,
        out_shape=(jax.ShapeDtypeStruct((t, LANES), F32),
                   jax.ShapeDtypeStruct((1, LANES), F32),
                   jax.ShapeDtypeStruct((t * PACK_SUB, LANES), jnp.uint32)),
        grid=(t // tm,),
        in_specs=[pl.BlockSpec((tm, d), lambda i: (i, 0)),
                  pl.BlockSpec((None, 1, d), lambda i: (l, 0, 0)),
                  pl.BlockSpec((None, d, LANES), lambda i: (l, 0, 0)),
                  pl.BlockSpec((None, 1, LANES), lambda i: (l, 0, 0))],
        out_specs=(pl.BlockSpec((tm, LANES), lambda i: (i, 0)),
                   pl.BlockSpec((1, LANES), lambda i: (0, 0)),
                   pl.BlockSpec((tm * PACK_SUB, LANES), lambda i: (i, 0))),
        scratch_shapes=[pltpu.VMEM((1, LANES), F32), pltpu.VMEM((tm, tm), BF16),
                        pltpu.VMEM((d, LANES), BF16)],
        compiler_params=_params(("arbitrary",)),
        name="router",
    )(x, g3, wr, br)


def _max_tiles(t):
    return (2 * t) // E_TM + N_EXPERTS


def dispatch_plan(route, counts_row):
    t = route.shape[0]
    max_tiles = _max_tiles(t)
    ids = route[:, 0:2].astype(jnp.int32)
    ranks = route[:, 4:6].astype(jnp.int32)
    counts = counts_row[0, :N_EXPERTS].astype(jnp.int32)
    padded = ((counts + E_TM - 1) // E_TM) * E_TM
    ends = jnp.cumsum(padded)
    offs = ends - padded
    onehot = ids[:, :, None] == jnp.arange(N_EXPERTS, dtype=jnp.int32)
    dest = (jnp.sum(jnp.where(onehot, offs, 0), axis=-1) + ranks).reshape(2 * t)
    n_tiles = ends[-1] // E_TM
    idx = jnp.arange(max_tiles, dtype=jnp.int32)
    tile_i = jnp.minimum(idx, n_tiles - 1)
    tile_expert = jnp.sum(ends[None, :] <= tile_i[:, None] * E_TM, axis=1).astype(jnp.int32)
    used_end = (offs + counts)[tile_expert]
    tile_rows = jnp.clip(used_end - tile_i * E_TM, 0, E_TM).astype(jnp.int32)
    tile_fill = jnp.where(idx >= n_tiles, FILL_TAIL,
                          jnp.where(tile_rows < E_TM, FILL_PADDED, 0)).astype(jnp.int32)
    first = (idx < n_tiles) & ((idx == 0) | (tile_expert != jnp.roll(tile_expert, 1)))
    slot = (jnp.cumsum(first) - 1) % 2
    first_pos = jnp.where(first, idx, max_tiles)
    after = jnp.flip(lax.cummin(jnp.flip(first_pos)))
    nxt_pos = jnp.concatenate([after[1:], jnp.full((1,), max_tiles, jnp.int32)])
    nxt = jnp.where(nxt_pos < max_tiles,
                    tile_expert[jnp.minimum(nxt_pos, max_tiles - 1)], -1)
    tile_plan = jnp.stack([first.astype(jnp.int32), slot.astype(jnp.int32),
                           nxt.astype(jnp.int32)])
    return dest, tile_expert, tile_fill, n_tiles.reshape(1).astype(jnp.int32), tile_plan


def _dispatch_kernel(dest_ref, fill_ref, h_ref, xs_hbm, stage0, stage1, sem, zbuf, zsem):
    i = pl.program_id(0)
    n = pl.num_programs(0)
    stages = (stage0, stage1)

    def fill_copy(j, kind):
        blk = E_TM * PACK_SUB
        return pltpu.make_async_copy(zbuf, xs_hbm.at[pl.ds(j * blk, blk)], zsem.at[kind - 1])

    def fill_each(kind, fn):
        for j in range(fill_ref.shape[0]):
            @pl.when(fill_ref[j] == kind)
            def _(j=j):
                fn(fill_copy(j, kind))

    @pl.when(i == 0)
    def _():
        zbuf[...] = jnp.zeros(zbuf.shape, zbuf.dtype)
        fill_each(FILL_PADDED, lambda c: c.start())
        fill_each(FILL_TAIL, lambda c: c.start())
        fill_each(FILL_PADDED, lambda c: c.wait())

    def drain(s):
        for _ in range(2):
            pltpu.make_async_copy(stages[s], xs_hbm.at[pl.ds(0, D_TM * PACK_SUB)],
                                  sem.at[s]).wait()

    def step(s):
        @pl.when(i >= 2)
        def _():
            drain(s)

        stages[s][...] = h_ref[...]
        base = 2 * i * D_TM
        for r in range(D_TM):
            for k in range(2):
                row = dest_ref[base + 2 * r + k]
                dst = xs_hbm.at[pl.ds(pl.multiple_of(row * PACK_SUB, PACK_SUB), PACK_SUB)]
                pltpu.make_async_copy(stages[s].at[pl.ds(r * PACK_SUB, PACK_SUB)], dst,
                                      sem.at[s]).start(priority=k)

        @pl.when(i == n - 1)
        def _():
            drain(s)

            @pl.when(n >= 2)
            def _():
                drain(1 - s)

    for s in range(2):
        @pl.when(i % 2 == s)
        def _(s=s):
            step(s)

    @pl.when(i == n - 1)
    def _():
        fill_each(FILL_TAIL, lambda c: c.wait())


def dispatch(hpk, dest, tile_fill):
    t = dest.shape[0] // 2
    blk = D_TM * PACK_SUB
    grid_spec = pltpu.PrefetchScalarGridSpec(
        num_scalar_prefetch=2,
        grid=(t // D_TM,),
        in_specs=[pl.BlockSpec((blk, LANES), lambda i, dest, fill: (i, 0))],
        out_specs=pl.BlockSpec(memory_space=pl.ANY),
        scratch_shapes=[pltpu.VMEM((blk, LANES), jnp.uint32),
                        pltpu.VMEM((blk, LANES), jnp.uint32),
                        pltpu.SemaphoreType.DMA((2,)),
                        pltpu.VMEM((E_TM * PACK_SUB, LANES), jnp.uint32),
                        pltpu.SemaphoreType.DMA((2,))],
    )
    return pl.pallas_call(
        _dispatch_kernel,
        out_shape=jax.ShapeDtypeStruct((tile_fill.shape[0] * E_TM * PACK_SUB, LANES), jnp.uint32),
        grid_spec=grid_spec,
        compiler_params=_params(("arbitrary",)),
        name="dispatch",
    )(dest, tile_fill, hpk)


def _experts_kernel(te_ref, nt_ref, plan_ref, xs_ref, w1_hbm, w3_hbm, w2_hbm, ys_ref,
                    w1f, w3f, w2f, wsem, w1b, w3b, w2b, *, layer):
    i = pl.program_id(0)
    nt = nt_ref[0]

    def weight_copies(e, slot):
        return [pltpu.make_async_copy(w_hbm.at[layer, e], wf.at[slot], wsem.at[slot])
                for w_hbm, wf in ((w1_hbm, w1f), (w3_hbm, w3f), (w2_hbm, w2f))]

    @pl.when(i == 0)
    def _():
        for c in weight_copies(te_ref[0], 0):
            c.start()

    @pl.when(i < nt)
    def _():
        @pl.when(plan_ref[0, i] == 1)
        def _():
            slot = plan_ref[1, i]
            for c in weight_copies(te_ref[i], slot):
                c.wait()
            w1b[...] = w1f[slot].astype(BF16)
            w3b[...] = w3f[slot].astype(BF16)
            w2b[...] = w2f[slot].astype(BF16)

            @pl.when(plan_ref[2, i] >= 0)
            def _():
                for c in weight_copies(plan_ref[2, i], 1 - slot):
                    c.start()

        lo, hi = _unpack_rows(xs_ref, E_TM)
        hb = jnp.concatenate([v.astype(BF16) for v in lo + hi], axis=1)
        a1 = jnp.dot(hb, w1b[...], preferred_element_type=F32)
        a3 = jnp.dot(hb, w3b[...], preferred_element_type=F32)
        hid = (a1 / (1.0 + jnp.exp(-a1))) * a3
        _pack_rows(jnp.dot(hid.astype(BF16), w2b[...], preferred_element_type=F32), ys_ref)

    @pl.when(i >= nt)
    def _():
        ys_ref[...] = jnp.zeros(ys_ref.shape, ys_ref.dtype)


def experts(xs, w1, w3, w2, l, tile_expert, n_tiles, tile_plan):
    d = w1.shape[-2]
    max_tiles = tile_expert.shape[0]
    blk = E_TM * PACK_SUB
    any_spec = pl.BlockSpec(memory_space=pl.ANY)
    grid_spec = pltpu.PrefetchScalarGridSpec(
        num_scalar_prefetch=3,
        grid=(max_tiles,),
        in_specs=[pl.BlockSpec((blk, LANES),
                               lambda i, te, nt, plan: (jnp.minimum(i, nt[0] - 1), 0)),
                  any_spec, any_spec, any_spec],
        out_specs=pl.BlockSpec((blk, LANES), lambda i, te, nt, plan: (i, 0)),
        scratch_shapes=[pltpu.VMEM((2, d, D_EXPERT), F32),
                        pltpu.VMEM((2, d, D_EXPERT), F32),
                        pltpu.VMEM((2, D_EXPERT, d), F32),
                        pltpu.SemaphoreType.DMA((2,)),
                        pltpu.VMEM((d, D_EXPERT), BF16),
                        pltpu.VMEM((d, D_EXPERT), BF16),
                        pltpu.VMEM((D_EXPERT, d), BF16)],
    )
    return pl.pallas_call(
        functools.partial(_experts_kernel, layer=l),
        out_shape=jax.ShapeDtypeStruct((max_tiles * blk, LANES), jnp.uint32),
        grid_spec=grid_spec,
        compiler_params=_params(("arbitrary",)),
        name="experts",
    )(tile_expert, n_tiles, tile_plan, xs, w1, w3, w2)


def _combine_kernel(dest_ref, ys_hbm, x_ref, route_ref, *rest, has_norm):
    g_ref = rest[0] if has_norm else None
    o_ref, ybuf0, ybuf1, sem = rest[1:] if has_norm else rest
    i = pl.program_id(0)
    n = pl.num_programs(0)
    bufs = (ybuf0, ybuf1)

    def row_copy(row, s, k, r):
        src = ys_hbm.at[pl.ds(pl.multiple_of(row * PACK_SUB, PACK_SUB), PACK_SUB)]
        return pltpu.make_async_copy(src, bufs[s].at[k, pl.ds(r * PACK_SUB, PACK_SUB)], sem.at[s])

    def tile_wait(s):
        for k in range(2):
            pltpu.make_async_copy(ys_hbm.at[pl.ds(0, C_TM * PACK_SUB)], bufs[s].at[k],
                                  sem.at[s]).wait()

    @pl.when(i == 0)
    def _():
        def body(r, c):
            for k in range(2):
                src = ys_hbm.at[pl.ds(pl.multiple_of(dest_ref[2 * r + k] * PACK_SUB, PACK_SUB),
                                      PACK_SUB)]
                dst = ybuf0.at[k, pl.ds(pl.multiple_of(r * PACK_SUB, PACK_SUB), PACK_SUB)]
                pltpu.make_async_copy(src, dst, sem.at[0]).start()
            return c

        lax.fori_loop(0, C_TM, body, 0, unroll=4)

    def step(s):
        tile_wait(s)
        base = 2 * jnp.minimum(i + 1, n - 1) * C_TM
        for r in range(C_TM):
            for k in range(2):
                row_copy(dest_ref[base + 2 * r + k], 1 - s, k, r).start(priority=k)

        route = route_ref[...]
        lane = lax.broadcasted_iota(jnp.int32, route.shape, 1)
        w1 = jnp.sum(jnp.where(lane == 2, route, 0.0), axis=-1, keepdims=True)
        w2 = jnp.sum(jnp.where(lane == 3, route, 0.0), axis=-1, keepdims=True)
        lo1, hi1 = _unpack_rows(bufs[s].at[0], C_TM)
        lo2, hi2 = _unpack_rows(bufs[s].at[1], C_TM)
        half = x_ref.shape[1] // 2
        for c in range(PACK_SUB):
            for off, y1, y2 in ((0, lo1, lo2), (half, hi1, hi2)):
                cols = slice(off + c * LANES, off + (c + 1) * LANES)
                o_ref[:, cols] = x_ref[:, cols] + w1 * y1[c] + w2 * y2[c]
        if has_norm:
            y = o_ref[...]
            ms = jnp.mean(y * y, axis=-1, keepdims=True)
            o_ref[...] = y * lax.rsqrt(ms + EPS) * g_ref[...]

        @pl.when(i == n - 1)
        def _():
            tile_wait(1 - s)

    for s in range(2):
        @pl.when(i % 2 == s)
        def _(s=s):
            step(s)


def combine(ys, x, route, dest, norm_gain=None):
    t, d = x.shape
    blk = C_TM * PACK_SUB
    grid_spec = pltpu.PrefetchScalarGridSpec(
        num_scalar_prefetch=1,
        grid=(t // C_TM,),
        in_specs=[pl.BlockSpec(memory_space=pl.ANY),
                  pl.BlockSpec((C_TM, d), lambda i, dest: (i, 0)),
                  pl.BlockSpec((C_TM, LANES), lambda i, dest: (i, 0))]
        + ([] if norm_gain is None else [pl.BlockSpec((None, 1, d), lambda i, dest: (0, 0, 0))]),
        out_specs=pl.BlockSpec((C_TM, d), lambda i, dest: (i, 0)),
        scratch_shapes=[pltpu.VMEM((2, blk, LANES), jnp.uint32),
                        pltpu.VMEM((2, blk, LANES), jnp.uint32),
                        pltpu.SemaphoreType.DMA((2,))],
    )
    return pl.pallas_call(
        functools.partial(_combine_kernel, has_norm=norm_gain is not None),
        out_shape=jax.ShapeDtypeStruct((t, d), F32),
        grid_spec=grid_spec,
        compiler_params=_params(("arbitrary",)),
        name="combine",
    )(dest, ys, x, route, *([] if norm_gain is None else [norm_gain]))


def kernel(x, mem, norm_mix, w_in, rel_bias_a, t5_table, diff_lambda, diff_subln, w_out,
           norm_cross, norm_mem, w_xq, w_xkv, w_xo, norm_ffn, w_group, b_group,
           w_expert, b_expert, w1, w3, w2, norm_final):
    batch, seq, d = x.shape
    t = batch * seq
    xf = x.reshape(t, d)
    memf = mem.reshape(batch * MEM_LEN, d)

    g_mix = norm_mix.reshape(DEPTH, 1, d)
    g_cross = norm_cross.reshape(DEPTH, 1, d)
    g_mem = norm_mem.reshape(DEPTH, 1, d)
    g_ffn = norm_ffn.reshape(DEPTH, 1, d)
    g_final = norm_final.reshape(1, 1, d)
    subln3 = diff_subln.reshape(DEPTH, 1, HEAD_DIM)

    in_scale = jnp.concatenate([
        jnp.full((A_WIDTH,), HEAD_DIM ** -0.5 * LOG2E, F32), jnp.ones((2 * A_WIDTH,), F32),
        jnp.full((B_WIDTH,), B_QK_DIM ** -0.5 * LOG2E, F32), jnp.ones((2 * B_WIDTH,), F32)]
    ).reshape(1, IN_WIDTH)
    xq_scale = jnp.full((1, d), X_HEAD_DIM ** -0.5, F32)

    far_b, bias_b = t5_bias_tables(t5_table)
    bias_a = band_bias_tables(rel_bias_a)

    pad = LANES - N_EXPERTS - N_GROUPS
    wr = jnp.concatenate([w_expert, w_group, jnp.zeros((DEPTH, d, pad), F32)], axis=-1)
    br = jnp.concatenate([b_expert, b_group, jnp.zeros((DEPTH, pad), F32)],
                         axis=-1).reshape(DEPTH, 1, LANES)

    for l in range(DEPTH):
        lam_init = 0.8 - 0.6 * math.exp(-0.3 * l)
        proj = matmul(xf, w_in, l, BF16, norm_gain=g_mix, colscale=in_scale, tm=1024)
        oa = attn_a(proj, bias_a, l, batch)
        ob = attn_b(proj, far_b, bias_b, diff_lambda, subln3, l, lam_init, batch)
        xf = matmul([oa, ob], w_out, l, F32, resid=xf, tm=1024)
        q = matmul(xf, w_xq, l, BF16, norm_gain=g_cross, colscale=xq_scale, tn=2048)
        kv = matmul(memf, w_xkv, l, BF16, norm_gain=g_mem)
        xo = xattn(q, kv, batch)
        xf = matmul(xo, w_xo, l, F32, resid=xf, tm=1024)
        route, counts, hpk = router(xf, g_ffn, wr, br, l)
        dest, tile_expert, tile_fill, n_tiles, tile_plan = dispatch_plan(route, counts)
        xs = dispatch(hpk, dest, tile_fill)
        ys = experts(xs, w1, w3, w2, l, tile_expert, n_tiles, tile_plan)
        xf = combine(ys, xf, route, dest, norm_gain=g_final if l == DEPTH - 1 else None)
    return xf.reshape(batch, seq, d)
```

```python
import functools
import math

import jax
import jax.numpy as jnp
import numpy as np
from jax import lax
from jax.experimental import pallas as pl
from jax.experimental.pallas import tpu as pltpu

D_MODEL = 2048
SEQ = 4096
DEPTH = 4
CHUNK = 64
LEFT_CHUNKS = 8
HEAD_DIM = 128
A_HEADS = 8
B_HEADS = 8
B_QK_DIM = 64
A_WIDTH = 1024
B_WIDTH = 1024
IN_WIDTH = 6144
REL_CLIP = 128
T5_BUCKETS = 32
T5_MAX_DIST = 512
MEM_LEN = 256
X_HEADS = 4
X_HEAD_DIM = 512
N_GROUPS = 4
EXPERTS_PER_GROUP = 8
N_EXPERTS = 32
D_EXPERT = 256
EPS = 1e-6
NEG_INF = -1e30

LANES = 128
VMEM_LIMIT = 56 * 1024 * 1024

BF16 = jnp.bfloat16
F32 = jnp.float32

A_QB = 256
A_WIN = A_QB + LEFT_CHUNKS * CHUNK
B_QB = 256
B_NEAR = 3
B_HP = 4
B_FAR_NK = 8
A_HP = 4
LOG2E = math.log2(math.e)
X_QB = 512

E_TM = 256
C_TM = 512
D_TM = 512
FILL_PADDED, FILL_TAIL = 1, 2
PACK_SUB = 8


def _params(sem):
    return pltpu.CompilerParams(dimension_semantics=sem, vmem_limit_bytes=VMEM_LIMIT)


def _matmul_kernel(*refs, n_parts, has_norm, has_scale, has_resid):
    a_refs, w_ref = refs[:n_parts], refs[n_parts]
    k = n_parts + 1
    g_ref = s_ref = r_ref = None
    if has_norm:
        g_ref = refs[k]; k += 1
    if has_scale:
        s_ref = refs[k]; k += 1
    if has_resid:
        r_ref = refs[k]; k += 1
    o_ref, wbf_ref = refs[k], refs[k + 1]

    @pl.when(pl.program_id(1) == 0)
    def _():
        wbf_ref[...] = w_ref[...].astype(BF16)

    acc = None
    k0 = 0
    for a_ref in a_refs:
        kp = a_ref.shape[1]
        a = a_ref[...]
        if has_norm:
            ms = jnp.mean(a * a, axis=-1, keepdims=True)
            a = (a * lax.rsqrt(ms + EPS) * g_ref[...]).astype(BF16)
        part = jnp.dot(a, wbf_ref[k0:k0 + kp, :], preferred_element_type=F32)
        acc = part if acc is None else acc + part
        k0 += kp
    if has_scale:
        acc = acc * s_ref[...]
    if has_resid:
        acc = acc + r_ref[...]
    o_ref[...] = acc.astype(o_ref.dtype)


def matmul(a_parts, w, l, out_dtype, norm_gain=None, colscale=None, resid=None,
           tm=512, tn=1024):
    if not isinstance(a_parts, (list, tuple)):
        a_parts = [a_parts]
    assert norm_gain is None or len(a_parts) == 1
    m = a_parts[0].shape[0]
    k = sum(a.shape[1] for a in a_parts)
    n = w.shape[-1]
    assert w.shape[-2] == k
    tm = min(tm, m)
    tn = min(tn, n)
    in_specs = [pl.BlockSpec((tm, a.shape[1]), lambda j, i: (i, 0)) for a in a_parts]
    in_specs.append(pl.BlockSpec((None, k, tn), lambda j, i: (l, 0, j)))
    args = list(a_parts) + [w]
    if norm_gain is not None:
        in_specs.append(pl.BlockSpec((None, 1, k), lambda j, i: (l, 0, 0)))
        args.append(norm_gain)
    if colscale is not None:
        in_specs.append(pl.BlockSpec((1, tn), lambda j, i: (0, j)))
        args.append(colscale)
    if resid is not None:
        in_specs.append(pl.BlockSpec((tm, tn), lambda j, i: (i, j)))
        args.append(resid)
    return pl.pallas_call(
        functools.partial(_matmul_kernel, n_parts=len(a_parts), has_norm=norm_gain is not None,
                          has_scale=colscale is not None, has_resid=resid is not None),
        out_shape=jax.ShapeDtypeStruct((m, n), out_dtype),
        grid=(n // tn, m // tm),
        in_specs=in_specs,
        out_specs=pl.BlockSpec((tm, tn), lambda j, i: (i, j)),
        scratch_shapes=[pltpu.VMEM((k, tn), BF16)],
        compiler_params=_params(("parallel", "arbitrary")),
        name="matmul",
    )(*args)


def _toeplitz(v, rows, cols):
    hh, n = v.shape
    assert n == rows + cols - 1
    vp = jnp.pad(v, ((0, 0), (0, 1)))
    skew = jnp.tile(vp, (1, rows))[:, :rows * n].reshape(hh, rows, n)
    return skew[:, :, rows - 1:]


def _attn_a_kernel(q_ref, k_ref, v_ref, b0_ref, b1_ref, b2_ref, o_ref, vt_ref, st_ref):
    i = pl.program_id(2)
    qb = A_QB
    n_kt = A_WIN // qb
    bias_refs = (b0_ref, b1_ref, b2_ref)

    @pl.when(i == 0)
    def _():
        for hh in range(A_HP):
            for c in range(SEQ // qb):
                blk = v_ref[c * qb:(c + 1) * qb, hh * HEAD_DIM:(hh + 1) * HEAD_DIM]
                vt_ref[hh, c] = blk.astype(F32).T.astype(BF16)

    t0 = jnp.maximum(i - (n_kt - 1), 0)
    start = pl.multiple_of(t0 * qb, qb)
    qts = [q_ref[:, hh * HEAD_DIM:(hh + 1) * HEAD_DIM].astype(F32).T.astype(BF16)
           for hh in range(A_HP)]
    for hh in range(A_HP):
        st_ref[hh] = jnp.dot(k_ref[pl.ds(start, A_WIN), hh * HEAD_DIM:(hh + 1) * HEAD_DIM],
                             qts[hh], preferred_element_type=F32)
    for hh in range(A_HP):
        st = st_ref[hh] + jnp.concatenate([r[hh] for r in bias_refs], axis=0)
        m = jnp.max(st, axis=0, keepdims=True)
        p = jnp.exp2(st - m)
        denom = jnp.sum(p, axis=0, keepdims=True)
        pb = p.astype(BF16)
        ot = None
        for u in range(n_kt):
            part = jnp.dot(vt_ref[hh, t0 + u], pb[u * qb:(u + 1) * qb],
                           preferred_element_type=F32)
            ot = part if ot is None else ot + part
        o_ref[:, hh * HEAD_DIM:(hh + 1) * HEAD_DIM] = (ot / denom).T.astype(o_ref.dtype)


def attn_a(proj, bias_a, l, batch):
    t = proj.shape[0]
    nq = SEQ // A_QB
    n_kt = A_WIN // A_QB
    wa = A_HP * HEAD_DIM
    ncol = A_WIDTH // wa

    def bias_spec(u):
        return pl.BlockSpec(
            (None, A_HP, None, A_QB, A_QB),
            lambda b, h, i: (l, h, (n_kt - 1) - jnp.minimum(i, n_kt - 1) + u, 0, 0))

    return pl.pallas_call(
        _attn_a_kernel,
        out_shape=jax.ShapeDtypeStruct((t, A_WIDTH), BF16),
        grid=(batch, A_HEADS // A_HP, nq),
        in_specs=[
            pl.BlockSpec((A_QB, wa), lambda b, h, i: (b * nq + i, h)),
            pl.BlockSpec((SEQ, wa), lambda b, h, i: (b, ncol + h)),
            pl.BlockSpec((SEQ, wa), lambda b, h, i: (b, 2 * ncol + h)),
            bias_spec(0), bias_spec(1), bias_spec(2),
        ],
        out_specs=pl.BlockSpec((A_QB, wa), lambda b, h, i: (b * nq + i, h)),
        scratch_shapes=[pltpu.VMEM((A_HP, SEQ // A_QB, HEAD_DIM, A_QB), BF16),
                        pltpu.VMEM((A_HP, A_WIN, A_QB), F32)],
        compiler_params=_params(("arbitrary", "arbitrary", "arbitrary")),
        name="attn_a",
    )(proj, proj, proj, bias_a, bias_a, bias_a)


def band_bias_tables(rel_tables):
    depth, _, heads = rel_tables.shape
    lead = LEFT_CHUNKS * CHUNK
    n_tiles = (A_WIN + lead) // A_QB
    kk = np.arange(A_QB)[:, None]
    r = np.arange(A_QB)[None, :]
    n = np.arange(2 * A_QB - 1)
    tables = rel_tables.astype(F32)
    tiles = []
    for t in range(n_tiles):
        idx = np.clip(t * A_QB + (A_QB - 1 - n) - lead, -REL_CLIP, REL_CLIP) + REL_CLIP
        v = tables[:, idx, :].transpose(0, 2, 1).reshape(depth * heads, -1)
        dchunk = (t * A_QB + kk - lead) // CHUNK - r // CHUNK
        valid = (dchunk >= -LEFT_CHUNKS) & (dchunk <= 0)
        tiles.append(jnp.where(valid[None], _toeplitz(v, A_QB, A_QB) * LOG2E, NEG_INF))
    return jnp.stack(tiles, axis=1).reshape(depth, heads, n_tiles, A_QB, A_QB)


def _attn_b_kernel(far_ref, q_ref, k_ref, v_ref, bias_ref, lam_ref, g_ref, o_ref,
                   qqt_ref, vt_ref, m_ref, l_ref, acc_ref, st_ref, *, lam_init):
    hp = pl.program_id(1)
    i = pl.program_id(2)
    qb = B_QB

    @pl.when(i == 0)
    def _():
        for hh in range(B_HP):
            for c in range(SEQ // qb):
                blk = v_ref[c * qb:(c + 1) * qb, hh * HEAD_DIM:(hh + 1) * HEAD_DIM]
                vt_ref[hh, c] = blk.astype(F32).T.astype(BF16)

    for hh in range(B_HP):
        qt = q_ref[:, hh * HEAD_DIM:(hh + 1) * HEAD_DIM].astype(F32).T
        row = lax.broadcasted_iota(jnp.int32, qt.shape, 0)
        qqt_ref[hh, :, 0:qb] = jnp.where(row < B_QK_DIM, qt, 0.0).astype(BF16)
        qqt_ref[hh, :, qb:2 * qb] = jnp.where(row >= B_QK_DIM, qt, 0.0).astype(BF16)
        m_ref[hh] = jnp.full((1, 2 * qb), NEG_INF, F32)
        l_ref[hh] = jnp.zeros((1, 2 * qb), F32)
        acc_ref[hh] = jnp.zeros((HEAD_DIM, 2 * qb), F32)

    def tile(j, d, nk=1):
        off = pl.multiple_of(j * qb, qb)
        for hh in range(B_HP):
            st_ref[hh, 0:nk * qb] = jnp.dot(
                k_ref[pl.ds(off, nk * qb), hh * HEAD_DIM:(hh + 1) * HEAD_DIM], qqt_ref[hh],
                preferred_element_type=F32)
        for hh in range(B_HP):
            st = st_ref[hh, 0:nk * qb]
            if d is None:
                shift = far_ref[hp * B_HP + hh]
            else:
                b = jnp.concatenate([bias_ref[hh, du] for du in d], axis=0)
                st = jnp.concatenate([st[:, 0:qb] + b, st[:, qb:2 * qb] + b], axis=1)
                shift = 0.0
            m_old = m_ref[hh]
            m_new = jnp.maximum(m_old, jnp.max(st, axis=0, keepdims=True) + shift)
            alpha = jnp.exp2(m_old - m_new)
            p = jnp.exp2(st - (m_new - shift))
            l_ref[hh] = alpha * l_ref[hh] + jnp.sum(p, axis=0, keepdims=True)
            pb = p.astype(BF16)
            pv = None
            for u in range(nk):
                part = jnp.dot(vt_ref[hh, j + u], pb[u * qb:(u + 1) * qb],
                               preferred_element_type=F32)
                pv = part if pv is None else pv + part
            acc_ref[hh] = alpha * acc_ref[hh] + pv
            m_ref[hh] = m_new

    n_far = jnp.maximum(i - (B_NEAR - 1), 0)

    def far_body(jj, c):
        tile(B_FAR_NK * jj, None, nk=B_FAR_NK)
        return c

    n_quads = n_far // B_FAR_NK
    lax.fori_loop(0, n_quads, far_body, 0)
    rem = n_far - B_FAR_NK * n_quads
    assert B_FAR_NK == 8

    @pl.when(rem >= 4)
    def _():
        tile(B_FAR_NK * n_quads, None, nk=4)

    @pl.when(rem % 4 >= 2)
    def _():
        tile(n_far - (rem % 4), None, nk=2)

    @pl.when(rem % 2 == 1)
    def _():
        tile(n_far - 1, None)

    assert B_NEAR == 3
    @pl.when(i >= 2)
    def _():
        tile(i - 2, (2, 1, 0), nk=3)

    @pl.when(i == 1)
    def _():
        tile(0, (1, 0), nk=2)

    @pl.when(i == 0)
    def _():
        tile(0, (0,))

    lv = lam_ref[...]
    lam = (jnp.exp(jnp.sum(lv[0:1] * lv[1:2], axis=-1, keepdims=True))
           - jnp.exp(jnp.sum(lv[2:3] * lv[3:4], axis=-1, keepdims=True)) + lam_init)
    for hh in range(B_HP):
        o = acc_ref[hh] / l_ref[hh]
        od = o[:, 0:qb] - lam * o[:, qb:2 * qb]
        ms = jnp.mean(od * od, axis=0, keepdims=True)
        y = (od * lax.rsqrt(ms + EPS)).T * g_ref[...]
        o_ref[:, hh * HEAD_DIM:(hh + 1) * HEAD_DIM] = (y * (1.0 - lam_init)).astype(o_ref.dtype)


def attn_b(proj, far_bias, bias_b, lam_vecs, subln3, l, lam_init, batch):
    t = proj.shape[0]
    nq = SEQ // B_QB
    wb = B_HP * HEAD_DIM
    qcol = 3 * A_WIDTH // wb
    ncol = B_WIDTH // wb
    return pl.pallas_call(
        functools.partial(_attn_b_kernel, lam_init=lam_init),
        out_shape=jax.ShapeDtypeStruct((t, B_WIDTH), BF16),
        grid=(batch, B_HEADS // B_HP, nq),
        in_specs=[
            pl.BlockSpec(memory_space=pltpu.SMEM),
            pl.BlockSpec((B_QB, wb), lambda b, h, i: (b * nq + i, qcol + h)),
            pl.BlockSpec((SEQ, wb), lambda b, h, i: (b, qcol + ncol + h)),
            pl.BlockSpec((SEQ, wb), lambda b, h, i: (b, qcol + 2 * ncol + h)),
            pl.BlockSpec((B_HP, B_NEAR, B_QB, B_QB), lambda b, h, i: (h, 0, 0, 0)),
            pl.BlockSpec((None, 4, B_QK_DIM), lambda b, h, i: (l, 0, 0)),
            pl.BlockSpec((None, 1, HEAD_DIM), lambda b, h, i: (l, 0, 0)),
        ],
        out_specs=pl.BlockSpec((B_QB, wb), lambda b, h, i: (b * nq + i, h)),
        scratch_shapes=[pltpu.VMEM((B_HP, HEAD_DIM, 2 * B_QB), BF16),
                        pltpu.VMEM((B_HP, SEQ // B_QB, HEAD_DIM, B_QB), BF16),
                        pltpu.VMEM((B_HP, 1, 2 * B_QB), F32),
                        pltpu.VMEM((B_HP, 1, 2 * B_QB), F32),
                        pltpu.VMEM((B_HP, HEAD_DIM, 2 * B_QB), F32),
                        pltpu.VMEM((B_HP, B_FAR_NK * B_QB, 2 * B_QB), F32)],
        compiler_params=_params(("arbitrary", "arbitrary", "arbitrary")),
        name="attn_b",
    )(far_bias, proj, proj, proj, bias_b, lam_vecs, subln3)


def _t5_bucket(rel):
    half = T5_BUCKETS // 2
    max_exact = half // 2
    ret = jnp.where(rel > 0, half, 0)
    n = jnp.abs(rel)
    n_f = jnp.maximum(n, max_exact).astype(jnp.float32)
    large = max_exact + (jnp.log(n_f / max_exact) / math.log(T5_MAX_DIST / max_exact)
                         * (half - max_exact)).astype(jnp.int32)
    large = jnp.minimum(large, half - 1)
    return ret + jnp.where(n < max_exact, n, large)


def t5_bias_tables(t5_table):
    assert B_NEAR * B_QB - (B_QB - 1) >= T5_MAX_DIST
    lo = -(B_NEAR * B_QB)
    bias1d = t5_table.astype(F32)[_t5_bucket(jnp.arange(lo, B_QB))].T
    far = t5_table.astype(F32)[_t5_bucket(jnp.array([-(SEQ - 1)]))][0]
    u = np.arange(2 * B_QB - 1)
    kk = np.arange(B_QB)[:, None]
    qq = np.arange(B_QB)[None, :]
    tiles = []
    for d in range(B_NEAR):
        t = _toeplitz(bias1d[:, (B_QB - 1 - u - d * B_QB) - lo], B_QB, B_QB)
        if d == 0:
            t = jnp.where(((kk // CHUNK) <= (qq // CHUNK))[None], t * LOG2E, NEG_INF)
        else:
            t = t * LOG2E
        tiles.append(t)
    return far * LOG2E, jnp.stack(tiles, axis=1)


def _xattn_kernel(q_ref, k_ref, v_ref, o_ref, s_ref):
    dh = X_HEAD_DIM
    for h in range(X_HEADS):
        s_ref[h] = lax.dot_general(q_ref[:, h * dh:(h + 1) * dh], k_ref[:, h * dh:(h + 1) * dh],
                                   (((1,), (1,)), ((), ())), preferred_element_type=F32)
    for h in range(X_HEADS):
        s = s_ref[h]
        m = jnp.max(s, axis=-1, keepdims=True)
        p = jnp.exp(s - m)
        denom = jnp.sum(p, axis=-1, keepdims=True)
        o = jnp.dot(p.astype(BF16), v_ref[:, h * dh:(h + 1) * dh], preferred_element_type=F32)
        o_ref[:, h * dh:(h + 1) * dh] = (o / denom).astype(o_ref.dtype)


def xattn(q, kv, batch):
    t, d = q.shape
    nq = SEQ // X_QB
    return pl.pallas_call(
        _xattn_kernel,
        out_shape=jax.ShapeDtypeStruct((t, d), BF16),
        grid=(batch, nq),
        in_specs=[
            pl.BlockSpec((X_QB, d), lambda b, i: (b * nq + i, 0)),
            pl.BlockSpec((MEM_LEN, d), lambda b, i: (b, 0)),
            pl.BlockSpec((MEM_LEN, d), lambda b, i: (b, 1)),
        ],
        out_specs=pl.BlockSpec((X_QB, d), lambda b, i: (b * nq + i, 0)),
        scratch_shapes=[pltpu.VMEM((X_HEADS, X_QB, MEM_LEN), F32)],
        compiler_params=_params(("parallel", "parallel")),
        name="xattn",
    )(q, kv, kv)


def _pack_rows(v, out_ref):
    rows, width = v.shape
    half = width // 2
    lo = lax.bitcast_convert_type(v[:, :half].astype(BF16).astype(F32), jnp.uint32)
    hi = lax.bitcast_convert_type(v[:, half:].astype(BF16).astype(F32), jnp.uint32)
    u = (lo >> 16) | hi
    for c in range(half // LANES):
        out_ref[pl.ds(c, rows, stride=PACK_SUB), :] = u[:, c * LANES:(c + 1) * LANES]


def _unpack_rows(ref, rows):
    lo, hi = [], []
    for c in range(PACK_SUB):
        u = ref[pl.ds(c, rows, stride=PACK_SUB), :]
        lo.append(lax.bitcast_convert_type(u << 16, F32))
        hi.append(lax.bitcast_convert_type(u & jnp.uint32(0xFFFF0000), F32))
    return lo, hi


def _router_kernel(x_ref, g_ref, wr_ref, br_ref, route_ref, cnt_ref, hpk_ref, carry_ref, tri_ref,
                   wrb_ref):
    tm = x_ref.shape[0]

    @pl.when(pl.program_id(0) == 0)
    def _():
        carry_ref[...] = jnp.zeros(carry_ref.shape, F32)
        wrb_ref[...] = wr_ref[...].astype(BF16)
        r = lax.broadcasted_iota(jnp.int32, (tm, tm), 0)
        c = lax.broadcasted_iota(jnp.int32, (tm, tm), 1)
        tri_ref[...] = jnp.where(c < r, 1.0, 0.0).astype(BF16)

    x = x_ref[...]
    ms = jnp.mean(x * x, axis=-1, keepdims=True)
    hn = x * lax.rsqrt(ms + EPS) * g_ref[...]
    _pack_rows(hn, hpk_ref)
    hb = hn.astype(BF16)
    logits = jnp.dot(hb, wrb_ref[...], preferred_element_type=F32) + br_ref[...]

    lane = lax.broadcasted_iota(jnp.int32, logits.shape, 1)
    lane_f = lane.astype(F32)
    big = jnp.float32(1e9)

    def top1(mask):
        v = jnp.where(mask, logits, -jnp.inf)
        vmax = jnp.max(v, axis=-1, keepdims=True)
        idx = jnp.min(jnp.where(mask & (v == vmax), lane_f, big), axis=-1, keepdims=True)
        return vmax, idx

    gmask = (lane >= N_EXPERTS) & (lane < N_EXPERTS + N_GROUPS)
    gmax, gidx = top1(gmask)
    gsum = jnp.sum(jnp.where(gmask, jnp.exp(logits - gmax), 0.0), axis=-1, keepdims=True)
    g_gate = 1.0 / gsum
    g_sel = gidx.astype(jnp.int32) - N_EXPERTS

    emask = (lane < N_EXPERTS) & ((lane // EXPERTS_PER_GROUP) == g_sel)
    v1, i1 = top1(emask)
    v2, i2 = top1(emask & (lane_f != i1))
    e2 = jnp.exp(v2 - v1)
    w1 = g_gate / (1.0 + e2)
    w2 = g_gate * e2 / (1.0 + e2)

    oh1 = jnp.where(lane_f == i1, 1.0, 0.0)
    oh2 = jnp.where(lane_f == i2, 1.0, 0.0)
    oh = oh1 + oh2
    before = jnp.dot(tri_ref[...], oh.astype(BF16), preferred_element_type=F32) + carry_ref[...]
    r1 = jnp.sum(oh1 * before, axis=-1, keepdims=True)
    r2 = jnp.sum(oh2 * before, axis=-1, keepdims=True)
    carry_ref[...] = carry_ref[...] + jnp.sum(oh, axis=0, keepdims=True)
    cnt_ref[...] = carry_ref[...]

    route = jnp.zeros(logits.shape, F32)
    for col, val in enumerate((i1, i2, w1, w2, r1, r2)):
        route = jnp.where(lane == col, val, route)
    route_ref[...] = route


def router(x, g3, wr, br, l, tm=512):
    t, d = x.shape
    return pl.pallas_call(
        _router_kernel,
        out_shape=(jax.ShapeDtypeStruct((t, LANES), F32),
                   jax.ShapeDtypeStruct((1, LANES), F32),
                   jax.ShapeDtypeStruct((t * PACK_SUB, LANES), jnp.uint32)),
        grid=(t // tm,),
        in_specs=[pl.BlockSpec((tm, d), lambda i: (i, 0)),
                  pl.BlockSpec((None, 1, d), lambda i: (l, 0, 0)),
                  pl.BlockSpec((None, d, LANES), lambda i: (l, 0, 0)),
                  pl.BlockSpec((None, 1, LANES), lambda i: (l, 0, 0))],
        out_specs=(pl.BlockSpec((tm, LANES), lambda i: (i, 0)),
                   pl.BlockSpec((1, LANES), lambda i: (0, 0)),
                   pl.BlockSpec((tm * PACK_SUB, LANES), lambda i: (i, 0))),
        scratch_shapes=[pltpu.VMEM((1, LANES), F32), pltpu.VMEM((tm, tm), BF16),
                        pltpu.VMEM((d, LANES), BF16)],
        compiler_params=_params(("arbitrary",)),
        name="router",
    )(x, g3, wr, br)


def _max_tiles(t):
    return (2 * t) // E_TM + N_EXPERTS


def dispatch_plan(route, counts_row):
    t = route.shape[0]
    max_tiles = _max_tiles(t)
    ids = route[:, 0:2].astype(jnp.int32)
    ranks = route[:, 4:6].astype(jnp.int32)
    counts = counts_row[0, :N_EXPERTS].astype(jnp.int32)
    padded = ((counts + E_TM - 1) // E_TM) * E_TM
    ends = jnp.cumsum(padded)
    offs = ends - padded
    onehot = ids[:, :, None] == jnp.arange(N_EXPERTS, dtype=jnp.int32)
    dest = (jnp.sum(jnp.where(onehot, offs, 0), axis=-1) + ranks).reshape(2 * t)
    n_tiles = ends[-1] // E_TM
    idx = jnp.arange(max_tiles, dtype=jnp.int32)
    tile_i = jnp.minimum(idx, n_tiles - 1)
    tile_expert = jnp.sum(ends[None, :] <= tile_i[:, None] * E_TM, axis=1).astype(jnp.int32)
    used_end = (offs + counts)[tile_expert]
    tile_rows = jnp.clip(used_end - tile_i * E_TM, 0, E_TM).astype(jnp.int32)
    tile_fill = jnp.where(idx >= n_tiles, FILL_TAIL,
                          jnp.where(tile_rows < E_TM, FILL_PADDED, 0)).astype(jnp.int32)
    first = (idx < n_tiles) & ((idx == 0) | (tile_expert != jnp.roll(tile_expert, 1)))
    slot = (jnp.cumsum(first) - 1) % 2
    first_pos = jnp.where(first, idx, max_tiles)
    after = jnp.flip(lax.cummin(jnp.flip(first_pos)))
    nxt_pos = jnp.concatenate([after[1:], jnp.full((1,), max_tiles, jnp.int32)])
    nxt = jnp.where(nxt_pos < max_tiles,
                    tile_expert[jnp.minimum(nxt_pos, max_tiles - 1)], -1)
    tile_plan = jnp.stack([first.astype(jnp.int32), slot.astype(jnp.int32),
                           nxt.astype(jnp.int32)])
    return dest, tile_expert, tile_fill, n_tiles.reshape(1).astype(jnp.int32), tile_plan


def _dispatch_kernel(dest_ref, fill_ref, h_ref, xs_hbm, stage0, stage1, sem, zbuf, zsem):
    i = pl.program_id(0)
    n = pl.num_programs(0)
    stages = (stage0, stage1)

    def fill_copy(j, kind):
        blk = E_TM * PACK_SUB
        return pltpu.make_async_copy(zbuf, xs_hbm.at[pl.ds(j * blk, blk)], zsem.at[kind - 1])

    def fill_each(kind, fn):
        for j in range(fill_ref.shape[0]):
            @pl.when(fill_ref[j] == kind)
            def _(j=j):
                fn(fill_copy(j, kind))

    @pl.when(i == 0)
    def _():
        zbuf[...] = jnp.zeros(zbuf.shape, zbuf.dtype)
        fill_each(FILL_PADDED, lambda c: c.start())
        fill_each(FILL_TAIL, lambda c: c.start())
        fill_each(FILL_PADDED, lambda c: c.wait())

    def drain(s):
        for _ in range(2):
            pltpu.make_async_copy(stages[s], xs_hbm.at[pl.ds(0, D_TM * PACK_SUB)],
                                  sem.at[s]).wait()

    def step(s):
        @pl.when(i >= 2)
        def _():
            drain(s)

        stages[s][...] = h_ref[...]
        base = 2 * i * D_TM
        for r in range(D_TM):
            for k in range(2):
                row = dest_ref[base + 2 * r + k]
                dst = xs_hbm.at[pl.ds(pl.multiple_of(row * PACK_SUB, PACK_SUB), PACK_SUB)]
                pltpu.make_async_copy(stages[s].at[pl.ds(r * PACK_SUB, PACK_SUB)], dst,
                                      sem.at[s]).start(priority=k)

        @pl.when(i == n - 1)
        def _():
            drain(s)

            @pl.when(n >= 2)
            def _():
                drain(1 - s)

    for s in range(2):
        @pl.when(i % 2 == s)
        def _(s=s):
            step(s)

    @pl.when(i == n - 1)
    def _():
        fill_each(FILL_TAIL, lambda c: c.wait())


def dispatch(hpk, dest, tile_fill):
    t = dest.shape[0] // 2
    blk = D_TM * PACK_SUB
    grid_spec = pltpu.PrefetchScalarGridSpec(
        num_scalar_prefetch=2,
        grid=(t // D_TM,),
        in_specs=[pl.BlockSpec((blk, LANES), lambda i, dest, fill: (i, 0))],
        out_specs=pl.BlockSpec(memory_space=pl.ANY),
        scratch_shapes=[pltpu.VMEM((blk, LANES), jnp.uint32),
                        pltpu.VMEM((blk, LANES), jnp.uint32),
                        pltpu.SemaphoreType.DMA((2,)),
                        pltpu.VMEM((E_TM * PACK_SUB, LANES), jnp.uint32),
                        pltpu.SemaphoreType.DMA((2,))],
    )
    return pl.pallas_call(
        _dispatch_kernel,
        out_shape=jax.ShapeDtypeStruct((tile_fill.shape[0] * E_TM * PACK_SUB, LANES), jnp.uint32),
        grid_spec=grid_spec,
        compiler_params=_params(("arbitrary",)),
        name="dispatch",
    )(dest, tile_fill, hpk)


def _experts_kernel(te_ref, nt_ref, plan_ref, xs_ref, w1_hbm, w3_hbm, w2_hbm, ys_ref,
                    w1f, w3f, w2f, wsem, w1b, w3b, w2b, *, layer):
    i = pl.program_id(0)
    nt = nt_ref[0]

    def weight_copies(e, slot):
        return [pltpu.make_async_copy(w_hbm.at[layer, e], wf.at[slot], wsem.at[slot])
                for w_hbm, wf in ((w1_hbm, w1f), (w3_hbm, w3f), (w2_hbm, w2f))]

    @pl.when(i == 0)
    def _():
        for c in weight_copies(te_ref[0], 0):
            c.start()

    @pl.when(i < nt)
    def _():
        @pl.when(plan_ref[0, i] == 1)
        def _():
            slot = plan_ref[1, i]
            for c in weight_copies(te_ref[i], slot):
                c.wait()
            w1b[...] = w1f[slot].astype(BF16)
            w3b[...] = w3f[slot].astype(BF16)
            w2b[...] = w2f[slot].astype(BF16)

            @pl.when(plan_ref[2, i] >= 0)
            def _():
                for c in weight_copies(plan_ref[2, i], 1 - slot):
                    c.start()

        lo, hi = _unpack_rows(xs_ref, E_TM)
        hb = jnp.concatenate([v.astype(BF16) for v in lo + hi], axis=1)
        a1 = jnp.dot(hb, w1b[...], preferred_element_type=F32)
        a3 = jnp.dot(hb, w3b[...], preferred_element_type=F32)
        hid = (a1 / (1.0 + jnp.exp(-a1))) * a3
        _pack_rows(jnp.dot(hid.astype(BF16), w2b[...], preferred_element_type=F32), ys_ref)

    @pl.when(i >= nt)
    def _():
        ys_ref[...] = jnp.zeros(ys_ref.shape, ys_ref.dtype)


def experts(xs, w1, w3, w2, l, tile_expert, n_tiles, tile_plan):
    d = w1.shape[-2]
    max_tiles = tile_expert.shape[0]
    blk = E_TM * PACK_SUB
    any_spec = pl.BlockSpec(memory_space=pl.ANY)
    grid_spec = pltpu.PrefetchScalarGridSpec(
        num_scalar_prefetch=3,
        grid=(max_tiles,),
        in_specs=[pl.BlockSpec((blk, LANES),
                               lambda i, te, nt, plan: (jnp.minimum(i, nt[0] - 1), 0)),
                  any_spec, any_spec, any_spec],
        out_specs=pl.BlockSpec((blk, LANES), lambda i, te, nt, plan: (i, 0)),
        scratch_shapes=[pltpu.VMEM((2, d, D_EXPERT), F32),
                        pltpu.VMEM((2, d, D_EXPERT), F32),
                        pltpu.VMEM((2, D_EXPERT, d), F32),
                        pltpu.SemaphoreType.DMA((2,)),
                        pltpu.VMEM((d, D_EXPERT), BF16),
                        pltpu.VMEM((d, D_EXPERT), BF16),
                        pltpu.VMEM((D_EXPERT, d), BF16)],
    )
    return pl.pallas_call(
        functools.partial(_experts_kernel, layer=l),
        out_shape=jax.ShapeDtypeStruct((max_tiles * blk, LANES), jnp.uint32),
        grid_spec=grid_spec,
        compiler_params=_params(("arbitrary",)),
        name="experts",
    )(tile_expert, n_tiles, tile_plan, xs, w1, w3, w2)


def _combine_kernel(dest_ref, ys_hbm, x_ref, route_ref, *rest, has_norm):
    g_ref = rest[0] if has_norm else None
    o_ref, ybuf0, ybuf1, sem = rest[1:] if has_norm else rest
    i = pl.program_id(0)
    n = pl.num_programs(0)
    bufs = (ybuf0, ybuf1)

    def row_copy(row, s, k, r):
        src = ys_hbm.at[pl.ds(pl.multiple_of(row * PACK_SUB, PACK_SUB), PACK_SUB)]
        return pltpu.make_async_copy(src, bufs[s].at[k, pl.ds(r * PACK_SUB, PACK_SUB)], sem.at[s])

    def tile_wait(s):
        for k in range(2):
            pltpu.make_async_copy(ys_hbm.at[pl.ds(0, C_TM * PACK_SUB)], bufs[s].at[k],
                                  sem.at[s]).wait()

    @pl.when(i == 0)
    def _():
        def body(r, c):
            for k in range(2):
                src = ys_hbm.at[pl.ds(pl.multiple_of(dest_ref[2 * r + k] * PACK_SUB, PACK_SUB),
                                      PACK_SUB)]
                dst = ybuf0.at[k, pl.ds(pl.multiple_of(r * PACK_SUB, PACK_SUB), PACK_SUB)]
                pltpu.make_async_copy(src, dst, sem.at[0]).start()
            return c

        lax.fori_loop(0, C_TM, body, 0, unroll=4)

    def step(s):
        tile_wait(s)
        base = 2 * jnp.minimum(i + 1, n - 1) * C_TM
        for r in range(C_TM):
            for k in range(2):
                row_copy(dest_ref[base + 2 * r + k], 1 - s, k, r).start(priority=k)

        route = route_ref[...]
        lane = lax.broadcasted_iota(jnp.int32, route.shape, 1)
        w1 = jnp.sum(jnp.where(lane == 2, route, 0.0), axis=-1, keepdims=True)
        w2 = jnp.sum(jnp.where(lane == 3, route, 0.0), axis=-1, keepdims=True)
        lo1, hi1 = _unpack_rows(bufs[s].at[0], C_TM)
        lo2, hi2 = _unpack_rows(bufs[s].at[1], C_TM)
        half = x_ref.shape[1] // 2
        for c in range(PACK_SUB):
            for off, y1, y2 in ((0, lo1, lo2), (half, hi1, hi2)):
                cols = slice(off + c * LANES, off + (c + 1) * LANES)
                o_ref[:, cols] = x_ref[:, cols] + w1 * y1[c] + w2 * y2[c]
        if has_norm:
            y = o_ref[...]
            ms = jnp.mean(y * y, axis=-1, keepdims=True)
            o_ref[...] = y * lax.rsqrt(ms + EPS) * g_ref[...]

        @pl.when(i == n - 1)
        def _():
            tile_wait(1 - s)

    for s in range(2):
        @pl.when(i % 2 == s)
        def _(s=s):
            step(s)


def combine(ys, x, route, dest, norm_gain=None):
    t, d = x.shape
    blk = C_TM * PACK_SUB
    grid_spec = pltpu.PrefetchScalarGridSpec(
        num_scalar_prefetch=1,
        grid=(t // C_TM,),
        in_specs=[pl.BlockSpec(memory_space=pl.ANY),
                  pl.BlockSpec((C_TM, d), lambda i, dest: (i, 0)),
                  pl.BlockSpec((C_TM, LANES), lambda i, dest: (i, 0))]
        + ([] if norm_gain is None else [pl.BlockSpec((None, 1, d), lambda i, dest: (0, 0, 0))]),
        out_specs=pl.BlockSpec((C_TM, d), lambda i, dest: (i, 0)),
        scratch_shapes=[pltpu.VMEM((2, blk, LANES), jnp.uint32),
                        pltpu.VMEM((2, blk, LANES), jnp.uint32),
                        pltpu.SemaphoreType.DMA((2,))],
    )
    return pl.pallas_call(
        functools.partial(_combine_kernel, has_norm=norm_gain is not None),
        out_shape=jax.ShapeDtypeStruct((t, d), F32),
        grid_spec=grid_spec,
        compiler_params=_params(("arbitrary",)),
        name="combine",
    )(dest, ys, x, route, *([] if norm_gain is None else [norm_gain]))


def kernel(x, mem, norm_mix, w_in, rel_bias_a, t5_table, diff_lambda, diff_subln, w_out,
           norm_cross, norm_mem, w_xq, w_xkv, w_xo, norm_ffn, w_group, b_group,
           w_expert, b_expert, w1, w3, w2, norm_final):
    batch, seq, d = x.shape
    t = batch * seq
    xf = x.reshape(t, d)
    memf = mem.reshape(batch * MEM_LEN, d)

    g_mix = norm_mix.reshape(DEPTH, 1, d)
    g_cross = norm_cross.reshape(DEPTH, 1, d)
    g_mem = norm_mem.reshape(DEPTH, 1, d)
    g_ffn = norm_ffn.reshape(DEPTH, 1, d)
    g_final = norm_final.reshape(1, 1, d)
    subln3 = diff_subln.reshape(DEPTH, 1, HEAD_DIM)

    in_scale = jnp.concatenate([
        jnp.full((A_WIDTH,), HEAD_DIM ** -0.5 * LOG2E, F32), jnp.ones((2 * A_WIDTH,), F32),
        jnp.full((B_WIDTH,), B_QK_DIM ** -0.5 * LOG2E, F32), jnp.ones((2 * B_WIDTH,), F32)]
    ).reshape(1, IN_WIDTH)
    xq_scale = jnp.full((1, d), X_HEAD_DIM ** -0.5, F32)

    far_b, bias_b = t5_bias_tables(t5_table)
    bias_a = band_bias_tables(rel_bias_a)

    pad = LANES - N_EXPERTS - N_GROUPS
    wr = jnp.concatenate([w_expert, w_group, jnp.zeros((DEPTH, d, pad), F32)], axis=-1)
    br = jnp.concatenate([b_expert, b_group, jnp.zeros((DEPTH, pad), F32)],
                         axis=-1).reshape(DEPTH, 1, LANES)

    for l in range(DEPTH):
        lam_init = 0.8 - 0.6 * math.exp(-0.3 * l)
        proj = matmul(xf, w_in, l, BF16, norm_gain=g_mix, colscale=in_scale, tm=1024)
        oa = attn_a(proj, bias_a, l, batch)
        ob = attn_b(proj, far_b, bias_b, diff_lambda, subln3, l, lam_init, batch)
        xf = matmul([oa, ob], w_out, l, F32, resid=xf, tm=1024)
        q = matmul(xf, w_xq, l, BF16, norm_gain=g_cross, colscale=xq_scale, tn=2048)
        kv = matmul(memf, w_xkv, l, BF16, norm_gain=g_mem)
        xo = xattn(q, kv, batch)
        xf = matmul(xo, w_xo, l, F32, resid=xf, tm=1024)
        route, counts, hpk = router(xf, g_ffn, wr, br, l)
        dest, tile_expert, tile_fill, n_tiles, tile_plan = dispatch_plan(route, counts)
        xs = dispatch(hpk, dest, tile_fill)
        ys = experts(xs, w1, w3, w2, l, tile_expert, n_tiles, tile_plan)
        xf = combine(ys, xf, route, dest, norm_gain=g_final if l == DEPTH - 1 else None)
    return xf.reshape(batch, seq, d)
```

```python
import functools
import math

import jax
import jax.numpy as jnp
import numpy as np
from jax import lax
from jax.experimental import pallas as pl
from jax.experimental.pallas import tpu as pltpu

D_MODEL = 2048
SEQ = 4096
DEPTH = 4
CHUNK = 64
LEFT_CHUNKS = 8
HEAD_DIM = 128
A_HEADS = 8
B_HEADS = 8
B_QK_DIM = 64
A_WIDTH = 1024
B_WIDTH = 1024
IN_WIDTH = 6144
REL_CLIP = 128
T5_BUCKETS = 32
T5_MAX_DIST = 512
MEM_LEN = 256
X_HEADS = 4
X_HEAD_DIM = 512
N_GROUPS = 4
EXPERTS_PER_GROUP = 8
N_EXPERTS = 32
D_EXPERT = 256
EPS = 1e-6
NEG_INF = -1e30

LANES = 128
VMEM_LIMIT = 56 * 1024 * 1024

BF16 = jnp.bfloat16
F32 = jnp.float32

A_QB = 256
A_WIN = A_QB + LEFT_CHUNKS * CHUNK
B_QB = 256
B_NEAR = 3
B_HP = 4
B_FAR_NK = 8
A_HP = 4
A_VT_ROWS = HEAD_DIM + 16
LOG2E = math.log2(math.e)
X_QB = 512

E_TM = 256
C_TM = 512
D_TM = 512
FILL_PADDED, FILL_TAIL = 1, 2
PACK_SUB = 8


def _params(sem):
    return pltpu.CompilerParams(dimension_semantics=sem, vmem_limit_bytes=VMEM_LIMIT)


def _matmul_kernel(*refs, n_parts, has_norm, has_scale, has_resid):
    a_refs, w_ref = refs[:n_parts], refs[n_parts]
    k = n_parts + 1
    g_ref = s_ref = r_ref = None
    if has_norm:
        g_ref = refs[k]; k += 1
    if has_scale:
        s_ref = refs[k]; k += 1
    if has_resid:
        r_ref = refs[k]; k += 1
    o_ref, wbf_ref = refs[k], refs[k + 1]

    @pl.when(pl.program_id(1) == 0)
    def _():
        wbf_ref[...] = w_ref[...].astype(BF16)

    acc = None
    k0 = 0
    for a_ref in a_refs:
        kp = a_ref.shape[1]
        a = a_ref[...]
        if has_norm:
            ms = jnp.mean(a * a, axis=-1, keepdims=True)
            a = (a * lax.rsqrt(ms + EPS) * g_ref[...]).astype(BF16)
        part = jnp.dot(a, wbf_ref[k0:k0 + kp, :], preferred_element_type=F32)
        acc = part if acc is None else acc + part
        k0 += kp
    if has_scale:
        acc = acc * s_ref[...]
    if has_resid:
        acc = acc + r_ref[...]
    o_ref[...] = acc.astype(o_ref.dtype)


def matmul(a_parts, w, l, out_dtype, norm_gain=None, colscale=None, resid=None,
           tm=512, tn=1024):
    if not isinstance(a_parts, (list, tuple)):
        a_parts = [a_parts]
    assert norm_gain is None or len(a_parts) == 1
    m = a_parts[0].shape[0]
    k = sum(a.shape[1] for a in a_parts)
    n = w.shape[-1]
    assert w.shape[-2] == k
    tm = min(tm, m)
    tn = min(tn, n)
    in_specs = [pl.BlockSpec((tm, a.shape[1]), lambda j, i: (i, 0)) for a in a_parts]
    in_specs.append(pl.BlockSpec((None, k, tn), lambda j, i: (l, 0, j)))
    args = list(a_parts) + [w]
    if norm_gain is not None:
        in_specs.append(pl.BlockSpec((None, 1, k), lambda j, i: (l, 0, 0)))
        args.append(norm_gain)
    if colscale is not None:
        in_specs.append(pl.BlockSpec((1, tn), lambda j, i: (0, j)))
        args.append(colscale)
    if resid is not None:
        in_specs.append(pl.BlockSpec((tm, tn), lambda j, i: (i, j)))
        args.append(resid)
    return pl.pallas_call(
        functools.partial(_matmul_kernel, n_parts=len(a_parts), has_norm=norm_gain is not None,
                          has_scale=colscale is not None, has_resid=resid is not None),
        out_shape=jax.ShapeDtypeStruct((m, n), out_dtype),
        grid=(n // tn, m // tm),
        in_specs=in_specs,
        out_specs=pl.BlockSpec((tm, tn), lambda j, i: (i, j)),
        scratch_shapes=[pltpu.VMEM((k, tn), BF16)],
        compiler_params=_params(("parallel", "arbitrary")),
        name="matmul",
    )(*args)


def _toeplitz(v, rows, cols):
    hh, n = v.shape
    assert n == rows + cols - 1
    vp = jnp.pad(v, ((0, 0), (0, 1)))
    skew = jnp.tile(vp, (1, rows))[:, :rows * n].reshape(hh, rows, n)
    return skew[:, :, rows - 1:]


def _attn_a_kernel(q_ref, k_ref, v_ref, b0_ref, b1_ref, b2_ref, o_ref, vt_ref, st_ref):
    i = pl.program_id(2)
    qb = A_QB
    n_kt = A_WIN // qb
    bias_refs = (b0_ref, b1_ref, b2_ref)

    @pl.when(i == 0)
    def _():
        for hh in range(A_HP):
            for c in range(SEQ // qb):
                blk = v_ref[c * qb:(c + 1) * qb, hh * HEAD_DIM:(hh + 1) * HEAD_DIM]
                vt_ref[hh, c, 0:HEAD_DIM] = blk.astype(F32).T.astype(BF16)
                ext = lax.broadcasted_iota(jnp.int32, (A_VT_ROWS - HEAD_DIM, qb), 0)
                vt_ref[hh, c, HEAD_DIM:A_VT_ROWS] = jnp.where(ext == 0, 1.0, 0.0).astype(BF16)

    t0 = jnp.maximum(i - (n_kt - 1), 0)
    start = pl.multiple_of(t0 * qb, qb)
    qts = [q_ref[:, hh * HEAD_DIM:(hh + 1) * HEAD_DIM].astype(F32).T.astype(BF16)
           for hh in range(A_HP)]
    for hh in range(A_HP):
        st_ref[hh] = jnp.dot(k_ref[pl.ds(start, A_WIN), hh * HEAD_DIM:(hh + 1) * HEAD_DIM],
                             qts[hh], preferred_element_type=F32)
    for hh in range(A_HP):
        st = st_ref[hh] + jnp.concatenate([r[hh] for r in bias_refs], axis=0)
        m = jnp.max(st, axis=0, keepdims=True)
        p = jnp.exp2(st - m)
        pb = p.astype(BF16)
        ot = None
        for u in range(n_kt):
            part = jnp.dot(vt_ref[hh, t0 + u], pb[u * qb:(u + 1) * qb],
                           preferred_element_type=F32)
            ot = part if ot is None else ot + part
        denom = ot[HEAD_DIM:HEAD_DIM + 1]
        o_ref[:, hh * HEAD_DIM:(hh + 1) * HEAD_DIM] = (
            ot[0:HEAD_DIM] / denom).T.astype(o_ref.dtype)


def attn_a(proj, bias_a, l, batch):
    t = proj.shape[0]
    nq = SEQ // A_QB
    n_kt = A_WIN // A_QB
    wa = A_HP * HEAD_DIM
    ncol = A_WIDTH // wa

    def bias_spec(u):
        return pl.BlockSpec(
            (None, A_HP, None, A_QB, A_QB),
            lambda b, h, i: (l, h, (n_kt - 1) - jnp.minimum(i, n_kt - 1) + u, 0, 0))

    return pl.pallas_call(
        _attn_a_kernel,
        out_shape=jax.ShapeDtypeStruct((t, A_WIDTH), BF16),
        grid=(batch, A_HEADS // A_HP, nq),
        in_specs=[
            pl.BlockSpec((A_QB, wa), lambda b, h, i: (b * nq + i, h)),
            pl.BlockSpec((SEQ, wa), lambda b, h, i: (b, ncol + h)),
            pl.BlockSpec((SEQ, wa), lambda b, h, i: (b, 2 * ncol + h)),
            bias_spec(0), bias_spec(1), bias_spec(2),
        ],
        out_specs=pl.BlockSpec((A_QB, wa), lambda b, h, i: (b * nq + i, h)),
        scratch_shapes=[pltpu.VMEM((A_HP, SEQ // A_QB, A_VT_ROWS, A_QB), BF16),
                        pltpu.VMEM((A_HP, A_WIN, A_QB), F32)],
        compiler_params=_params(("arbitrary", "arbitrary", "arbitrary")),
        name="attn_a",
    )(proj, proj, proj, bias_a, bias_a, bias_a)


def band_bias_tables(rel_tables):
    depth, _, heads = rel_tables.shape
    lead = LEFT_CHUNKS * CHUNK
    n_tiles = (A_WIN + lead) // A_QB
    kk = np.arange(A_QB)[:, None]
    r = np.arange(A_QB)[None, :]
    n = np.arange(2 * A_QB - 1)
    tables = rel_tables.astype(F32)
    tiles = []
    for t in range(n_tiles):
        idx = np.clip(t * A_QB + (A_QB - 1 - n) - lead, -REL_CLIP, REL_CLIP) + REL_CLIP
        v = tables[:, idx, :].transpose(0, 2, 1).reshape(depth * heads, -1)
        dchunk = (t * A_QB + kk - lead) // CHUNK - r // CHUNK
        valid = (dchunk >= -LEFT_CHUNKS) & (dchunk <= 0)
        tiles.append(jnp.where(valid[None], _toeplitz(v, A_QB, A_QB) * LOG2E, NEG_INF))
    return jnp.stack(tiles, axis=1).reshape(depth, heads, n_tiles, A_QB, A_QB)


def _attn_b_kernel(far_ref, q_ref, k_ref, v_ref, bias_ref, lam_ref, g_ref, o_ref,
                   qqt_ref, vt_ref, m_ref, l_ref, acc_ref, st_ref, *, lam_init):
    hp = pl.program_id(1)
    i = pl.program_id(2)
    qb = B_QB

    @pl.when(i == 0)
    def _():
        for hh in range(B_HP):
            for c in range(SEQ // qb):
                blk = v_ref[c * qb:(c + 1) * qb, hh * HEAD_DIM:(hh + 1) * HEAD_DIM]
                vt_ref[hh, c] = blk.astype(F32).T.astype(BF16)

    for hh in range(B_HP):
        qt = q_ref[:, hh * HEAD_DIM:(hh + 1) * HEAD_DIM].astype(F32).T
        row = lax.broadcasted_iota(jnp.int32, qt.shape, 0)
        qqt_ref[hh, :, 0:qb] = jnp.where(row < B_QK_DIM, qt, 0.0).astype(BF16)
        qqt_ref[hh, :, qb:2 * qb] = jnp.where(row >= B_QK_DIM, qt, 0.0).astype(BF16)
        m_ref[hh] = jnp.full((1, 2 * qb), NEG_INF, F32)
        l_ref[hh] = jnp.zeros((1, 2 * qb), F32)
        acc_ref[hh] = jnp.zeros((HEAD_DIM, 2 * qb), F32)

    def tile(j, d, nk=1):
        off = pl.multiple_of(j * qb, qb)
        for hh in range(B_HP):
            st_ref[hh, 0:nk * qb] = jnp.dot(
                k_ref[pl.ds(off, nk * qb), hh * HEAD_DIM:(hh + 1) * HEAD_DIM], qqt_ref[hh],
                preferred_element_type=F32)
        for hh in range(B_HP):
            st = st_ref[hh, 0:nk * qb]
            if d is None:
                shift = far_ref[hp * B_HP + hh]
            else:
                b = jnp.concatenate([bias_ref[hh, du] for du in d], axis=0)
                st = jnp.concatenate([st[:, 0:qb] + b, st[:, qb:2 * qb] + b], axis=1)
                shift = 0.0
            m_old = m_ref[hh]
            m_new = jnp.maximum(m_old, jnp.max(st, axis=0, keepdims=True) + shift)
            alpha = jnp.exp2(m_old - m_new)
            p = jnp.exp2(st - (m_new - shift))
            l_ref[hh] = alpha * l_ref[hh] + jnp.sum(p, axis=0, keepdims=True)
            pb = p.astype(BF16)
            pv = None
            for u in range(nk):
                part = jnp.dot(vt_ref[hh, j + u], pb[u * qb:(u + 1) * qb],
                               preferred_element_type=F32)
                pv = part if pv is None else pv + part
            acc_ref[hh] = alpha * acc_ref[hh] + pv
            m_ref[hh] = m_new

    n_far = jnp.maximum(i - (B_NEAR - 1), 0)

    def far_body(jj, c):
        tile(B_FAR_NK * jj, None, nk=B_FAR_NK)
        return c

    n_quads = n_far // B_FAR_NK
    lax.fori_loop(0, n_quads, far_body, 0)
    rem = n_far - B_FAR_NK * n_quads
    assert B_FAR_NK == 8

    @pl.when(rem >= 4)
    def _():
        tile(B_FAR_NK * n_quads, None, nk=4)

    @pl.when(rem % 4 >= 2)
    def _():
        tile(n_far - (rem % 4), None, nk=2)

    @pl.when(rem % 2 == 1)
    def _():
        tile(n_far - 1, None)

    assert B_NEAR == 3
    @pl.when(i >= 2)
    def _():
        tile(i - 2, (2, 1, 0), nk=3)

    @pl.when(i == 1)
    def _():
        tile(0, (1, 0), nk=2)

    @pl.when(i == 0)
    def _():
        tile(0, (0,))

    lv = lam_ref[...]
    lam = (jnp.exp(jnp.sum(lv[0:1] * lv[1:2], axis=-1, keepdims=True))
           - jnp.exp(jnp.sum(lv[2:3] * lv[3:4], axis=-1, keepdims=True)) + lam_init)
    for hh in range(B_HP):
        o = acc_ref[hh] / l_ref[hh]
        od = o[:, 0:qb] - lam * o[:, qb:2 * qb]
        ms = jnp.mean(od * od, axis=0, keepdims=True)
        y = (od * lax.rsqrt(ms + EPS)).T * g_ref[...]
        o_ref[:, hh * HEAD_DIM:(hh + 1) * HEAD_DIM] = (y * (1.0 - lam_init)).astype(o_ref.dtype)


def attn_b(proj, far_bias, bias_b, lam_vecs, subln3, l, lam_init, batch):
    t = proj.shape[0]
    nq = SEQ // B_QB
    wb = B_HP * HEAD_DIM
    qcol = 3 * A_WIDTH // wb
    ncol = B_WIDTH // wb
    return pl.pallas_call(
        functools.partial(_attn_b_kernel, lam_init=lam_init),
        out_shape=jax.ShapeDtypeStruct((t, B_WIDTH), BF16),
        grid=(batch, B_HEADS // B_HP, nq),
        in_specs=[
            pl.BlockSpec(memory_space=pltpu.SMEM),
            pl.BlockSpec((B_QB, wb), lambda b, h, i: (b * nq + i, qcol + h)),
            pl.BlockSpec((SEQ, wb), lambda b, h, i: (b, qcol + ncol + h)),
            pl.BlockSpec((SEQ, wb), lambda b, h, i: (b, qcol + 2 * ncol + h)),
            pl.BlockSpec((B_HP, B_NEAR, B_QB, B_QB), lambda b, h, i: (h, 0, 0, 0)),
            pl.BlockSpec((None, 4, B_QK_DIM), lambda b, h, i: (l, 0, 0)),
            pl.BlockSpec((None, 1, HEAD_DIM), lambda b, h, i: (l, 0, 0)),
        ],
        out_specs=pl.BlockSpec((B_QB, wb), lambda b, h, i: (b * nq + i, h)),
        scratch_shapes=[pltpu.VMEM((B_HP, HEAD_DIM, 2 * B_QB), BF16),
                        pltpu.VMEM((B_HP, SEQ // B_QB, HEAD_DIM, B_QB), BF16),
                        pltpu.VMEM((B_HP, 1, 2 * B_QB), F32),
                        pltpu.VMEM((B_HP, 1, 2 * B_QB), F32),
                        pltpu.VMEM((B_HP, HEAD_DIM, 2 * B_QB), F32),
                        pltpu.VMEM((B_HP, B_FAR_NK * B_QB, 2 * B_QB), F32)],
        compiler_params=_params(("arbitrary", "arbitrary", "arbitrary")),
        name="attn_b",
    )(far_bias, proj, proj, proj, bias_b, lam_vecs, subln3)


def _t5_bucket(rel):
    half = T5_BUCKETS // 2
    max_exact = half // 2
    ret = jnp.where(rel > 0, half, 0)
    n = jnp.abs(rel)
    n_f = jnp.maximum(n, max_exact).astype(jnp.float32)
    large = max_exact + (jnp.log(n_f / max_exact) / math.log(T5_MAX_DIST / max_exact)
                         * (half - max_exact)).astype(jnp.int32)
    large = jnp.minimum(large, half - 1)
    return ret + jnp.where(n < max_exact, n, large)


def t5_bias_tables(t5_table):
    assert B_NEAR * B_QB - (B_QB - 1) >= T5_MAX_DIST
    lo = -(B_NEAR * B_QB)
    bias1d = t5_table.astype(F32)[_t5_bucket(jnp.arange(lo, B_QB))].T
    far = t5_table.astype(F32)[_t5_bucket(jnp.array([-(SEQ - 1)]))][0]
    u = np.arange(2 * B_QB - 1)
    kk = np.arange(B_QB)[:, None]
    qq = np.arange(B_QB)[None, :]
    tiles = []
    for d in range(B_NEAR):
        t = _toeplitz(bias1d[:, (B_QB - 1 - u - d * B_QB) - lo], B_QB, B_QB)
        if d == 0:
            t = jnp.where(((kk // CHUNK) <= (qq // CHUNK))[None], t * LOG2E, NEG_INF)
        else:
            t = t * LOG2E
        tiles.append(t)
    return far * LOG2E, jnp.stack(tiles, axis=1)


def _xattn_kernel(q_ref, k_ref, v_ref, o_ref, s_ref):
    dh = X_HEAD_DIM
    for h in range(X_HEADS):
        s_ref[h] = lax.dot_general(q_ref[:, h * dh:(h + 1) * dh], k_ref[:, h * dh:(h + 1) * dh],
                                   (((1,), (1,)), ((), ())), preferred_element_type=F32)
    for h in range(X_HEADS):
        s = s_ref[h]
        m = jnp.max(s, axis=-1, keepdims=True)
        p = jnp.exp(s - m)
        denom = jnp.sum(p, axis=-1, keepdims=True)
        o = jnp.dot(p.astype(BF16), v_ref[:, h * dh:(h + 1) * dh], preferred_element_type=F32)
        o_ref[:, h * dh:(h + 1) * dh] = (o / denom).astype(o_ref.dtype)


def xattn(q, kv, batch):
    t, d = q.shape
    nq = SEQ // X_QB
    return pl.pallas_call(
        _xattn_kernel,
        out_shape=jax.ShapeDtypeStruct((t, d), BF16),
        grid=(batch, nq),
        in_specs=[
            pl.BlockSpec((X_QB, d), lambda b, i: (b * nq + i, 0)),
            pl.BlockSpec((MEM_LEN, d), lambda b, i: (b, 0)),
            pl.BlockSpec((MEM_LEN, d), lambda b, i: (b, 1)),
        ],
        out_specs=pl.BlockSpec((X_QB, d), lambda b, i: (b * nq + i, 0)),
        scratch_shapes=[pltpu.VMEM((X_HEADS, X_QB, MEM_LEN), F32)],
        compiler_params=_params(("parallel", "parallel")),
        name="xattn",
    )(q, kv, kv)


def _pack_rows(v, out_ref):
    rows, width = v.shape
    half = width // 2
    lo = lax.bitcast_convert_type(v[:, :half].astype(BF16).astype(F32), jnp.uint32)
    hi = lax.bitcast_convert_type(v[:, half:].astype(BF16).astype(F32), jnp.uint32)
    u = (lo >> 16) | hi
    for c in range(half // LANES):
        out_ref[pl.ds(c, rows, stride=PACK_SUB), :] = u[:, c * LANES:(c + 1) * LANES]


def _unpack_rows(ref, rows):
    lo, hi = [], []
    for c in range(PACK_SUB):
        u = ref[pl.ds(c, rows, stride=PACK_SUB), :]
        lo.append(lax.bitcast_convert_type(u << 16, F32))
        hi.append(lax.bitcast_convert_type(u & jnp.uint32(0xFFFF0000), F32))
    return lo, hi


def _router_kernel(x_ref, g_ref, wr_ref, br_ref, route_ref, cnt_ref, hpk_ref, carry_ref, tri_ref,
                   wrb_ref):
    tm = x_ref.shape[0]

    @pl.when(pl.program_id(0) == 0)
    def _():
        carry_ref[...] = jnp.zeros(carry_ref.shape, F32)
        wrb_ref[...] = wr_ref[...].astype(BF16)
        r = lax.broadcasted_iota(jnp.int32, (tm, tm), 0)
        c = lax.broadcasted_iota(jnp.int32, (tm, tm), 1)
        tri_ref[...] = jnp.where(c < r, 1.0, 0.0).astype(BF16)

    x = x_ref[...]
    ms = jnp.mean(x * x, axis=-1, keepdims=True)
    hn = x * lax.rsqrt(ms + EPS) * g_ref[...]
    _pack_rows(hn, hpk_ref)
    hb = hn.astype(BF16)
    logits = jnp.dot(hb, wrb_ref[...], preferred_element_type=F32) + br_ref[...]

    lane = lax.broadcasted_iota(jnp.int32, logits.shape, 1)
    lane_f = lane.astype(F32)
    big = jnp.float32(1e9)

    def top1(mask):
        v = jnp.where(mask, logits, -jnp.inf)
        vmax = jnp.max(v, axis=-1, keepdims=True)
        idx = jnp.min(jnp.where(mask & (v == vmax), lane_f, big), axis=-1, keepdims=True)
        return vmax, idx

    gmask = (lane >= N_EXPERTS) & (lane < N_EXPERTS + N_GROUPS)
    gmax, gidx = top1(gmask)
    gsum = jnp.sum(jnp.where(gmask, jnp.exp(logits - gmax), 0.0), axis=-1, keepdims=True)
    g_gate = 1.0 / gsum
    g_sel = gidx.astype(jnp.int32) - N_EXPERTS

    emask = (lane < N_EXPERTS) & ((lane // EXPERTS_PER_GROUP) == g_sel)
    v1, i1 = top1(emask)
    v2, i2 = top1(emask & (lane_f != i1))
    e2 = jnp.exp(v2 - v1)
    w1 = g_gate / (1.0 + e2)
    w2 = g_gate * e2 / (1.0 + e2)

    oh1 = jnp.where(lane_f == i1, 1.0, 0.0)
    oh2 = jnp.where(lane_f == i2, 1.0, 0.0)
    oh = oh1 + oh2
    before = jnp.dot(tri_ref[...], oh.astype(BF16), preferred_element_type=F32) + carry_ref[...]
    r1 = jnp.sum(oh1 * before, axis=-1, keepdims=True)
    r2 = jnp.sum(oh2 * before, axis=-1, keepdims=True)
    carry_ref[...] = carry_ref[...] + jnp.sum(oh, axis=0, keepdims=True)
    cnt_ref[...] = carry_ref[...]

    route = jnp.zeros(logits.shape, F32)
    for col, val in enumerate((i1, i2, w1, w2, r1, r2)):
        route = jnp.where(lane == col, val, route)
    route_ref[...] = route


def router(x, g3, wr, br, l, tm=512):
    t, d = x.shape
    return pl.pallas_call(
        _router_kernel,
        out_shape=(jax.ShapeDtypeStruct((t, LANES), F32),
                   jax.ShapeDtypeStruct((1, LANES), F32),
                   jax.ShapeDtypeStruct((t * PACK_SUB, LANES), jnp.uint32)),
        grid=(t // tm,),
        in_specs=[pl.BlockSpec((tm, d), lambda i: (i, 0)),
                  pl.BlockSpec((None, 1, d), lambda i: (l, 0, 0)),
                  pl.BlockSpec((None, d, LANES), lambda i: (l, 0, 0)),
                  pl.BlockSpec((None, 1, LANES), lambda i: (l, 0, 0))],
        out_specs=(pl.BlockSpec((tm, LANES), lambda i: (i, 0)),
                   pl.BlockSpec((1, LANES), lambda i: (0, 0)),
                   pl.BlockSpec((tm * PACK_SUB, LANES), lambda i: (i, 0))),
        scratch_shapes=[pltpu.VMEM((1, LANES), F32), pltpu.VMEM((tm, tm), BF16),
                        pltpu.VMEM((d, LANES), BF16)],
        compiler_params=_params(("arbitrary",)),
        name="router",
    )(x, g3, wr, br)


def _max_tiles(t):
    return (2 * t) // E_TM + N_EXPERTS


def dispatch_plan(route, counts_row):
    t = route.shape[0]
    max_tiles = _max_tiles(t)
    ids = route[:, 0:2].astype(jnp.int32)
    ranks = route[:, 4:6].astype(jnp.int32)
    counts = counts_row[0, :N_EXPERTS].astype(jnp.int32)
    padded = ((counts + E_TM - 1) // E_TM) * E_TM
    ends = jnp.cumsum(padded)
    offs = ends - padded
    onehot = ids[:, :, None] == jnp.arange(N_EXPERTS, dtype=jnp.int32)
    dest = (jnp.sum(jnp.where(onehot, offs, 0), axis=-1) + ranks).reshape(2 * t)
    n_tiles = ends[-1] // E_TM
    idx = jnp.arange(max_tiles, dtype=jnp.int32)
    tile_i = jnp.minimum(idx, n_tiles - 1)
    tile_expert = jnp.sum(ends[None, :] <= tile_i[:, None] * E_TM, axis=1).astype(jnp.int32)
    used_end = (offs + counts)[tile_expert]
    tile_rows = jnp.clip(used_end - tile_i * E_TM, 0, E_TM).astype(jnp.int32)
    tile_fill = jnp.where(idx >= n_tiles, FILL_TAIL,
                          jnp.where(tile_rows < E_TM, FILL_PADDED, 0)).astype(jnp.int32)
    first = (idx < n_tiles) & ((idx == 0) | (tile_expert != jnp.roll(tile_expert, 1)))
    slot = (jnp.cumsum(first) - 1) % 2
    first_pos = jnp.where(first, idx, max_tiles)
    after = jnp.flip(lax.cummin(jnp.flip(first_pos)))
    nxt_pos = jnp.concatenate([after[1:], jnp.full((1,), max_tiles, jnp.int32)])
    nxt = jnp.where(nxt_pos < max_tiles,
                    tile_expert[jnp.minimum(nxt_pos, max_tiles - 1)], -1)
    tile_plan = jnp.stack([first.astype(jnp.int32), slot.astype(jnp.int32),
                           nxt.astype(jnp.int32)])
    return dest, tile_expert, tile_fill, n_tiles.reshape(1).astype(jnp.int32), tile_plan


def _dispatch_kernel(dest_ref, fill_ref, h_ref, xs_hbm, stage0, stage1, sem, zbuf, zsem):
    i = pl.program_id(0)
    n = pl.num_programs(0)
    stages = (stage0, stage1)

    def fill_copy(j, kind):
        blk = E_TM * PACK_SUB
        return pltpu.make_async_copy(zbuf, xs_hbm.at[pl.ds(j * blk, blk)], zsem.at[kind - 1])

    def fill_each(kind, fn):
        for j in range(fill_ref.shape[0]):
            @pl.when(fill_ref[j] == kind)
            def _(j=j):
                fn(fill_copy(j, kind))

    @pl.when(i == 0)
    def _():
        zbuf[...] = jnp.zeros(zbuf.shape, zbuf.dtype)
        fill_each(FILL_PADDED, lambda c: c.start())
        fill_each(FILL_TAIL, lambda c: c.start())
        fill_each(FILL_PADDED, lambda c: c.wait())

    def drain(s):
        for _ in range(2):
            pltpu.make_async_copy(stages[s], xs_hbm.at[pl.ds(0, D_TM * PACK_SUB)],
                                  sem.at[s]).wait()

    def step(s):
        @pl.when(i >= 2)
        def _():
            drain(s)

        stages[s][...] = h_ref[...]
        base = 2 * i * D_TM
        for r in range(D_TM):
            for k in range(2):
                row = dest_ref[base + 2 * r + k]
                dst = xs_hbm.at[pl.ds(pl.multiple_of(row * PACK_SUB, PACK_SUB), PACK_SUB)]
                pltpu.make_async_copy(stages[s].at[pl.ds(r * PACK_SUB, PACK_SUB)], dst,
                                      sem.at[s]).start(priority=k)

        @pl.when(i == n - 1)
        def _():
            drain(s)

            @pl.when(n >= 2)
            def _():
                drain(1 - s)

    for s in range(2):
        @pl.when(i % 2 == s)
        def _(s=s):
            step(s)

    @pl.when(i == n - 1)
    def _():
        fill_each(FILL_TAIL, lambda c: c.wait())


def dispatch(hpk, dest, tile_fill):
    t = dest.shape[0] // 2
    blk = D_TM * PACK_SUB
    grid_spec = pltpu.PrefetchScalarGridSpec(
        num_scalar_prefetch=2,
        grid=(t // D_TM,),
        in_specs=[pl.BlockSpec((blk, LANES), lambda i, dest, fill: (i, 0))],
        out_specs=pl.BlockSpec(memory_space=pl.ANY),
        scratch_shapes=[pltpu.VMEM((blk, LANES), jnp.uint32),
                        pltpu.VMEM((blk, LANES), jnp.uint32),
                        pltpu.SemaphoreType.DMA((2,)),
                        pltpu.VMEM((E_TM * PACK_SUB, LANES), jnp.uint32),
                        pltpu.SemaphoreType.DMA((2,))],
    )
    return pl.pallas_call(
        _dispatch_kernel,
        out_shape=jax.ShapeDtypeStruct((tile_fill.shape[0] * E_TM * PACK_SUB, LANES), jnp.uint32),
        grid_spec=grid_spec,
        compiler_params=_params(("arbitrary",)),
        name="dispatch",
    )(dest, tile_fill, hpk)


def _experts_kernel(te_ref, nt_ref, plan_ref, xs_ref, w1_hbm, w3_hbm, w2_hbm, ys_ref,
                    w1f, w3f, w2f, wsem, w1b, w3b, w2b, *, layer):
    i = pl.program_id(0)
    nt = nt_ref[0]

    def weight_copies(e, slot):
        return [pltpu.make_async_copy(w_hbm.at[layer, e], wf.at[slot], wsem.at[slot])
                for w_hbm, wf in ((w1_hbm, w1f), (w3_hbm, w3f), (w2_hbm, w2f))]

    @pl.when(i == 0)
    def _():
        for c in weight_copies(te_ref[0], 0):
            c.start()

    @pl.when(i < nt)
    def _():
        @pl.when(plan_ref[0, i] == 1)
        def _():
            slot = plan_ref[1, i]
            for c in weight_copies(te_ref[i], slot):
                c.wait()
            w1b[...] = w1f[slot].astype(BF16)
            w3b[...] = w3f[slot].astype(BF16)
            w2b[...] = w2f[slot].astype(BF16)

            @pl.when(plan_ref[2, i] >= 0)
            def _():
                for c in weight_copies(plan_ref[2, i], 1 - slot):
                    c.start()

        lo, hi = _unpack_rows(xs_ref, E_TM)
        hb = jnp.concatenate([v.astype(BF16) for v in lo + hi], axis=1)
        a1 = jnp.dot(hb, w1b[...], preferred_element_type=F32)
        a3 = jnp.dot(hb, w3b[...], preferred_element_type=F32)
        hid = (a1 / (1.0 + jnp.exp(-a1))) * a3
        _pack_rows(jnp.dot(hid.astype(BF16), w2b[...], preferred_element_type=F32), ys_ref)

    @pl.when(i >= nt)
    def _():
        ys_ref[...] = jnp.zeros(ys_ref.shape, ys_ref.dtype)


def experts(xs, w1, w3, w2, l, tile_expert, n_tiles, tile_plan):
    d = w1.shape[-2]
    max_tiles = tile_expert.shape[0]
    blk = E_TM * PACK_SUB
    any_spec = pl.BlockSpec(memory_space=pl.ANY)
    grid_spec = pltpu.PrefetchScalarGridSpec(
        num_scalar_prefetch=3,
        grid=(max_tiles,),
        in_specs=[pl.BlockSpec((blk, LANES),
                               lambda i, te, nt, plan: (jnp.minimum(i, nt[0] - 1), 0)),
                  any_spec, any_spec, any_spec],
        out_specs=pl.BlockSpec((blk, LANES), lambda i, te, nt, plan: (i, 0)),
        scratch_shapes=[pltpu.VMEM((2, d, D_EXPERT), F32),
                        pltpu.VMEM((2, d, D_EXPERT), F32),
                        pltpu.VMEM((2, D_EXPERT, d), F32),
                        pltpu.SemaphoreType.DMA((2,)),
                        pltpu.VMEM((d, D_EXPERT), BF16),
                        pltpu.VMEM((d, D_EXPERT), BF16),
                        pltpu.VMEM((D_EXPERT, d), BF16)],
    )
    return pl.pallas_call(
        functools.partial(_experts_kernel, layer=l),
        out_shape=jax.ShapeDtypeStruct((max_tiles * blk, LANES), jnp.uint32),
        grid_spec=grid_spec,
        compiler_params=_params(("arbitrary",)),
        name="experts",
    )(tile_expert, n_tiles, tile_plan, xs, w1, w3, w2)


def _combine_kernel(dest_ref, ys_hbm, x_ref, route_ref, *rest, has_norm):
    g_ref = rest[0] if has_norm else None
    o_ref, ybuf0, ybuf1, sem = rest[1:] if has_norm else rest
    i = pl.program_id(0)
    n = pl.num_programs(0)
    bufs = (ybuf0, ybuf1)

    def row_copy(row, s, k, r):
        src = ys_hbm.at[pl.ds(pl.multiple_of(row * PACK_SUB, PACK_SUB), PACK_SUB)]
        return pltpu.make_async_copy(src, bufs[s].at[k, pl.ds(r * PACK_SUB, PACK_SUB)], sem.at[s])

    def tile_wait(s):
        for k in range(2):
            pltpu.make_async_copy(ys_hbm.at[pl.ds(0, C_TM * PACK_SUB)], bufs[s].at[k],
                                  sem.at[s]).wait()

    @pl.when(i == 0)
    def _():
        def body(r, c):
            for k in range(2):
                src = ys_hbm.at[pl.ds(pl.multiple_of(dest_ref[2 * r + k] * PACK_SUB, PACK_SUB),
                                      PACK_SUB)]
                dst = ybuf0.at[k, pl.ds(pl.multiple_of(r * PACK_SUB, PACK_SUB), PACK_SUB)]
                pltpu.make_async_copy(src, dst, sem.at[0]).start()
            return c

        lax.fori_loop(0, C_TM, body, 0, unroll=4)

    def step(s):
        tile_wait(s)
        base = 2 * jnp.minimum(i + 1, n - 1) * C_TM
        for r in range(C_TM):
            for k in range(2):
                row_copy(dest_ref[base + 2 * r + k], 1 - s, k, r).start(priority=k)

        route = route_ref[...]
        lane = lax.broadcasted_iota(jnp.int32, route.shape, 1)
        w1 = jnp.sum(jnp.where(lane == 2, route, 0.0), axis=-1, keepdims=True)
        w2 = jnp.sum(jnp.where(lane == 3, route, 0.0), axis=-1, keepdims=True)
        lo1, hi1 = _unpack_rows(bufs[s].at[0], C_TM)
        lo2, hi2 = _unpack_rows(bufs[s].at[1], C_TM)
        half = x_ref.shape[1] // 2
        for c in range(PACK_SUB):
            for off, y1, y2 in ((0, lo1, lo2), (half, hi1, hi2)):
                cols = slice(off + c * LANES, off + (c + 1) * LANES)
                o_ref[:, cols] = x_ref[:, cols] + w1 * y1[c] + w2 * y2[c]
        if has_norm:
            y = o_ref[...]
            ms = jnp.mean(y * y, axis=-1, keepdims=True)
            o_ref[...] = y * lax.rsqrt(ms + EPS) * g_ref[...]

        @pl.when(i == n - 1)
        def _():
            tile_wait(1 - s)

    for s in range(2):
        @pl.when(i % 2 == s)
        def _(s=s):
            step(s)


def combine(ys, x, route, dest, norm_gain=None):
    t, d = x.shape
    blk = C_TM * PACK_SUB
    grid_spec = pltpu.PrefetchScalarGridSpec(
        num_scalar_prefetch=1,
        grid=(t // C_TM,),
        in_specs=[pl.BlockSpec(memory_space=pl.ANY),
                  pl.BlockSpec((C_TM, d), lambda i, dest: (i, 0)),
                  pl.BlockSpec((C_TM, LANES), lambda i, dest: (i, 0))]
        + ([] if norm_gain is None else [pl.BlockSpec((None, 1, d), lambda i, dest: (0, 0, 0))]),
        out_specs=pl.BlockSpec((C_TM, d), lambda i, dest: (i, 0)),
        scratch_shapes=[pltpu.VMEM((2, blk, LANES), jnp.uint32),
                        pltpu.VMEM((2, blk, LANES), jnp.uint32),
                        pltpu.SemaphoreType.DMA((2,))],
    )
    return pl.pallas_call(
        functools.partial(_combine_kernel, has_norm=norm_gain is not None),
        out_shape=jax.ShapeDtypeStruct((t, d), F32),
        grid_spec=grid_spec,
        compiler_params=_params(("arbitrary",)),
        name="combine",
    )(dest, ys, x, route, *([] if norm_gain is None else [norm_gain]))


def kernel(x, mem, norm_mix, w_in, rel_bias_a, t5_table, diff_lambda, diff_subln, w_out,
           norm_cross, norm_mem, w_xq, w_xkv, w_xo, norm_ffn, w_group, b_group,
           w_expert, b_expert, w1, w3, w2, norm_final):
    batch, seq, d = x.shape
    t = batch * seq
    xf = x.reshape(t, d)
    memf = mem.reshape(batch * MEM_LEN, d)

    g_mix = norm_mix.reshape(DEPTH, 1, d)
    g_cross = norm_cross.reshape(DEPTH, 1, d)
    g_mem = norm_mem.reshape(DEPTH, 1, d)
    g_ffn = norm_ffn.reshape(DEPTH, 1, d)
    g_final = norm_final.reshape(1, 1, d)
    subln3 = diff_subln.reshape(DEPTH, 1, HEAD_DIM)

    in_scale = jnp.concatenate([
        jnp.full((A_WIDTH,), HEAD_DIM ** -0.5 * LOG2E, F32), jnp.ones((2 * A_WIDTH,), F32),
        jnp.full((B_WIDTH,), B_QK_DIM ** -0.5 * LOG2E, F32), jnp.ones((2 * B_WIDTH,), F32)]
    ).reshape(1, IN_WIDTH)
    xq_scale = jnp.full((1, d), X_HEAD_DIM ** -0.5, F32)

    far_b, bias_b = t5_bias_tables(t5_table)
    bias_a = band_bias_tables(rel_bias_a)

    pad = LANES - N_EXPERTS - N_GROUPS
    wr = jnp.concatenate([w_expert, w_group, jnp.zeros((DEPTH, d, pad), F32)], axis=-1)
    br = jnp.concatenate([b_expert, b_group, jnp.zeros((DEPTH, pad), F32)],
                         axis=-1).reshape(DEPTH, 1, LANES)

    for l in range(DEPTH):
        lam_init = 0.8 - 0.6 * math.exp(-0.3 * l)
        proj = matmul(xf, w_in, l, BF16, norm_gain=g_mix, colscale=in_scale, tm=1024)
        oa = attn_a(proj, bias_a, l, batch)
        ob = attn_b(proj, far_b, bias_b, diff_lambda, subln3, l, lam_init, batch)
        xf = matmul([oa, ob], w_out, l, F32, resid=xf, tm=1024)
        q = matmul(xf, w_xq, l, BF16, norm_gain=g_cross, colscale=xq_scale, tn=2048)
        kv = matmul(memf, w_xkv, l, BF16, norm_gain=g_mem)
        xo = xattn(q, kv, batch)
        xf = matmul(xo, w_xo, l, F32, resid=xf, tm=1024)
        route, counts, hpk = router(xf, g_ffn, wr, br, l)
        dest, tile_expert, tile_fill, n_tiles, tile_plan = dispatch_plan(route, counts)
        xs = dispatch(hpk, dest, tile_fill)
        ys = experts(xs, w1, w3, w2, l, tile_expert, n_tiles, tile_plan)
        xf = combine(ys, xf, route, dest, norm_gain=g_final if l == DEPTH - 1 else None)
    return xf.reshape(batch, seq, d)
```
